```python
import jax, jax.numpy as jnp
from jax import lax
import numpy as np

D_MODEL = 1024
BATCH = 8
SEQ = 2048
DEPTH = 1

EPS = 1e-6
DN_HEADS = 8
DN_DK = 128
DN_DV = 128
DN_QK_WIDTH = DN_HEADS * DN_DK
DN_WIDTH = DN_HEADS * DN_DV
CONV_K = 4
CHUNK = 64
DIL_CONFIGS = ((128, 1), (512, 4), (2048, 16))
HEADS_PER_GROUP = 4
ATTN_HEADS = HEADS_PER_GROUP * len(DIL_CONFIGS)
ATTN_HEAD_DIM = 64
ATTN_WIDTH = ATTN_HEADS * ATTN_HEAD_DIM
ATTN_BLOCK = 128
IN_SPLITS = (DN_QK_WIDTH, DN_QK_WIDTH, DN_WIDTH, DN_WIDTH, DN_HEADS, DN_HEADS,
             ATTN_WIDTH, ATTN_WIDTH, ATTN_WIDTH, ATTN_WIDTH, D_MODEL, D_MODEL)
N_IN = sum(IN_SPLITS)

kernel_name = "hybrid_gated_deltanet_dilated_attn_block"


def rms_norm(x, w):
    xf = x.astype(jnp.float32)
    y = xf * lax.rsqrt(jnp.mean(xf * xf, axis=-1, keepdims=True) + EPS)
    return (y * w.astype(jnp.float32)).astype(x.dtype)


def l2_normalize(x):
    xf = x.astype(jnp.float32)
    return xf * lax.rsqrt(jnp.sum(xf * xf, axis=-1, keepdims=True) + EPS)


def causal_dwconv_silu(x, w):
    k_w = w.shape[0]
    s = x.shape[1]
    xp = jnp.pad(x, ((0, 0), (k_w - 1, 0), (0, 0)))
    y = sum(xp[:, j:j + s] * w[j] for j in range(k_w))
    return jax.nn.silu(y)


def gated_delta_rule(q, k, v, g, beta):
    b, s, h, dk = q.shape
    n = s // CHUNK

    def to_chunks(t):
        return jnp.moveaxis(t.reshape((b, n, CHUNK) + t.shape[2:]), 3, 2)

    q, k, v, g, beta = map(to_chunks, (q * dk ** -0.5, k, v, g, beta))
    gc = jnp.cumsum(g, axis=-1)
    causal = jnp.tril(jnp.ones((CHUNK, CHUNK), dtype=bool))
    strict = jnp.tril(jnp.ones((CHUNK, CHUNK), dtype=bool), -1)
    decay = jnp.exp(jnp.where(causal, gc[..., :, None] - gc[..., None, :], -jnp.inf))
    k_beta = k * beta[..., None]
    v_beta = v * beta[..., None]
    lmat = jnp.where(strict, jnp.einsum('bnhcd,bnhed->bnhce', k_beta, k) * decay, 0.0)
    eye = jnp.eye(CHUNK, dtype=jnp.float32)
    t_inv = lax.linalg.triangular_solve(eye + lmat, jnp.broadcast_to(eye, lmat.shape),
                                        left_side=True, lower=True, unit_diagonal=True)
    u = jnp.einsum('bnhce,bnhed->bnhcd', t_inv, v_beta)
    w = jnp.einsum('bnhce,bnhed->bnhcd', t_inv, k_beta * jnp.exp(gc)[..., None])
    a_qk = jnp.einsum('bnhcd,bnhed->bnhce', q, k) * decay
    q_dec = q * jnp.exp(gc)[..., None]
    k_dec = k * jnp.exp(gc[..., -1:] - gc)[..., None]
    g_last = jnp.exp(gc[..., -1])

    def step(state, xs):
        q_c, k_c, u_c, w_c, a_c, gl = xs
        v_new = u_c - jnp.einsum('bhcd,bhde->bhce', w_c, state)
        o_c = jnp.einsum('bhcd,bhde->bhce', q_c, state) + jnp.einsum('bhce,bhed->bhcd', a_c, v_new)
        state = state * gl[..., None, None] + jnp.einsum('bhcd,bhce->bhde', k_c, v_new)
        return state, o_c

    xs = tuple(jnp.moveaxis(t, 1, 0) for t in (q_dec, k_dec, u, w, a_qk, g_last))
    state0 = jnp.zeros((b, h, dk, v.shape[-1]), jnp.float32)
    _, o = lax.scan(step, state0, xs)
    o = jnp.moveaxis(jnp.moveaxis(o, 0, 1), 2, 3)
    return o.reshape(b, s, h, -1)


def dilated_window_attention(q, k, v, window, dilation):
    b, s, h, dh = q.shape
    n_back = window // dilation
    sub_len = s // dilation
    nb = -(-sub_len // ATTN_BLOCK)
    lp = nb * ATTN_BLOCK
    bb = b * dilation

    def to_sub(t):
        t = t.reshape(b, sub_len, dilation, h, dh).transpose(0, 2, 1, 3, 4).reshape(bb, sub_len, h, dh)
        return jnp.pad(t, ((0, 0), (0, lp - sub_len), (0, 0), (0, 0)))

    def band(t):
        tp = jnp.pad(t, ((0, 0), (ATTN_BLOCK, 0), (0, 0), (0, 0))).reshape(bb, nb + 1, ATTN_BLOCK, h, dh)
        return jnp.concatenate([tp[:, :-1], tp[:, 1:]], axis=2)

    qb = to_sub(q).reshape(bb, nb, ATTN_BLOCK, h, dh)
    kb = band(to_sub(k))
    vb = band(to_sub(v))
    scores = jnp.einsum('bnqhd,bnkhd->bnhqk', qb, kb,
                        preferred_element_type=jnp.float32) * (dh ** -0.5)
    blk = jnp.arange(nb)[:, None, None]
    qpos = blk * ATTN_BLOCK + jnp.arange(ATTN_BLOCK)[None, :, None]
    kpos = (blk - 1) * ATTN_BLOCK + jnp.arange(2 * ATTN_BLOCK)[None, None, :]
    dist = qpos - kpos
    mask = (dist >= 0) & (dist <= n_back) & (kpos >= 0)
    scores = jnp.where(mask[None, :, None], scores, -jnp.inf)
    m = jnp.max(scores, axis=-1, keepdims=True)
    p = jnp.exp(scores - m)
    denom = jnp.sum(p, axis=-1, keepdims=True)
    o = jnp.einsum('bnhqk,bnkhd->bnqhd', (p / denom).astype(v.dtype), vb)
    lse = (m + jnp.log(denom))[..., 0]
    o = o.reshape(bb, lp, h, dh)[:, :sub_len]
    o = o.reshape(b, dilation, sub_len, h, dh).transpose(0, 2, 1, 3, 4).reshape(b, s, h, dh)
    lse = lse.transpose(0, 1, 3, 2).reshape(bb, lp, h)[:, :sub_len]
    lse = lse.reshape(b, dilation, sub_len, h).transpose(0, 2, 1, 3).reshape(b, s, h)
    return o, lse


def setup_inputs(seed: int = 0) -> dict:
    key = jax.random.key(seed)
    ks = jax.random.split(key, 16)
    f32 = jnp.float32
    d = D_MODEL
    x = jax.random.normal(ks[0], (BATCH, SEQ, d), f32)
    c = jax.random.normal(ks[1], (BATCH, d), f32)
    norm_w = 1.0 + 0.02 * jax.random.normal(ks[2], (DEPTH, d), f32)
    ada_w = jax.random.normal(ks[3], (DEPTH, d, 3 * d), f32) * d ** -0.5
    ada_b = 0.01 * jax.random.normal(ks[4], (DEPTH, 3 * d), f32)
    w_in = jax.random.normal(ks[5], (DEPTH, d, N_IN), f32) * d ** -0.5
    conv_w = jax.random.normal(ks[6], (DEPTH, CONV_K, 2 * DN_QK_WIDTH + DN_WIDTH), f32) * CONV_K ** -0.5
    a_log = jnp.log(jax.random.uniform(ks[7], (DEPTH, DN_HEADS), f32, 1.0, 16.0))
    dt = jnp.exp(jax.random.uniform(ks[8], (DEPTH, DN_HEADS), f32, np.log(1e-3), np.log(1e-1)))
    dt_bias = dt + jnp.log(-jnp.expm1(-dt))
    dn_norm_w = 1.0 + 0.02 * jax.random.normal(ks[9], (DEPTH, DN_DV), f32)
    w_proj_a = jax.random.normal(ks[10], (DEPTH, DN_WIDTH, d), f32) * DN_WIDTH ** -0.5
    w_proj_b = jax.random.normal(ks[11], (DEPTH, ATTN_WIDTH, d), f32) * ATTN_WIDTH ** -0.5
    w_out = jax.random.normal(ks[12], (DEPTH, d, d), f32) * d ** -0.5
    final_norm_w = 1.0 + 0.02 * jax.random.normal(ks[13], (d,), f32)
    return {"x": x, "c": c, "norm_w": norm_w, "ada_w": ada_w, "ada_b": ada_b, "w_in": w_in,
            "conv_w": conv_w, "a_log": a_log, "dt_bias": dt_bias, "dn_norm_w": dn_norm_w,
            "w_proj_a": w_proj_a, "w_proj_b": w_proj_b, "w_out": w_out, "final_norm_w": final_norm_w}


def reference(x, c, norm_w, ada_w, ada_b, w_in, conv_w, a_log, dt_bias, dn_norm_w,
              w_proj_a, w_proj_b, w_out, final_norm_w):
    b, s, _ = x.shape
    split_points = [int(p) for p in np.cumsum(IN_SPLITS)[:-1]]
    for l in range(DEPTH):
        mod = jax.nn.silu(c) @ ada_w[l] + ada_b[l]
        shift, scale, gate = jnp.split(mod, 3, axis=-1)
        h = rms_norm(x, norm_w[l]) * (1.0 + scale[:, None]) + shift[:, None]
        proj = h @ w_in[l]
        (qa, ka, va, za, beta_in, a_in, qb, kb, vb, zb, ga, gb) = jnp.split(proj, split_points, axis=-1)

        qkv = causal_dwconv_silu(jnp.concatenate([qa, ka, va], axis=-1), conv_w[l])
        qa, ka, va = jnp.split(qkv, [DN_QK_WIDTH, 2 * DN_QK_WIDTH], axis=-1)
        q_dn = l2_normalize(qa.reshape(b, s, DN_HEADS, DN_DK))
        k_dn = l2_normalize(ka.reshape(b, s, DN_HEADS, DN_DK))
        v_dn = va.reshape(b, s, DN_HEADS, DN_DV).astype(jnp.float32)
        beta = jax.nn.sigmoid(beta_in.astype(jnp.float32))
        g = -jnp.exp(a_log[l].astype(jnp.float32)) * jax.nn.softplus(
            a_in.astype(jnp.float32) + dt_bias[l].astype(jnp.float32))
        o_a = gated_delta_rule(q_dn, k_dn, v_dn, g, beta).astype(x.dtype)
        y_a = rms_norm(o_a, dn_norm_w[l]).reshape(b, s, DN_WIDTH) * jax.nn.silu(za)
        y_a = y_a @ w_proj_a[l]

        qb = qb.reshape(b, s, ATTN_HEADS, ATTN_HEAD_DIM)
        kb = kb.reshape(b, s, ATTN_HEADS, ATTN_HEAD_DIM)
        vb = vb.reshape(b, s, ATTN_HEADS, ATTN_HEAD_DIM)
        outs, lses = [], []
        for gi, (win, dil) in enumerate(DIL_CONFIGS):
            hs = slice(gi * HEADS_PER_GROUP, (gi + 1) * HEADS_PER_GROUP)
            o_g, lse_g = dilated_window_attention(qb[:, :, hs], kb[:, :, hs], vb[:, :, hs], win, dil)
            outs.append(o_g)
            lses.append(lse_g)
        o_b = jnp.stack(outs, axis=2)
        w_g = jax.nn.softmax(jnp.stack(lses, axis=2), axis=2)
        y_b = (o_b * w_g[..., None].astype(o_b.dtype)).reshape(b, s, ATTN_WIDTH) * jax.nn.silu(zb)
        y_b = y_b @ w_proj_b[l]

        merged = jax.nn.sigmoid(ga) * y_a + jax.nn.sigmoid(gb) * y_b
        x = x + gate[:, None] * (merged @ w_out[l])
    return rms_norm(x, final_norm_w)
```

```python
import functools

import jax
import jax.numpy as jnp
import numpy as np
from jax import lax
from jax.experimental import pallas as pl
from jax.experimental.pallas import tpu as pltpu

F32 = jnp.float32
BF16 = jnp.bfloat16
HIGHEST = lax.Precision.HIGHEST

EPS = 1e-6
DN_HEADS = 8
DN_DK = 128
DN_DV = 128
DN_WIDTH = DN_HEADS * DN_DV
CONV_K = 4
CHUNK = 64
DIL_CONFIGS = ((128, 1), (512, 4), (2048, 16))
HEADS_PER_GROUP = 4
ATTN_HEAD_DIM = 64
ATTN_GROUP_WIDTH = HEADS_PER_GROUP * ATTN_HEAD_DIM
ATTN_WIDTH = ATTN_GROUP_WIDTH * len(DIL_CONFIGS)
ATTN_BLOCK = 128
LANES = 128
HALO = 8
VMEM_LIMIT = 56 * 1024 * 1024


def _mm(a, b):
    return jnp.dot(a.astype(BF16), b.astype(BF16), preferred_element_type=F32)


def _mm_nt(a, b):
    return lax.dot_general(a.astype(BF16), b.astype(BF16), (((1,), (1,)), ((), ())),
                           preferred_element_type=F32)


def _mm_tn(a, b):
    return lax.dot_general(a.astype(BF16), b.astype(BF16), (((0,), (0,)), ((), ())),
                           preferred_element_type=F32)


def _sigmoid(x):
    return 1.0 / (1.0 + jnp.exp(-x))


def _silu(x):
    return x * _sigmoid(x)


def _mod_kernel(c_ref, w_ref, b_ref, o_ref):
    sc = _silu(c_ref[...])
    o_ref[...] = jnp.dot(sc, w_ref[...], preferred_element_type=F32, precision=HIGHEST) + b_ref[...]


def _adaln_mod(c, ada_w, ada_b):
    b, d = c.shape
    return pl.pallas_call(
        _mod_kernel,
        grid=(3,),
        in_specs=[pl.BlockSpec((b, d), lambda j: (0, 0)),
                  pl.BlockSpec((d, d), lambda j: (0, j)),
                  pl.BlockSpec((1, d), lambda j: (0, j))],
        out_specs=pl.BlockSpec((b, d), lambda j: (0, j)),
        out_shape=jax.ShapeDtypeStruct((b, 3 * d), F32),
        name="adaln_mod",
    )(c, ada_w, ada_b.reshape(1, 3 * d))


def _inproj_kernel(x_ref, mod_ref, nw_ref, w_ref, ws_ref, p_ref, small_ref, h_scr):
    @pl.when(pl.program_id(2) == 0)
    def _():
        x = x_ref[0]
        m = mod_ref[0]
        y = x * lax.rsqrt(jnp.mean(x * x, axis=-1, keepdims=True) + EPS) * nw_ref[...]
        h = (y * (1.0 + m[1:2]) + m[0:1]).astype(BF16)
        h_scr[...] = h
        small_ref[0] = jnp.dot(h, ws_ref[...], preferred_element_type=F32)

    p_ref[0] = jnp.dot(h_scr[...], w_ref[...], preferred_element_type=F32).astype(BF16)


def _inproj(x, mod3, norm_w, w_main, w_small, tm, tn):
    b, s, d = x.shape
    n_main = w_main.shape[1]
    return pl.pallas_call(
        _inproj_kernel,
        grid=(b, s // tm, n_main // tn),
        in_specs=[pl.BlockSpec((1, tm, d), lambda bi, i, j: (bi, i, 0)),
                  pl.BlockSpec((1, 3, d), lambda bi, i, j: (bi, 0, 0)),
                  pl.BlockSpec((1, d), lambda bi, i, j: (0, 0)),
                  pl.BlockSpec((d, tn), lambda bi, i, j: (0, j)),
                  pl.BlockSpec((d, LANES), lambda bi, i, j: (0, 0))],
        out_specs=[pl.BlockSpec((1, tm, tn), lambda bi, i, j: (bi, i, j)),
                   pl.BlockSpec((1, tm, LANES), lambda bi, i, j: (bi, i, 0))],
        out_shape=[jax.ShapeDtypeStruct((b, s, n_main), BF16),
                   jax.ShapeDtypeStruct((b, s, LANES), F32)],
        scratch_shapes=[pltpu.VMEM((tm, d), BF16)],
        compiler_params=pltpu.CompilerParams(
            dimension_semantics=("parallel", "parallel", "arbitrary"), vmem_limit_bytes=VMEM_LIMIT),
        name="norm_inproj",
    )(x, mod3, norm_w.reshape(1, d), w_main, w_small)


def _level_mask(level):
    i = lax.broadcasted_iota(jnp.int32, (CHUNK, CHUNK), 0)
    j = lax.broadcasted_iota(jnp.int32, (CHUNK, CHUNK), 1)
    same_block = (i >> (level + 1)) == (j >> (level + 1))
    return same_block & (((i >> level) & 1) == 1) & (((j >> level) & 1) == 0)


def _unit_lower_inverse(lmat):
    i = lax.broadcasted_iota(jnp.int32, (CHUNK, CHUNK), 0)
    j = lax.broadcasted_iota(jnp.int32, (CHUNK, CHUNK), 1)
    eye = (i == j).astype(F32)
    x = eye - jnp.where(_level_mask(0), lmat, 0.0)
    for level in range(1, CHUNK.bit_length() - 1):
        c = jnp.where(_level_mask(level), lmat, 0.0)
        x = x - _mm(x, _mm(c, x))
    return x


def _deltanet_kernel(q_ref, k_ref, v_ref, small_ref, cw_ref, alog_ref, dtb_ref, o_ref,
                     xbuf, act, state, *, ts):
    t = pl.program_id(1)
    width = DN_HEADS * DN_DK

    @pl.when(t == 0)
    def _():
        xbuf[0:HALO, :] = jnp.zeros((HALO, 3 * width), F32)
        state[...] = jnp.zeros_like(state)

    @pl.when(t > 0)
    def _():
        xbuf[0:HALO, :] = xbuf[ts:ts + HALO, :]

    xbuf[HALO:HALO + ts, 0:width] = q_ref[0].astype(F32)
    xbuf[HALO:HALO + ts, width:2 * width] = k_ref[0].astype(F32)
    xbuf[HALO:HALO + ts, 2 * width:3 * width] = v_ref[0].astype(F32)

    for seg in range(3):
        for h in range(DN_HEADS):
            cols = slice(seg * width + h * DN_DK, seg * width + (h + 1) * DN_DK)
            acc = None
            for j in range(CONV_K):
                start = HALO - (CONV_K - 1) + j
                term = xbuf[start:start + ts, cols] * cw_ref[j:j + 1, cols]
                acc = term if acc is None else acc + term
            y = _silu(acc)
            if seg < 2:
                y = y * lax.rsqrt(jnp.sum(y * y, axis=-1, keepdims=True) + EPS)
            if seg == 0:
                y = y * (DN_DK ** -0.5)
            act[:, cols] = y

    row = lax.broadcasted_iota(jnp.int32, (CHUNK, CHUNK), 0)
    col = lax.broadcasted_iota(jnp.int32, (CHUNK, CHUNK), 1)
    causal = row >= col
    strict = row > col
    tri = causal.astype(F32)
    neg_a = -jnp.exp(alog_ref[...])
    dtb = dtb_ref[...]

    def chunk_body(c, carry):
        r0 = pl.multiple_of(c * CHUNK, CHUNK)
        sm = small_ref[0, pl.ds(r0, CHUNK), :]
        beta_all = _sigmoid(sm)
        z = sm + dtb
        softplus = jnp.maximum(z, 0.0) + jnp.log(1.0 + jnp.exp(-jnp.abs(z)))
        g_all = neg_a * softplus
        gc_all = jnp.dot(tri, g_all, preferred_element_type=F32, precision=HIGHEST)
        gc_t = gc_all.T
        g_last = gc_all[CHUNK - 1:CHUNK, :]
        e_in_all = jnp.exp(gc_all)
        e_out_all = jnp.exp(g_last - gc_all)
        e_last_all = jnp.exp(g_last)

        for h in range(DN_HEADS):
            cq = slice(h * DN_DK, (h + 1) * DN_DK)
            q = act[pl.ds(r0, CHUNK), cq]
            k = act[pl.ds(r0, CHUNK), slice(width + h * DN_DK, width + (h + 1) * DN_DK)]
            v = act[pl.ds(r0, CHUNK), slice(2 * width + h * DN_DV, 2 * width + (h + 1) * DN_DV)]
            beta = beta_all[:, h:h + 1]
            gl = DN_HEADS + h
            gc_col = gc_all[:, gl:gl + 1]
            gc_row = gc_t[gl:gl + 1, :]
            e_in = e_in_all[:, gl:gl + 1]
            e_out = e_out_all[:, gl:gl + 1]
            e_last = e_last_all[:, gl:gl + 1]

            decay = jnp.exp(jnp.where(causal, gc_col - gc_row, -jnp.inf))
            k_beta = k * beta
            v_beta = v * beta
            kk_qk = _mm_nt(jnp.concatenate([k_beta, q], axis=0), k)
            lmat = jnp.where(strict, kk_qk[0:CHUNK] * decay, 0.0)
            a_qk = kk_qk[CHUNK:2 * CHUNK] * decay
            t_inv = _unit_lower_inverse(lmat)
            uw = _mm(t_inv, jnp.concatenate([v_beta, k_beta * e_in], axis=1))
            u = uw[:, 0:DN_DV]
            w = uw[:, DN_DV:DN_DV + DN_DK]
            q_dec = q * e_in
            k_dec = k * e_out

            s_h = state[h]
            r = _mm(jnp.concatenate([w, q_dec], axis=0), s_h)
            v_new = u - r[0:CHUNK]
            o = r[CHUNK:2 * CHUNK] + _mm(a_qk, v_new)
            state[h] = s_h * e_last + _mm_tn(k_dec, v_new)
            o_ref[0, pl.ds(r0, CHUNK), cq] = o.astype(o_ref.dtype)
        return carry

    lax.fori_loop(0, ts // CHUNK, chunk_body, 0)


def _deltanet(p, small, conv_w, alog_row, dtb_row, ts):
    b, s, _ = p.shape
    width = DN_HEADS * DN_DK
    kern = functools.partial(_deltanet_kernel, ts=ts)
    return pl.pallas_call(
        kern,
        grid=(b, s // ts),
        in_specs=[pl.BlockSpec((1, ts, width), lambda bi, t: (bi, t, 0)),
                  pl.BlockSpec((1, ts, width), lambda bi, t: (bi, t, 1)),
                  pl.BlockSpec((1, ts, width), lambda bi, t: (bi, t, 2)),
                  pl.BlockSpec((1, ts, LANES), lambda bi, t: (bi, t, 0)),
                  pl.BlockSpec((CONV_K, 3 * width), lambda bi, t: (0, 0)),
                  pl.BlockSpec((1, LANES), lambda bi, t: (0, 0)),
                  pl.BlockSpec((1, LANES), lambda bi, t: (0, 0))],
        out_specs=pl.BlockSpec((1, ts, DN_WIDTH), lambda bi, t: (bi, t, 0)),
        out_shape=jax.ShapeDtypeStruct((b, s, DN_WIDTH), BF16),
        scratch_shapes=[pltpu.VMEM((HALO + ts, 3 * width), F32),
                        pltpu.VMEM((ts, 3 * width), F32),
                        pltpu.VMEM((DN_HEADS, DN_DK, DN_DV), F32)],
        compiler_params=pltpu.CompilerParams(
            dimension_semantics=("parallel", "arbitrary"), vmem_limit_bytes=VMEM_LIMIT),
        name="gated_deltanet",
    )(p, p, p, small, conv_w, alog_row, dtb_row)


def _attn_kernel(q_ref, k_ref, v_ref, o_ref, lse_ref, qs, ks, vs, os_, ls, *, dil, n_back):
    s = q_ref.shape[1]
    sub_len = s // dil
    nb = sub_len // ATTN_BLOCK
    n_pairs = ATTN_GROUP_WIDTH // LANES

    for src, dst in ((q_ref, qs), (k_ref, ks), (v_ref, vs)):
        for pair in range(n_pairs):
            os_[pair] = src[0, :, pair * LANES:(pair + 1) * LANES].astype(F32)
            for r in range(dil):
                dst[pair, r * sub_len:(r + 1) * sub_len, :] = os_[pair, pl.ds(r, sub_len, stride=dil), :]

    qi = lax.broadcasted_iota(jnp.int32, (ATTN_BLOCK, 2 * ATTN_BLOCK), 0)
    ki = lax.broadcasted_iota(jnp.int32, (ATTN_BLOCK, 2 * ATTN_BLOCK), 1)
    dist = qi + ATTN_BLOCK - ki
    band = (dist >= 0) & (dist <= n_back)
    lane = lax.broadcasted_iota(jnp.int32, (ATTN_BLOCK, LANES), 1)
    low_half = lane < ATTN_HEAD_DIM
    scale = ATTN_HEAD_DIM ** -0.5

    def block_body(f, carry):
        i = f % nb
        q0 = pl.multiple_of(f * ATTN_BLOCK, ATTN_BLOCK)
        k0 = pl.multiple_of(jnp.maximum(f - 1, 0) * ATTN_BLOCK, ATTN_BLOCK)
        mask = band & ((ki >= ATTN_BLOCK) | (i > 0))
        for pair in range(n_pairs):
            qp = qs[pair, pl.ds(q0, ATTN_BLOCK), :]
            kp = jnp.concatenate([ks[pair, pl.ds(k0, ATTN_BLOCK), :], ks[pair, pl.ds(q0, ATTN_BLOCK), :]], axis=0)
            vp = jnp.concatenate([vs[pair, pl.ds(k0, ATTN_BLOCK), :], vs[pair, pl.ds(q0, ATTN_BLOCK), :]], axis=0)
            outs, lses = [], []
            for half in range(2):
                sel = low_half if half == 0 else ~low_half
                qm = jnp.where(sel, qp, 0.0)
                sc = _mm_nt(qm, kp) * scale
                sc = jnp.where(mask, sc, -jnp.inf)
                m = jnp.max(sc, axis=-1, keepdims=True)
                p = jnp.exp(sc - m)
                denom = jnp.sum(p, axis=-1, keepdims=True)
                outs.append(_mm(p / denom, vp))
                lses.append(m + jnp.log(denom))
            os_[pair, pl.ds(q0, ATTN_BLOCK), :] = jnp.where(low_half, outs[0], outs[1])
            ls[pair, pl.ds(q0, ATTN_BLOCK), :] = jnp.where(low_half, lses[0], lses[1])
        return carry

    lax.fori_loop(0, dil * nb, block_body, 0)

    for pair in range(n_pairs):
        for r in range(dil):
            qs[pair, pl.ds(r, sub_len, stride=dil), :] = os_[pair, r * sub_len:(r + 1) * sub_len, :]
            ks[pair, pl.ds(r, sub_len, stride=dil), :] = ls[pair, r * sub_len:(r + 1) * sub_len, :]
        o_ref[0, :, pair * LANES:(pair + 1) * LANES] = qs[pair].astype(o_ref.dtype)
        lse_ref[0, :, pair * LANES:(pair + 1) * LANES] = ks[pair]


def _dilated_attention(p, group, col_base):
    b, s, _ = p.shape
    window, dil = DIL_CONFIGS[group]
    gw = ATTN_GROUP_WIDTH
    blk = col_base // gw + group
    step = ATTN_WIDTH // gw
    kern = functools.partial(_attn_kernel, dil=dil, n_back=window // dil)
    return pl.pallas_call(
        kern,
        grid=(b,),
        in_specs=[pl.BlockSpec((1, s, gw), lambda bi: (bi, 0, blk)),
                  pl.BlockSpec((1, s, gw), lambda bi: (bi, 0, blk + step)),
                  pl.BlockSpec((1, s, gw), lambda bi: (bi, 0, blk + 2 * step))],
        out_specs=[pl.BlockSpec((1, s, gw), lambda bi: (bi, 0, 0)),
                   pl.BlockSpec((1, s, gw), lambda bi: (bi, 0, 0))],
        out_shape=[jax.ShapeDtypeStruct((b, s, gw), BF16),
                   jax.ShapeDtypeStruct((b, s, gw), F32)],
        scratch_shapes=[pltpu.VMEM((gw // LANES, s, LANES), F32) for _ in range(5)],
        compiler_params=pltpu.CompilerParams(
            dimension_semantics=("parallel",), vmem_limit_bytes=VMEM_LIMIT),
        name=f"dilated_attn_g{group}",
    )(p, p, p)


def _merge_kernel(x_ref, mod_ref, oa_ref, za_ref, ga_ref, gb_ref, zb_ref,
                  ob0_ref, ob1_ref, ob2_ref, l0_ref, l1_ref, l2_ref,
                  dnw_ref, pa_ref, pb_ref, wo_ref, fw_ref, out_ref):
    oa = oa_ref[0].astype(F32)
    za = za_ref[0].astype(F32)
    parts = []
    for h in range(DN_HEADS):
        cq = slice(h * DN_DV, (h + 1) * DN_DV)
        blk = oa[:, cq]
        y = blk * lax.rsqrt(jnp.mean(blk * blk, axis=-1, keepdims=True) + EPS) * dnw_ref[...]
        parts.append(y * _silu(za[:, cq]))
    y_a = _mm(jnp.concatenate(parts, axis=1), pa_ref[...])

    l0, l1, l2 = l0_ref[0], l1_ref[0], l2_ref[0]
    m = jnp.maximum(jnp.maximum(l0, l1), l2)
    e0, e1, e2 = jnp.exp(l0 - m), jnp.exp(l1 - m), jnp.exp(l2 - m)
    den = e0 + e1 + e2
    ob = jnp.concatenate([ob0_ref[0].astype(F32) * (e0 / den),
                          ob1_ref[0].astype(F32) * (e1 / den),
                          ob2_ref[0].astype(F32) * (e2 / den)], axis=1)
    y_b = _mm(ob * _silu(zb_ref[0].astype(F32)), pb_ref[...])

    merged = _sigmoid(ga_ref[0].astype(F32)) * y_a + _sigmoid(gb_ref[0].astype(F32)) * y_b
    gate = mod_ref[0][2:3]
    xo = x_ref[0] + gate * _mm(merged, wo_ref[...])
    out_ref[0] = xo * lax.rsqrt(jnp.mean(xo * xo, axis=-1, keepdims=True) + EPS) * fw_ref[...]


def _merge(x, mod3, o_a, p, obs, lses, dn_norm_w, pa, pb, wo, final_w, tm, cols):
    b, s, d = x.shape
    gw = ATTN_GROUP_WIDTH

    def row_spec(width, blk):
        return pl.BlockSpec((1, tm, width), lambda bi, i: (bi, i, blk))

    def full_spec(shape):
        return pl.BlockSpec(shape, lambda bi, i: (0,) * len(shape))

    return pl.pallas_call(
        _merge_kernel,
        grid=(b, s // tm),
        in_specs=[row_spec(d, 0),
                  pl.BlockSpec((1, 3, d), lambda bi, i: (bi, 0, 0)),
                  row_spec(DN_WIDTH, 0),
                  row_spec(DN_WIDTH, cols["za"] // DN_WIDTH),
                  row_spec(d, cols["ga"] // d),
                  row_spec(d, cols["gb"] // d),
                  row_spec(ATTN_WIDTH, cols["zb"] // ATTN_WIDTH),
                  row_spec(gw, 0), row_spec(gw, 0), row_spec(gw, 0),
                  row_spec(gw, 0), row_spec(gw, 0), row_spec(gw, 0),
                  full_spec((1, DN_DV)), full_spec(pa.shape), full_spec(pb.shape), full_spec(wo.shape),
                  full_spec((1, d))],
        out_specs=row_spec(d, 0),
        out_shape=jax.ShapeDtypeStruct((b, s, d), F32),
        compiler_params=pltpu.CompilerParams(
            dimension_semantics=("parallel", "parallel"), vmem_limit_bytes=VMEM_LIMIT),
        name="merge_outproj",
    )(x, mod3, o_a, p, p, p, p, *obs, *lses, dn_norm_w.reshape(1, DN_DV), pa, pb, wo, final_w.reshape(1, d))


def _layer(x, c, norm_w, ada_w, ada_b, w_in, conv_w, a_log, dt_bias, dn_norm_w, w_proj_a, w_proj_b, w_out,
           final_norm_w):
    b, s, d = x.shape
    qk = DN_HEADS * DN_DK
    splits = (qk, qk, DN_WIDTH, DN_WIDTH, DN_HEADS, DN_HEADS, ATTN_WIDTH, ATTN_WIDTH, ATTN_WIDTH, ATTN_WIDTH, d, d)
    names = ("qa", "ka", "va", "za", "beta", "a", "qb", "kb", "vb", "zb", "ga", "gb")
    offs = dict(zip(names, np.cumsum((0,) + splits[:-1]).tolist()))
    widths = dict(zip(names, splits))
    order = ("qa", "ka", "va", "za", "ga", "gb", "qb", "kb", "vb", "zb")
    cols, pos = {}, 0
    for n in order:
        assert pos % widths[n] == 0, (n, pos)
        cols[n] = pos
        pos += widths[n]
    w_main = jnp.concatenate([w_in[:, offs[n]:offs[n] + widths[n]] for n in order], axis=1).astype(BF16)
    w_small = jnp.concatenate([w_in[:, offs["beta"]:offs["beta"] + 2 * DN_HEADS],
                               jnp.zeros((d, LANES - 2 * DN_HEADS), w_in.dtype)], axis=1).astype(BF16)
    pad_lo = jnp.zeros((DN_HEADS,), F32)
    pad_hi = jnp.zeros((LANES - 2 * DN_HEADS,), F32)
    alog_row = jnp.concatenate([pad_lo, a_log.astype(F32), pad_hi]).reshape(1, LANES)
    dtb_row = jnp.concatenate([pad_lo, dt_bias.astype(F32), pad_hi]).reshape(1, LANES)

    mod3 = _adaln_mod(c, ada_w, ada_b).reshape(b, 3, d)
    tn = next(t for t in (1024, 512, 256) if pos % t == 0)
    p, small = _inproj(x, mod3, norm_w, w_main, w_small, tm=min(s, 1024), tn=tn)
    o_a = _deltanet(p, small, conv_w, alog_row, dtb_row, ts=min(s, 256))
    obs, lses = [], []
    for g in range(len(DIL_CONFIGS)):
        o_g, lse_g = _dilated_attention(p, g, cols["qb"])
        obs.append(o_g)
        lses.append(lse_g)
    return _merge(x, mod3, o_a, p, obs, lses, dn_norm_w, w_proj_a.astype(BF16), w_proj_b.astype(BF16),
                  w_out.astype(BF16), final_norm_w, tm=min(s, 256), cols=cols)


def kernel(x, c, norm_w, ada_w, ada_b, w_in, conv_w, a_log, dt_bias, dn_norm_w, w_proj_a, w_proj_b, w_out,
           final_norm_w):
    depth = norm_w.shape[0]
    assert depth == 1, "the final RMSNorm is fused into the single layer's output kernel"
    return _layer(x, c, norm_w[0], ada_w[0], ada_b[0], w_in[0], conv_w[0], a_log[0], dt_bias[0], dn_norm_w[0],
                  w_proj_a[0], w_proj_b[0], w_out[0], final_norm_w)
```

```python
import functools

import jax
import jax.numpy as jnp
import numpy as np
from jax import lax
from jax.experimental import pallas as pl
from jax.experimental.pallas import tpu as pltpu

F32 = jnp.float32
BF16 = jnp.bfloat16
HIGHEST = lax.Precision.HIGHEST

EPS = 1e-6
DN_HEADS = 8
DN_DK = 128
DN_DV = 128
DN_WIDTH = DN_HEADS * DN_DV
CONV_K = 4
CHUNK = 64
DIL_CONFIGS = ((128, 1), (512, 4), (2048, 16))
HEADS_PER_GROUP = 4
ATTN_HEAD_DIM = 64
ATTN_GROUP_WIDTH = HEADS_PER_GROUP * ATTN_HEAD_DIM
ATTN_WIDTH = ATTN_GROUP_WIDTH * len(DIL_CONFIGS)
ATTN_BLOCK = 128
LANES = 128
HALO = 8
VMEM_LIMIT = 56 * 1024 * 1024


def _mm(a, b):
    return jnp.dot(a.astype(BF16), b.astype(BF16), preferred_element_type=F32)


def _mm_nt(a, b):
    return lax.dot_general(a.astype(BF16), b.astype(BF16), (((1,), (1,)), ((), ())),
                           preferred_element_type=F32)


def _mm_tn(a, b):
    return lax.dot_general(a.astype(BF16), b.astype(BF16), (((0,), (0,)), ((), ())),
                           preferred_element_type=F32)


def _sigmoid(x):
    return 1.0 / (1.0 + jnp.exp(-x))


def _silu(x):
    return x * _sigmoid(x)


def _mod_kernel(c_ref, w_ref, b_ref, o_ref):
    sc = _silu(c_ref[...])
    o_ref[...] = jnp.dot(sc, w_ref[...], preferred_element_type=F32, precision=HIGHEST) + b_ref[...]


def _adaln_mod(c, ada_w, ada_b):
    b, d = c.shape
    return pl.pallas_call(
        _mod_kernel,
        grid=(3,),
        in_specs=[pl.BlockSpec((b, d), lambda j: (0, 0)),
                  pl.BlockSpec((d, d), lambda j: (0, j)),
                  pl.BlockSpec((1, d), lambda j: (0, j))],
        out_specs=pl.BlockSpec((b, d), lambda j: (0, j)),
        out_shape=jax.ShapeDtypeStruct((b, 3 * d), F32),
        name="adaln_mod",
    )(c, ada_w, ada_b.reshape(1, 3 * d))


def _inproj_kernel(x_ref, mod_ref, nw_ref, w_ref, ws_ref, p_ref, small_ref, h_scr):
    @pl.when(pl.program_id(2) == 0)
    def _():
        x = x_ref[0]
        m = mod_ref[0]
        y = x * lax.rsqrt(jnp.mean(x * x, axis=-1, keepdims=True) + EPS) * nw_ref[...]
        h = (y * (1.0 + m[1:2]) + m[0:1]).astype(BF16)
        h_scr[...] = h
        small_ref[0] = jnp.dot(h, ws_ref[...], preferred_element_type=F32)

    p_ref[0] = jnp.dot(h_scr[...], w_ref[...], preferred_element_type=F32).astype(BF16)


def _inproj(x, mod3, norm_w, w_main, w_small, tm, tn):
    b, s, d = x.shape
    n_main = w_main.shape[1]
    return pl.pallas_call(
        _inproj_kernel,
        grid=(b, s // tm, n_main // tn),
        in_specs=[pl.BlockSpec((1, tm, d), lambda bi, i, j: (bi, i, 0)),
                  pl.BlockSpec((1, 3, d), lambda bi, i, j: (bi, 0, 0)),
                  pl.BlockSpec((1, d), lambda bi, i, j: (0, 0)),
                  pl.BlockSpec((d, tn), lambda bi, i, j: (0, j)),
                  pl.BlockSpec((d, LANES), lambda bi, i, j: (0, 0))],
        out_specs=[pl.BlockSpec((1, tm, tn), lambda bi, i, j: (bi, i, j)),
                   pl.BlockSpec((1, tm, LANES), lambda bi, i, j: (bi, i, 0))],
        out_shape=[jax.ShapeDtypeStruct((b, s, n_main), BF16),
                   jax.ShapeDtypeStruct((b, s, LANES), F32)],
        scratch_shapes=[pltpu.VMEM((tm, d), BF16)],
        compiler_params=pltpu.CompilerParams(
            dimension_semantics=("parallel", "parallel", "arbitrary"), vmem_limit_bytes=VMEM_LIMIT),
        name="norm_inproj",
    )(x, mod3, norm_w.reshape(1, d), w_main, w_small)


def _level_mask(level):
    i = lax.broadcasted_iota(jnp.int32, (CHUNK, CHUNK), 0)
    j = lax.broadcasted_iota(jnp.int32, (CHUNK, CHUNK), 1)
    same_block = (i >> (level + 1)) == (j >> (level + 1))
    return same_block & (((i >> level) & 1) == 1) & (((j >> level) & 1) == 0)


N_LEVELS = CHUNK.bit_length() - 1


def _deltanet_kernel(q_ref, k_ref, v_ref, small_ref, cw_ref, alog_ref, dtb_ref, o_ref,
                     xbuf, act, state, u_s, wq_s, akd_s, el_s, *, ts, group):
    t = pl.program_id(1)
    width = DN_HEADS * DN_DK

    @pl.when(t == 0)
    def _():
        xbuf[0:HALO, :] = jnp.zeros((HALO, 3 * width), F32)
        state[...] = jnp.zeros_like(state)

    @pl.when(t > 0)
    def _():
        xbuf[0:HALO, :] = xbuf[ts:ts + HALO, :]

    xbuf[HALO:HALO + ts, 0:width] = q_ref[0].astype(F32)
    xbuf[HALO:HALO + ts, width:2 * width] = k_ref[0].astype(F32)
    xbuf[HALO:HALO + ts, 2 * width:3 * width] = v_ref[0].astype(F32)

    for seg in range(3):
        for h in range(DN_HEADS):
            cols = slice(seg * width + h * DN_DK, seg * width + (h + 1) * DN_DK)
            acc = None
            for j in range(CONV_K):
                start = HALO - (CONV_K - 1) + j
                term = xbuf[start:start + ts, cols] * cw_ref[j:j + 1, cols]
                acc = term if acc is None else acc + term
            y = _silu(acc)
            if seg < 2:
                y = y * lax.rsqrt(jnp.sum(y * y, axis=-1, keepdims=True) + EPS)
            if seg == 0:
                y = y * (DN_DK ** -0.5)
            act[:, cols] = y

    row = lax.broadcasted_iota(jnp.int32, (CHUNK, CHUNK), 0)
    col = lax.broadcasted_iota(jnp.int32, (CHUNK, CHUNK), 1)
    causal = row >= col
    tri = causal.astype(F32)
    tri_strict = (row > col).astype(F32)
    neg_a = -jnp.exp(alog_ref[...])
    dtb = dtb_ref[...]

    eye = (row == col).astype(F32)
    level_masks = [_level_mask(level).astype(F32) for level in range(N_LEVELS)]
    n_chunks = ts // CHUNK

    def prep_body(gi, carry):
        chains = []
        for ci in range(group):
            c = gi * group + ci
            r0 = pl.multiple_of(c * CHUNK, CHUNK)
            sm = small_ref[0, pl.ds(r0, CHUNK), :]
            beta_all = _sigmoid(sm)
            z = sm + dtb
            softplus = jnp.maximum(z, 0.0) + jnp.log(1.0 + jnp.exp(-jnp.abs(z)))
            g_all = neg_a * softplus
            gc_all = jnp.dot(tri, g_all, preferred_element_type=F32, precision=HIGHEST)
            gc_t = gc_all.T
            g_last = gc_all[CHUNK - 1:CHUNK, :]
            e_in_all = jnp.exp(gc_all)
            e_out_all = jnp.exp(g_last - gc_all)
            el_s[c] = jnp.broadcast_to(jnp.exp(g_last), (8, LANES))
            for h in range(DN_HEADS):
                gl = DN_HEADS + h
                q = act[pl.ds(r0, CHUNK), slice(h * DN_DK, (h + 1) * DN_DK)]
                k = act[pl.ds(r0, CHUNK), slice(width + h * DN_DK, width + (h + 1) * DN_DK)]
                v = act[pl.ds(r0, CHUNK), slice(2 * width + h * DN_DV, 2 * width + (h + 1) * DN_DV)]
                beta = beta_all[:, h:h + 1]
                e_in = e_in_all[:, gl:gl + 1]
                k_beta = k * beta
                decay = jnp.exp(jnp.where(causal, gc_all[:, gl:gl + 1] - gc_t[gl:gl + 1, :], -jnp.inf))
                chains.append(dict(
                    c=c, h=h, r0=r0, decay=decay,
                    kq=jnp.concatenate([k_beta, q], axis=0).astype(BF16), k=k.astype(BF16),
                    rhs=jnp.concatenate([v * beta, k_beta * e_in], axis=1).astype(BF16),
                    q_dec=(q * e_in).astype(BF16),
                    k_dec_t=(k * e_out_all[:, gl:gl + 1]).T.astype(BF16)))

        for ch in chains:
            ch["kk_qk"] = _mm_nt(ch["kq"], ch["k"])
        for ch in chains:
            kk_qk = ch.pop("kk_qk")
            decay = ch.pop("decay")
            ch["lmat"] = kk_qk[0:CHUNK] * decay * tri_strict
            ch["a_qk"] = (kk_qk[CHUNK:2 * CHUNK] * decay).astype(BF16)
            ch["x"] = eye - ch["lmat"] * level_masks[0]
        for level in range(1, N_LEVELS):
            for ch in chains:
                ch["y"] = _mm(ch["lmat"] * level_masks[level], ch["x"])
            for ch in chains:
                ch["x"] = ch["x"] - _mm(ch["x"], ch.pop("y"))
        for ch in chains:
            ch["uw"] = _mm(ch.pop("x"), ch.pop("rhs"))
        for ch in chains:
            c, h, r0 = ch["c"], ch["h"], ch["r0"]
            uw = ch.pop("uw")
            u_s[pl.ds(r0, CHUNK), slice(h * DN_DV, (h + 1) * DN_DV)] = uw[:, 0:DN_DV]
            wq_s[c, h] = jnp.concatenate([uw[:, DN_DV:DN_DV + DN_DK].astype(BF16), ch["q_dec"]], axis=0)
            akd_s[c, h] = jnp.concatenate([ch["a_qk"], ch["k_dec_t"]], axis=0)
        return carry

    lax.fori_loop(0, n_chunks // group, prep_body, 0)

    def rec_body(c, carry):
        r0 = pl.multiple_of(c * CHUNK, CHUNK)
        e_last_all = el_s[c]
        rs = [_mm(wq_s[c, h], state[h]) for h in range(DN_HEADS)]
        v_new = [u_s[pl.ds(r0, CHUNK), slice(h * DN_DV, (h + 1) * DN_DV)] - rs[h][0:CHUNK]
                 for h in range(DN_HEADS)]
        av = [_mm(akd_s[c, h], v_new[h]) for h in range(DN_HEADS)]
        for h in range(DN_HEADS):
            gl = DN_HEADS + h
            o = rs[h][CHUNK:2 * CHUNK] + av[h][0:CHUNK]
            state[h] = state[h] * e_last_all[0:1, gl:gl + 1] + av[h][CHUNK:CHUNK + DN_DK]
            o_ref[0, pl.ds(r0, CHUNK), slice(h * DN_DV, (h + 1) * DN_DV)] = o.astype(o_ref.dtype)
        return carry

    lax.fori_loop(0, n_chunks, rec_body, 0)


def _deltanet(p, small, conv_w, alog_row, dtb_row, ts, group):
    b, s, _ = p.shape
    width = DN_HEADS * DN_DK
    n_chunks = ts // CHUNK
    kern = functools.partial(_deltanet_kernel, ts=ts, group=group)
    return pl.pallas_call(
        kern,
        grid=(b, s // ts),
        in_specs=[pl.BlockSpec((1, ts, width), lambda bi, t: (bi, t, 0)),
                  pl.BlockSpec((1, ts, width), lambda bi, t: (bi, t, 1)),
                  pl.BlockSpec((1, ts, width), lambda bi, t: (bi, t, 2)),
                  pl.BlockSpec((1, ts, LANES), lambda bi, t: (bi, t, 0)),
                  pl.BlockSpec((CONV_K, 3 * width), lambda bi, t: (0, 0)),
                  pl.BlockSpec((1, LANES), lambda bi, t: (0, 0)),
                  pl.BlockSpec((1, LANES), lambda bi, t: (0, 0))],
        out_specs=pl.BlockSpec((1, ts, DN_WIDTH), lambda bi, t: (bi, t, 0)),
        out_shape=jax.ShapeDtypeStruct((b, s, DN_WIDTH), BF16),
        scratch_shapes=[pltpu.VMEM((HALO + ts, 3 * width), F32),
                        pltpu.VMEM((ts, 3 * width), F32),
                        pltpu.VMEM((DN_HEADS, DN_DK, DN_DV), F32),
                        pltpu.VMEM((ts, DN_WIDTH), F32),
                        pltpu.VMEM((n_chunks, DN_HEADS, 2 * CHUNK, DN_DK), BF16),
                        pltpu.VMEM((n_chunks, DN_HEADS, CHUNK + DN_DK, CHUNK), BF16),
                        pltpu.VMEM((n_chunks, 8, LANES), F32)],
        compiler_params=pltpu.CompilerParams(
            dimension_semantics=("parallel", "arbitrary"), vmem_limit_bytes=VMEM_LIMIT),
        name="gated_deltanet",
    )(p, p, p, small, conv_w, alog_row, dtb_row)


def _attn_kernel(q_ref, k_ref, v_ref, o_ref, lse_ref, qs, ks, vs, os_, ls, *, dil, n_back):
    s = q_ref.shape[1]
    sub_len = s // dil
    nb = sub_len // ATTN_BLOCK
    n_pairs = ATTN_GROUP_WIDTH // LANES

    for src, dst in ((q_ref, qs), (k_ref, ks), (v_ref, vs)):
        for pair in range(n_pairs):
            os_[pair] = src[0, :, pair * LANES:(pair + 1) * LANES].astype(F32)
            for r in range(dil):
                dst[pair, r * sub_len:(r + 1) * sub_len, :] = os_[pair, pl.ds(r, sub_len, stride=dil), :]

    qi = lax.broadcasted_iota(jnp.int32, (ATTN_BLOCK, 2 * ATTN_BLOCK), 0)
    ki = lax.broadcasted_iota(jnp.int32, (ATTN_BLOCK, 2 * ATTN_BLOCK), 1)
    dist = qi + ATTN_BLOCK - ki
    band = (dist >= 0) & (dist <= n_back)
    lane = lax.broadcasted_iota(jnp.int32, (ATTN_BLOCK, LANES), 1)
    low_half = lane < ATTN_HEAD_DIM
    scale = ATTN_HEAD_DIM ** -0.5

    def block_body(f, carry):
        i = f % nb
        q0 = pl.multiple_of(f * ATTN_BLOCK, ATTN_BLOCK)
        k0 = pl.multiple_of(jnp.maximum(f - 1, 0) * ATTN_BLOCK, ATTN_BLOCK)
        mask = band & ((ki >= ATTN_BLOCK) | (i > 0))
        for pair in range(n_pairs):
            qp = qs[pair, pl.ds(q0, ATTN_BLOCK), :]
            kp = jnp.concatenate([ks[pair, pl.ds(k0, ATTN_BLOCK), :], ks[pair, pl.ds(q0, ATTN_BLOCK), :]], axis=0)
            vp = jnp.concatenate([vs[pair, pl.ds(k0, ATTN_BLOCK), :], vs[pair, pl.ds(q0, ATTN_BLOCK), :]], axis=0)
            outs, lses = [], []
            for half in range(2):
                sel = low_half if half == 0 else ~low_half
                qm = jnp.where(sel, qp, 0.0)
                sc = _mm_nt(qm, kp) * scale
                sc = jnp.where(mask, sc, -jnp.inf)
                m = jnp.max(sc, axis=-1, keepdims=True)
                p = jnp.exp(sc - m)
                denom = jnp.sum(p, axis=-1, keepdims=True)
                outs.append(_mm(p / denom, vp))
                lses.append(m + jnp.log(denom))
            os_[pair, pl.ds(q0, ATTN_BLOCK), :] = jnp.where(low_half, outs[0], outs[1])
            ls[pair, pl.ds(q0, ATTN_BLOCK), :] = jnp.where(low_half, lses[0], lses[1])
        return carry

    lax.fori_loop(0, dil * nb, block_body, 0)

    for pair in range(n_pairs):
        for r in range(dil):
            qs[pair, pl.ds(r, sub_len, stride=dil), :] = os_[pair, r * sub_len:(r + 1) * sub_len, :]
            ks[pair, pl.ds(r, sub_len, stride=dil), :] = ls[pair, r * sub_len:(r + 1) * sub_len, :]
        o_ref[0, :, pair * LANES:(pair + 1) * LANES] = qs[pair].astype(o_ref.dtype)
        lse_ref[0, :, pair * LANES:(pair + 1) * LANES] = ks[pair]


def _dilated_attention(p, group, col_base):
    b, s, _ = p.shape
    window, dil = DIL_CONFIGS[group]
    gw = ATTN_GROUP_WIDTH
    blk = col_base // gw + group
    step = ATTN_WIDTH // gw
    kern = functools.partial(_attn_kernel, dil=dil, n_back=window // dil)
    return pl.pallas_call(
        kern,
        grid=(b,),
        in_specs=[pl.BlockSpec((1, s, gw), lambda bi: (bi, 0, blk)),
                  pl.BlockSpec((1, s, gw), lambda bi: (bi, 0, blk + step)),
                  pl.BlockSpec((1, s, gw), lambda bi: (bi, 0, blk + 2 * step))],
        out_specs=[pl.BlockSpec((1, s, gw), lambda bi: (bi, 0, 0)),
                   pl.BlockSpec((1, s, gw), lambda bi: (bi, 0, 0))],
        out_shape=[jax.ShapeDtypeStruct((b, s, gw), BF16),
                   jax.ShapeDtypeStruct((b, s, gw), F32)],
        scratch_shapes=[pltpu.VMEM((gw // LANES, s, LANES), F32) for _ in range(5)],
        compiler_params=pltpu.CompilerParams(
            dimension_semantics=("parallel",), vmem_limit_bytes=VMEM_LIMIT),
        name=f"dilated_attn_g{group}",
    )(p, p, p)


def _merge_kernel(x_ref, mod_ref, oa_ref, za_ref, ga_ref, gb_ref, zb_ref,
                  ob0_ref, ob1_ref, ob2_ref, l0_ref, l1_ref, l2_ref,
                  dnw_ref, pa_ref, pb_ref, wo_ref, fw_ref, out_ref):
    oa = oa_ref[0].astype(F32)
    za = za_ref[0].astype(F32)
    parts = []
    for h in range(DN_HEADS):
        cq = slice(h * DN_DV, (h + 1) * DN_DV)
        blk = oa[:, cq]
        y = blk * lax.rsqrt(jnp.mean(blk * blk, axis=-1, keepdims=True) + EPS) * dnw_ref[...]
        parts.append(y * _silu(za[:, cq]))
    y_a = _mm(jnp.concatenate(parts, axis=1), pa_ref[...])

    l0, l1, l2 = l0_ref[0], l1_ref[0], l2_ref[0]
    m = jnp.maximum(jnp.maximum(l0, l1), l2)
    e0, e1, e2 = jnp.exp(l0 - m), jnp.exp(l1 - m), jnp.exp(l2 - m)
    den = e0 + e1 + e2
    ob = jnp.concatenate([ob0_ref[0].astype(F32) * (e0 / den),
                          ob1_ref[0].astype(F32) * (e1 / den),
                          ob2_ref[0].astype(F32) * (e2 / den)], axis=1)
    y_b = _mm(ob * _silu(zb_ref[0].astype(F32)), pb_ref[...])

    merged = _sigmoid(ga_ref[0].astype(F32)) * y_a + _sigmoid(gb_ref[0].astype(F32)) * y_b
    gate = mod_ref[0][2:3]
    xo = x_ref[0] + gate * _mm(merged, wo_ref[...])
    out_ref[0] = xo * lax.rsqrt(jnp.mean(xo * xo, axis=-1, keepdims=True) + EPS) * fw_ref[...]


def _merge(x, mod3, o_a, p, obs, lses, dn_norm_w, pa, pb, wo, final_w, tm, cols):
    b, s, d = x.shape
    gw = ATTN_GROUP_WIDTH

    def row_spec(width, blk):
        return pl.BlockSpec((1, tm, width), lambda bi, i: (bi, i, blk))

    def full_spec(shape):
        return pl.BlockSpec(shape, lambda bi, i: (0,) * len(shape))

    return pl.pallas_call(
        _merge_kernel,
        grid=(b, s // tm),
        in_specs=[row_spec(d, 0),
                  pl.BlockSpec((1, 3, d), lambda bi, i: (bi, 0, 0)),
                  row_spec(DN_WIDTH, 0),
                  row_spec(DN_WIDTH, cols["za"] // DN_WIDTH),
                  row_spec(d, cols["ga"] // d),
                  row_spec(d, cols["gb"] // d),
                  row_spec(ATTN_WIDTH, cols["zb"] // ATTN_WIDTH),
                  row_spec(gw, 0), row_spec(gw, 0), row_spec(gw, 0),
                  row_spec(gw, 0), row_spec(gw, 0), row_spec(gw, 0),
                  full_spec((1, DN_DV)), full_spec(pa.shape), full_spec(pb.shape), full_spec(wo.shape),
                  full_spec((1, d))],
        out_specs=row_spec(d, 0),
        out_shape=jax.ShapeDtypeStruct((b, s, d), F32),
        compiler_params=pltpu.CompilerParams(
            dimension_semantics=("parallel", "parallel"), vmem_limit_bytes=VMEM_LIMIT),
        name="merge_outproj",
    )(x, mod3, o_a, p, p, p, p, *obs, *lses, dn_norm_w.reshape(1, DN_DV), pa, pb, wo, final_w.reshape(1, d))


def _layer(x, c, norm_w, ada_w, ada_b, w_in, conv_w, a_log, dt_bias, dn_norm_w, w_proj_a, w_proj_b, w_out,
           final_norm_w):
    b, s, d = x.shape
    qk = DN_HEADS * DN_DK
    splits = (qk, qk, DN_WIDTH, DN_WIDTH, DN_HEADS, DN_HEADS, ATTN_WIDTH, ATTN_WIDTH, ATTN_WIDTH, ATTN_WIDTH, d, d)
    names = ("qa", "ka", "va", "za", "beta", "a", "qb", "kb", "vb", "zb", "ga", "gb")
    offs = dict(zip(names, np.cumsum((0,) + splits[:-1]).tolist()))
    widths = dict(zip(names, splits))
    order = ("qa", "ka", "va", "za", "ga", "gb", "qb", "kb", "vb", "zb")
    cols, pos = {}, 0
    for n in order:
        assert pos % widths[n] == 0, (n, pos)
        cols[n] = pos
        pos += widths[n]
    w_main = jnp.concatenate([w_in[:, offs[n]:offs[n] + widths[n]] for n in order], axis=1).astype(BF16)
    w_small = jnp.concatenate([w_in[:, offs["beta"]:offs["beta"] + 2 * DN_HEADS],
                               jnp.zeros((d, LANES - 2 * DN_HEADS), w_in.dtype)], axis=1).astype(BF16)
    pad_lo = jnp.zeros((DN_HEADS,), F32)
    pad_hi = jnp.zeros((LANES - 2 * DN_HEADS,), F32)
    alog_row = jnp.concatenate([pad_lo, a_log.astype(F32), pad_hi]).reshape(1, LANES)
    dtb_row = jnp.concatenate([pad_lo, dt_bias.astype(F32), pad_hi]).reshape(1, LANES)

    mod3 = _adaln_mod(c, ada_w, ada_b).reshape(b, 3, d)
    tn = next(t for t in (1024, 512, 256) if pos % t == 0)
    p, small = _inproj(x, mod3, norm_w, w_main, w_small, tm=min(s, 1024), tn=tn)
    o_a = _deltanet(p, small, conv_w, alog_row, dtb_row, ts=min(s, 256), group=2)
    obs, lses = [], []
    for g in range(len(DIL_CONFIGS)):
        o_g, lse_g = _dilated_attention(p, g, cols["qb"])
        obs.append(o_g)
        lses.append(lse_g)
    return _merge(x, mod3, o_a, p, obs, lses, dn_norm_w, w_proj_a.astype(BF16), w_proj_b.astype(BF16),
                  w_out.astype(BF16), final_norm_w, tm=min(s, 256), cols=cols)


def kernel(x, c, norm_w, ada_w, ada_b, w_in, conv_w, a_log, dt_bias, dn_norm_w, w_proj_a, w_proj_b, w_out,
           final_norm_w):
    depth = norm_w.shape[0]
    assert depth == 1, "the final RMSNorm is fused into the single layer's output kernel"
    return _layer(x, c, norm_w[0], ada_w[0], ada_b[0], w_in[0], conv_w[0], a_log[0], dt_bias[0], dn_norm_w[0],
                  w_proj_a[0], w_proj_b[0], w_out[0], final_norm_w)
```

```python
import functools

import jax
import jax.numpy as jnp
import numpy as np
from jax import lax
from jax.experimental import pallas as pl
from jax.experimental.pallas import tpu as pltpu

F32 = jnp.float32
BF16 = jnp.bfloat16
HIGHEST = lax.Precision.HIGHEST

EPS = 1e-6
DN_HEADS = 8
DN_DK = 128
DN_DV = 128
DN_WIDTH = DN_HEADS * DN_DV
CONV_K = 4
CHUNK = 64
DIL_CONFIGS = ((128, 1), (512, 4), (2048, 16))
HEADS_PER_GROUP = 4
ATTN_HEAD_DIM = 64
ATTN_GROUP_WIDTH = HEADS_PER_GROUP * ATTN_HEAD_DIM
ATTN_WIDTH = ATTN_GROUP_WIDTH * len(DIL_CONFIGS)
ATTN_BLOCK = 128
LANES = 128
HALO = 8
VMEM_LIMIT = 56 * 1024 * 1024


def _mm(a, b):
    return jnp.dot(a.astype(BF16), b.astype(BF16), preferred_element_type=F32)


def _mm_nt(a, b):
    return lax.dot_general(a.astype(BF16), b.astype(BF16), (((1,), (1,)), ((), ())),
                           preferred_element_type=F32)


def _mm_tn(a, b):
    return lax.dot_general(a.astype(BF16), b.astype(BF16), (((0,), (0,)), ((), ())),
                           preferred_element_type=F32)


def _sigmoid(x):
    return 1.0 / (1.0 + jnp.exp(-x))


def _silu(x):
    return x * _sigmoid(x)


def _mod_kernel(c_ref, w_ref, b_ref, o_ref):
    sc = _silu(c_ref[...])
    o_ref[...] = jnp.dot(sc, w_ref[...], preferred_element_type=F32, precision=HIGHEST) + b_ref[...]


def _adaln_mod(c, ada_w, ada_b):
    b, d = c.shape
    return pl.pallas_call(
        _mod_kernel,
        grid=(3,),
        in_specs=[pl.BlockSpec((b, d), lambda j: (0, 0)),
                  pl.BlockSpec((d, d), lambda j: (0, j)),
                  pl.BlockSpec((1, d), lambda j: (0, j))],
        out_specs=pl.BlockSpec((b, d), lambda j: (0, j)),
        out_shape=jax.ShapeDtypeStruct((b, 3 * d), F32),
        name="adaln_mod",
    )(c, ada_w, ada_b.reshape(1, 3 * d))


def _inproj_kernel(x_ref, mod_ref, nw_ref, w_ref, ws_ref, p_ref, small_ref, h_scr):
    @pl.when(pl.program_id(2) == 0)
    def _():
        x = x_ref[0]
        m = mod_ref[0]
        y = x * lax.rsqrt(jnp.mean(x * x, axis=-1, keepdims=True) + EPS) * nw_ref[...]
        h = (y * (1.0 + m[1:2]) + m[0:1]).astype(BF16)
        h_scr[...] = h
        small_ref[0] = jnp.dot(h, ws_ref[...], preferred_element_type=F32)

    p_ref[0] = jnp.dot(h_scr[...], w_ref[...], preferred_element_type=F32).astype(BF16)


def _inproj(x, mod3, norm_w, w_main, w_small, tm, tn):
    b, s, d = x.shape
    n_main = w_main.shape[1]
    return pl.pallas_call(
        _inproj_kernel,
        grid=(b, s // tm, n_main // tn),
        in_specs=[pl.BlockSpec((1, tm, d), lambda bi, i, j: (bi, i, 0)),
                  pl.BlockSpec((1, 3, d), lambda bi, i, j: (bi, 0, 0)),
                  pl.BlockSpec((1, d), lambda bi, i, j: (0, 0)),
                  pl.BlockSpec((d, tn), lambda bi, i, j: (0, j)),
                  pl.BlockSpec((d, LANES), lambda bi, i, j: (0, 0))],
        out_specs=[pl.BlockSpec((1, tm, tn), lambda bi, i, j: (bi, i, j)),
                   pl.BlockSpec((1, tm, LANES), lambda bi, i, j: (bi, i, 0))],
        out_shape=[jax.ShapeDtypeStruct((b, s, n_main), BF16),
                   jax.ShapeDtypeStruct((b, s, LANES), F32)],
        scratch_shapes=[pltpu.VMEM((tm, d), BF16)],
        compiler_params=pltpu.CompilerParams(
            dimension_semantics=("parallel", "parallel", "arbitrary"), vmem_limit_bytes=VMEM_LIMIT),
        name="norm_inproj",
    )(x, mod3, norm_w.reshape(1, d), w_main, w_small)


def _level_mask(level):
    i = lax.broadcasted_iota(jnp.int32, (CHUNK, CHUNK), 0)
    j = lax.broadcasted_iota(jnp.int32, (CHUNK, CHUNK), 1)
    same_block = (i >> (level + 1)) == (j >> (level + 1))
    return same_block & (((i >> level) & 1) == 1) & (((j >> level) & 1) == 0)


N_LEVELS = CHUNK.bit_length() - 1


def _deltanet_kernel(q_ref, k_ref, v_ref, small_ref, cw_ref, alog_ref, dtb_ref, o_ref,
                     xbuf, act, state, u_s, wq_s, akd_s, el_s, *, ts, group):
    t = pl.program_id(1)
    width = DN_HEADS * DN_DK

    @pl.when(t == 0)
    def _():
        xbuf[0:HALO, :] = jnp.zeros((HALO, 3 * width), F32)
        state[...] = jnp.zeros_like(state)

    @pl.when(t > 0)
    def _():
        xbuf[0:HALO, :] = xbuf[ts:ts + HALO, :]

    xbuf[HALO:HALO + ts, 0:width] = q_ref[0].astype(F32)
    xbuf[HALO:HALO + ts, width:2 * width] = k_ref[0].astype(F32)
    xbuf[HALO:HALO + ts, 2 * width:3 * width] = v_ref[0].astype(F32)

    for seg in range(3):
        for h in range(DN_HEADS):
            cols = slice(seg * width + h * DN_DK, seg * width + (h + 1) * DN_DK)
            acc = None
            for j in range(CONV_K):
                start = HALO - (CONV_K - 1) + j
                term = xbuf[start:start + ts, cols] * cw_ref[j:j + 1, cols]
                acc = term if acc is None else acc + term
            y = _silu(acc)
            if seg < 2:
                y = y * lax.rsqrt(jnp.sum(y * y, axis=-1, keepdims=True) + EPS)
            if seg == 0:
                y = y * (DN_DK ** -0.5)
            act[:, cols] = y

    row = lax.broadcasted_iota(jnp.int32, (CHUNK, CHUNK), 0)
    col = lax.broadcasted_iota(jnp.int32, (CHUNK, CHUNK), 1)
    causal = row >= col
    tri = causal.astype(F32)
    tri_strict = (row > col).astype(F32)
    neg_a = -jnp.exp(alog_ref[...])
    dtb = dtb_ref[...]

    eye = (row == col).astype(F32)
    level_masks = [_level_mask(level).astype(F32) for level in range(N_LEVELS)]
    n_chunks = ts // CHUNK

    def prep_body(gi, carry):
        chains = []
        for ci in range(group):
            c = gi * group + ci
            r0 = pl.multiple_of(c * CHUNK, CHUNK)
            sm = small_ref[0, pl.ds(r0, CHUNK), :]
            beta_all = _sigmoid(sm)
            z = sm + dtb
            softplus = jnp.maximum(z, 0.0) + jnp.log(1.0 + jnp.exp(-jnp.abs(z)))
            g_all = neg_a * softplus
            gc_all = jnp.dot(tri, g_all, preferred_element_type=F32, precision=HIGHEST)
            gc_t = gc_all.T
            g_last = gc_all[CHUNK - 1:CHUNK, :]
            e_in_all = jnp.exp(gc_all)
            e_out_all = jnp.exp(g_last - gc_all)
            el_s[c] = jnp.broadcast_to(jnp.exp(g_last), (8, LANES))
            for h in range(DN_HEADS):
                gl = DN_HEADS + h
                q = act[pl.ds(r0, CHUNK), slice(h * DN_DK, (h + 1) * DN_DK)]
                k = act[pl.ds(r0, CHUNK), slice(width + h * DN_DK, width + (h + 1) * DN_DK)]
                v = act[pl.ds(r0, CHUNK), slice(2 * width + h * DN_DV, 2 * width + (h + 1) * DN_DV)]
                beta = beta_all[:, h:h + 1]
                e_in = e_in_all[:, gl:gl + 1]
                k_beta = k * beta
                decay = jnp.exp(jnp.where(causal, gc_all[:, gl:gl + 1] - gc_t[gl:gl + 1, :], -jnp.inf))
                chains.append(dict(
                    c=c, h=h, r0=r0, decay=decay,
                    kq=jnp.concatenate([k_beta, q], axis=0).astype(BF16), k=k.astype(BF16),
                    rhs=jnp.concatenate([v * beta, k_beta * e_in], axis=1).astype(BF16),
                    q_dec=(q * e_in).astype(BF16),
                    k_dec_t=(k * e_out_all[:, gl:gl + 1]).T.astype(BF16)))

        for ch in chains:
            ch["kk_qk"] = _mm_nt(ch["kq"], ch["k"])
        for ch in chains:
            kk_qk = ch.pop("kk_qk")
            decay = ch.pop("decay")
            ch["lmat"] = kk_qk[0:CHUNK] * decay * tri_strict
            ch["a_qk"] = (kk_qk[CHUNK:2 * CHUNK] * decay).astype(BF16)
            ch["x"] = eye - ch["lmat"] * level_masks[0]
        for level in range(1, N_LEVELS):
            for ch in chains:
                ch["y"] = _mm(ch["lmat"] * level_masks[level], ch["x"])
            for ch in chains:
                ch["x"] = ch["x"] - _mm(ch["x"], ch.pop("y"))
        for ch in chains:
            ch["uw"] = _mm(ch.pop("x"), ch.pop("rhs"))
        for ch in chains:
            c, h, r0 = ch["c"], ch["h"], ch["r0"]
            uw = ch.pop("uw")
            u_s[pl.ds(r0, CHUNK), slice(h * DN_DV, (h + 1) * DN_DV)] = uw[:, 0:DN_DV]
            wq_s[c, h] = jnp.concatenate([uw[:, DN_DV:DN_DV + DN_DK].astype(BF16), ch["q_dec"]], axis=0)
            akd_s[c, h] = jnp.concatenate([ch["a_qk"], ch["k_dec_t"]], axis=0)
        return carry

    lax.fori_loop(0, n_chunks // group, prep_body, 0)

    def rec_body(c, carry):
        r0 = pl.multiple_of(c * CHUNK, CHUNK)
        e_last_all = el_s[c]
        rs = [_mm(wq_s[c, h], state[h]) for h in range(DN_HEADS)]
        v_new = [u_s[pl.ds(r0, CHUNK), slice(h * DN_DV, (h + 1) * DN_DV)] - rs[h][0:CHUNK]
                 for h in range(DN_HEADS)]
        av = [_mm(akd_s[c, h], v_new[h]) for h in range(DN_HEADS)]
        for h in range(DN_HEADS):
            gl = DN_HEADS + h
            o = rs[h][CHUNK:2 * CHUNK] + av[h][0:CHUNK]
            state[h] = state[h] * e_last_all[0:1, gl:gl + 1] + av[h][CHUNK:CHUNK + DN_DK]
            o_ref[0, pl.ds(r0, CHUNK), slice(h * DN_DV, (h + 1) * DN_DV)] = o.astype(o_ref.dtype)
        return carry

    lax.fori_loop(0, n_chunks, rec_body, 0)


def _deltanet(p, small, conv_w, alog_row, dtb_row, ts, group):
    b, s, _ = p.shape
    width = DN_HEADS * DN_DK
    n_chunks = ts // CHUNK
    kern = functools.partial(_deltanet_kernel, ts=ts, group=group)
    return pl.pallas_call(
        kern,
        grid=(b, s // ts),
        in_specs=[pl.BlockSpec((1, ts, width), lambda bi, t: (bi, t, 0)),
                  pl.BlockSpec((1, ts, width), lambda bi, t: (bi, t, 1)),
                  pl.BlockSpec((1, ts, width), lambda bi, t: (bi, t, 2)),
                  pl.BlockSpec((1, ts, LANES), lambda bi, t: (bi, t, 0)),
                  pl.BlockSpec((CONV_K, 3 * width), lambda bi, t: (0, 0)),
                  pl.BlockSpec((1, LANES), lambda bi, t: (0, 0)),
                  pl.BlockSpec((1, LANES), lambda bi, t: (0, 0))],
        out_specs=pl.BlockSpec((1, ts, DN_WIDTH), lambda bi, t: (bi, t, 0)),
        out_shape=jax.ShapeDtypeStruct((b, s, DN_WIDTH), BF16),
        scratch_shapes=[pltpu.VMEM((HALO + ts, 3 * width), F32),
                        pltpu.VMEM((ts, 3 * width), F32),
                        pltpu.VMEM((DN_HEADS, DN_DK, DN_DV), F32),
                        pltpu.VMEM((ts, DN_WIDTH), F32),
                        pltpu.VMEM((n_chunks, DN_HEADS, 2 * CHUNK, DN_DK), BF16),
                        pltpu.VMEM((n_chunks, DN_HEADS, CHUNK + DN_DK, CHUNK), BF16),
                        pltpu.VMEM((n_chunks, 8, LANES), F32)],
        compiler_params=pltpu.CompilerParams(
            dimension_semantics=("parallel", "arbitrary"), vmem_limit_bytes=VMEM_LIMIT),
        name="gated_deltanet",
    )(p, p, p, small, conv_w, alog_row, dtb_row)


ATTN_BLOCKS_PER_ITER = 2


def _attn_kernel(q_ref, k_ref, v_ref, o_ref, lse_ref, *scratch, dil, n_back):
    s = q_ref.shape[1]
    sub_len = s // dil
    nb = sub_len // ATTN_BLOCK
    n_pairs = ATTN_GROUP_WIDTH // LANES
    prev_block = nb > 1

    def lanes(pair):
        return slice(pair * LANES, (pair + 1) * LANES)

    if dil > 1:
        stage, qs, ks, vs, os_, ls = scratch
        for src, dst in ((q_ref, qs), (k_ref, ks), (v_ref, vs)):
            for pair in range(n_pairs):
                stage[...] = src[0, :, lanes(pair)].astype(F32)
                for r in range(dil):
                    dst[pair, r * sub_len:(r + 1) * sub_len, :] = (
                        stage[pl.ds(r, sub_len, stride=dil), :].astype(BF16))

        def load(ref, src, pair, r0):
            return src[pair, pl.ds(r0, ATTN_BLOCK), :]
    else:
        qs, ks, vs = q_ref, k_ref, v_ref

        def load(ref, src, pair, r0):
            return ref[0, pl.ds(r0, ATTN_BLOCK), lanes(pair)]

    n_keys = 2 * ATTN_BLOCK if prev_block else ATTN_BLOCK
    qi = lax.broadcasted_iota(jnp.int32, (ATTN_BLOCK, n_keys), 0)
    ki = lax.broadcasted_iota(jnp.int32, (ATTN_BLOCK, n_keys), 1)
    dist = qi + (n_keys - ATTN_BLOCK) - ki
    band = (dist >= 0) & (dist <= n_back)
    bias_full = jnp.where(band, 0.0, -jnp.inf).astype(F32)
    bias_first = jnp.where(band & (ki >= ATTN_BLOCK), 0.0, -jnp.inf).astype(F32)
    lane = lax.broadcasted_iota(jnp.int32, (ATTN_BLOCK, LANES), 1)
    low_half = lane < ATTN_HEAD_DIM
    scale = ATTN_HEAD_DIM ** -0.5

    def block_body(fi, carry):
        chains = []
        for bi in range(ATTN_BLOCKS_PER_ITER):
            f = fi * ATTN_BLOCKS_PER_ITER + bi
            q0 = pl.multiple_of(f * ATTN_BLOCK, ATTN_BLOCK)
            if prev_block:
                k0 = pl.multiple_of(jnp.maximum(f - 1, 0) * ATTN_BLOCK, ATTN_BLOCK)
                bias = jnp.where(f % nb > 0, bias_full, bias_first)
            else:
                bias = bias_full
            for pair in range(n_pairs):
                qp = load(q_ref, qs, pair, q0) * scale
                kp = load(k_ref, ks, pair, q0)
                vp = load(v_ref, vs, pair, q0)
                if prev_block:
                    kp = jnp.concatenate([load(k_ref, ks, pair, k0), kp], axis=0)
                    vp = jnp.concatenate([load(v_ref, vs, pair, k0), vp], axis=0)
                for half in range(2):
                    sel = low_half if half == 0 else ~low_half
                    chains.append(dict(q0=q0, pair=pair, half=half, bias=bias, kp=kp, vp=vp,
                                       qm=jnp.where(sel, qp, jnp.zeros_like(qp))))
        for ch in chains:
            ch["sc"] = _mm_nt(ch.pop("qm"), ch.pop("kp"))
        for ch in chains:
            sc = ch.pop("sc") + ch.pop("bias")
            m = jnp.max(sc, axis=-1, keepdims=True)
            p = jnp.exp(sc - m)
            ch["denom"] = jnp.sum(p, axis=-1, keepdims=True)
            ch["m"] = m
            ch["p"] = p.astype(BF16)
        for ch in chains:
            ch["pv"] = _mm(ch.pop("p"), ch.pop("vp"))
        for c0 in range(0, len(chains), 2):
            lo, hi = chains[c0], chains[c0 + 1]
            o = jnp.where(low_half, lo["pv"] / lo["denom"], hi["pv"] / hi["denom"])
            lse = jnp.where(low_half, lo["m"] + jnp.log(lo["denom"]), hi["m"] + jnp.log(hi["denom"]))
            q0, pair = lo["q0"], lo["pair"]
            if dil > 1:
                os_[pair, pl.ds(q0, ATTN_BLOCK), :] = o
                ls[pair, pl.ds(q0, ATTN_BLOCK), :] = lse
            else:
                o_ref[0, pl.ds(q0, ATTN_BLOCK), lanes(pair)] = o.astype(o_ref.dtype)
                lse_ref[0, pl.ds(q0, ATTN_BLOCK), lanes(pair)] = lse
        return carry

    lax.fori_loop(0, dil * nb // ATTN_BLOCKS_PER_ITER, block_body, 0)

    if dil > 1:
        for pair in range(n_pairs):
            for src, dst in ((os_, o_ref), (ls, lse_ref)):
                for r in range(dil):
                    stage[pl.ds(r, sub_len, stride=dil), :] = src[pair, r * sub_len:(r + 1) * sub_len, :]
                dst[0, :, lanes(pair)] = stage[...].astype(dst.dtype)


def _dilated_attention(p, group, col_base):
    b, s, _ = p.shape
    window, dil = DIL_CONFIGS[group]
    gw = ATTN_GROUP_WIDTH
    blk = col_base // gw + group
    step = ATTN_WIDTH // gw
    kern = functools.partial(_attn_kernel, dil=dil, n_back=window // dil)
    return pl.pallas_call(
        kern,
        grid=(b,),
        in_specs=[pl.BlockSpec((1, s, gw), lambda bi: (bi, 0, blk)),
                  pl.BlockSpec((1, s, gw), lambda bi: (bi, 0, blk + step)),
                  pl.BlockSpec((1, s, gw), lambda bi: (bi, 0, blk + 2 * step))],
        out_specs=[pl.BlockSpec((1, s, gw), lambda bi: (bi, 0, 0)),
                   pl.BlockSpec((1, s, gw), lambda bi: (bi, 0, 0))],
        out_shape=[jax.ShapeDtypeStruct((b, s, gw), BF16),
                   jax.ShapeDtypeStruct((b, s, gw), F32)],
        scratch_shapes=([] if dil == 1 else
                        [pltpu.VMEM((s, LANES), F32)]
                        + [pltpu.VMEM((gw // LANES, s, LANES), BF16) for _ in range(3)]
                        + [pltpu.VMEM((gw // LANES, s, LANES), F32) for _ in range(2)]),
        compiler_params=pltpu.CompilerParams(
            dimension_semantics=("parallel",), vmem_limit_bytes=VMEM_LIMIT),
        name=f"dilated_attn_g{group}",
    )(p, p, p)


def _merge_kernel(x_ref, mod_ref, oa_ref, za_ref, ga_ref, gb_ref, zb_ref,
                  ob0_ref, ob1_ref, ob2_ref, l0_ref, l1_ref, l2_ref,
                  dnw_ref, pa_ref, pb_ref, wo_ref, fw_ref, out_ref):
    oa = oa_ref[0].astype(F32)
    za = za_ref[0].astype(F32)
    parts = []
    for h in range(DN_HEADS):
        cq = slice(h * DN_DV, (h + 1) * DN_DV)
        blk = oa[:, cq]
        y = blk * lax.rsqrt(jnp.mean(blk * blk, axis=-1, keepdims=True) + EPS) * dnw_ref[...]
        parts.append(y * _silu(za[:, cq]))
    y_a = _mm(jnp.concatenate(parts, axis=1), pa_ref[...])

    l0, l1, l2 = l0_ref[0], l1_ref[0], l2_ref[0]
    m = jnp.maximum(jnp.maximum(l0, l1), l2)
    e0, e1, e2 = jnp.exp(l0 - m), jnp.exp(l1 - m), jnp.exp(l2 - m)
    den = e0 + e1 + e2
    ob = jnp.concatenate([ob0_ref[0].astype(F32) * (e0 / den),
                          ob1_ref[0].astype(F32) * (e1 / den),
                          ob2_ref[0].astype(F32) * (e2 / den)], axis=1)
    y_b = _mm(ob * _silu(zb_ref[0].astype(F32)), pb_ref[...])

    merged = _sigmoid(ga_ref[0].astype(F32)) * y_a + _sigmoid(gb_ref[0].astype(F32)) * y_b
    gate = mod_ref[0][2:3]
    xo = x_ref[0] + gate * _mm(merged, wo_ref[...])
    out_ref[0] = xo * lax.rsqrt(jnp.mean(xo * xo, axis=-1, keepdims=True) + EPS) * fw_ref[...]


def _merge(x, mod3, o_a, p, obs, lses, dn_norm_w, pa, pb, wo, final_w, tm, cols):
    b, s, d = x.shape
    gw = ATTN_GROUP_WIDTH

    def row_spec(width, blk):
        return pl.BlockSpec((1, tm, width), lambda bi, i: (bi, i, blk))

    def full_spec(shape):
        return pl.BlockSpec(shape, lambda bi, i: (0,) * len(shape))

    return pl.pallas_call(
        _merge_kernel,
        grid=(b, s // tm),
        in_specs=[row_spec(d, 0),
                  pl.BlockSpec((1, 3, d), lambda bi, i: (bi, 0, 0)),
                  row_spec(DN_WIDTH, 0),
                  row_spec(DN_WIDTH, cols["za"] // DN_WIDTH),
                  row_spec(d, cols["ga"] // d),
                  row_spec(d, cols["gb"] // d),
                  row_spec(ATTN_WIDTH, cols["zb"] // ATTN_WIDTH),
                  row_spec(gw, 0), row_spec(gw, 0), row_spec(gw, 0),
                  row_spec(gw, 0), row_spec(gw, 0), row_spec(gw, 0),
                  full_spec((1, DN_DV)), full_spec(pa.shape), full_spec(pb.shape), full_spec(wo.shape),
                  full_spec((1, d))],
        out_specs=row_spec(d, 0),
        out_shape=jax.ShapeDtypeStruct((b, s, d), F32),
        compiler_params=pltpu.CompilerParams(
            dimension_semantics=("parallel", "parallel"), vmem_limit_bytes=VMEM_LIMIT),
        name="merge_outproj",
    )(x, mod3, o_a, p, p, p, p, *obs, *lses, dn_norm_w.reshape(1, DN_DV), pa, pb, wo, final_w.reshape(1, d))


def _layer(x, c, norm_w, ada_w, ada_b, w_in, conv_w, a_log, dt_bias, dn_norm_w, w_proj_a, w_proj_b, w_out,
           final_norm_w):
    b, s, d = x.shape
    qk = DN_HEADS * DN_DK
    splits = (qk, qk, DN_WIDTH, DN_WIDTH, DN_HEADS, DN_HEADS, ATTN_WIDTH, ATTN_WIDTH, ATTN_WIDTH, ATTN_WIDTH, d, d)
    names = ("qa", "ka", "va", "za", "beta", "a", "qb", "kb", "vb", "zb", "ga", "gb")
    offs = dict(zip(names, np.cumsum((0,) + splits[:-1]).tolist()))
    widths = dict(zip(names, splits))
    order = ("qa", "ka", "va", "za", "ga", "gb", "qb", "kb", "vb", "zb")
    cols, pos = {}, 0
    for n in order:
        assert pos % widths[n] == 0, (n, pos)
        cols[n] = pos
        pos += widths[n]
    w_main = jnp.concatenate([w_in[:, offs[n]:offs[n] + widths[n]] for n in order], axis=1).astype(BF16)
    w_small = jnp.concatenate([w_in[:, offs["beta"]:offs["beta"] + 2 * DN_HEADS],
                               jnp.zeros((d, LANES - 2 * DN_HEADS), w_in.dtype)], axis=1).astype(BF16)
    pad_lo = jnp.zeros((DN_HEADS,), F32)
    pad_hi = jnp.zeros((LANES - 2 * DN_HEADS,), F32)
    alog_row = jnp.concatenate([pad_lo, a_log.astype(F32), pad_hi]).reshape(1, LANES)
    dtb_row = jnp.concatenate([pad_lo, dt_bias.astype(F32), pad_hi]).reshape(1, LANES)

    mod3 = _adaln_mod(c, ada_w, ada_b).reshape(b, 3, d)
    tn = next(t for t in (1024, 512, 256) if pos % t == 0)
    p, small = _inproj(x, mod3, norm_w, w_main, w_small, tm=min(s, 1024), tn=tn)
    o_a = _deltanet(p, small, conv_w, alog_row, dtb_row, ts=min(s, 256), group=2)
    obs, lses = [], []
    for g in range(len(DIL_CONFIGS)):
        o_g, lse_g = _dilated_attention(p, g, cols["qb"])
        obs.append(o_g)
        lses.append(lse_g)
    return _merge(x, mod3, o_a, p, obs, lses, dn_norm_w, w_proj_a.astype(BF16), w_proj_b.astype(BF16),
                  w_out.astype(BF16), final_norm_w, tm=min(s, 256), cols=cols)


def kernel(x, c, norm_w, ada_w, ada_b, w_in, conv_w, a_log, dt_bias, dn_norm_w, w_proj_a, w_proj_b, w_out,
           final_norm_w):
    depth = norm_w.shape[0]
    assert depth == 1, "the final RMSNorm is fused into the single layer's output kernel"
    return _layer(x, c, norm_w[0], ada_w[0], ada_b[0], w_in[0], conv_w[0], a_log[0], dt_bias[0], dn_norm_w[0],
                  w_proj_a[0], w_proj_b[0], w_out[0], final_norm_w)
```

```python
import functools

import jax
import jax.numpy as jnp
import numpy as np
from jax import lax
from jax.experimental import pallas as pl
from jax.experimental.pallas import tpu as pltpu

F32 = jnp.float32
BF16 = jnp.bfloat16
HIGHEST = lax.Precision.HIGHEST

EPS = 1e-6
DN_HEADS = 8
DN_DK = 128
DN_DV = 128
DN_WIDTH = DN_HEADS * DN_DV
CONV_K = 4
CHUNK = 64
DIL_CONFIGS = ((128, 1), (512, 4), (2048, 16))
HEADS_PER_GROUP = 4
ATTN_HEAD_DIM = 64
ATTN_GROUP_WIDTH = HEADS_PER_GROUP * ATTN_HEAD_DIM
ATTN_WIDTH = ATTN_GROUP_WIDTH * len(DIL_CONFIGS)
ATTN_BLOCK = 128
LANES = 128
HALO = 8
VMEM_LIMIT = 56 * 1024 * 1024


def _mm(a, b):
    return jnp.dot(a.astype(BF16), b.astype(BF16), preferred_element_type=F32)


def _mm_nt(a, b):
    return lax.dot_general(a.astype(BF16), b.astype(BF16), (((1,), (1,)), ((), ())),
                           preferred_element_type=F32)


def _mm_tn(a, b):
    return lax.dot_general(a.astype(BF16), b.astype(BF16), (((0,), (0,)), ((), ())),
                           preferred_element_type=F32)


def _sigmoid(x):
    return 0.5 * jnp.tanh(0.5 * x) + 0.5


def _silu(x):
    return x * _sigmoid(x)


def _mod_kernel(c_ref, w_ref, b_ref, o_ref):
    sc = _silu(c_ref[...])
    o_ref[...] = jnp.dot(sc, w_ref[...], preferred_element_type=F32, precision=HIGHEST) + b_ref[...]


def _adaln_mod(c, ada_w, ada_b):
    b, d = c.shape
    return pl.pallas_call(
        _mod_kernel,
        grid=(3,),
        in_specs=[pl.BlockSpec((b, d), lambda j: (0, 0)),
                  pl.BlockSpec((d, d), lambda j: (0, j)),
                  pl.BlockSpec((1, d), lambda j: (0, j))],
        out_specs=pl.BlockSpec((b, d), lambda j: (0, j)),
        out_shape=jax.ShapeDtypeStruct((b, 3 * d), F32),
        name="adaln_mod",
    )(c, ada_w, ada_b.reshape(1, 3 * d))


CONV_ROWS = 64
CONV_SUB = 256
SUBLANES = 8


def _conv_unit(buf, slot, r0, cl, half_taps, shift_masks, norm_scale):
    a = buf[slot, r0:r0 + HALO + CONV_ROWS, cl]
    n = CONV_ROWS // SUBLANES
    vs = [a[SUBLANES * i:SUBLANES * (i + 1)] for i in range(n + 1)]
    acc = [vs[i + 1] * half_taps[CONV_K - 1] for i in range(n)]
    for shift in range(1, CONV_K):
        rs = [pltpu.roll(v, shift, 0) for v in vs]
        tap = half_taps[CONV_K - 1 - shift]
        for i in range(n):
            acc[i] = acc[i] + jnp.where(shift_masks[shift], rs[i], rs[i + 1]) * tap
    h = jnp.concatenate(acc, axis=0)
    y = h + h * jnp.tanh(h)
    if norm_scale is not None:
        y = y * (lax.rsqrt(jnp.sum(y * y, axis=-1, keepdims=True) + EPS) * norm_scale)
    return y.astype(BF16)


def _inproj_kernel(x_ref, mod_ref, nw_ref, w_ref, ws_ref, cw_ref, p_ref, small_ref, h_scr, cbuf,
                   *, tiles_per_seg):
    j = pl.program_id(1)
    s = x_ref.shape[1]
    tn = w_ref.shape[1]

    @pl.when(j == 0)
    def _():
        x = x_ref[0]
        m = mod_ref[0]
        y = x * lax.rsqrt(jnp.mean(x * x, axis=-1, keepdims=True) + EPS) * nw_ref[...]
        h = (y * (1.0 + m[1:2]) + m[0:1]).astype(BF16)
        h_scr[...] = h
        small_ref[0] = jnp.dot(h, ws_ref[...], preferred_element_type=F32)
        cbuf[:, 0:HALO, :] = jnp.zeros((2, HALO, CONV_SUB), F32)

    def conv_tile(norm_scale):
        sub = lax.broadcasted_iota(jnp.int32, (SUBLANES, LANES), 0)
        shift_masks = [sub < shift for shift in range(CONV_K)]

        def matmul(q):
            cbuf[q % 2, HALO:HALO + s, :] = jnp.dot(
                h_scr[...], w_ref[:, q * CONV_SUB:(q + 1) * CONV_SUB], preferred_element_type=F32)

        def epilogue(q):
            for c0 in range(q * CONV_SUB, (q + 1) * CONV_SUB, DN_DK):
                half_taps = [cw_ref[t:t + 1, c0:c0 + DN_DK] * 0.5 for t in range(CONV_K)]
                cl = slice(c0 - q * CONV_SUB, c0 - q * CONV_SUB + DN_DK)
                for r0 in range(0, s, CONV_ROWS):
                    p_ref[0, r0:r0 + CONV_ROWS, c0:c0 + DN_DK] = _conv_unit(
                        cbuf, q % 2, r0, cl, half_taps, shift_masks, norm_scale)

        n_sub = tn // CONV_SUB
        matmul(0)
        for q in range(n_sub):
            if q + 1 < n_sub:
                matmul(q + 1)
            epilogue(q)

    @pl.when(j < 2 * tiles_per_seg)
    def _():
        conv_tile(jnp.where(j < tiles_per_seg, DN_DK ** -0.5, 1.0))

    @pl.when((j >= 2 * tiles_per_seg) & (j < 3 * tiles_per_seg))
    def _():
        conv_tile(None)

    @pl.when(j >= 3 * tiles_per_seg)
    def _():
        p_ref[0] = jnp.dot(h_scr[...], w_ref[...], preferred_element_type=F32).astype(BF16)


def _inproj(x, mod3, norm_w, w_main, w_small, conv_w, tn):
    b, s, d = x.shape
    n_main = w_main.shape[1]
    conv_width = conv_w.shape[1]
    tiles_per_seg = conv_width // 3 // tn
    assert tiles_per_seg * tn * 3 == conv_width and n_main % tn == 0 and s % CONV_ROWS == 0
    assert tn % CONV_SUB == 0
    n_conv_tiles = 3 * tiles_per_seg
    kern = functools.partial(_inproj_kernel, tiles_per_seg=tiles_per_seg)
    return pl.pallas_call(
        kern,
        grid=(b, n_main // tn),
        in_specs=[pl.BlockSpec((1, s, d), lambda bi, j: (bi, 0, 0)),
                  pl.BlockSpec((1, 3, d), lambda bi, j: (bi, 0, 0)),
                  pl.BlockSpec((1, d), lambda bi, j: (0, 0)),
                  pl.BlockSpec((d, tn), lambda bi, j: (0, j)),
                  pl.BlockSpec((d, LANES), lambda bi, j: (0, 0)),
                  pl.BlockSpec((CONV_K, tn), lambda bi, j: (0, jnp.minimum(j, n_conv_tiles - 1)))],
        out_specs=[pl.BlockSpec((1, s, tn), lambda bi, j: (bi, 0, j)),
                   pl.BlockSpec((1, s, LANES), lambda bi, j: (bi, 0, 0))],
        out_shape=[jax.ShapeDtypeStruct((b, s, n_main), BF16),
                   jax.ShapeDtypeStruct((b, s, LANES), F32)],
        scratch_shapes=[pltpu.VMEM((s, d), BF16),
                        pltpu.VMEM((2, HALO + s, CONV_SUB), F32)],
        compiler_params=pltpu.CompilerParams(
            dimension_semantics=("parallel", "arbitrary"), vmem_limit_bytes=VMEM_LIMIT),
        name="norm_inproj",
    )(x, mod3, norm_w.reshape(1, d), w_main, w_small, conv_w)


def _level_mask(level):
    i = lax.broadcasted_iota(jnp.int32, (CHUNK, CHUNK), 0)
    j = lax.broadcasted_iota(jnp.int32, (CHUNK, CHUNK), 1)
    same_block = (i >> (level + 1)) == (j >> (level + 1))
    return same_block & (((i >> level) & 1) == 1) & (((j >> level) & 1) == 0)


N_LEVELS = CHUNK.bit_length() - 1


def _deltanet_kernel(q_ref, k_ref, v_ref, small_ref, alog_ref, dtb_ref, o_ref,
                     state, u_s, wq_s, akd_s, el_s, *, ts, group):
    @pl.when(pl.program_id(1) == 0)
    def _():
        state[...] = jnp.zeros_like(state)

    row = lax.broadcasted_iota(jnp.int32, (CHUNK, CHUNK), 0)
    col = lax.broadcasted_iota(jnp.int32, (CHUNK, CHUNK), 1)
    causal = row >= col
    tri = causal.astype(F32)
    tri_strict = (row > col).astype(F32)
    neg_a = -jnp.exp(alog_ref[...])
    dtb = dtb_ref[...]

    eye = (row == col).astype(F32)
    level_masks = [_level_mask(level).astype(F32) for level in range(N_LEVELS)]
    n_chunks = ts // CHUNK

    def prep_body(gi, carry):
        chains = []
        for ci in range(group):
            c = gi * group + ci
            r0 = pl.multiple_of(c * CHUNK, CHUNK)
            sm = small_ref[0, pl.ds(r0, CHUNK), :]
            beta_all = _sigmoid(sm)
            z = sm + dtb
            softplus = jnp.maximum(z, 0.0) + jnp.log(1.0 + jnp.exp(-jnp.abs(z)))
            g_all = neg_a * softplus
            gc_all = jnp.dot(tri, g_all, preferred_element_type=F32, precision=HIGHEST)
            gc_t = gc_all.T
            g_last = gc_all[CHUNK - 1:CHUNK, :]
            e_in_all = jnp.exp(gc_all)
            e_out_all = jnp.exp(g_last - gc_all)
            el_s[c] = jnp.broadcast_to(jnp.exp(g_last), (8, LANES))
            for h in range(DN_HEADS):
                gl = DN_HEADS + h
                cq = slice(h * DN_DK, (h + 1) * DN_DK)
                q = q_ref[0, pl.ds(r0, CHUNK), cq].astype(F32)
                k = k_ref[0, pl.ds(r0, CHUNK), cq].astype(F32)
                v = v_ref[0, pl.ds(r0, CHUNK), slice(h * DN_DV, (h + 1) * DN_DV)].astype(F32)
                beta = beta_all[:, h:h + 1]
                e_in = e_in_all[:, gl:gl + 1]
                k_beta = k * beta
                decay = jnp.exp(jnp.where(causal, gc_all[:, gl:gl + 1] - gc_t[gl:gl + 1, :], -jnp.inf))
                chains.append(dict(
                    c=c, h=h, r0=r0, decay=decay,
                    kq=jnp.concatenate([k_beta, q], axis=0).astype(BF16), k=k.astype(BF16),
                    rhs=jnp.concatenate([v * beta, k_beta * e_in], axis=1).astype(BF16),
                    q_dec=(q * e_in).astype(BF16),
                    k_dec_t=(k * e_out_all[:, gl:gl + 1]).T.astype(BF16)))

        for ch in chains:
            ch["kk_qk"] = _mm_nt(ch["kq"], ch["k"])
        for ch in chains:
            kk_qk = ch.pop("kk_qk")
            decay = ch.pop("decay")
            ch["lmat"] = kk_qk[0:CHUNK] * decay * tri_strict
            ch["a_qk"] = (kk_qk[CHUNK:2 * CHUNK] * decay).astype(BF16)
            ch["x"] = eye - ch["lmat"] * level_masks[0]
        for level in range(1, N_LEVELS):
            for ch in chains:
                ch["y"] = _mm(ch["lmat"] * level_masks[level], ch["x"])
            for ch in chains:
                ch["x"] = ch["x"] - _mm(ch["x"], ch.pop("y"))
        for ch in chains:
            ch["uw"] = _mm(ch.pop("x"), ch.pop("rhs"))
        for ch in chains:
            c, h, r0 = ch["c"], ch["h"], ch["r0"]
            uw = ch.pop("uw")
            u_s[pl.ds(r0, CHUNK), slice(h * DN_DV, (h + 1) * DN_DV)] = uw[:, 0:DN_DV]
            wq_s[c, h] = jnp.concatenate([uw[:, DN_DV:DN_DV + DN_DK].astype(BF16), ch["q_dec"]], axis=0)
            akd_s[c, h] = jnp.concatenate([ch["a_qk"], ch["k_dec_t"]], axis=0)
        return carry

    lax.fori_loop(0, n_chunks // group, prep_body, 0)

    def rec_body(c, carry):
        r0 = pl.multiple_of(c * CHUNK, CHUNK)
        e_last_all = el_s[c]
        rs = [_mm(wq_s[c, h], state[h]) for h in range(DN_HEADS)]
        v_new = [u_s[pl.ds(r0, CHUNK), slice(h * DN_DV, (h + 1) * DN_DV)] - rs[h][0:CHUNK]
                 for h in range(DN_HEADS)]
        av = [_mm(akd_s[c, h], v_new[h]) for h in range(DN_HEADS)]
        for h in range(DN_HEADS):
            gl = DN_HEADS + h
            o = rs[h][CHUNK:2 * CHUNK] + av[h][0:CHUNK]
            state[h] = state[h] * e_last_all[0:1, gl:gl + 1] + av[h][CHUNK:CHUNK + DN_DK]
            o_ref[0, pl.ds(r0, CHUNK), slice(h * DN_DV, (h + 1) * DN_DV)] = o.astype(o_ref.dtype)
        return carry

    lax.fori_loop(0, n_chunks, rec_body, 0)


def _deltanet(p, small, alog_row, dtb_row, ts, group):
    b, s, _ = p.shape
    width = DN_HEADS * DN_DK
    n_chunks = ts // CHUNK
    kern = functools.partial(_deltanet_kernel, ts=ts, group=group)
    return pl.pallas_call(
        kern,
        grid=(b, s // ts),
        in_specs=[pl.BlockSpec((1, ts, width), lambda bi, t: (bi, t, 0)),
                  pl.BlockSpec((1, ts, width), lambda bi, t: (bi, t, 1)),
                  pl.BlockSpec((1, ts, width), lambda bi, t: (bi, t, 2)),
                  pl.BlockSpec((1, ts, LANES), lambda bi, t: (bi, t, 0)),
                  pl.BlockSpec((1, LANES), lambda bi, t: (0, 0)),
                  pl.BlockSpec((1, LANES), lambda bi, t: (0, 0))],
        out_specs=pl.BlockSpec((1, ts, DN_WIDTH), lambda bi, t: (bi, t, 0)),
        out_shape=jax.ShapeDtypeStruct((b, s, DN_WIDTH), BF16),
        scratch_shapes=[pltpu.VMEM((DN_HEADS, DN_DK, DN_DV), F32),
                        pltpu.VMEM((ts, DN_WIDTH), F32),
                        pltpu.VMEM((n_chunks, DN_HEADS, 2 * CHUNK, DN_DK), BF16),
                        pltpu.VMEM((n_chunks, DN_HEADS, CHUNK + DN_DK, CHUNK), BF16),
                        pltpu.VMEM((n_chunks, 8, LANES), F32)],
        compiler_params=pltpu.CompilerParams(
            dimension_semantics=("parallel", "arbitrary"), vmem_limit_bytes=VMEM_LIMIT),
        name="gated_deltanet",
    )(p, p, p, small, alog_row, dtb_row)


ATTN_BLOCKS_PER_ITER = 2


def _attn_kernel(q_ref, k_ref, v_ref, o_ref, lse_ref, *scratch, dil, n_back):
    s = q_ref.shape[1]
    sub_len = s // dil
    nb = sub_len // ATTN_BLOCK
    n_pairs = ATTN_GROUP_WIDTH // LANES
    prev_block = nb > 1

    def lanes(pair):
        return slice(pair * LANES, (pair + 1) * LANES)

    if dil > 1:
        stage, qs, ks, vs, os_, ls = scratch
        for src, dst in ((q_ref, qs), (k_ref, ks), (v_ref, vs)):
            for pair in range(n_pairs):
                stage[...] = src[0, :, lanes(pair)].astype(F32)
                for r in range(dil):
                    dst[pair, r * sub_len:(r + 1) * sub_len, :] = (
                        stage[pl.ds(r, sub_len, stride=dil), :].astype(BF16))

        def load(ref, src, pair, r0):
            return src[pair, pl.ds(r0, ATTN_BLOCK), :]
    else:
        qs, ks, vs = q_ref, k_ref, v_ref

        def load(ref, src, pair, r0):
            return ref[0, pl.ds(r0, ATTN_BLOCK), lanes(pair)]

    n_keys = 2 * ATTN_BLOCK if prev_block else ATTN_BLOCK
    qi = lax.broadcasted_iota(jnp.int32, (ATTN_BLOCK, n_keys), 0)
    ki = lax.broadcasted_iota(jnp.int32, (ATTN_BLOCK, n_keys), 1)
    dist = qi + (n_keys - ATTN_BLOCK) - ki
    band = (dist >= 0) & (dist <= n_back)
    bias_full = jnp.where(band, 0.0, -jnp.inf).astype(F32)
    bias_first = jnp.where(band & (ki >= ATTN_BLOCK), 0.0, -jnp.inf).astype(F32)
    lane = lax.broadcasted_iota(jnp.int32, (ATTN_BLOCK, LANES), 1)
    low_half = lane < ATTN_HEAD_DIM
    scale = ATTN_HEAD_DIM ** -0.5

    def block_body(fi, carry):
        chains = []
        for bi in range(ATTN_BLOCKS_PER_ITER):
            f = fi * ATTN_BLOCKS_PER_ITER + bi
            q0 = pl.multiple_of(f * ATTN_BLOCK, ATTN_BLOCK)
            if prev_block:
                k0 = pl.multiple_of(jnp.maximum(f - 1, 0) * ATTN_BLOCK, ATTN_BLOCK)
                bias = jnp.where(f % nb > 0, bias_full, bias_first)
            else:
                bias = bias_full
            for pair in range(n_pairs):
                qp = load(q_ref, qs, pair, q0) * scale
                kp = load(k_ref, ks, pair, q0)
                vp = load(v_ref, vs, pair, q0)
                if prev_block:
                    kp = jnp.concatenate([load(k_ref, ks, pair, k0), kp], axis=0)
                    vp = jnp.concatenate([load(v_ref, vs, pair, k0), vp], axis=0)
                for half in range(2):
                    sel = low_half if half == 0 else ~low_half
                    chains.append(dict(q0=q0, pair=pair, half=half, bias=bias, kp=kp, vp=vp,
                                       qm=jnp.where(sel, qp, jnp.zeros_like(qp))))
        for ch in chains:
            ch["sc"] = _mm_nt(ch.pop("qm"), ch.pop("kp"))
        for ch in chains:
            sc = ch.pop("sc") + ch.pop("bias")
            m = jnp.max(sc, axis=-1, keepdims=True)
            p = jnp.exp(sc - m)
            ch["denom"] = jnp.sum(p, axis=-1, keepdims=True)
            ch["m"] = m
            ch["p"] = p.astype(BF16)
        for ch in chains:
            ch["pv"] = _mm(ch.pop("p"), ch.pop("vp"))
        for c0 in range(0, len(chains), 2):
            lo, hi = chains[c0], chains[c0 + 1]
            o = jnp.where(low_half, lo["pv"] / lo["denom"], hi["pv"] / hi["denom"])
            lse = jnp.where(low_half, lo["m"] + jnp.log(lo["denom"]), hi["m"] + jnp.log(hi["denom"]))
            q0, pair = lo["q0"], lo["pair"]
            if dil > 1:
                os_[pair, pl.ds(q0, ATTN_BLOCK), :] = o
                ls[pair, pl.ds(q0, ATTN_BLOCK), :] = lse
            else:
                o_ref[0, pl.ds(q0, ATTN_BLOCK), lanes(pair)] = o.astype(o_ref.dtype)
                lse_ref[0, pl.ds(q0, ATTN_BLOCK), lanes(pair)] = lse
        return carry

    lax.fori_loop(0, dil * nb // ATTN_BLOCKS_PER_ITER, block_body, 0)

    if dil > 1:
        for pair in range(n_pairs):
            for src, dst in ((os_, o_ref), (ls, lse_ref)):
                for r in range(dil):
                    stage[pl.ds(r, sub_len, stride=dil), :] = src[pair, r * sub_len:(r + 1) * sub_len, :]
                dst[0, :, lanes(pair)] = stage[...].astype(dst.dtype)


def _dilated_attention(p, group, col_base):
    b, s, _ = p.shape
    window, dil = DIL_CONFIGS[group]
    gw = ATTN_GROUP_WIDTH
    blk = col_base // gw + group
    step = ATTN_WIDTH // gw
    kern = functools.partial(_attn_kernel, dil=dil, n_back=window // dil)
    return pl.pallas_call(
        kern,
        grid=(b,),
        in_specs=[pl.BlockSpec((1, s, gw), lambda bi: (bi, 0, blk)),
                  pl.BlockSpec((1, s, gw), lambda bi: (bi, 0, blk + step)),
                  pl.BlockSpec((1, s, gw), lambda bi: (bi, 0, blk + 2 * step))],
        out_specs=[pl.BlockSpec((1, s, gw), lambda bi: (bi, 0, 0)),
                   pl.BlockSpec((1, s, gw), lambda bi: (bi, 0, 0))],
        out_shape=[jax.ShapeDtypeStruct((b, s, gw), BF16),
                   jax.ShapeDtypeStruct((b, s, gw), F32)],
        scratch_shapes=([] if dil == 1 else
                        [pltpu.VMEM((s, LANES), F32)]
                        + [pltpu.VMEM((gw // LANES, s, LANES), BF16) for _ in range(3)]
                        + [pltpu.VMEM((gw // LANES, s, LANES), F32) for _ in range(2)]),
        compiler_params=pltpu.CompilerParams(
            dimension_semantics=("parallel",), vmem_limit_bytes=VMEM_LIMIT),
        name=f"dilated_attn_g{group}",
    )(p, p, p)


def _merge_kernel(x_ref, mod_ref, oa_ref, za_ref, ga_ref, gb_ref, zb_ref,
                  ob0_ref, ob1_ref, ob2_ref, l0_ref, l1_ref, l2_ref,
                  dnw_ref, pa_ref, pb_ref, wo_ref, fw_ref, out_ref):
    oa = oa_ref[0].astype(F32)
    za = za_ref[0].astype(F32)
    parts = []
    for h in range(DN_HEADS):
        cq = slice(h * DN_DV, (h + 1) * DN_DV)
        blk = oa[:, cq]
        y = blk * lax.rsqrt(jnp.mean(blk * blk, axis=-1, keepdims=True) + EPS) * dnw_ref[...]
        parts.append(y * _silu(za[:, cq]))
    y_a = _mm(jnp.concatenate(parts, axis=1), pa_ref[...])

    l0, l1, l2 = l0_ref[0], l1_ref[0], l2_ref[0]
    m = jnp.maximum(jnp.maximum(l0, l1), l2)
    e0, e1, e2 = jnp.exp(l0 - m), jnp.exp(l1 - m), jnp.exp(l2 - m)
    den = e0 + e1 + e2
    ob = jnp.concatenate([ob0_ref[0].astype(F32) * (e0 / den),
                          ob1_ref[0].astype(F32) * (e1 / den),
                          ob2_ref[0].astype(F32) * (e2 / den)], axis=1)
    y_b = _mm(ob * _silu(zb_ref[0].astype(F32)), pb_ref[...])

    merged = _sigmoid(ga_ref[0].astype(F32)) * y_a + _sigmoid(gb_ref[0].astype(F32)) * y_b
    gate = mod_ref[0][2:3]
    xo = x_ref[0] + gate * _mm(merged, wo_ref[...])
    out_ref[0] = xo * lax.rsqrt(jnp.mean(xo * xo, axis=-1, keepdims=True) + EPS) * fw_ref[...]


def _merge(x, mod3, o_a, p, obs, lses, dn_norm_w, pa, pb, wo, final_w, tm, cols):
    b, s, d = x.shape
    gw = ATTN_GROUP_WIDTH

    def row_spec(width, blk):
        return pl.BlockSpec((1, tm, width), lambda bi, i: (bi, i, blk))

    def full_spec(shape):
        return pl.BlockSpec(shape, lambda bi, i: (0,) * len(shape))

    return pl.pallas_call(
        _merge_kernel,
        grid=(b, s // tm),
        in_specs=[row_spec(d, 0),
                  pl.BlockSpec((1, 3, d), lambda bi, i: (bi, 0, 0)),
                  row_spec(DN_WIDTH, 0),
                  row_spec(DN_WIDTH, cols["za"] // DN_WIDTH),
                  row_spec(d, cols["ga"] // d),
                  row_spec(d, cols["gb"] // d),
                  row_spec(ATTN_WIDTH, cols["zb"] // ATTN_WIDTH),
                  row_spec(gw, 0), row_spec(gw, 0), row_spec(gw, 0),
                  row_spec(gw, 0), row_spec(gw, 0), row_spec(gw, 0),
                  full_spec((1, DN_DV)), full_spec(pa.shape), full_spec(pb.shape), full_spec(wo.shape),
                  full_spec((1, d))],
        out_specs=row_spec(d, 0),
        out_shape=jax.ShapeDtypeStruct((b, s, d), F32),
        compiler_params=pltpu.CompilerParams(
            dimension_semantics=("parallel", "parallel"), vmem_limit_bytes=VMEM_LIMIT),
        name="merge_outproj",
    )(x, mod3, o_a, p, p, p, p, *obs, *lses, dn_norm_w.reshape(1, DN_DV), pa, pb, wo, final_w.reshape(1, d))


def _layer(x, c, norm_w, ada_w, ada_b, w_in, conv_w, a_log, dt_bias, dn_norm_w, w_proj_a, w_proj_b, w_out,
           final_norm_w):
    b, s, d = x.shape
    qk = DN_HEADS * DN_DK
    splits = (qk, qk, DN_WIDTH, DN_WIDTH, DN_HEADS, DN_HEADS, ATTN_WIDTH, ATTN_WIDTH, ATTN_WIDTH, ATTN_WIDTH, d, d)
    names = ("qa", "ka", "va", "za", "beta", "a", "qb", "kb", "vb", "zb", "ga", "gb")
    offs = dict(zip(names, np.cumsum((0,) + splits[:-1]).tolist()))
    widths = dict(zip(names, splits))
    order = ("qa", "ka", "va", "za", "ga", "gb", "qb", "kb", "vb", "zb")
    cols, pos = {}, 0
    for n in order:
        assert pos % widths[n] == 0, (n, pos)
        cols[n] = pos
        pos += widths[n]
    w_main = jnp.concatenate([w_in[:, offs[n]:offs[n] + widths[n]] for n in order], axis=1).astype(BF16)
    w_small = jnp.concatenate([w_in[:, offs["beta"]:offs["beta"] + 2 * DN_HEADS],
                               jnp.zeros((d, LANES - 2 * DN_HEADS), w_in.dtype)], axis=1).astype(BF16)
    pad_lo = jnp.zeros((DN_HEADS,), F32)
    pad_hi = jnp.zeros((LANES - 2 * DN_HEADS,), F32)
    alog_row = jnp.concatenate([pad_lo, a_log.astype(F32), pad_hi]).reshape(1, LANES)
    dtb_row = jnp.concatenate([pad_lo, dt_bias.astype(F32), pad_hi]).reshape(1, LANES)

    mod3 = _adaln_mod(c, ada_w, ada_b).reshape(b, 3, d)
    p, small = _inproj(x, mod3, norm_w, w_main, w_small, conv_w, tn=1024 if pos % 1024 == 0 else 512)
    o_a = _deltanet(p, small, alog_row, dtb_row, ts=min(s, 256), group=2)
    obs, lses = [], []
    for g in range(len(DIL_CONFIGS)):
        o_g, lse_g = _dilated_attention(p, g, cols["qb"])
        obs.append(o_g)
        lses.append(lse_g)
    return _merge(x, mod3, o_a, p, obs, lses, dn_norm_w, w_proj_a.astype(BF16), w_proj_b.astype(BF16),
                  w_out.astype(BF16), final_norm_w, tm=min(s, 256), cols=cols)


def kernel(x, c, norm_w, ada_w, ada_b, w_in, conv_w, a_log, dt_bias, dn_norm_w, w_proj_a, w_proj_b, w_out,
           final_norm_w):
    depth = norm_w.shape[0]
    assert depth == 1, "the final RMSNorm is fused into the single layer's output kernel"
    return _layer(x, c, norm_w[0], ada_w[0], ada_b[0], w_in[0], conv_w[0], a_log[0], dt_bias[0], dn_norm_w[0],
                  w_proj_a[0], w_proj_b[0], w_out[0], final_norm_w)
```

```python
import functools

import jax
import jax.numpy as jnp
import numpy as np
from jax import lax
from jax.experimental import pallas as pl
from jax.experimental.pallas import tpu as pltpu

F32 = jnp.float32
BF16 = jnp.bfloat16
HIGHEST = lax.Precision.HIGHEST

EPS = 1e-6
DN_HEADS = 8
DN_DK = 128
DN_DV = 128
DN_WIDTH = DN_HEADS * DN_DV
CONV_K = 4
CHUNK = 64
DIL_CONFIGS = ((128, 1), (512, 4), (2048, 16))
HEADS_PER_GROUP = 4
ATTN_HEAD_DIM = 64
ATTN_GROUP_WIDTH = HEADS_PER_GROUP * ATTN_HEAD_DIM
ATTN_WIDTH = ATTN_GROUP_WIDTH * len(DIL_CONFIGS)
ATTN_BLOCK = 128
LANES = 128
HALO = 8
VMEM_LIMIT = 56 * 1024 * 1024


def _mm(a, b):
    return jnp.dot(a.astype(BF16), b.astype(BF16), preferred_element_type=F32)


def _mm_nt(a, b):
    return lax.dot_general(a.astype(BF16), b.astype(BF16), (((1,), (1,)), ((), ())),
                           preferred_element_type=F32)


def _mm_tn(a, b):
    return lax.dot_general(a.astype(BF16), b.astype(BF16), (((0,), (0,)), ((), ())),
                           preferred_element_type=F32)


def _sigmoid(x):
    return 0.5 * jnp.tanh(0.5 * x) + 0.5


def _silu(x):
    return x * _sigmoid(x)


def _mod_kernel(c_ref, w_ref, b_ref, o_ref):
    sc = _silu(c_ref[...])
    o_ref[...] = jnp.dot(sc, w_ref[...], preferred_element_type=F32, precision=HIGHEST) + b_ref[...]


def _adaln_mod(c, ada_w, ada_b):
    b, d = c.shape
    return pl.pallas_call(
        _mod_kernel,
        grid=(3,),
        in_specs=[pl.BlockSpec((b, d), lambda j: (0, 0)),
                  pl.BlockSpec((d, d), lambda j: (0, j)),
                  pl.BlockSpec((1, d), lambda j: (0, j))],
        out_specs=pl.BlockSpec((b, d), lambda j: (0, j)),
        out_shape=jax.ShapeDtypeStruct((b, 3 * d), F32),
        name="adaln_mod",
    )(c, ada_w, ada_b.reshape(1, 3 * d))


CONV_ROWS = 64
CONV_SUB = 256
SUBLANES = 8


def _conv_unit(buf, r0, cl, half_taps, shift_masks, norm_scale, skip_norm):
    a = buf[r0:r0 + HALO + CONV_ROWS, cl]
    n = CONV_ROWS // SUBLANES
    vs = [a[SUBLANES * i:SUBLANES * (i + 1)] for i in range(n + 1)]
    acc = [vs[i + 1] * half_taps[CONV_K - 1] for i in range(n)]
    for shift in range(1, CONV_K):
        rs = [pltpu.roll(v, shift, 0) for v in vs]
        tap = half_taps[CONV_K - 1 - shift]
        for i in range(n):
            acc[i] = acc[i] + jnp.where(shift_masks[shift], rs[i], rs[i + 1]) * tap
    h = jnp.concatenate(acc, axis=0)
    y = h + h * jnp.tanh(h)
    norm = lax.rsqrt(jnp.sum(y * y, axis=-1, keepdims=True) + EPS) * norm_scale
    return (y * jnp.where(skip_norm, 1.0, norm)).astype(BF16)


def _inproj_kernel(x_ref, mod_ref, nw_ref, wc_ref, wp_ref, ws_ref, cw_ref, pc_ref, pp_ref, small_ref,
                   h_scr, cbuf, *, steps_per_seg):
    j = pl.program_id(1)
    s = x_ref.shape[1]

    @pl.when(j == 0)
    def _():
        x = x_ref[0]
        m = mod_ref[0]
        y = x * lax.rsqrt(jnp.mean(x * x, axis=-1, keepdims=True) + EPS) * nw_ref[...]
        h = (y * (1.0 + m[1:2]) + m[0:1]).astype(BF16)
        h_scr[...] = h
        small_ref[0] = jnp.dot(h, ws_ref[...], preferred_element_type=F32)
        cbuf[0:HALO, :] = jnp.zeros((HALO, CONV_SUB), F32)

    cbuf[HALO:HALO + s, :] = jnp.dot(h_scr[...], wc_ref[...], preferred_element_type=F32)
    pp_ref[0] = jnp.dot(h_scr[...], wp_ref[...], preferred_element_type=F32).astype(BF16)

    sub = lax.broadcasted_iota(jnp.int32, (SUBLANES, LANES), 0)
    shift_masks = [sub < shift for shift in range(CONV_K)]
    q_scale = jnp.where(j < steps_per_seg, DN_DK ** -0.5, 1.0)
    is_v = j >= 2 * steps_per_seg
    for c0 in range(0, CONV_SUB, DN_DK):
        cl = slice(c0, c0 + DN_DK)
        half_taps = [cw_ref[t:t + 1, cl] * 0.5 for t in range(CONV_K)]
        for r0 in range(0, s, CONV_ROWS):
            pc_ref[0, r0:r0 + CONV_ROWS, cl] = _conv_unit(cbuf, r0, cl, half_taps, shift_masks, q_scale, is_v)


def _inproj(x, mod3, norm_w, w_conv, w_plain, w_small, conv_w):
    b, s, d = x.shape
    conv_width = w_conv.shape[1]
    n_steps = conv_width // CONV_SUB
    tp = w_plain.shape[1] // n_steps
    assert n_steps % 3 == 0 and n_steps * CONV_SUB == conv_width and conv_w.shape[1] == conv_width
    assert tp * n_steps == w_plain.shape[1] and tp % LANES == 0 and s % CONV_ROWS == 0
    kern = functools.partial(_inproj_kernel, steps_per_seg=n_steps // 3)
    return pl.pallas_call(
        kern,
        grid=(b, n_steps),
        in_specs=[pl.BlockSpec((1, s, d), lambda bi, j: (bi, 0, 0)),
                  pl.BlockSpec((1, 3, d), lambda bi, j: (bi, 0, 0)),
                  pl.BlockSpec((1, d), lambda bi, j: (0, 0)),
                  pl.BlockSpec((d, CONV_SUB), lambda bi, j: (0, j)),
                  pl.BlockSpec((d, tp), lambda bi, j: (0, j)),
                  pl.BlockSpec((d, LANES), lambda bi, j: (0, 0)),
                  pl.BlockSpec((CONV_K, CONV_SUB), lambda bi, j: (0, j))],
        out_specs=[pl.BlockSpec((1, s, CONV_SUB), lambda bi, j: (bi, 0, j)),
                   pl.BlockSpec((1, s, tp), lambda bi, j: (bi, 0, j)),
                   pl.BlockSpec((1, s, LANES), lambda bi, j: (bi, 0, 0))],
        out_shape=[jax.ShapeDtypeStruct((b, s, conv_width), BF16),
                   jax.ShapeDtypeStruct((b, s, w_plain.shape[1]), BF16),
                   jax.ShapeDtypeStruct((b, s, LANES), F32)],
        scratch_shapes=[pltpu.VMEM((s, d), BF16),
                        pltpu.VMEM((HALO + s, CONV_SUB), F32)],
        compiler_params=pltpu.CompilerParams(
            dimension_semantics=("parallel", "arbitrary"), vmem_limit_bytes=VMEM_LIMIT),
        name="norm_inproj",
    )(x, mod3, norm_w.reshape(1, d), w_conv, w_plain, w_small, conv_w)


def _level_mask(level):
    i = lax.broadcasted_iota(jnp.int32, (CHUNK, CHUNK), 0)
    j = lax.broadcasted_iota(jnp.int32, (CHUNK, CHUNK), 1)
    same_block = (i >> (level + 1)) == (j >> (level + 1))
    return same_block & (((i >> level) & 1) == 1) & (((j >> level) & 1) == 0)


N_LEVELS = CHUNK.bit_length() - 1


def _deltanet_kernel(q_ref, k_ref, v_ref, small_ref, alog_ref, dtb_ref, o_ref,
                     state, u_s, wq_s, akd_s, el_s, *, ts, group):
    @pl.when(pl.program_id(1) == 0)
    def _():
        state[...] = jnp.zeros_like(state)

    row = lax.broadcasted_iota(jnp.int32, (CHUNK, CHUNK), 0)
    col = lax.broadcasted_iota(jnp.int32, (CHUNK, CHUNK), 1)
    causal = row >= col
    tri = causal.astype(F32)
    tri_strict = (row > col).astype(F32)
    neg_a = -jnp.exp(alog_ref[...])
    dtb = dtb_ref[...]

    eye = (row == col).astype(F32)
    level_masks = [_level_mask(level).astype(F32) for level in range(N_LEVELS)]
    n_chunks = ts // CHUNK

    def prep_stages(gi):
        chains = []
        for ci in range(group):
            c = gi * group + ci
            r0 = c * CHUNK if isinstance(c, int) else pl.multiple_of(c * CHUNK, CHUNK)
            sm = small_ref[0, pl.ds(r0, CHUNK), :]
            beta_all = _sigmoid(sm)
            z = sm + dtb
            softplus = jnp.maximum(z, 0.0) + jnp.log(1.0 + jnp.exp(-jnp.abs(z)))
            g_all = neg_a * softplus
            gc_all = jnp.dot(tri, g_all, preferred_element_type=F32, precision=HIGHEST)
            gc_t = gc_all.T
            g_last = gc_all[CHUNK - 1:CHUNK, :]
            e_in_all = jnp.exp(gc_all)
            e_out_all = jnp.exp(g_last - gc_all)
            el_s[c] = jnp.broadcast_to(jnp.exp(g_last), (8, LANES))
            for h in range(DN_HEADS):
                gl = DN_HEADS + h
                cq = slice(h * DN_DK, (h + 1) * DN_DK)
                q = q_ref[0, pl.ds(r0, CHUNK), cq].astype(F32)
                k = k_ref[0, pl.ds(r0, CHUNK), cq].astype(F32)
                v = v_ref[0, pl.ds(r0, CHUNK), slice(h * DN_DV, (h + 1) * DN_DV)].astype(F32)
                beta = beta_all[:, h:h + 1]
                e_in = e_in_all[:, gl:gl + 1]
                k_beta = k * beta
                decay = jnp.exp(jnp.where(causal, gc_all[:, gl:gl + 1] - gc_t[gl:gl + 1, :], -jnp.inf))
                chains.append(dict(
                    c=c, h=h, r0=r0, decay=decay,
                    kq=jnp.concatenate([k_beta, q], axis=0).astype(BF16), k=k.astype(BF16),
                    rhs=jnp.concatenate([v * beta, k_beta * e_in], axis=1).astype(BF16),
                    q_dec=(q * e_in).astype(BF16),
                    k_dec_t=(k * e_out_all[:, gl:gl + 1]).T.astype(BF16)))

        for ch in chains:
            ch["kk_qk"] = _mm_nt(ch["kq"], ch["k"])
        yield
        for ch in chains:
            kk_qk = ch.pop("kk_qk")
            decay = ch.pop("decay")
            ch["lmat"] = kk_qk[0:CHUNK] * decay * tri_strict
            ch["a_qk"] = (kk_qk[CHUNK:2 * CHUNK] * decay).astype(BF16)
            ch["x"] = eye - ch["lmat"] * level_masks[0]
        for level in range(1, N_LEVELS):
            for ch in chains:
                ch["y"] = _mm(ch["lmat"] * level_masks[level], ch["x"])
            yield
            for ch in chains:
                ch["x"] = ch["x"] - _mm(ch["x"], ch.pop("y"))
            yield
        for ch in chains:
            ch["uw"] = _mm(ch.pop("x"), ch.pop("rhs"))
        for ch in chains:
            c, h, r0 = ch["c"], ch["h"], ch["r0"]
            uw = ch.pop("uw")
            u_s[pl.ds(r0, CHUNK), slice(h * DN_DV, (h + 1) * DN_DV)] = uw[:, 0:DN_DV]
            wq_s[c, h] = jnp.concatenate([uw[:, DN_DV:DN_DV + DN_DK].astype(BF16), ch["q_dec"]], axis=0)
            akd_s[c, h] = jnp.concatenate([ch["a_qk"], ch["k_dec_t"]], axis=0)
        yield

    def rec_stages(gi):
        for ci in range(group):
            c = gi * group + ci
            r0 = c * CHUNK if isinstance(c, int) else pl.multiple_of(c * CHUNK, CHUNK)
            e_last_all = el_s[c]
            rs = [_mm(wq_s[c, h], state[h]) for h in range(DN_HEADS)]
            yield
            v_new = [u_s[pl.ds(r0, CHUNK), slice(h * DN_DV, (h + 1) * DN_DV)] - rs[h][0:CHUNK]
                     for h in range(DN_HEADS)]
            av = [_mm(akd_s[c, h], v_new[h]) for h in range(DN_HEADS)]
            for h in range(DN_HEADS):
                gl = DN_HEADS + h
                o = rs[h][CHUNK:2 * CHUNK] + av[h][0:CHUNK]
                state[h] = state[h] * e_last_all[0:1, gl:gl + 1] + av[h][CHUNK:CHUNK + DN_DK]
                o_ref[0, pl.ds(r0, CHUNK), slice(h * DN_DV, (h + 1) * DN_DV)] = o.astype(o_ref.dtype)
            yield

    def run(*gens):
        gens = list(gens)
        while gens:
            for g in list(gens):
                if next(g, StopIteration) is StopIteration:
                    gens.remove(g)

    n_groups = n_chunks // group
    run(prep_stages(0))

    def both(gi, carry):
        run(prep_stages(gi), rec_stages(gi - 1))
        return carry

    lax.fori_loop(1, n_groups, both, 0)
    run(rec_stages(n_groups - 1))


def _deltanet(p, small, alog_row, dtb_row, ts, group):
    b, s, _ = p.shape
    width = DN_HEADS * DN_DK
    n_chunks = ts // CHUNK
    kern = functools.partial(_deltanet_kernel, ts=ts, group=group)
    return pl.pallas_call(
        kern,
        grid=(b, s // ts),
        in_specs=[pl.BlockSpec((1, ts, width), lambda bi, t: (bi, t, 0)),
                  pl.BlockSpec((1, ts, width), lambda bi, t: (bi, t, 1)),
                  pl.BlockSpec((1, ts, width), lambda bi, t: (bi, t, 2)),
                  pl.BlockSpec((1, ts, LANES), lambda bi, t: (bi, t, 0)),
                  pl.BlockSpec((1, LANES), lambda bi, t: (0, 0)),
                  pl.BlockSpec((1, LANES), lambda bi, t: (0, 0))],
        out_specs=pl.BlockSpec((1, ts, DN_WIDTH), lambda bi, t: (bi, t, 0)),
        out_shape=jax.ShapeDtypeStruct((b, s, DN_WIDTH), BF16),
        scratch_shapes=[pltpu.VMEM((DN_HEADS, DN_DK, DN_DV), F32),
                        pltpu.VMEM((ts, DN_WIDTH), F32),
                        pltpu.VMEM((n_chunks, DN_HEADS, 2 * CHUNK, DN_DK), BF16),
                        pltpu.VMEM((n_chunks, DN_HEADS, CHUNK + DN_DK, CHUNK), BF16),
                        pltpu.VMEM((n_chunks, 8, LANES), F32)],
        compiler_params=pltpu.CompilerParams(
            dimension_semantics=("parallel", "arbitrary"), vmem_limit_bytes=VMEM_LIMIT),
        name="gated_deltanet",
    )(p, p, p, small, alog_row, dtb_row)


ATTN_BLOCKS_PER_ITER = 2


def _attn_kernel(q_ref, k_ref, v_ref, o_ref, lse_ref, *scratch, dil, n_back):
    s = q_ref.shape[1]
    sub_len = s // dil
    nb = sub_len // ATTN_BLOCK
    n_pairs = ATTN_GROUP_WIDTH // LANES
    prev_block = nb > 1

    def lanes(pair):
        return slice(pair * LANES, (pair + 1) * LANES)

    if dil > 1:
        stage, qs, ks, vs, os_, ls = scratch
        for src, dst in ((q_ref, qs), (k_ref, ks), (v_ref, vs)):
            for pair in range(n_pairs):
                stage[...] = src[0, :, lanes(pair)].astype(F32)
                for r in range(dil):
                    dst[pair, r * sub_len:(r + 1) * sub_len, :] = (
                        stage[pl.ds(r, sub_len, stride=dil), :].astype(BF16))

        def load(ref, src, pair, r0):
            return src[pair, pl.ds(r0, ATTN_BLOCK), :]
    else:
        qs, ks, vs = q_ref, k_ref, v_ref

        def load(ref, src, pair, r0):
            return ref[0, pl.ds(r0, ATTN_BLOCK), lanes(pair)]

    n_keys = 2 * ATTN_BLOCK if prev_block else ATTN_BLOCK
    qi = lax.broadcasted_iota(jnp.int32, (ATTN_BLOCK, n_keys), 0)
    ki = lax.broadcasted_iota(jnp.int32, (ATTN_BLOCK, n_keys), 1)
    dist = qi + (n_keys - ATTN_BLOCK) - ki
    band = (dist >= 0) & (dist <= n_back)
    bias_full = jnp.where(band, 0.0, -jnp.inf).astype(F32)
    bias_first = jnp.where(band & (ki >= ATTN_BLOCK), 0.0, -jnp.inf).astype(F32)
    lane = lax.broadcasted_iota(jnp.int32, (ATTN_BLOCK, LANES), 1)
    low_half = lane < ATTN_HEAD_DIM
    scale = ATTN_HEAD_DIM ** -0.5

    def block_body(fi, carry):
        chains = []
        for bi in range(ATTN_BLOCKS_PER_ITER):
            f = fi * ATTN_BLOCKS_PER_ITER + bi
            q0 = pl.multiple_of(f * ATTN_BLOCK, ATTN_BLOCK)
            if prev_block:
                k0 = pl.multiple_of(jnp.maximum(f - 1, 0) * ATTN_BLOCK, ATTN_BLOCK)
                bias = jnp.where(f % nb > 0, bias_full, bias_first)
            else:
                bias = bias_full
            for pair in range(n_pairs):
                qp = load(q_ref, qs, pair, q0) * scale
                kp = load(k_ref, ks, pair, q0)
                vp = load(v_ref, vs, pair, q0)
                if prev_block:
                    kp = jnp.concatenate([load(k_ref, ks, pair, k0), kp], axis=0)
                    vp = jnp.concatenate([load(v_ref, vs, pair, k0), vp], axis=0)
                for half in range(2):
                    sel = low_half if half == 0 else ~low_half
                    chains.append(dict(q0=q0, pair=pair, half=half, bias=bias, kp=kp, vp=vp,
                                       qm=jnp.where(sel, qp, jnp.zeros_like(qp))))
        for ch in chains:
            ch["sc"] = _mm_nt(ch.pop("qm"), ch.pop("kp"))
        for ch in chains:
            sc = ch.pop("sc") + ch.pop("bias")
            m = jnp.max(sc, axis=-1, keepdims=True)
            p = jnp.exp(sc - m)
            ch["denom"] = jnp.sum(p, axis=-1, keepdims=True)
            ch["m"] = m
            ch["p"] = p.astype(BF16)
        for ch in chains:
            ch["pv"] = _mm(ch.pop("p"), ch.pop("vp"))
        for c0 in range(0, len(chains), 2):
            lo, hi = chains[c0], chains[c0 + 1]
            o = jnp.where(low_half, lo["pv"] / lo["denom"], hi["pv"] / hi["denom"])
            lse = jnp.where(low_half, lo["m"] + jnp.log(lo["denom"]), hi["m"] + jnp.log(hi["denom"]))
            q0, pair = lo["q0"], lo["pair"]
            if dil > 1:
                os_[pair, pl.ds(q0, ATTN_BLOCK), :] = o
                ls[pair, pl.ds(q0, ATTN_BLOCK), :] = lse
            else:
                o_ref[0, pl.ds(q0, ATTN_BLOCK), lanes(pair)] = o.astype(o_ref.dtype)
                lse_ref[0, pl.ds(q0, ATTN_BLOCK), lanes(pair)] = lse
        return carry

    lax.fori_loop(0, dil * nb // ATTN_BLOCKS_PER_ITER, block_body, 0)

    if dil > 1:
        for pair in range(n_pairs):
            for src, dst in ((os_, o_ref), (ls, lse_ref)):
                for r in range(dil):
                    stage[pl.ds(r, sub_len, stride=dil), :] = src[pair, r * sub_len:(r + 1) * sub_len, :]
                dst[0, :, lanes(pair)] = stage[...].astype(dst.dtype)


def _dilated_attention(p, group, col_base):
    b, s, _ = p.shape
    window, dil = DIL_CONFIGS[group]
    gw = ATTN_GROUP_WIDTH
    blk = col_base // gw + group
    step = ATTN_WIDTH // gw
    kern = functools.partial(_attn_kernel, dil=dil, n_back=window // dil)
    return pl.pallas_call(
        kern,
        grid=(b,),
        in_specs=[pl.BlockSpec((1, s, gw), lambda bi: (bi, 0, blk)),
                  pl.BlockSpec((1, s, gw), lambda bi: (bi, 0, blk + step)),
                  pl.BlockSpec((1, s, gw), lambda bi: (bi, 0, blk + 2 * step))],
        out_specs=[pl.BlockSpec((1, s, gw), lambda bi: (bi, 0, 0)),
                   pl.BlockSpec((1, s, gw), lambda bi: (bi, 0, 0))],
        out_shape=[jax.ShapeDtypeStruct((b, s, gw), BF16),
                   jax.ShapeDtypeStruct((b, s, gw), F32)],
        scratch_shapes=([] if dil == 1 else
                        [pltpu.VMEM((s, LANES), F32)]
                        + [pltpu.VMEM((gw // LANES, s, LANES), BF16) for _ in range(3)]
                        + [pltpu.VMEM((gw // LANES, s, LANES), F32) for _ in range(2)]),
        compiler_params=pltpu.CompilerParams(
            dimension_semantics=("parallel",), vmem_limit_bytes=VMEM_LIMIT),
        name=f"dilated_attn_g{group}",
    )(p, p, p)


def _merge_kernel(x_ref, mod_ref, oa_ref, za_ref, ga_ref, gb_ref, zb_ref,
                  ob0_ref, ob1_ref, ob2_ref, l0_ref, l1_ref, l2_ref,
                  dnw_ref, pa_ref, pb_ref, wo_ref, fw_ref, out_ref):
    oa = oa_ref[0].astype(F32)
    za = za_ref[0].astype(F32)
    parts = []
    for h in range(DN_HEADS):
        cq = slice(h * DN_DV, (h + 1) * DN_DV)
        blk = oa[:, cq]
        y = blk * lax.rsqrt(jnp.mean(blk * blk, axis=-1, keepdims=True) + EPS) * dnw_ref[...]
        parts.append(y * _silu(za[:, cq]))
    y_a = _mm(jnp.concatenate(parts, axis=1), pa_ref[...])

    l0, l1, l2 = l0_ref[0], l1_ref[0], l2_ref[0]
    m = jnp.maximum(jnp.maximum(l0, l1), l2)
    e0, e1, e2 = jnp.exp(l0 - m), jnp.exp(l1 - m), jnp.exp(l2 - m)
    den = e0 + e1 + e2
    ob = jnp.concatenate([ob0_ref[0].astype(F32) * (e0 / den),
                          ob1_ref[0].astype(F32) * (e1 / den),
                          ob2_ref[0].astype(F32) * (e2 / den)], axis=1)
    y_b = _mm(ob * _silu(zb_ref[0].astype(F32)), pb_ref[...])

    merged = _sigmoid(ga_ref[0].astype(F32)) * y_a + _sigmoid(gb_ref[0].astype(F32)) * y_b
    gate = mod_ref[0][2:3]
    xo = x_ref[0] + gate * _mm(merged, wo_ref[...])
    out_ref[0] = xo * lax.rsqrt(jnp.mean(xo * xo, axis=-1, keepdims=True) + EPS) * fw_ref[...]


def _merge(x, mod3, o_a, p, obs, lses, dn_norm_w, pa, pb, wo, final_w, tm, cols):
    b, s, d = x.shape
    gw = ATTN_GROUP_WIDTH

    def row_spec(width, blk):
        return pl.BlockSpec((1, tm, width), lambda bi, i: (bi, i, blk))

    def full_spec(shape):
        return pl.BlockSpec(shape, lambda bi, i: (0,) * len(shape))

    return pl.pallas_call(
        _merge_kernel,
        grid=(b, s // tm),
        in_specs=[row_spec(d, 0),
                  pl.BlockSpec((1, 3, d), lambda bi, i: (bi, 0, 0)),
                  row_spec(DN_WIDTH, 0),
                  row_spec(DN_WIDTH, cols["za"] // DN_WIDTH),
                  row_spec(d, cols["ga"] // d),
                  row_spec(d, cols["gb"] // d),
                  row_spec(ATTN_WIDTH, cols["zb"] // ATTN_WIDTH),
                  row_spec(gw, 0), row_spec(gw, 0), row_spec(gw, 0),
                  row_spec(gw, 0), row_spec(gw, 0), row_spec(gw, 0),
                  full_spec((1, DN_DV)), full_spec(pa.shape), full_spec(pb.shape), full_spec(wo.shape),
                  full_spec((1, d))],
        out_specs=row_spec(d, 0),
        out_shape=jax.ShapeDtypeStruct((b, s, d), F32),
        compiler_params=pltpu.CompilerParams(
            dimension_semantics=("parallel", "parallel"), vmem_limit_bytes=VMEM_LIMIT),
        name="merge_outproj",
    )(x, mod3, o_a, p, p, p, p, *obs, *lses, dn_norm_w.reshape(1, DN_DV), pa, pb, wo, final_w.reshape(1, d))


def _layer(x, c, norm_w, ada_w, ada_b, w_in, conv_w, a_log, dt_bias, dn_norm_w, w_proj_a, w_proj_b, w_out,
           final_norm_w):
    b, s, d = x.shape
    qk = DN_HEADS * DN_DK
    splits = (qk, qk, DN_WIDTH, DN_WIDTH, DN_HEADS, DN_HEADS, ATTN_WIDTH, ATTN_WIDTH, ATTN_WIDTH, ATTN_WIDTH, d, d)
    names = ("qa", "ka", "va", "za", "beta", "a", "qb", "kb", "vb", "zb", "ga", "gb")
    offs = dict(zip(names, np.cumsum((0,) + splits[:-1]).tolist()))
    widths = dict(zip(names, splits))
    order = ("za", "ga", "gb", "qb", "kb", "vb", "zb")
    cols, pos = {}, 0
    for n in order:
        assert pos % widths[n] == 0, (n, pos)
        cols[n] = pos
        pos += widths[n]
    assert offs["ka"] == offs["qa"] + qk and offs["va"] == offs["ka"] + qk
    w_conv = w_in[:, offs["qa"]:offs["va"] + DN_WIDTH].astype(BF16)
    w_plain = jnp.concatenate([w_in[:, offs[n]:offs[n] + widths[n]] for n in order], axis=1).astype(BF16)
    w_small = jnp.concatenate([w_in[:, offs["beta"]:offs["beta"] + 2 * DN_HEADS],
                               jnp.zeros((d, LANES - 2 * DN_HEADS), w_in.dtype)], axis=1).astype(BF16)
    pad_lo = jnp.zeros((DN_HEADS,), F32)
    pad_hi = jnp.zeros((LANES - 2 * DN_HEADS,), F32)
    alog_row = jnp.concatenate([pad_lo, a_log.astype(F32), pad_hi]).reshape(1, LANES)
    dtb_row = jnp.concatenate([pad_lo, dt_bias.astype(F32), pad_hi]).reshape(1, LANES)

    mod3 = _adaln_mod(c, ada_w, ada_b).reshape(b, 3, d)
    p_conv, p, small = _inproj(x, mod3, norm_w, w_conv, w_plain, w_small, conv_w)
    o_a = _deltanet(p_conv, small, alog_row, dtb_row, ts=min(s, 1024), group=4)
    obs, lses = [], []
    for g in range(len(DIL_CONFIGS)):
        o_g, lse_g = _dilated_attention(p, g, cols["qb"])
        obs.append(o_g)
        lses.append(lse_g)
    return _merge(x, mod3, o_a, p, obs, lses, dn_norm_w, w_proj_a.astype(BF16), w_proj_b.astype(BF16),
                  w_out.astype(BF16), final_norm_w, tm=min(s, 256), cols=cols)


def kernel(x, c, norm_w, ada_w, ada_b, w_in, conv_w, a_log, dt_bias, dn_norm_w, w_proj_a, w_proj_b, w_out,
           final_norm_w):
    depth = norm_w.shape[0]
    assert depth == 1, "the final RMSNorm is fused into the single layer's output kernel"
    return _layer(x, c, norm_w[0], ada_w[0], ada_b[0], w_in[0], conv_w[0], a_log[0], dt_bias[0], dn_norm_w[0],
                  w_proj_a[0], w_proj_b[0], w_out[0], final_norm_w)
```

```python
import functools

import jax
import jax.numpy as jnp
import numpy as np
from jax import lax
from jax.experimental import pallas as pl
from jax.experimental.pallas import tpu as pltpu

F32 = jnp.float32
BF16 = jnp.bfloat16
HIGHEST = lax.Precision.HIGHEST

EPS = 1e-6
DN_HEADS = 8
DN_DK = 128
DN_DV = 128
DN_WIDTH = DN_HEADS * DN_DV
CONV_K = 4
CHUNK = 64
DIL_CONFIGS = ((128, 1), (512, 4), (2048, 16))
HEADS_PER_GROUP = 4
ATTN_HEAD_DIM = 64
ATTN_GROUP_WIDTH = HEADS_PER_GROUP * ATTN_HEAD_DIM
ATTN_WIDTH = ATTN_GROUP_WIDTH * len(DIL_CONFIGS)
ATTN_BLOCK = 128
LANES = 128
HALO = 8
VMEM_LIMIT = 56 * 1024 * 1024


def _mm(a, b):
    return jnp.dot(a.astype(BF16), b.astype(BF16), preferred_element_type=F32)


def _mm_nt(a, b):
    return lax.dot_general(a.astype(BF16), b.astype(BF16), (((1,), (1,)), ((), ())),
                           preferred_element_type=F32)


def _mm_tn(a, b):
    return lax.dot_general(a.astype(BF16), b.astype(BF16), (((0,), (0,)), ((), ())),
                           preferred_element_type=F32)


def _sigmoid(x):
    return 0.5 * jnp.tanh(0.5 * x) + 0.5


def _silu(x):
    return x * _sigmoid(x)


def _mod_kernel(c_ref, w_ref, b_ref, o_ref):
    sc = _silu(c_ref[...])
    o_ref[...] = jnp.dot(sc, w_ref[...], preferred_element_type=F32, precision=HIGHEST) + b_ref[...]


def _adaln_mod(c, ada_w, ada_b):
    b, d = c.shape
    return pl.pallas_call(
        _mod_kernel,
        grid=(3,),
        in_specs=[pl.BlockSpec((b, d), lambda j: (0, 0)),
                  pl.BlockSpec((d, d), lambda j: (0, j)),
                  pl.BlockSpec((1, d), lambda j: (0, j))],
        out_specs=pl.BlockSpec((b, d), lambda j: (0, j)),
        out_shape=jax.ShapeDtypeStruct((b, 3 * d), F32),
        name="adaln_mod",
    )(c, ada_w, ada_b.reshape(1, 3 * d))


CONV_ROWS = 64
CONV_SUB = 256
SUBLANES = 8


def _conv_unit(buf, r0, cl, half_taps, shift_masks, norm_scale, skip_norm):
    a = buf[r0:r0 + HALO + CONV_ROWS, cl]
    n = CONV_ROWS // SUBLANES
    vs = [a[SUBLANES * i:SUBLANES * (i + 1)] for i in range(n + 1)]
    acc = [vs[i + 1] * half_taps[CONV_K - 1] for i in range(n)]
    for shift in range(1, CONV_K):
        rs = [pltpu.roll(v, shift, 0) for v in vs]
        tap = half_taps[CONV_K - 1 - shift]
        for i in range(n):
            acc[i] = acc[i] + jnp.where(shift_masks[shift], rs[i], rs[i + 1]) * tap
    h = jnp.concatenate(acc, axis=0)
    y = h + h * jnp.tanh(h)
    norm = lax.rsqrt(jnp.sum(y * y, axis=-1, keepdims=True) + EPS) * norm_scale
    return (y * jnp.where(skip_norm, 1.0, norm)).astype(BF16)


def _inproj_kernel(x_ref, mod_ref, nw_ref, wc_ref, wp_ref, ws_ref, cw_ref, pc_ref, pp_ref, small_ref,
                   h_scr, cbuf, *, steps_per_seg):
    j = pl.program_id(1)
    s = x_ref.shape[1]

    @pl.when(j == 0)
    def _():
        x = x_ref[0]
        m = mod_ref[0]
        y = x * lax.rsqrt(jnp.mean(x * x, axis=-1, keepdims=True) + EPS) * nw_ref[...]
        h = (y * (1.0 + m[1:2]) + m[0:1]).astype(BF16)
        h_scr[...] = h
        small_ref[0] = jnp.dot(h, ws_ref[...], preferred_element_type=F32)
        cbuf[0:HALO, :] = jnp.zeros((HALO, CONV_SUB), F32)

    cbuf[HALO:HALO + s, :] = jnp.dot(h_scr[...], wc_ref[...], preferred_element_type=F32)
    pp_ref[0] = jnp.dot(h_scr[...], wp_ref[...], preferred_element_type=F32).astype(BF16)

    sub = lax.broadcasted_iota(jnp.int32, (SUBLANES, LANES), 0)
    shift_masks = [sub < shift for shift in range(CONV_K)]
    q_scale = jnp.where(j < steps_per_seg, DN_DK ** -0.5, 1.0)
    is_v = j >= 2 * steps_per_seg
    for c0 in range(0, CONV_SUB, DN_DK):
        cl = slice(c0, c0 + DN_DK)
        half_taps = [cw_ref[t:t + 1, cl] * 0.5 for t in range(CONV_K)]
        for r0 in range(0, s, CONV_ROWS):
            pc_ref[0, r0:r0 + CONV_ROWS, cl] = _conv_unit(cbuf, r0, cl, half_taps, shift_masks, q_scale, is_v)


def _inproj(x, mod3, norm_w, w_conv, w_plain, w_small, conv_w):
    b, s, d = x.shape
    conv_width = w_conv.shape[1]
    n_steps = conv_width // CONV_SUB
    tp = w_plain.shape[1] // n_steps
    assert n_steps % 3 == 0 and n_steps * CONV_SUB == conv_width and conv_w.shape[1] == conv_width
    assert tp * n_steps == w_plain.shape[1] and tp % LANES == 0 and s % CONV_ROWS == 0
    kern = functools.partial(_inproj_kernel, steps_per_seg=n_steps // 3)
    return pl.pallas_call(
        kern,
        grid=(b, n_steps),
        in_specs=[pl.BlockSpec((1, s, d), lambda bi, j: (bi, 0, 0)),
                  pl.BlockSpec((1, 3, d), lambda bi, j: (bi, 0, 0)),
                  pl.BlockSpec((1, d), lambda bi, j: (0, 0)),
                  pl.BlockSpec((d, CONV_SUB), lambda bi, j: (0, j)),
                  pl.BlockSpec((d, tp), lambda bi, j: (0, j)),
                  pl.BlockSpec((d, LANES), lambda bi, j: (0, 0)),
                  pl.BlockSpec((CONV_K, CONV_SUB), lambda bi, j: (0, j))],
        out_specs=[pl.BlockSpec((1, s, CONV_SUB), lambda bi, j: (bi, 0, j)),
                   pl.BlockSpec((1, s, tp), lambda bi, j: (bi, 0, j)),
                   pl.BlockSpec((1, s, LANES), lambda bi, j: (bi, 0, 0))],
        out_shape=[jax.ShapeDtypeStruct((b, s, conv_width), BF16),
                   jax.ShapeDtypeStruct((b, s, w_plain.shape[1]), BF16),
                   jax.ShapeDtypeStruct((b, s, LANES), F32)],
        scratch_shapes=[pltpu.VMEM((s, d), BF16),
                        pltpu.VMEM((HALO + s, CONV_SUB), F32)],
        compiler_params=pltpu.CompilerParams(
            dimension_semantics=("parallel", "arbitrary"), vmem_limit_bytes=VMEM_LIMIT),
        name="norm_inproj",
    )(x, mod3, norm_w.reshape(1, d), w_conv, w_plain, w_small, conv_w)


def _level_mask(level):
    i = lax.broadcasted_iota(jnp.int32, (CHUNK, CHUNK), 0)
    j = lax.broadcasted_iota(jnp.int32, (CHUNK, CHUNK), 1)
    same_block = (i >> (level + 1)) == (j >> (level + 1))
    return same_block & (((i >> level) & 1) == 1) & (((j >> level) & 1) == 0)


N_LEVELS = CHUNK.bit_length() - 1


def _deltanet_kernel(q_ref, k_ref, v_ref, small_ref, alog_ref, dtb_ref, o_ref,
                     state, u_s, wq_s, akd_s, el_s, *, ts, group):
    @pl.when(pl.program_id(1) == 0)
    def _():
        state[...] = jnp.zeros_like(state)

    row = lax.broadcasted_iota(jnp.int32, (CHUNK, CHUNK), 0)
    col = lax.broadcasted_iota(jnp.int32, (CHUNK, CHUNK), 1)
    causal = row >= col
    tri = causal.astype(F32)
    tri_strict = (row > col).astype(F32)
    neg_a = -jnp.exp(alog_ref[...])
    dtb = dtb_ref[...]

    eye = (row == col).astype(F32)
    level_masks = [_level_mask(level).astype(F32) for level in range(N_LEVELS)]
    n_chunks = ts // CHUNK

    def prep_stages(gi):
        chains = []
        for ci in range(group):
            c = gi * group + ci
            r0 = c * CHUNK if isinstance(c, int) else pl.multiple_of(c * CHUNK, CHUNK)
            sm = small_ref[0, pl.ds(r0, CHUNK), :]
            beta_all = _sigmoid(sm)
            z = sm + dtb
            softplus = jnp.maximum(z, 0.0) + jnp.log(1.0 + jnp.exp(-jnp.abs(z)))
            g_all = neg_a * softplus
            gc_all = jnp.dot(tri, g_all, preferred_element_type=F32, precision=HIGHEST)
            gc_t = gc_all.T
            g_last = gc_all[CHUNK - 1:CHUNK, :]
            e_in_all = jnp.exp(gc_all)
            e_out_all = jnp.exp(g_last - gc_all)
            el_s[c] = jnp.broadcast_to(jnp.exp(g_last), (8, LANES))
            for h in range(DN_HEADS):
                gl = DN_HEADS + h
                cq = slice(h * DN_DK, (h + 1) * DN_DK)
                q = q_ref[0, pl.ds(r0, CHUNK), cq].astype(F32)
                k = k_ref[0, pl.ds(r0, CHUNK), cq].astype(F32)
                v = v_ref[0, pl.ds(r0, CHUNK), slice(h * DN_DV, (h + 1) * DN_DV)].astype(F32)
                beta = beta_all[:, h:h + 1]
                e_in = e_in_all[:, gl:gl + 1]
                k_beta = k * beta
                decay = jnp.exp(jnp.where(causal, gc_all[:, gl:gl + 1] - gc_t[gl:gl + 1, :], -jnp.inf))
                chains.append(dict(
                    c=c, h=h, r0=r0, decay=decay,
                    kq=jnp.concatenate([k_beta, q], axis=0).astype(BF16), k=k.astype(BF16),
                    rhs=jnp.concatenate([v * beta, k_beta * e_in], axis=1).astype(BF16),
                    q_dec=(q * e_in).astype(BF16),
                    k_dec_t=(k * e_out_all[:, gl:gl + 1]).T.astype(BF16)))

        for ch in chains:
            ch["kk_qk"] = _mm_nt(ch["kq"], ch["k"])
        yield
        for ch in chains:
            kk_qk = ch.pop("kk_qk")
            decay = ch.pop("decay")
            ch["lmat"] = kk_qk[0:CHUNK] * decay * tri_strict
            ch["a_qk"] = (kk_qk[CHUNK:2 * CHUNK] * decay).astype(BF16)
            ch["x"] = eye - ch["lmat"] * level_masks[0]
        for level in range(1, N_LEVELS):
            for ch in chains:
                ch["y"] = _mm(ch["lmat"] * level_masks[level], ch["x"])
            yield
            for ch in chains:
                ch["x"] = ch["x"] - _mm(ch["x"], ch.pop("y"))
            yield
        for ch in chains:
            ch["uw"] = _mm(ch.pop("x"), ch.pop("rhs"))
        for ch in chains:
            c, h, r0 = ch["c"], ch["h"], ch["r0"]
            uw = ch.pop("uw")
            u_s[pl.ds(r0, CHUNK), slice(h * DN_DV, (h + 1) * DN_DV)] = uw[:, 0:DN_DV]
            wq_s[c, h] = jnp.concatenate([uw[:, DN_DV:DN_DV + DN_DK].astype(BF16), ch["q_dec"]], axis=0)
            akd_s[c, h] = jnp.concatenate([ch["a_qk"], ch["k_dec_t"]], axis=0)
        yield

    def rec_stages(gi):
        for ci in range(group):
            c = gi * group + ci
            r0 = c * CHUNK if isinstance(c, int) else pl.multiple_of(c * CHUNK, CHUNK)
            e_last_all = el_s[c]
            rs = [_mm(wq_s[c, h], state[h]) for h in range(DN_HEADS)]
            yield
            v_new = [u_s[pl.ds(r0, CHUNK), slice(h * DN_DV, (h + 1) * DN_DV)] - rs[h][0:CHUNK]
                     for h in range(DN_HEADS)]
            av = [_mm(akd_s[c, h], v_new[h]) for h in range(DN_HEADS)]
            for h in range(DN_HEADS):
                gl = DN_HEADS + h
                o = rs[h][CHUNK:2 * CHUNK] + av[h][0:CHUNK]
                state[h] = state[h] * e_last_all[0:1, gl:gl + 1] + av[h][CHUNK:CHUNK + DN_DK]
                o_ref[0, pl.ds(r0, CHUNK), slice(h * DN_DV, (h + 1) * DN_DV)] = o.astype(o_ref.dtype)
            yield

    def run(*gens):
        gens = list(gens)
        while gens:
            for g in list(gens):
                if next(g, StopIteration) is StopIteration:
                    gens.remove(g)

    n_groups = n_chunks // group
    run(prep_stages(0))

    def both(gi, carry):
        run(prep_stages(gi), rec_stages(gi - 1))
        return carry

    lax.fori_loop(1, n_groups, both, 0)
    run(rec_stages(n_groups - 1))


def _deltanet(p, small, alog_row, dtb_row, ts, group):
    b, s, _ = p.shape
    width = DN_HEADS * DN_DK
    n_chunks = ts // CHUNK
    kern = functools.partial(_deltanet_kernel, ts=ts, group=group)
    return pl.pallas_call(
        kern,
        grid=(b, s // ts),
        in_specs=[pl.BlockSpec((1, ts, width), lambda bi, t: (bi, t, 0)),
                  pl.BlockSpec((1, ts, width), lambda bi, t: (bi, t, 1)),
                  pl.BlockSpec((1, ts, width), lambda bi, t: (bi, t, 2)),
                  pl.BlockSpec((1, ts, LANES), lambda bi, t: (bi, t, 0)),
                  pl.BlockSpec((1, LANES), lambda bi, t: (0, 0)),
                  pl.BlockSpec((1, LANES), lambda bi, t: (0, 0))],
        out_specs=pl.BlockSpec((1, ts, DN_WIDTH), lambda bi, t: (bi, t, 0)),
        out_shape=jax.ShapeDtypeStruct((b, s, DN_WIDTH), BF16),
        scratch_shapes=[pltpu.VMEM((DN_HEADS, DN_DK, DN_DV), F32),
                        pltpu.VMEM((ts, DN_WIDTH), F32),
                        pltpu.VMEM((n_chunks, DN_HEADS, 2 * CHUNK, DN_DK), BF16),
                        pltpu.VMEM((n_chunks, DN_HEADS, CHUNK + DN_DK, CHUNK), BF16),
                        pltpu.VMEM((n_chunks, 8, LANES), F32)],
        compiler_params=pltpu.CompilerParams(
            dimension_semantics=("parallel", "arbitrary"), vmem_limit_bytes=VMEM_LIMIT),
        name="gated_deltanet",
    )(p, p, p, small, alog_row, dtb_row)


ATTN_BLOCKS_PER_ITER = 2


def _attn_kernel(q_ref, k_ref, v_ref, o_ref, lse_ref, *scratch, dil, n_back):
    s = q_ref.shape[1]
    sub_len = s // dil
    nb = sub_len // ATTN_BLOCK
    n_pairs = ATTN_GROUP_WIDTH // LANES
    prev_block = nb > 1

    def lanes(pair):
        return slice(pair * LANES, (pair + 1) * LANES)

    if dil > 1:
        stage, qs, ks, vs, os_, ls = scratch
        for src, dst in ((q_ref, qs), (k_ref, ks), (v_ref, vs)):
            for pair in range(n_pairs):
                stage[...] = src[0, :, lanes(pair)].astype(F32)
                for r in range(dil):
                    dst[pair, r * sub_len:(r + 1) * sub_len, :] = (
                        stage[pl.ds(r, sub_len, stride=dil), :].astype(BF16))

        def load(ref, src, pair, r0):
            return src[pair, pl.ds(r0, ATTN_BLOCK), :]
    else:
        qs, ks, vs = q_ref, k_ref, v_ref

        def load(ref, src, pair, r0):
            return ref[0, pl.ds(r0, ATTN_BLOCK), lanes(pair)]

    n_keys = 2 * ATTN_BLOCK if prev_block else ATTN_BLOCK
    qi = lax.broadcasted_iota(jnp.int32, (ATTN_BLOCK, n_keys), 0)
    ki = lax.broadcasted_iota(jnp.int32, (ATTN_BLOCK, n_keys), 1)
    dist = qi + (n_keys - ATTN_BLOCK) - ki
    band = (dist >= 0) & (dist <= n_back)
    bias_full = jnp.where(band, 0.0, -jnp.inf).astype(F32)
    bias_first = jnp.where(band & (ki >= ATTN_BLOCK), 0.0, -jnp.inf).astype(F32)
    lane = lax.broadcasted_iota(jnp.int32, (ATTN_BLOCK, LANES), 1)
    low_half = lane < ATTN_HEAD_DIM
    scale = ATTN_HEAD_DIM ** -0.5

    def block_body(fi, carry):
        chains = []
        for bi in range(ATTN_BLOCKS_PER_ITER):
            f = fi * ATTN_BLOCKS_PER_ITER + bi
            q0 = pl.multiple_of(f * ATTN_BLOCK, ATTN_BLOCK)
            if prev_block:
                k0 = pl.multiple_of(jnp.maximum(f - 1, 0) * ATTN_BLOCK, ATTN_BLOCK)
                bias = jnp.where(f % nb > 0, bias_full, bias_first)
            else:
                bias = bias_full
            for pair in range(n_pairs):
                qp = load(q_ref, qs, pair, q0) * scale
                kp = load(k_ref, ks, pair, q0)
                vp = load(v_ref, vs, pair, q0)
                if prev_block:
                    kp = jnp.concatenate([load(k_ref, ks, pair, k0), kp], axis=0)
                    vp = jnp.concatenate([load(v_ref, vs, pair, k0), vp], axis=0)
                for half in range(2):
                    sel = low_half if half == 0 else ~low_half
                    chains.append(dict(q0=q0, pair=pair, half=half, bias=bias, kp=kp, vp=vp,
                                       qm=jnp.where(sel, qp, jnp.zeros_like(qp))))
        for ch in chains:
            ch["sc"] = _mm_nt(ch.pop("qm"), ch.pop("kp"))
        for ch in chains:
            sc = ch.pop("sc") + ch.pop("bias")
            m = jnp.max(sc, axis=-1, keepdims=True)
            p = jnp.exp(sc - m)
            ch["denom"] = jnp.sum(p, axis=-1, keepdims=True)
            ch["m"] = m
            ch["p"] = p.astype(BF16)
        for ch in chains:
            ch["pv"] = _mm(ch.pop("p"), ch.pop("vp"))
        for c0 in range(0, len(chains), 2):
            lo, hi = chains[c0], chains[c0 + 1]
            o = jnp.where(low_half, lo["pv"] / lo["denom"], hi["pv"] / hi["denom"])
            lse = jnp.where(low_half, lo["m"] + jnp.log(lo["denom"]), hi["m"] + jnp.log(hi["denom"]))
            q0, pair = lo["q0"], lo["pair"]
            if dil > 1:
                os_[pair, pl.ds(q0, ATTN_BLOCK), :] = o
                ls[pair, pl.ds(q0, ATTN_BLOCK), :] = lse
            else:
                o_ref[0, pl.ds(q0, ATTN_BLOCK), lanes(pair)] = o.astype(o_ref.dtype)
                lse_ref[0, pl.ds(q0, ATTN_BLOCK), lanes(pair)] = lse
        return carry

    lax.fori_loop(0, dil * nb // ATTN_BLOCKS_PER_ITER, block_body, 0)

    if dil > 1:
        for pair in range(n_pairs):
            for src, dst in ((os_, o_ref), (ls, lse_ref)):
                for r in range(dil):
                    stage[pl.ds(r, sub_len, stride=dil), :] = src[pair, r * sub_len:(r + 1) * sub_len, :]
                dst[0, :, lanes(pair)] = stage[...].astype(dst.dtype)


def _dilated_attention(p, group, col_base):
    b, s, _ = p.shape
    window, dil = DIL_CONFIGS[group]
    gw = ATTN_GROUP_WIDTH
    blk = col_base // gw + group
    step = ATTN_WIDTH // gw
    kern = functools.partial(_attn_kernel, dil=dil, n_back=window // dil)
    return pl.pallas_call(
        kern,
        grid=(b,),
        in_specs=[pl.BlockSpec((1, s, gw), lambda bi: (bi, 0, blk)),
                  pl.BlockSpec((1, s, gw), lambda bi: (bi, 0, blk + step)),
                  pl.BlockSpec((1, s, gw), lambda bi: (bi, 0, blk + 2 * step))],
        out_specs=[pl.BlockSpec((1, s, gw), lambda bi: (bi, 0, 0)),
                   pl.BlockSpec((1, s, gw), lambda bi: (bi, 0, 0))],
        out_shape=[jax.ShapeDtypeStruct((b, s, gw), BF16),
                   jax.ShapeDtypeStruct((b, s, gw), F32)],
        scratch_shapes=([] if dil == 1 else
                        [pltpu.VMEM((s, LANES), F32)]
                        + [pltpu.VMEM((gw // LANES, s, LANES), BF16) for _ in range(3)]
                        + [pltpu.VMEM((gw // LANES, s, LANES), F32) for _ in range(2)]),
        compiler_params=pltpu.CompilerParams(
            dimension_semantics=("parallel",), vmem_limit_bytes=VMEM_LIMIT),
        name=f"dilated_attn_g{group}",
    )(p, p, p)


def _merge_kernel(x_ref, mod_ref, oa_ref, za_ref, ga_ref, gb_ref, zb_ref,
                  ob0_ref, ob1_ref, ob2_ref, l0_ref, l1_ref, l2_ref,
                  dnw_ref, pa_ref, pb_ref, wo_ref, fw_ref, out_ref):
    def silu_of_half(h):
        return h + h * jnp.tanh(h)

    oa = oa_ref[0].astype(F32)
    za_half = za_ref[0].astype(F32)
    parts = []
    for h in range(DN_HEADS):
        cq = slice(h * DN_DV, (h + 1) * DN_DV)
        blk = oa[:, cq]
        y = blk * lax.rsqrt(jnp.mean(blk * blk, axis=-1, keepdims=True) + EPS) * dnw_ref[...]
        parts.append(y * silu_of_half(za_half[:, cq]))
    ya_half = _mm(jnp.concatenate(parts, axis=1), pa_ref[...])

    l0, l1, l2 = l0_ref[0], l1_ref[0], l2_ref[0]
    m = jnp.maximum(jnp.maximum(l0, l1), l2)
    e0, e1, e2 = jnp.exp(l0 - m), jnp.exp(l1 - m), jnp.exp(l2 - m)
    inv_den = 1.0 / (e0 + e1 + e2)
    ob = jnp.concatenate([ob0_ref[0].astype(F32) * (e0 * inv_den),
                          ob1_ref[0].astype(F32) * (e1 * inv_den),
                          ob2_ref[0].astype(F32) * (e2 * inv_den)], axis=1)
    yb_half = _mm(ob * silu_of_half(zb_ref[0].astype(F32)), pb_ref[...])

    merged = (ya_half + ya_half * jnp.tanh(ga_ref[0].astype(F32))
              + (yb_half + yb_half * jnp.tanh(gb_ref[0].astype(F32))))
    gate = mod_ref[0][2:3]
    xo = x_ref[0] + gate * _mm(merged, wo_ref[...])
    out_ref[0] = xo * lax.rsqrt(jnp.mean(xo * xo, axis=-1, keepdims=True) + EPS) * fw_ref[...]


def _merge(x, mod3, o_a, p, obs, lses, dn_norm_w, pa, pb, wo, final_w, tm, cols):
    b, s, d = x.shape
    gw = ATTN_GROUP_WIDTH

    def row_spec(width, blk):
        return pl.BlockSpec((1, tm, width), lambda bi, i: (bi, i, blk))

    def full_spec(shape):
        return pl.BlockSpec(shape, lambda bi, i: (0,) * len(shape))

    return pl.pallas_call(
        _merge_kernel,
        grid=(b, s // tm),
        in_specs=[row_spec(d, 0),
                  pl.BlockSpec((1, 3, d), lambda bi, i: (bi, 0, 0)),
                  row_spec(DN_WIDTH, 0),
                  row_spec(DN_WIDTH, cols["za"] // DN_WIDTH),
                  row_spec(d, cols["ga"] // d),
                  row_spec(d, cols["gb"] // d),
                  row_spec(ATTN_WIDTH, cols["zb"] // ATTN_WIDTH),
                  row_spec(gw, 0), row_spec(gw, 0), row_spec(gw, 0),
                  row_spec(gw, 0), row_spec(gw, 0), row_spec(gw, 0),
                  full_spec((1, DN_DV)), full_spec(pa.shape), full_spec(pb.shape), full_spec(wo.shape),
                  full_spec((1, d))],
        out_specs=row_spec(d, 0),
        out_shape=jax.ShapeDtypeStruct((b, s, d), F32),
        compiler_params=pltpu.CompilerParams(
            dimension_semantics=("parallel", "parallel"), vmem_limit_bytes=VMEM_LIMIT),
        name="merge_outproj",
    )(x, mod3, o_a, p, p, p, p, *obs, *lses, dn_norm_w.reshape(1, DN_DV), pa, pb, wo, final_w.reshape(1, d))


W_PREP_ROWS = 128


def _wprep_kernel(w_ref, wc_ref, wp_ref, ws_ref, *, conv_width, segments, small_off):
    wc_ref[...] = w_ref[:, 0:conv_width].astype(BF16)
    for src, width, dst, scale in segments:
        blk = w_ref[:, src:src + width]
        wp_ref[:, dst:dst + width] = (blk if scale == 1.0 else blk * scale).astype(BF16)
    lane = lax.broadcasted_iota(jnp.int32, (w_ref.shape[0], LANES), 1)
    ws_ref[...] = jnp.where(lane < 2 * DN_HEADS, w_ref[:, small_off:small_off + LANES], 0.0).astype(BF16)


def _weight_prep(w_in, conv_width, plain_width, segments, small_off):
    d, n_in = w_in.shape
    assert small_off % LANES == 0 and small_off + LANES <= n_in and d % W_PREP_ROWS == 0
    kern = functools.partial(_wprep_kernel, conv_width=conv_width, segments=segments, small_off=small_off)
    return pl.pallas_call(
        kern,
        grid=(d // W_PREP_ROWS,),
        in_specs=[pl.BlockSpec((W_PREP_ROWS, n_in), lambda i: (i, 0))],
        out_specs=[pl.BlockSpec((W_PREP_ROWS, conv_width), lambda i: (i, 0)),
                   pl.BlockSpec((W_PREP_ROWS, plain_width), lambda i: (i, 0)),
                   pl.BlockSpec((W_PREP_ROWS, LANES), lambda i: (i, 0))],
        out_shape=[jax.ShapeDtypeStruct((d, conv_width), BF16),
                   jax.ShapeDtypeStruct((d, plain_width), BF16),
                   jax.ShapeDtypeStruct((d, LANES), BF16)],
        compiler_params=pltpu.CompilerParams(dimension_semantics=("parallel",), vmem_limit_bytes=VMEM_LIMIT),
        name="weight_prep",
    )(w_in)


def _layer(x, c, norm_w, ada_w, ada_b, w_in, conv_w, a_log, dt_bias, dn_norm_w, w_proj_a, w_proj_b, w_out,
           final_norm_w):
    b, s, d = x.shape
    qk = DN_HEADS * DN_DK
    splits = (qk, qk, DN_WIDTH, DN_WIDTH, DN_HEADS, DN_HEADS, ATTN_WIDTH, ATTN_WIDTH, ATTN_WIDTH, ATTN_WIDTH, d, d)
    names = ("qa", "ka", "va", "za", "beta", "a", "qb", "kb", "vb", "zb", "ga", "gb")
    offs = dict(zip(names, np.cumsum((0,) + splits[:-1]).tolist()))
    widths = dict(zip(names, splits))
    order = ("za", "ga", "gb", "qb", "kb", "vb", "zb")
    cols, pos = {}, 0
    for n in order:
        assert pos % widths[n] == 0, (n, pos)
        cols[n] = pos
        pos += widths[n]
    assert offs["qa"] == 0 and offs["ka"] == qk and offs["va"] == 2 * qk and offs["a"] == offs["beta"] + DN_HEADS
    halved = ("za", "ga", "gb", "zb")
    segments = tuple((offs[n], widths[n], cols[n], 0.5 if n in halved else 1.0) for n in order)
    w_conv, w_plain, w_small = _weight_prep(w_in, 2 * qk + DN_WIDTH, pos, segments, offs["beta"])
    pad_lo = jnp.zeros((DN_HEADS,), F32)
    pad_hi = jnp.zeros((LANES - 2 * DN_HEADS,), F32)
    alog_row = jnp.concatenate([pad_lo, a_log.astype(F32), pad_hi]).reshape(1, LANES)
    dtb_row = jnp.concatenate([pad_lo, dt_bias.astype(F32), pad_hi]).reshape(1, LANES)

    mod3 = _adaln_mod(c, ada_w, ada_b).reshape(b, 3, d)
    p_conv, p, small = _inproj(x, mod3, norm_w, w_conv, w_plain, w_small, conv_w)
    o_a = _deltanet(p_conv, small, alog_row, dtb_row, ts=min(s, 1024), group=4)
    obs, lses = [], []
    for g in range(len(DIL_CONFIGS)):
        o_g, lse_g = _dilated_attention(p, g, cols["qb"])
        obs.append(o_g)
        lses.append(lse_g)
    return _merge(x, mod3, o_a, p, obs, lses, dn_norm_w, (w_proj_a * 0.5).astype(BF16),
                  (w_proj_b * 0.5).astype(BF16), w_out.astype(BF16), final_norm_w, tm=min(s, 256), cols=cols)


def kernel(x, c, norm_w, ada_w, ada_b, w_in, conv_w, a_log, dt_bias, dn_norm_w, w_proj_a, w_proj_b, w_out,
           final_norm_w):
    depth = norm_w.shape[0]
    assert depth == 1, "the final RMSNorm is fused into the single layer's output kernel"
    return _layer(x, c, norm_w[0], ada_w[0], ada_b[0], w_in[0], conv_w[0], a_log[0], dt_bias[0], dn_norm_w[0],
                  w_proj_a[0], w_proj_b[0], w_out[0], final_norm_w)
```

```python
import functools

import jax
import jax.numpy as jnp
import numpy as np
from jax import lax
from jax.experimental import pallas as pl
from jax.experimental.pallas import tpu as pltpu

F32 = jnp.float32
BF16 = jnp.bfloat16
HIGHEST = lax.Precision.HIGHEST

EPS = 1e-6
DN_HEADS = 8
DN_DK = 128
DN_DV = 128
DN_WIDTH = DN_HEADS * DN_DV
CONV_K = 4
CHUNK = 64
DIL_CONFIGS = ((128, 1), (512, 4), (2048, 16))
HEADS_PER_GROUP = 4
ATTN_HEAD_DIM = 64
ATTN_GROUP_WIDTH = HEADS_PER_GROUP * ATTN_HEAD_DIM
ATTN_WIDTH = ATTN_GROUP_WIDTH * len(DIL_CONFIGS)
ATTN_BLOCK = 128
LANES = 128
HALO = 8
VMEM_LIMIT = 56 * 1024 * 1024


def _mm(a, b):
    return jnp.dot(a.astype(BF16), b.astype(BF16), preferred_element_type=F32)


def _mm_nt(a, b):
    return lax.dot_general(a.astype(BF16), b.astype(BF16), (((1,), (1,)), ((), ())),
                           preferred_element_type=F32)


def _mm_tn(a, b):
    return lax.dot_general(a.astype(BF16), b.astype(BF16), (((0,), (0,)), ((), ())),
                           preferred_element_type=F32)


def _sigmoid(x):
    return 0.5 * jnp.tanh(0.5 * x) + 0.5


def _silu(x):
    return x * _sigmoid(x)


def _mod_kernel(c_ref, w_ref, b_ref, o_ref):
    sc = _silu(c_ref[...])
    o_ref[...] = jnp.dot(sc, w_ref[...], preferred_element_type=F32, precision=HIGHEST) + b_ref[...]


def _adaln_mod(c, ada_w, ada_b):
    b, d = c.shape
    return pl.pallas_call(
        _mod_kernel,
        grid=(3,),
        in_specs=[pl.BlockSpec((b, d), lambda j: (0, 0)),
                  pl.BlockSpec((d, d), lambda j: (0, j)),
                  pl.BlockSpec((1, d), lambda j: (0, j))],
        out_specs=pl.BlockSpec((b, d), lambda j: (0, j)),
        out_shape=jax.ShapeDtypeStruct((b, 3 * d), F32),
        name="adaln_mod",
    )(c, ada_w, ada_b.reshape(1, 3 * d))


CONV_ROWS = 64
CONV_SUB = 256
SUBLANES = 8


def _conv_unit(buf, r0, cl, half_taps, shift_masks, norm_scale, skip_norm):
    a = buf[r0:r0 + HALO + CONV_ROWS, cl]
    n = CONV_ROWS // SUBLANES
    vs = [a[SUBLANES * i:SUBLANES * (i + 1)] for i in range(n + 1)]
    acc = [vs[i + 1] * half_taps[CONV_K - 1] for i in range(n)]
    for shift in range(1, CONV_K):
        rs = [pltpu.roll(v, shift, 0) for v in vs]
        tap = half_taps[CONV_K - 1 - shift]
        for i in range(n):
            acc[i] = acc[i] + jnp.where(shift_masks[shift], rs[i], rs[i + 1]) * tap
    h = jnp.concatenate(acc, axis=0)
    y = h + h * jnp.tanh(h)
    norm = lax.rsqrt(jnp.sum(y * y, axis=-1, keepdims=True) + EPS) * norm_scale
    return (y * jnp.where(skip_norm, 1.0, norm)).astype(BF16)


def _inproj_kernel(x_ref, mod_ref, nw_ref, wc_ref, wp_ref, ws_ref, cw_ref, pc_ref, pp_ref, small_ref,
                   h_scr, cbuf, *, steps_per_seg):
    j = pl.program_id(1)
    s = x_ref.shape[1]

    @pl.when(j == 0)
    def _():
        x = x_ref[0]
        m = mod_ref[0]
        y = x * lax.rsqrt(jnp.mean(x * x, axis=-1, keepdims=True) + EPS) * nw_ref[...]
        h = (y * (1.0 + m[1:2]) + m[0:1]).astype(BF16)
        h_scr[...] = h
        small_ref[0] = _mm_nt(h, ws_ref[...])
        cbuf[0:HALO, :] = jnp.zeros((HALO, CONV_SUB), F32)

    cbuf[HALO:HALO + s, :] = _mm_nt(h_scr[...], wc_ref[...])
    pp_ref[0] = _mm_nt(h_scr[...], wp_ref[...]).astype(BF16)

    sub = lax.broadcasted_iota(jnp.int32, (SUBLANES, LANES), 0)
    shift_masks = [sub < shift for shift in range(CONV_K)]
    q_scale = jnp.where(j < steps_per_seg, DN_DK ** -0.5, 1.0)
    is_v = j >= 2 * steps_per_seg
    for c0 in range(0, CONV_SUB, DN_DK):
        cl = slice(c0, c0 + DN_DK)
        half_taps = [cw_ref[t:t + 1, cl] * 0.5 for t in range(CONV_K)]
        for r0 in range(0, s, CONV_ROWS):
            pc_ref[0, r0:r0 + CONV_ROWS, cl] = _conv_unit(cbuf, r0, cl, half_taps, shift_masks, q_scale, is_v)


def _inproj(x, mod3, norm_w, w_conv, w_plain, w_small, conv_w):
    b, s, d = x.shape
    conv_width, plain_width = w_conv.shape[0], w_plain.shape[0]
    n_steps = conv_width // CONV_SUB
    tp = plain_width // n_steps
    assert n_steps % 3 == 0 and n_steps * CONV_SUB == conv_width and conv_w.shape[1] == conv_width
    assert tp * n_steps == plain_width and tp % LANES == 0 and s % CONV_ROWS == 0
    kern = functools.partial(_inproj_kernel, steps_per_seg=n_steps // 3)
    return pl.pallas_call(
        kern,
        grid=(b, n_steps),
        in_specs=[pl.BlockSpec((1, s, d), lambda bi, j: (bi, 0, 0)),
                  pl.BlockSpec((1, 3, d), lambda bi, j: (bi, 0, 0)),
                  pl.BlockSpec((1, d), lambda bi, j: (0, 0)),
                  pl.BlockSpec((CONV_SUB, d), lambda bi, j: (j, 0)),
                  pl.BlockSpec((tp, d), lambda bi, j: (j, 0)),
                  pl.BlockSpec((LANES, d), lambda bi, j: (0, 0)),
                  pl.BlockSpec((CONV_K, CONV_SUB), lambda bi, j: (0, j))],
        out_specs=[pl.BlockSpec((1, s, CONV_SUB), lambda bi, j: (bi, 0, j)),
                   pl.BlockSpec((1, s, tp), lambda bi, j: (bi, 0, j)),
                   pl.BlockSpec((1, s, LANES), lambda bi, j: (bi, 0, 0))],
        out_shape=[jax.ShapeDtypeStruct((b, s, conv_width), BF16),
                   jax.ShapeDtypeStruct((b, s, plain_width), BF16),
                   jax.ShapeDtypeStruct((b, s, LANES), F32)],
        scratch_shapes=[pltpu.VMEM((s, d), BF16),
                        pltpu.VMEM((HALO + s, CONV_SUB), F32)],
        compiler_params=pltpu.CompilerParams(
            dimension_semantics=("parallel", "arbitrary"), vmem_limit_bytes=VMEM_LIMIT),
        name="norm_inproj",
    )(x, mod3, norm_w.reshape(1, d), w_conv, w_plain, w_small, conv_w)


def _level_mask(level):
    i = lax.broadcasted_iota(jnp.int32, (CHUNK, CHUNK), 0)
    j = lax.broadcasted_iota(jnp.int32, (CHUNK, CHUNK), 1)
    same_block = (i >> (level + 1)) == (j >> (level + 1))
    return same_block & (((i >> level) & 1) == 1) & (((j >> level) & 1) == 0)


N_LEVELS = CHUNK.bit_length() - 1


def _deltanet_kernel(q_ref, k_ref, v_ref, small_ref, alog_ref, dtb_ref, o_ref,
                     state, u_s, wq_s, akd_s, el_s, *, ts, group):
    @pl.when(pl.program_id(1) == 0)
    def _():
        state[...] = jnp.zeros_like(state)

    row = lax.broadcasted_iota(jnp.int32, (CHUNK, CHUNK), 0)
    col = lax.broadcasted_iota(jnp.int32, (CHUNK, CHUNK), 1)
    causal = row >= col
    tri = causal.astype(F32)
    tri_strict = (row > col).astype(F32)
    neg_a = -jnp.exp(alog_ref[...])
    dtb = dtb_ref[...]

    eye = (row == col).astype(F32)
    level_masks = [_level_mask(level).astype(F32) for level in range(N_LEVELS)]
    n_chunks = ts // CHUNK

    def prep_stages(gi):
        chains = []
        for ci in range(group):
            c = gi * group + ci
            r0 = c * CHUNK if isinstance(c, int) else pl.multiple_of(c * CHUNK, CHUNK)
            sm = small_ref[0, pl.ds(r0, CHUNK), :]
            beta_all = _sigmoid(sm)
            z = sm + dtb
            softplus = jnp.maximum(z, 0.0) + jnp.log(1.0 + jnp.exp(-jnp.abs(z)))
            g_all = neg_a * softplus
            gc_all = jnp.dot(tri, g_all, preferred_element_type=F32, precision=HIGHEST)
            gc_t = gc_all.T
            g_last = gc_all[CHUNK - 1:CHUNK, :]
            e_in_all = jnp.exp(gc_all)
            e_out_all = jnp.exp(g_last - gc_all)
            el_s[c] = jnp.broadcast_to(jnp.exp(g_last), (8, LANES))
            for h in range(DN_HEADS):
                gl = DN_HEADS + h
                cq = slice(h * DN_DK, (h + 1) * DN_DK)
                q = q_ref[0, pl.ds(r0, CHUNK), cq].astype(F32)
                k = k_ref[0, pl.ds(r0, CHUNK), cq].astype(F32)
                v = v_ref[0, pl.ds(r0, CHUNK), slice(h * DN_DV, (h + 1) * DN_DV)].astype(F32)
                beta = beta_all[:, h:h + 1]
                e_in = e_in_all[:, gl:gl + 1]
                k_beta = k * beta
                decay = jnp.exp(jnp.where(causal, gc_all[:, gl:gl + 1] - gc_t[gl:gl + 1, :], -jnp.inf))
                chains.append(dict(
                    c=c, h=h, r0=r0, decay=decay,
                    kq=jnp.concatenate([k_beta, q], axis=0).astype(BF16), k=k.astype(BF16),
                    rhs=jnp.concatenate([v * beta, k_beta * e_in], axis=1).astype(BF16),
                    q_dec=(q * e_in).astype(BF16),
                    k_dec_t=(k * e_out_all[:, gl:gl + 1]).T.astype(BF16)))

        for ch in chains:
            ch["kk_qk"] = _mm_nt(ch["kq"], ch["k"])
        yield
        for ch in chains:
            kk_qk = ch.pop("kk_qk")
            decay = ch.pop("decay")
            ch["lmat"] = kk_qk[0:CHUNK] * decay * tri_strict
            ch["a_qk"] = (kk_qk[CHUNK:2 * CHUNK] * decay).astype(BF16)
            ch["x"] = eye - ch["lmat"] * level_masks[0]
        for level in range(1, N_LEVELS):
            for ch in chains:
                ch["y"] = _mm(ch["lmat"] * level_masks[level], ch["x"])
            yield
            for ch in chains:
                ch["x"] = ch["x"] - _mm(ch["x"], ch.pop("y"))
            yield
        for ch in chains:
            ch["uw"] = _mm(ch.pop("x"), ch.pop("rhs"))
        for ch in chains:
            c, h, r0 = ch["c"], ch["h"], ch["r0"]
            uw = ch.pop("uw")
            u_s[pl.ds(r0, CHUNK), slice(h * DN_DV, (h + 1) * DN_DV)] = uw[:, 0:DN_DV]
            wq_s[c, h] = jnp.concatenate([uw[:, DN_DV:DN_DV + DN_DK].astype(BF16), ch["q_dec"]], axis=0)
            akd_s[c, h] = jnp.concatenate([ch["a_qk"], ch["k_dec_t"]], axis=0)
        yield

    def rec_stages(gi):
        for ci in range(group):
            c = gi * group + ci
            r0 = c * CHUNK if isinstance(c, int) else pl.multiple_of(c * CHUNK, CHUNK)
            e_last_all = el_s[c]
            rs = [_mm(wq_s[c, h], state[h]) for h in range(DN_HEADS)]
            yield
            v_new = [u_s[pl.ds(r0, CHUNK), slice(h * DN_DV, (h + 1) * DN_DV)] - rs[h][0:CHUNK]
                     for h in range(DN_HEADS)]
            av = [_mm(akd_s[c, h], v_new[h]) for h in range(DN_HEADS)]
            for h in range(DN_HEADS):
                gl = DN_HEADS + h
                o = rs[h][CHUNK:2 * CHUNK] + av[h][0:CHUNK]
                state[h] = state[h] * e_last_all[0:1, gl:gl + 1] + av[h][CHUNK:CHUNK + DN_DK]
                o_ref[0, pl.ds(r0, CHUNK), slice(h * DN_DV, (h + 1) * DN_DV)] = o.astype(o_ref.dtype)
            yield

    def run(*gens):
        gens = list(gens)
        while gens:
            for g in list(gens):
                if next(g, StopIteration) is StopIteration:
                    gens.remove(g)

    n_groups = n_chunks // group
    run(prep_stages(0))

    def both(gi, carry):
        run(prep_stages(gi), rec_stages(gi - 1))
        return carry

    lax.fori_loop(1, n_groups, both, 0)
    run(rec_stages(n_groups - 1))


def _deltanet(p, small, alog_row, dtb_row, ts, group):
    b, s, _ = p.shape
    width = DN_HEADS * DN_DK
    n_chunks = ts // CHUNK
    kern = functools.partial(_deltanet_kernel, ts=ts, group=group)
    return pl.pallas_call(
        kern,
        grid=(b, s // ts),
        in_specs=[pl.BlockSpec((1, ts, width), lambda bi, t: (bi, t, 0)),
                  pl.BlockSpec((1, ts, width), lambda bi, t: (bi, t, 1)),
                  pl.BlockSpec((1, ts, width), lambda bi, t: (bi, t, 2)),
                  pl.BlockSpec((1, ts, LANES), lambda bi, t: (bi, t, 0)),
                  pl.BlockSpec((1, LANES), lambda bi, t: (0, 0)),
                  pl.BlockSpec((1, LANES), lambda bi, t: (0, 0))],
        out_specs=pl.BlockSpec((1, ts, DN_WIDTH), lambda bi, t: (bi, t, 0)),
        out_shape=jax.ShapeDtypeStruct((b, s, DN_WIDTH), BF16),
        scratch_shapes=[pltpu.VMEM((DN_HEADS, DN_DK, DN_DV), F32),
                        pltpu.VMEM((ts, DN_WIDTH), F32),
                        pltpu.VMEM((n_chunks, DN_HEADS, 2 * CHUNK, DN_DK), BF16),
                        pltpu.VMEM((n_chunks, DN_HEADS, CHUNK + DN_DK, CHUNK), BF16),
                        pltpu.VMEM((n_chunks, 8, LANES), F32)],
        compiler_params=pltpu.CompilerParams(
            dimension_semantics=("parallel", "arbitrary"), vmem_limit_bytes=VMEM_LIMIT),
        name="gated_deltanet",
    )(p, p, p, small, alog_row, dtb_row)


ATTN_BLOCKS_PER_ITER = 2


def _attn_kernel(q_ref, k_ref, v_ref, o_ref, lse_ref, *scratch, dil, n_back):
    s = q_ref.shape[1]
    sub_len = s // dil
    nb = sub_len // ATTN_BLOCK
    n_pairs = ATTN_GROUP_WIDTH // LANES
    prev_block = nb > 1

    def lanes(pair):
        return slice(pair * LANES, (pair + 1) * LANES)

    if dil > 1:
        stage, qs, ks, vs, os_, ls = scratch
        for src, dst in ((q_ref, qs), (k_ref, ks), (v_ref, vs)):
            for pair in range(n_pairs):
                stage[...] = src[0, :, lanes(pair)].astype(F32)
                for r in range(dil):
                    dst[pair, r * sub_len:(r + 1) * sub_len, :] = (
                        stage[pl.ds(r, sub_len, stride=dil), :].astype(BF16))

        def load(ref, src, pair, r0):
            return src[pair, pl.ds(r0, ATTN_BLOCK), :]
    else:
        qs, ks, vs = q_ref, k_ref, v_ref

        def load(ref, src, pair, r0):
            return ref[0, pl.ds(r0, ATTN_BLOCK), lanes(pair)]

    n_keys = 2 * ATTN_BLOCK if prev_block else ATTN_BLOCK
    qi = lax.broadcasted_iota(jnp.int32, (ATTN_BLOCK, n_keys), 0)
    ki = lax.broadcasted_iota(jnp.int32, (ATTN_BLOCK, n_keys), 1)
    dist = qi + (n_keys - ATTN_BLOCK) - ki
    band = (dist >= 0) & (dist <= n_back)
    bias_full = jnp.where(band, 0.0, -jnp.inf).astype(F32)
    bias_first = jnp.where(band & (ki >= ATTN_BLOCK), 0.0, -jnp.inf).astype(F32)
    lane = lax.broadcasted_iota(jnp.int32, (ATTN_BLOCK, LANES), 1)
    low_half = lane < ATTN_HEAD_DIM
    scale = ATTN_HEAD_DIM ** -0.5

    def block_body(fi, carry):
        chains = []
        for bi in range(ATTN_BLOCKS_PER_ITER):
            f = fi * ATTN_BLOCKS_PER_ITER + bi
            q0 = pl.multiple_of(f * ATTN_BLOCK, ATTN_BLOCK)
            if prev_block:
                k0 = pl.multiple_of(jnp.maximum(f - 1, 0) * ATTN_BLOCK, ATTN_BLOCK)
                bias = jnp.where(f % nb > 0, bias_full, bias_first)
            else:
                bias = bias_full
            for pair in range(n_pairs):
                qp = load(q_ref, qs, pair, q0) * scale
                kp = load(k_ref, ks, pair, q0)
                vp = load(v_ref, vs, pair, q0)
                if prev_block:
                    kp = jnp.concatenate([load(k_ref, ks, pair, k0), kp], axis=0)
                    vp = jnp.concatenate([load(v_ref, vs, pair, k0), vp], axis=0)
                for half in range(2):
                    sel = low_half if half == 0 else ~low_half
                    chains.append(dict(q0=q0, pair=pair, half=half, bias=bias, kp=kp, vp=vp,
                                       qm=jnp.where(sel, qp, jnp.zeros_like(qp))))
        for ch in chains:
            ch["sc"] = _mm_nt(ch.pop("qm"), ch.pop("kp"))
        for ch in chains:
            sc = ch.pop("sc") + ch.pop("bias")
            m = jnp.max(sc, axis=-1, keepdims=True)
            p = jnp.exp(sc - m)
            ch["denom"] = jnp.sum(p, axis=-1, keepdims=True)
            ch["m"] = m
            ch["p"] = p.astype(BF16)
        for ch in chains:
            ch["pv"] = _mm(ch.pop("p"), ch.pop("vp"))
        for c0 in range(0, len(chains), 2):
            lo, hi = chains[c0], chains[c0 + 1]
            o = jnp.where(low_half, lo["pv"] / lo["denom"], hi["pv"] / hi["denom"])
            lse = jnp.where(low_half, lo["m"] + jnp.log(lo["denom"]), hi["m"] + jnp.log(hi["denom"]))
            q0, pair = lo["q0"], lo["pair"]
            if dil > 1:
                os_[pair, pl.ds(q0, ATTN_BLOCK), :] = o
                ls[pair, pl.ds(q0, ATTN_BLOCK), :] = lse
            else:
                o_ref[0, pl.ds(q0, ATTN_BLOCK), lanes(pair)] = o.astype(o_ref.dtype)
                lse_ref[0, pl.ds(q0, ATTN_BLOCK), lanes(pair)] = lse
        return carry

    lax.fori_loop(0, dil * nb // ATTN_BLOCKS_PER_ITER, block_body, 0)

    if dil > 1:
        for pair in range(n_pairs):
            for src, dst in ((os_, o_ref), (ls, lse_ref)):
                for r in range(dil):
                    stage[pl.ds(r, sub_len, stride=dil), :] = src[pair, r * sub_len:(r + 1) * sub_len, :]
                dst[0, :, lanes(pair)] = stage[...].astype(dst.dtype)


def _dilated_attention(p, group, col_base):
    b, s, _ = p.shape
    window, dil = DIL_CONFIGS[group]
    gw = ATTN_GROUP_WIDTH
    blk = col_base // gw + group
    step = ATTN_WIDTH // gw
    kern = functools.partial(_attn_kernel, dil=dil, n_back=window // dil)
    return pl.pallas_call(
        kern,
        grid=(b,),
        in_specs=[pl.BlockSpec((1, s, gw), lambda bi: (bi, 0, blk)),
                  pl.BlockSpec((1, s, gw), lambda bi: (bi, 0, blk + step)),
                  pl.BlockSpec((1, s, gw), lambda bi: (bi, 0, blk + 2 * step))],
        out_specs=[pl.BlockSpec((1, s, gw), lambda bi: (bi, 0, 0)),
                   pl.BlockSpec((1, s, gw), lambda bi: (bi, 0, 0))],
        out_shape=[jax.ShapeDtypeStruct((b, s, gw), BF16),
                   jax.ShapeDtypeStruct((b, s, gw), F32)],
        scratch_shapes=([] if dil == 1 else
                        [pltpu.VMEM((s, LANES), F32)]
                        + [pltpu.VMEM((gw // LANES, s, LANES), BF16) for _ in range(3)]
                        + [pltpu.VMEM((gw // LANES, s, LANES), F32) for _ in range(2)]),
        compiler_params=pltpu.CompilerParams(
            dimension_semantics=("parallel",), vmem_limit_bytes=VMEM_LIMIT),
        name=f"dilated_attn_g{group}",
    )(p, p, p)


def _merge_kernel(x_ref, mod_ref, oa_ref, za_ref, ga_ref, gb_ref, zb_ref,
                  ob0_ref, ob1_ref, ob2_ref, l0_ref, l1_ref, l2_ref,
                  dnw_ref, pa_ref, pb_ref, wo_ref, fw_ref, out_ref):
    def silu_of_half(h):
        return h + h * jnp.tanh(h)

    oa = oa_ref[0].astype(F32)
    za_half = za_ref[0].astype(F32)
    parts = []
    for h in range(DN_HEADS):
        cq = slice(h * DN_DV, (h + 1) * DN_DV)
        blk = oa[:, cq]
        y = blk * lax.rsqrt(jnp.mean(blk * blk, axis=-1, keepdims=True) + EPS) * dnw_ref[...]
        parts.append(y * silu_of_half(za_half[:, cq]))
    ya_half = _mm(jnp.concatenate(parts, axis=1), pa_ref[...])

    l0, l1, l2 = l0_ref[0], l1_ref[0], l2_ref[0]
    m = jnp.maximum(jnp.maximum(l0, l1), l2)
    e0, e1, e2 = jnp.exp(l0 - m), jnp.exp(l1 - m), jnp.exp(l2 - m)
    inv_den = 1.0 / (e0 + e1 + e2)
    ob = jnp.concatenate([ob0_ref[0].astype(F32) * (e0 * inv_den),
                          ob1_ref[0].astype(F32) * (e1 * inv_den),
                          ob2_ref[0].astype(F32) * (e2 * inv_den)], axis=1)
    yb_half = _mm(ob * silu_of_half(zb_ref[0].astype(F32)), pb_ref[...])

    merged = (ya_half + ya_half * jnp.tanh(ga_ref[0].astype(F32))
              + (yb_half + yb_half * jnp.tanh(gb_ref[0].astype(F32))))
    gate = mod_ref[0][2:3]
    xo = x_ref[0] + gate * _mm(merged, wo_ref[...])
    out_ref[0] = xo * lax.rsqrt(jnp.mean(xo * xo, axis=-1, keepdims=True) + EPS) * fw_ref[...]


def _merge(x, mod3, o_a, p, obs, lses, dn_norm_w, pa, pb, wo, final_w, tm, cols):
    b, s, d = x.shape
    gw = ATTN_GROUP_WIDTH

    def row_spec(width, blk):
        return pl.BlockSpec((1, tm, width), lambda bi, i: (bi, i, blk))

    def full_spec(shape):
        return pl.BlockSpec(shape, lambda bi, i: (0,) * len(shape))

    return pl.pallas_call(
        _merge_kernel,
        grid=(b, s // tm),
        in_specs=[row_spec(d, 0),
                  pl.BlockSpec((1, 3, d), lambda bi, i: (bi, 0, 0)),
                  row_spec(DN_WIDTH, 0),
                  row_spec(DN_WIDTH, cols["za"] // DN_WIDTH),
                  row_spec(d, cols["ga"] // d),
                  row_spec(d, cols["gb"] // d),
                  row_spec(ATTN_WIDTH, cols["zb"] // ATTN_WIDTH),
                  row_spec(gw, 0), row_spec(gw, 0), row_spec(gw, 0),
                  row_spec(gw, 0), row_spec(gw, 0), row_spec(gw, 0),
                  full_spec((1, DN_DV)), full_spec(pa.shape), full_spec(pb.shape), full_spec(wo.shape),
                  full_spec((1, d))],
        out_specs=row_spec(d, 0),
        out_shape=jax.ShapeDtypeStruct((b, s, d), F32),
        compiler_params=pltpu.CompilerParams(
            dimension_semantics=("parallel", "parallel"), vmem_limit_bytes=VMEM_LIMIT),
        name="merge_outproj",
    )(x, mod3, o_a, p, p, p, p, *obs, *lses, dn_norm_w.reshape(1, DN_DV), pa, pb, wo, final_w.reshape(1, d))


W_PREP_COLS = 128


def _wprep_kernel(wt_ref, wc_ref, wp_ref, ws_ref, *, conv_width, segments, small_off):
    wc_ref[...] = wt_ref[0:conv_width, :].astype(BF16)
    for src, width, dst, scale in segments:
        blk = wt_ref[src:src + width, :]
        wp_ref[dst:dst + width, :] = (blk if scale == 1.0 else blk * scale).astype(BF16)
    n_small = 2 * DN_HEADS
    ws_ref[0:n_small, :] = wt_ref[small_off:small_off + n_small, :].astype(BF16)
    ws_ref[n_small:LANES, :] = jnp.zeros((LANES - n_small, wt_ref.shape[1]), BF16)


def _weight_prep(wt, conv_width, plain_width, segments, small_off):
    n_in, d = wt.shape
    assert d % W_PREP_COLS == 0 and small_off % 16 == 0 and all(sg[0] % 16 == 0 for sg in segments)
    kern = functools.partial(_wprep_kernel, conv_width=conv_width, segments=segments, small_off=small_off)
    return pl.pallas_call(
        kern,
        grid=(d // W_PREP_COLS,),
        in_specs=[pl.BlockSpec((n_in, W_PREP_COLS), lambda i: (0, i))],
        out_specs=[pl.BlockSpec((conv_width, W_PREP_COLS), lambda i: (0, i)),
                   pl.BlockSpec((plain_width, W_PREP_COLS), lambda i: (0, i)),
                   pl.BlockSpec((LANES, W_PREP_COLS), lambda i: (0, i))],
        out_shape=[jax.ShapeDtypeStruct((conv_width, d), BF16),
                   jax.ShapeDtypeStruct((plain_width, d), BF16),
                   jax.ShapeDtypeStruct((LANES, d), BF16)],
        compiler_params=pltpu.CompilerParams(dimension_semantics=("parallel",), vmem_limit_bytes=VMEM_LIMIT),
        name="weight_prep",
    )(wt)


def _layer(x, c, norm_w, ada_w, ada_b, w_in, conv_w, a_log, dt_bias, dn_norm_w, w_proj_a, w_proj_b, w_out,
           final_norm_w):
    b, s, d = x.shape
    qk = DN_HEADS * DN_DK
    splits = (qk, qk, DN_WIDTH, DN_WIDTH, DN_HEADS, DN_HEADS, ATTN_WIDTH, ATTN_WIDTH, ATTN_WIDTH, ATTN_WIDTH, d, d)
    names = ("qa", "ka", "va", "za", "beta", "a", "qb", "kb", "vb", "zb", "ga", "gb")
    offs = dict(zip(names, np.cumsum((0,) + splits[:-1]).tolist()))
    widths = dict(zip(names, splits))
    order = ("za", "ga", "gb", "qb", "kb", "vb", "zb")
    cols, pos = {}, 0
    for n in order:
        assert pos % widths[n] == 0, (n, pos)
        cols[n] = pos
        pos += widths[n]
    assert offs["qa"] == 0 and offs["ka"] == qk and offs["va"] == 2 * qk and offs["a"] == offs["beta"] + DN_HEADS
    halved = ("za", "ga", "gb", "zb")
    segments = tuple((offs[n], widths[n], cols[n], 0.5 if n in halved else 1.0) for n in order)
    w_conv, w_plain, w_small = _weight_prep(w_in.T, 2 * qk + DN_WIDTH, pos, segments, offs["beta"])
    pad_lo = jnp.zeros((DN_HEADS,), F32)
    pad_hi = jnp.zeros((LANES - 2 * DN_HEADS,), F32)
    alog_row = jnp.concatenate([pad_lo, a_log.astype(F32), pad_hi]).reshape(1, LANES)
    dtb_row = jnp.concatenate([pad_lo, dt_bias.astype(F32), pad_hi]).reshape(1, LANES)

    mod3 = _adaln_mod(c, ada_w, ada_b).reshape(b, 3, d)
    p_conv, p, small = _inproj(x, mod3, norm_w, w_conv, w_plain, w_small, conv_w)
    o_a = _deltanet(p_conv, small, alog_row, dtb_row, ts=min(s, 1024), group=4)
    obs, lses = [], []
    for g in range(len(DIL_CONFIGS)):
        o_g, lse_g = _dilated_attention(p, g, cols["qb"])
        obs.append(o_g)
        lses.append(lse_g)
    return _merge(x, mod3, o_a, p, obs, lses, dn_norm_w, (w_proj_a * 0.5).astype(BF16),
                  (w_proj_b * 0.5).astype(BF16), w_out.astype(BF16), final_norm_w, tm=min(s, 256), cols=cols)


def kernel(x, c, norm_w, ada_w, ada_b, w_in, conv_w, a_log, dt_bias, dn_norm_w, w_proj_a, w_proj_b, w_out,
           final_norm_w):
    depth = norm_w.shape[0]
    assert depth == 1, "the final RMSNorm is fused into the single layer's output kernel"
    return _layer(x, c, norm_w[0], ada_w[0], ada_b[0], w_in[0], conv_w[0], a_log[0], dt_bias[0], dn_norm_w[0],
                  w_proj_a[0], w_proj_b[0], w_out[0], final_norm_w)
```

```python
import functools

import jax
import jax.numpy as jnp
import numpy as np
from jax import lax
from jax.experimental import pallas as pl
from jax.experimental.pallas import tpu as pltpu

F32 = jnp.float32
BF16 = jnp.bfloat16
HIGHEST = lax.Precision.HIGHEST

EPS = 1e-6
DN_HEADS = 8
DN_DK = 128
DN_DV = 128
DN_WIDTH = DN_HEADS * DN_DV
CONV_K = 4
CHUNK = 64
DIL_CONFIGS = ((128, 1), (512, 4), (2048, 16))
HEADS_PER_GROUP = 4
ATTN_HEAD_DIM = 64
ATTN_GROUP_WIDTH = HEADS_PER_GROUP * ATTN_HEAD_DIM
ATTN_WIDTH = ATTN_GROUP_WIDTH * len(DIL_CONFIGS)
ATTN_BLOCK = 128
LANES = 128
HALO = 8
VMEM_LIMIT = 56 * 1024 * 1024


def _mm(a, b):
    return jnp.dot(a.astype(BF16), b.astype(BF16), preferred_element_type=F32)


def _mm_nt(a, b):
    return lax.dot_general(a.astype(BF16), b.astype(BF16), (((1,), (1,)), ((), ())),
                           preferred_element_type=F32)


def _mm_tn(a, b):
    return lax.dot_general(a.astype(BF16), b.astype(BF16), (((0,), (0,)), ((), ())),
                           preferred_element_type=F32)


def _sigmoid(x):
    return 0.5 * jnp.tanh(0.5 * x) + 0.5


def _silu(x):
    return x * _sigmoid(x)


def _mod_kernel(c_ref, w_ref, b_ref, o_ref):
    sc = _silu(c_ref[...])
    o_ref[...] = jnp.dot(sc, w_ref[...], preferred_element_type=F32, precision=HIGHEST) + b_ref[...]


def _adaln_mod(c, ada_w, ada_b):
    b, d = c.shape
    return pl.pallas_call(
        _mod_kernel,
        grid=(3,),
        in_specs=[pl.BlockSpec((b, d), lambda j: (0, 0)),
                  pl.BlockSpec((d, d), lambda j: (0, j)),
                  pl.BlockSpec((1, d), lambda j: (0, j))],
        out_specs=pl.BlockSpec((b, d), lambda j: (0, j)),
        out_shape=jax.ShapeDtypeStruct((b, 3 * d), F32),
        name="adaln_mod",
    )(c, ada_w, ada_b.reshape(1, 3 * d))


CONV_ROWS = 64
CONV_SUB = 256
SUBLANES = 8


def _conv_unit(buf, r0, cl, half_taps, shift_masks, norm_scale, skip_norm):
    a = buf[r0:r0 + HALO + CONV_ROWS, cl]
    n = CONV_ROWS // SUBLANES
    vs = [a[SUBLANES * i:SUBLANES * (i + 1)] for i in range(n + 1)]
    acc = [vs[i + 1] * half_taps[CONV_K - 1] for i in range(n)]
    for shift in range(1, CONV_K):
        rs = [pltpu.roll(v, shift, 0) for v in vs]
        tap = half_taps[CONV_K - 1 - shift]
        for i in range(n):
            acc[i] = acc[i] + jnp.where(shift_masks[shift], rs[i], rs[i + 1]) * tap
    h = jnp.concatenate(acc, axis=0)
    y = h + h * jnp.tanh(h)
    norm = lax.rsqrt(jnp.sum(y * y, axis=-1, keepdims=True) + EPS) * norm_scale
    return (y * jnp.where(skip_norm, 1.0, norm)).astype(BF16)


def _inproj_kernel(x_ref, mod_ref, nw_ref, wc_ref, wp_ref, ws_ref, cw_ref, pc_ref, pp_ref, small_ref,
                   h_scr, cbuf, *, steps_per_seg):
    j = pl.program_id(1)
    s = x_ref.shape[1]

    @pl.when(j == 0)
    def _():
        x = x_ref[0]
        m = mod_ref[0]
        y = x * lax.rsqrt(jnp.mean(x * x, axis=-1, keepdims=True) + EPS) * nw_ref[...]
        h = (y * (1.0 + m[1:2]) + m[0:1]).astype(BF16)
        h_scr[...] = h
        small_ref[0] = _mm_nt(h, ws_ref[...])
        cbuf[0:HALO, :] = jnp.zeros((HALO, CONV_SUB), F32)

    cbuf[HALO:HALO + s, :] = _mm_nt(h_scr[...], wc_ref[...])
    pp_ref[0] = _mm_nt(h_scr[...], wp_ref[...]).astype(BF16)

    sub = lax.broadcasted_iota(jnp.int32, (SUBLANES, LANES), 0)
    shift_masks = [sub < shift for shift in range(CONV_K)]
    q_scale = jnp.where(j < steps_per_seg, DN_DK ** -0.5, 1.0)
    is_v = j >= 2 * steps_per_seg
    for c0 in range(0, CONV_SUB, DN_DK):
        cl = slice(c0, c0 + DN_DK)
        half_taps = [cw_ref[t:t + 1, cl] * 0.5 for t in range(CONV_K)]
        for r0 in range(0, s, CONV_ROWS):
            pc_ref[0, r0:r0 + CONV_ROWS, cl] = _conv_unit(cbuf, r0, cl, half_taps, shift_masks, q_scale, is_v)


def _inproj(x, mod3, norm_w, w_conv, w_plain, w_small, conv_w):
    b, s, d = x.shape
    conv_width, plain_width = w_conv.shape[0], w_plain.shape[0]
    n_steps = conv_width // CONV_SUB
    tp = plain_width // n_steps
    assert n_steps % 3 == 0 and n_steps * CONV_SUB == conv_width and conv_w.shape[1] == conv_width
    assert tp * n_steps == plain_width and tp % LANES == 0 and s % CONV_ROWS == 0
    kern = functools.partial(_inproj_kernel, steps_per_seg=n_steps // 3)
    return pl.pallas_call(
        kern,
        grid=(b, n_steps),
        in_specs=[pl.BlockSpec((1, s, d), lambda bi, j: (bi, 0, 0)),
                  pl.BlockSpec((1, 3, d), lambda bi, j: (bi, 0, 0)),
                  pl.BlockSpec((1, d), lambda bi, j: (0, 0)),
                  pl.BlockSpec((CONV_SUB, d), lambda bi, j: (j, 0)),
                  pl.BlockSpec((tp, d), lambda bi, j: (j, 0)),
                  pl.BlockSpec((LANES, d), lambda bi, j: (0, 0)),
                  pl.BlockSpec((CONV_K, CONV_SUB), lambda bi, j: (0, j))],
        out_specs=[pl.BlockSpec((1, s, CONV_SUB), lambda bi, j: (bi, 0, j)),
                   pl.BlockSpec((1, s, tp), lambda bi, j: (bi, 0, j)),
                   pl.BlockSpec((1, s, LANES), lambda bi, j: (bi, 0, 0))],
        out_shape=[jax.ShapeDtypeStruct((b, s, conv_width), BF16),
                   jax.ShapeDtypeStruct((b, s, plain_width), BF16),
                   jax.ShapeDtypeStruct((b, s, LANES), F32)],
        scratch_shapes=[pltpu.VMEM((s, d), BF16),
                        pltpu.VMEM((HALO + s, CONV_SUB), F32)],
        compiler_params=pltpu.CompilerParams(
            dimension_semantics=("parallel", "arbitrary"), vmem_limit_bytes=VMEM_LIMIT),
        name="norm_inproj",
    )(x, mod3, norm_w.reshape(1, d), w_conv, w_plain, w_small, conv_w)


def _level_mask(level):
    i = lax.broadcasted_iota(jnp.int32, (CHUNK, CHUNK), 0)
    j = lax.broadcasted_iota(jnp.int32, (CHUNK, CHUNK), 1)
    same_block = (i >> (level + 1)) == (j >> (level + 1))
    return same_block & (((i >> level) & 1) == 1) & (((j >> level) & 1) == 0)


N_LEVELS = CHUNK.bit_length() - 1


def _deltanet_kernel(q_ref, k_ref, v_ref, small_ref, alog_ref, dtb_ref, o_ref,
                     state, u_s, wq_s, akd_s, el_s, *, ts, group):
    @pl.when(pl.program_id(1) == 0)
    def _():
        state[...] = jnp.zeros_like(state)

    row = lax.broadcasted_iota(jnp.int32, (CHUNK, CHUNK), 0)
    col = lax.broadcasted_iota(jnp.int32, (CHUNK, CHUNK), 1)
    causal = row >= col
    tri = causal.astype(F32)
    tri_strict = (row > col).astype(F32)
    neg_a = -jnp.exp(alog_ref[...])
    dtb = dtb_ref[...]

    eye = (row == col).astype(F32)
    level_masks = [_level_mask(level).astype(F32) for level in range(N_LEVELS)]
    n_chunks = ts // CHUNK

    def prep_stages(gi):
        chains = []
        for ci in range(group):
            c = gi * group + ci
            r0 = c * CHUNK if isinstance(c, int) else pl.multiple_of(c * CHUNK, CHUNK)
            sm = small_ref[0, pl.ds(r0, CHUNK), :]
            beta_all = _sigmoid(sm)
            z = sm + dtb
            softplus = jnp.maximum(z, 0.0) + jnp.log(1.0 + jnp.exp(-jnp.abs(z)))
            g_all = neg_a * softplus
            gc_all = jnp.dot(tri, g_all, preferred_element_type=F32, precision=HIGHEST)
            gc_t = gc_all.T
            g_last = gc_all[CHUNK - 1:CHUNK, :]
            e_in_all = jnp.exp(gc_all)
            e_out_all = jnp.exp(g_last - gc_all)
            el_s[c] = jnp.broadcast_to(jnp.exp(g_last), (8, LANES))
            for h in range(DN_HEADS):
                gl = DN_HEADS + h
                cq = slice(h * DN_DK, (h + 1) * DN_DK)
                q = q_ref[0, pl.ds(r0, CHUNK), cq].astype(F32)
                k = k_ref[0, pl.ds(r0, CHUNK), cq].astype(F32)
                v = v_ref[0, pl.ds(r0, CHUNK), slice(h * DN_DV, (h + 1) * DN_DV)].astype(F32)
                beta = beta_all[:, h:h + 1]
                e_in = e_in_all[:, gl:gl + 1]
                k_beta = k * beta
                decay = jnp.exp(jnp.where(causal, gc_all[:, gl:gl + 1] - gc_t[gl:gl + 1, :], -jnp.inf))
                chains.append(dict(
                    c=c, h=h, r0=r0, decay=decay,
                    kq=jnp.concatenate([k_beta, q], axis=0).astype(BF16), k=k.astype(BF16),
                    rhs=jnp.concatenate([v * beta, k_beta * e_in], axis=1).astype(BF16),
                    q_dec=(q * e_in).astype(BF16),
                    k_dec_t=(k * e_out_all[:, gl:gl + 1]).T.astype(BF16)))

        for ch in chains:
            ch["kk_qk"] = _mm_nt(ch["kq"], ch["k"])
        yield
        for ch in chains:
            kk_qk = ch.pop("kk_qk")
            decay = ch.pop("decay")
            ch["lmat"] = kk_qk[0:CHUNK] * decay * tri_strict
            ch["a_qk"] = (kk_qk[CHUNK:2 * CHUNK] * decay).astype(BF16)
            ch["x"] = eye - ch["lmat"] * level_masks[0]
        for level in range(1, N_LEVELS):
            for ch in chains:
                ch["y"] = _mm(ch["lmat"] * level_masks[level], ch["x"])
            yield
            for ch in chains:
                ch["x"] = ch["x"] - _mm(ch["x"], ch.pop("y"))
            yield
        for ch in chains:
            ch["uw"] = _mm(ch.pop("x"), ch.pop("rhs"))
        for ch in chains:
            c, h, r0 = ch["c"], ch["h"], ch["r0"]
            uw = ch.pop("uw")
            u_s[pl.ds(r0, CHUNK), slice(h * DN_DV, (h + 1) * DN_DV)] = uw[:, 0:DN_DV]
            wq_s[c, h] = jnp.concatenate([uw[:, DN_DV:DN_DV + DN_DK].astype(BF16), ch["q_dec"]], axis=0)
            akd_s[c, h] = jnp.concatenate([ch["a_qk"], ch["k_dec_t"]], axis=0)
        yield

    def rec_stages(gi):
        for ci in range(group):
            c = gi * group + ci
            r0 = c * CHUNK if isinstance(c, int) else pl.multiple_of(c * CHUNK, CHUNK)
            e_last_all = el_s[c]
            rs = [_mm(wq_s[c, h], state[h]) for h in range(DN_HEADS)]
            yield
            v_new = [u_s[pl.ds(r0, CHUNK), slice(h * DN_DV, (h + 1) * DN_DV)] - rs[h][0:CHUNK]
                     for h in range(DN_HEADS)]
            av = [_mm(akd_s[c, h], v_new[h]) for h in range(DN_HEADS)]
            for h in range(DN_HEADS):
                gl = DN_HEADS + h
                o = rs[h][CHUNK:2 * CHUNK] + av[h][0:CHUNK]
                state[h] = state[h] * e_last_all[0:1, gl:gl + 1] + av[h][CHUNK:CHUNK + DN_DK]
                o_ref[0, pl.ds(r0, CHUNK), slice(h * DN_DV, (h + 1) * DN_DV)] = o.astype(o_ref.dtype)
            yield

    def run(*gens):
        gens = list(gens)
        while gens:
            for g in list(gens):
                if next(g, StopIteration) is StopIteration:
                    gens.remove(g)

    n_groups = n_chunks // group
    run(prep_stages(0))

    def both(gi, carry):
        run(prep_stages(gi), rec_stages(gi - 1))
        return carry

    lax.fori_loop(1, n_groups, both, 0)
    run(rec_stages(n_groups - 1))


def _deltanet(p, small, alog_row, dtb_row, ts, group):
    b, s, _ = p.shape
    width = DN_HEADS * DN_DK
    n_chunks = ts // CHUNK
    kern = functools.partial(_deltanet_kernel, ts=ts, group=group)
    return pl.pallas_call(
        kern,
        grid=(b, s // ts),
        in_specs=[pl.BlockSpec((1, ts, width), lambda bi, t: (bi, t, 0)),
                  pl.BlockSpec((1, ts, width), lambda bi, t: (bi, t, 1)),
                  pl.BlockSpec((1, ts, width), lambda bi, t: (bi, t, 2)),
                  pl.BlockSpec((1, ts, LANES), lambda bi, t: (bi, t, 0)),
                  pl.BlockSpec((1, LANES), lambda bi, t: (0, 0)),
                  pl.BlockSpec((1, LANES), lambda bi, t: (0, 0))],
        out_specs=pl.BlockSpec((1, ts, DN_WIDTH), lambda bi, t: (bi, t, 0)),
        out_shape=jax.ShapeDtypeStruct((b, s, DN_WIDTH), BF16),
        scratch_shapes=[pltpu.VMEM((DN_HEADS, DN_DK, DN_DV), F32),
                        pltpu.VMEM((ts, DN_WIDTH), F32),
                        pltpu.VMEM((n_chunks, DN_HEADS, 2 * CHUNK, DN_DK), BF16),
                        pltpu.VMEM((n_chunks, DN_HEADS, CHUNK + DN_DK, CHUNK), BF16),
                        pltpu.VMEM((n_chunks, 8, LANES), F32)],
        compiler_params=pltpu.CompilerParams(
            dimension_semantics=("parallel", "arbitrary"), vmem_limit_bytes=VMEM_LIMIT),
        name="gated_deltanet",
    )(p, p, p, small, alog_row, dtb_row)


ATTN_BLOCKS_PER_ITER = 4
ATTN_STRIDE = 4


def _attn_kernel(q_ref, k_ref, v_ref, o_ref, lse_ref, *scratch, dil, n_back):
    s = q_ref.shape[1]
    sub_len = s // dil
    nb = sub_len // ATTN_BLOCK
    n_pairs = ATTN_GROUP_WIDTH // LANES
    prev_block = nb > 1

    def lanes(pair):
        return slice(pair * LANES, (pair + 1) * LANES)

    if dil > 1:
        stage, tmp, qs, ks, vs, os_, ls = scratch
        assert dil in (ATTN_STRIDE, ATTN_STRIDE * ATTN_STRIDE)
        q_len = s // ATTN_STRIDE

        def classes():
            if dil == ATTN_STRIDE:
                return [(r, stage, r) for r in range(dil)]
            return [(r_lo + ATTN_STRIDE * r_hi, tmp, r_lo * q_len + r_hi)
                    for r_lo in range(ATTN_STRIDE) for r_hi in range(ATTN_STRIDE)]

        def deinterleave(dst, pair):
            if dil > ATTN_STRIDE:
                for r_lo in range(ATTN_STRIDE):
                    tmp[r_lo * q_len:(r_lo + 1) * q_len, :] = stage[pl.ds(r_lo, q_len, stride=ATTN_STRIDE), :]
            for r, buf, start in classes():
                dst[pair, r * sub_len:(r + 1) * sub_len, :] = (
                    buf[pl.ds(start, sub_len, stride=ATTN_STRIDE), :].astype(dst.dtype))

        def interleave(src, pair):
            for r, buf, start in classes():
                buf[pl.ds(start, sub_len, stride=ATTN_STRIDE), :] = src[pair, r * sub_len:(r + 1) * sub_len, :]
            if dil > ATTN_STRIDE:
                for r_lo in range(ATTN_STRIDE):
                    stage[pl.ds(r_lo, q_len, stride=ATTN_STRIDE), :] = tmp[r_lo * q_len:(r_lo + 1) * q_len, :]

        for src, dst in ((q_ref, qs), (k_ref, ks), (v_ref, vs)):
            for pair in range(n_pairs):
                stage[...] = src[0, :, lanes(pair)].astype(F32)
                deinterleave(dst, pair)

        def load(ref, src, pair, r0):
            return src[pair, pl.ds(r0, ATTN_BLOCK), :]
    else:
        qs, ks, vs = q_ref, k_ref, v_ref

        def load(ref, src, pair, r0):
            return ref[0, pl.ds(r0, ATTN_BLOCK), lanes(pair)]

    n_keys = 2 * ATTN_BLOCK if prev_block else ATTN_BLOCK
    qi = lax.broadcasted_iota(jnp.int32, (ATTN_BLOCK, n_keys), 0)
    ki = lax.broadcasted_iota(jnp.int32, (ATTN_BLOCK, n_keys), 1)
    dist = qi + (n_keys - ATTN_BLOCK) - ki
    band = (dist >= 0) & (dist <= n_back)
    bias_full = jnp.where(band, 0.0, -jnp.inf).astype(F32)
    bias_first = jnp.where(band & (ki >= ATTN_BLOCK), 0.0, -jnp.inf).astype(F32)
    lane = lax.broadcasted_iota(jnp.int32, (ATTN_BLOCK, LANES), 1)
    low_half = lane < ATTN_HEAD_DIM
    scale = ATTN_HEAD_DIM ** -0.5

    def block_body(fi, carry):
        chains = []
        for bi in range(ATTN_BLOCKS_PER_ITER):
            f = fi * ATTN_BLOCKS_PER_ITER + bi
            q0 = pl.multiple_of(f * ATTN_BLOCK, ATTN_BLOCK)
            if prev_block:
                k0 = pl.multiple_of(jnp.maximum(f - 1, 0) * ATTN_BLOCK, ATTN_BLOCK)
                bias = jnp.where(f % nb > 0, bias_full, bias_first)
            else:
                bias = bias_full
            for pair in range(n_pairs):
                qp = load(q_ref, qs, pair, q0) * scale
                kp = load(k_ref, ks, pair, q0)
                vp = load(v_ref, vs, pair, q0)
                if prev_block:
                    kp = jnp.concatenate([load(k_ref, ks, pair, k0), kp], axis=0)
                    vp = jnp.concatenate([load(v_ref, vs, pair, k0), vp], axis=0)
                for half in range(2):
                    sel = low_half if half == 0 else ~low_half
                    chains.append(dict(q0=q0, pair=pair, half=half, bias=bias, kp=kp, vp=vp,
                                       qm=jnp.where(sel, qp, jnp.zeros_like(qp))))
        for ch in chains:
            ch["sc"] = _mm_nt(ch.pop("qm"), ch.pop("kp"))
        for ch in chains:
            sc = ch.pop("sc") + ch.pop("bias")
            m = jnp.max(sc, axis=-1, keepdims=True)
            p = jnp.exp(sc - m)
            ch["denom"] = jnp.sum(p, axis=-1, keepdims=True)
            ch["m"] = m
            ch["p"] = p.astype(BF16)
        for ch in chains:
            ch["pv"] = _mm(ch.pop("p"), ch.pop("vp"))
        for c0 in range(0, len(chains), 2):
            lo, hi = chains[c0], chains[c0 + 1]
            o = jnp.where(low_half, lo["pv"] / lo["denom"], hi["pv"] / hi["denom"])
            lse = jnp.where(low_half, lo["m"] + jnp.log(lo["denom"]), hi["m"] + jnp.log(hi["denom"]))
            q0, pair = lo["q0"], lo["pair"]
            if dil > 1:
                os_[pair, pl.ds(q0, ATTN_BLOCK), :] = o
                ls[pair, pl.ds(q0, ATTN_BLOCK), :] = lse
            else:
                o_ref[0, pl.ds(q0, ATTN_BLOCK), lanes(pair)] = o.astype(o_ref.dtype)
                lse_ref[0, pl.ds(q0, ATTN_BLOCK), lanes(pair)] = lse
        return carry

    lax.fori_loop(0, dil * nb // ATTN_BLOCKS_PER_ITER, block_body, 0)

    if dil > 1:
        for pair in range(n_pairs):
            for src, dst in ((os_, o_ref), (ls, lse_ref)):
                interleave(src, pair)
                dst[0, :, lanes(pair)] = stage[...].astype(dst.dtype)


def _dilated_attention(p, group, col_base):
    b, s, _ = p.shape
    window, dil = DIL_CONFIGS[group]
    gw = ATTN_GROUP_WIDTH
    blk = col_base // gw + group
    step = ATTN_WIDTH // gw
    kern = functools.partial(_attn_kernel, dil=dil, n_back=window // dil)
    return pl.pallas_call(
        kern,
        grid=(b,),
        in_specs=[pl.BlockSpec((1, s, gw), lambda bi: (bi, 0, blk)),
                  pl.BlockSpec((1, s, gw), lambda bi: (bi, 0, blk + step)),
                  pl.BlockSpec((1, s, gw), lambda bi: (bi, 0, blk + 2 * step))],
        out_specs=[pl.BlockSpec((1, s, gw), lambda bi: (bi, 0, 0)),
                   pl.BlockSpec((1, s, gw), lambda bi: (bi, 0, 0))],
        out_shape=[jax.ShapeDtypeStruct((b, s, gw), BF16),
                   jax.ShapeDtypeStruct((b, s, gw), F32)],
        scratch_shapes=([] if dil == 1 else
                        [pltpu.VMEM((s, LANES), F32) for _ in range(2)]
                        + [pltpu.VMEM((gw // LANES, s, LANES), BF16) for _ in range(3)]
                        + [pltpu.VMEM((gw // LANES, s, LANES), F32) for _ in range(2)]),
        compiler_params=pltpu.CompilerParams(
            dimension_semantics=("parallel",), vmem_limit_bytes=VMEM_LIMIT),
        name=f"dilated_attn_g{group}",
    )(p, p, p)


def _merge_kernel(x_ref, mod_ref, oa_ref, za_ref, ga_ref, gb_ref, zb_ref,
                  ob0_ref, ob1_ref, ob2_ref, l0_ref, l1_ref, l2_ref,
                  dnw_ref, pa_ref, pb_ref, wo_ref, fw_ref, out_ref):
    def silu_of_half(h):
        return h + h * jnp.tanh(h)

    oa = oa_ref[0].astype(F32)
    za_half = za_ref[0].astype(F32)
    parts = []
    for h in range(DN_HEADS):
        cq = slice(h * DN_DV, (h + 1) * DN_DV)
        blk = oa[:, cq]
        y = blk * lax.rsqrt(jnp.mean(blk * blk, axis=-1, keepdims=True) + EPS) * dnw_ref[...]
        parts.append(y * silu_of_half(za_half[:, cq]))
    ya_half = _mm(jnp.concatenate(parts, axis=1), pa_ref[...])

    l0, l1, l2 = l0_ref[0], l1_ref[0], l2_ref[0]
    m = jnp.maximum(jnp.maximum(l0, l1), l2)
    e0, e1, e2 = jnp.exp(l0 - m), jnp.exp(l1 - m), jnp.exp(l2 - m)
    inv_den = 1.0 / (e0 + e1 + e2)
    ob = jnp.concatenate([ob0_ref[0].astype(F32) * (e0 * inv_den),
                          ob1_ref[0].astype(F32) * (e1 * inv_den),
                          ob2_ref[0].astype(F32) * (e2 * inv_den)], axis=1)
    yb_half = _mm(ob * silu_of_half(zb_ref[0].astype(F32)), pb_ref[...])

    merged = (ya_half + ya_half * jnp.tanh(ga_ref[0].astype(F32))
              + (yb_half + yb_half * jnp.tanh(gb_ref[0].astype(F32))))
    gate = mod_ref[0][2:3]
    xo = x_ref[0] + gate * _mm(merged, wo_ref[...])
    out_ref[0] = xo * lax.rsqrt(jnp.mean(xo * xo, axis=-1, keepdims=True) + EPS) * fw_ref[...]


def _merge(x, mod3, o_a, p, obs, lses, dn_norm_w, pa, pb, wo, final_w, tm, cols):
    b, s, d = x.shape
    gw = ATTN_GROUP_WIDTH

    def row_spec(width, blk):
        return pl.BlockSpec((1, tm, width), lambda bi, i: (bi, i, blk))

    def full_spec(shape):
        return pl.BlockSpec(shape, lambda bi, i: (0,) * len(shape))

    return pl.pallas_call(
        _merge_kernel,
        grid=(b, s // tm),
        in_specs=[row_spec(d, 0),
                  pl.BlockSpec((1, 3, d), lambda bi, i: (bi, 0, 0)),
                  row_spec(DN_WIDTH, 0),
                  row_spec(DN_WIDTH, cols["za"] // DN_WIDTH),
                  row_spec(d, cols["ga"] // d),
                  row_spec(d, cols["gb"] // d),
                  row_spec(ATTN_WIDTH, cols["zb"] // ATTN_WIDTH),
                  row_spec(gw, 0), row_spec(gw, 0), row_spec(gw, 0),
                  row_spec(gw, 0), row_spec(gw, 0), row_spec(gw, 0),
                  full_spec((1, DN_DV)), full_spec(pa.shape), full_spec(pb.shape), full_spec(wo.shape),
                  full_spec((1, d))],
        out_specs=row_spec(d, 0),
        out_shape=jax.ShapeDtypeStruct((b, s, d), F32),
        compiler_params=pltpu.CompilerParams(
            dimension_semantics=("parallel", "parallel"), vmem_limit_bytes=VMEM_LIMIT),
        name="merge_outproj",
    )(x, mod3, o_a, p, p, p, p, *obs, *lses, dn_norm_w.reshape(1, DN_DV), pa, pb, wo, final_w.reshape(1, d))


W_PREP_COLS = 128


def _wprep_kernel(wt_ref, wc_ref, wp_ref, ws_ref, *, conv_width, segments, small_off):
    wc_ref[...] = wt_ref[0:conv_width, :].astype(BF16)
    for src, width, dst, scale in segments:
        blk = wt_ref[src:src + width, :]
        wp_ref[dst:dst + width, :] = (blk if scale == 1.0 else blk * scale).astype(BF16)
    n_small = 2 * DN_HEADS
    ws_ref[0:n_small, :] = wt_ref[small_off:small_off + n_small, :].astype(BF16)
    ws_ref[n_small:LANES, :] = jnp.zeros((LANES - n_small, wt_ref.shape[1]), BF16)


def _weight_prep(wt, conv_width, plain_width, segments, small_off):
    n_in, d = wt.shape
    assert d % W_PREP_COLS == 0 and small_off % 16 == 0 and all(sg[0] % 16 == 0 for sg in segments)
    kern = functools.partial(_wprep_kernel, conv_width=conv_width, segments=segments, small_off=small_off)
    return pl.pallas_call(
        kern,
        grid=(d // W_PREP_COLS,),
        in_specs=[pl.BlockSpec((n_in, W_PREP_COLS), lambda i: (0, i))],
        out_specs=[pl.BlockSpec((conv_width, W_PREP_COLS), lambda i: (0, i)),
                   pl.BlockSpec((plain_width, W_PREP_COLS), lambda i: (0, i)),
                   pl.BlockSpec((LANES, W_PREP_COLS), lambda i: (0, i))],
        out_shape=[jax.ShapeDtypeStruct((conv_width, d), BF16),
                   jax.ShapeDtypeStruct((plain_width, d), BF16),
                   jax.ShapeDtypeStruct((LANES, d), BF16)],
        compiler_params=pltpu.CompilerParams(dimension_semantics=("parallel",), vmem_limit_bytes=VMEM_LIMIT),
        name="weight_prep",
    )(wt)


def _layer(x, c, norm_w, ada_w, ada_b, w_in, conv_w, a_log, dt_bias, dn_norm_w, w_proj_a, w_proj_b, w_out,
           final_norm_w):
    b, s, d = x.shape
    qk = DN_HEADS * DN_DK
    splits = (qk, qk, DN_WIDTH, DN_WIDTH, DN_HEADS, DN_HEADS, ATTN_WIDTH, ATTN_WIDTH, ATTN_WIDTH, ATTN_WIDTH, d, d)
    names = ("qa", "ka", "va", "za", "beta", "a", "qb", "kb", "vb", "zb", "ga", "gb")
    offs = dict(zip(names, np.cumsum((0,) + splits[:-1]).tolist()))
    widths = dict(zip(names, splits))
    order = ("za", "ga", "gb", "qb", "kb", "vb", "zb")
    cols, pos = {}, 0
    for n in order:
        assert pos % widths[n] == 0, (n, pos)
        cols[n] = pos
        pos += widths[n]
    assert offs["qa"] == 0 and offs["ka"] == qk and offs["va"] == 2 * qk and offs["a"] == offs["beta"] + DN_HEADS
    halved = ("za", "ga", "gb", "zb")
    segments = tuple((offs[n], widths[n], cols[n], 0.5 if n in halved else 1.0) for n in order)
    w_conv, w_plain, w_small = _weight_prep(w_in.T, 2 * qk + DN_WIDTH, pos, segments, offs["beta"])
    pad_lo = jnp.zeros((DN_HEADS,), F32)
    pad_hi = jnp.zeros((LANES - 2 * DN_HEADS,), F32)
    alog_row = jnp.concatenate([pad_lo, a_log.astype(F32), pad_hi]).reshape(1, LANES)
    dtb_row = jnp.concatenate([pad_lo, dt_bias.astype(F32), pad_hi]).reshape(1, LANES)

    mod3 = _adaln_mod(c, ada_w, ada_b).reshape(b, 3, d)
    p_conv, p, small = _inproj(x, mod3, norm_w, w_conv, w_plain, w_small, conv_w)
    o_a = _deltanet(p_conv, small, alog_row, dtb_row, ts=min(s, 1024), group=4)
    obs, lses = [], []
    for g in range(len(DIL_CONFIGS)):
        o_g, lse_g = _dilated_attention(p, g, cols["qb"])
        obs.append(o_g)
        lses.append(lse_g)
    return _merge(x, mod3, o_a, p, obs, lses, dn_norm_w, (w_proj_a * 0.5).astype(BF16),
                  (w_proj_b * 0.5).astype(BF16), w_out.astype(BF16), final_norm_w, tm=min(s, 256), cols=cols)


def kernel(x, c, norm_w, ada_w, ada_b, w_in, conv_w, a_log, dt_bias, dn_norm_w, w_proj_a, w_proj_b, w_out,
           final_norm_w):
    depth = norm_w.shape[0]
    assert depth == 1, "the final RMSNorm is fused into the single layer's output kernel"
    return _layer(x, c, norm_w[0], ada_w[0], ada_b[0], w_in[0], conv_w[0], a_log[0], dt_bias[0], dn_norm_w[0],
                  w_proj_a[0], w_proj_b[0], w_out[0], final_norm_w)
```

```python
import functools

import jax
import jax.numpy as jnp
import numpy as np
from jax import lax
from jax.experimental import pallas as pl
from jax.experimental.pallas import tpu as pltpu

F32 = jnp.float32
BF16 = jnp.bfloat16
HIGHEST = lax.Precision.HIGHEST

EPS = 1e-6
DN_HEADS = 8
DN_DK = 128
DN_DV = 128
DN_WIDTH = DN_HEADS * DN_DV
CONV_K = 4
CHUNK = 64
DIL_CONFIGS = ((128, 1), (512, 4), (2048, 16))
HEADS_PER_GROUP = 4
ATTN_HEAD_DIM = 64
ATTN_GROUP_WIDTH = HEADS_PER_GROUP * ATTN_HEAD_DIM
ATTN_WIDTH = ATTN_GROUP_WIDTH * len(DIL_CONFIGS)
ATTN_BLOCK = 128
LANES = 128
HALO = 8
VMEM_LIMIT = 56 * 1024 * 1024


def _mm(a, b):
    return jnp.dot(a.astype(BF16), b.astype(BF16), preferred_element_type=F32)


def _mm_nt(a, b):
    return lax.dot_general(a.astype(BF16), b.astype(BF16), (((1,), (1,)), ((), ())),
                           preferred_element_type=F32)


def _mm_tn(a, b):
    return lax.dot_general(a.astype(BF16), b.astype(BF16), (((0,), (0,)), ((), ())),
                           preferred_element_type=F32)


def _sigmoid(x):
    return 0.5 * jnp.tanh(0.5 * x) + 0.5


def _silu(x):
    return x * _sigmoid(x)


def _mod_kernel(c_ref, w_ref, b_ref, o_ref):
    sc = _silu(c_ref[...])
    o_ref[...] = jnp.dot(sc, w_ref[...], preferred_element_type=F32, precision=HIGHEST) + b_ref[...]


def _adaln_mod(c, ada_w, ada_b):
    b, d = c.shape
    return pl.pallas_call(
        _mod_kernel,
        grid=(3,),
        in_specs=[pl.BlockSpec((b, d), lambda j: (0, 0)),
                  pl.BlockSpec((d, d), lambda j: (0, j)),
                  pl.BlockSpec((1, d), lambda j: (0, j))],
        out_specs=pl.BlockSpec((b, d), lambda j: (0, j)),
        out_shape=jax.ShapeDtypeStruct((b, 3 * d), F32),
        name="adaln_mod",
    )(c, ada_w, ada_b.reshape(1, 3 * d))


CONV_ROWS = 64
CONV_SUB = 256
SUBLANES = 8
TOKEN_CHAINS = 3
TOKEN_POINT = "ss"


def _conv_unit(buf, r0, cl, half_taps, shift_masks, norm_scale, skip_norm):
    a = buf[r0:r0 + HALO + CONV_ROWS, cl]
    n = CONV_ROWS // SUBLANES
    vs = [a[SUBLANES * i:SUBLANES * (i + 1)] for i in range(n + 1)]
    acc = [vs[i + 1] * half_taps[CONV_K - 1] for i in range(n)]
    for shift in range(1, CONV_K):
        rs = [pltpu.roll(v, shift, 0) for v in vs]
        tap = half_taps[CONV_K - 1 - shift]
        for i in range(n):
            acc[i] = acc[i] + jnp.where(shift_masks[shift], rs[i], rs[i + 1]) * tap
    h = jnp.concatenate(acc, axis=0)
    y = h + h * jnp.tanh(h)
    ss = jnp.sum(y * y, axis=-1, keepdims=True)
    norm = lax.rsqrt(ss + EPS) * norm_scale
    out = y * jnp.where(skip_norm, 1.0, norm)
    token = {"silu": y, "ss": jnp.broadcast_to(ss, y.shape), "out": out}[TOKEN_POINT][CONV_ROWS - SUBLANES:]
    return out.astype(BF16), token


def _inproj_kernel(x_ref, mod_ref, nw_ref, wc_ref, wp_ref, ws_ref, cw_ref, pc_ref, pp_ref, small_ref,
                   h_scr, cbuf, tok, *, steps_per_seg):
    j = pl.program_id(1)
    s = x_ref.shape[1]

    @pl.when(j == 0)
    def _():
        x = x_ref[0]
        m = mod_ref[0]
        y = x * lax.rsqrt(jnp.mean(x * x, axis=-1, keepdims=True) + EPS) * nw_ref[...]
        h = (y * (1.0 + m[1:2]) + m[0:1]).astype(BF16)
        h_scr[...] = h
        small_ref[0] = _mm_nt(h, ws_ref[...])
        cbuf[0:HALO, :] = jnp.zeros((HALO, CONV_SUB), F32)

    cbuf[HALO:HALO + s, :] = _mm_nt(h_scr[...], wc_ref[...])
    pp_ref[0] = _mm_nt(h_scr[...], wp_ref[...]).astype(BF16)

    sub = lax.broadcasted_iota(jnp.int32, (SUBLANES, LANES), 0)
    shift_masks = [sub < shift for shift in range(CONV_K)]
    q_scale = jnp.where(j < steps_per_seg, DN_DK ** -0.5, 1.0)
    is_v = j >= 2 * steps_per_seg
    never = j < 0
    for c in range(TOKEN_CHAINS):
        tok[c] = jnp.zeros((SUBLANES, LANES), F32)
    n_unit = 0
    for c0 in range(0, CONV_SUB, DN_DK):
        cl = slice(c0, c0 + DN_DK)
        half_taps = [cw_ref[t:t + 1, cl] * 0.5 for t in range(CONV_K)]
        for r0 in range(0, s, CONV_ROWS):
            slot = n_unit % TOKEN_CHAINS
            held = tok[slot][0:1]
            taps = [jnp.where(never, held, tap) for tap in half_taps]
            out, token = _conv_unit(cbuf, r0, cl, taps, shift_masks, q_scale, is_v)
            pc_ref[0, r0:r0 + CONV_ROWS, cl] = out
            tok[slot] = token
            n_unit += 1


def _inproj(x, mod3, norm_w, w_conv, w_plain, w_small, conv_w):
    b, s, d = x.shape
    conv_width, plain_width = w_conv.shape[0], w_plain.shape[0]
    n_steps = conv_width // CONV_SUB
    tp = plain_width // n_steps
    assert n_steps % 3 == 0 and n_steps * CONV_SUB == conv_width and conv_w.shape[1] == conv_width
    assert tp * n_steps == plain_width and tp % LANES == 0 and s % CONV_ROWS == 0
    kern = functools.partial(_inproj_kernel, steps_per_seg=n_steps // 3)
    return pl.pallas_call(
        kern,
        grid=(b, n_steps),
        in_specs=[pl.BlockSpec((1, s, d), lambda bi, j: (bi, 0, 0)),
                  pl.BlockSpec((1, 3, d), lambda bi, j: (bi, 0, 0)),
                  pl.BlockSpec((1, d), lambda bi, j: (0, 0)),
                  pl.BlockSpec((CONV_SUB, d), lambda bi, j: (j, 0)),
                  pl.BlockSpec((tp, d), lambda bi, j: (j, 0)),
                  pl.BlockSpec((LANES, d), lambda bi, j: (0, 0)),
                  pl.BlockSpec((CONV_K, CONV_SUB), lambda bi, j: (0, j))],
        out_specs=[pl.BlockSpec((1, s, CONV_SUB), lambda bi, j: (bi, 0, j)),
                   pl.BlockSpec((1, s, tp), lambda bi, j: (bi, 0, j)),
                   pl.BlockSpec((1, s, LANES), lambda bi, j: (bi, 0, 0))],
        out_shape=[jax.ShapeDtypeStruct((b, s, conv_width), BF16),
                   jax.ShapeDtypeStruct((b, s, plain_width), BF16),
                   jax.ShapeDtypeStruct((b, s, LANES), F32)],
        scratch_shapes=[pltpu.VMEM((s, d), BF16),
                        pltpu.VMEM((HALO + s, CONV_SUB), F32),
                        pltpu.VMEM((TOKEN_CHAINS, SUBLANES, LANES), F32)],
        compiler_params=pltpu.CompilerParams(
            dimension_semantics=("parallel", "arbitrary"), vmem_limit_bytes=VMEM_LIMIT),
        name="norm_inproj",
    )(x, mod3, norm_w.reshape(1, d), w_conv, w_plain, w_small, conv_w)


def _level_mask(level):
    i = lax.broadcasted_iota(jnp.int32, (CHUNK, CHUNK), 0)
    j = lax.broadcasted_iota(jnp.int32, (CHUNK, CHUNK), 1)
    same_block = (i >> (level + 1)) == (j >> (level + 1))
    return same_block & (((i >> level) & 1) == 1) & (((j >> level) & 1) == 0)


N_LEVELS = CHUNK.bit_length() - 1


def _deltanet_kernel(q_ref, k_ref, v_ref, small_ref, alog_ref, dtb_ref, o_ref,
                     state, u_s, wq_s, akd_s, el_s, *, ts, group):
    @pl.when(pl.program_id(1) == 0)
    def _():
        state[...] = jnp.zeros_like(state)

    row = lax.broadcasted_iota(jnp.int32, (CHUNK, CHUNK), 0)
    col = lax.broadcasted_iota(jnp.int32, (CHUNK, CHUNK), 1)
    causal = row >= col
    tri = causal.astype(F32)
    tri_strict = (row > col).astype(F32)
    neg_a = -jnp.exp(alog_ref[...])
    dtb = dtb_ref[...]

    eye = (row == col).astype(F32)
    level_masks = [_level_mask(level).astype(F32) for level in range(N_LEVELS)]
    n_chunks = ts // CHUNK

    def prep_stages(gi):
        chains = []
        for ci in range(group):
            c = gi * group + ci
            r0 = c * CHUNK if isinstance(c, int) else pl.multiple_of(c * CHUNK, CHUNK)
            sm = small_ref[0, pl.ds(r0, CHUNK), :]
            beta_all = _sigmoid(sm)
            z = sm + dtb
            softplus = jnp.maximum(z, 0.0) + jnp.log(1.0 + jnp.exp(-jnp.abs(z)))
            g_all = neg_a * softplus
            gc_all = jnp.dot(tri, g_all, preferred_element_type=F32, precision=HIGHEST)
            gc_t = gc_all.T
            g_last = gc_all[CHUNK - 1:CHUNK, :]
            e_in_all = jnp.exp(gc_all)
            e_out_all = jnp.exp(g_last - gc_all)
            el_s[c] = jnp.broadcast_to(jnp.exp(g_last), (8, LANES))
            for h in range(DN_HEADS):
                gl = DN_HEADS + h
                cq = slice(h * DN_DK, (h + 1) * DN_DK)
                q = q_ref[0, pl.ds(r0, CHUNK), cq].astype(F32)
                k = k_ref[0, pl.ds(r0, CHUNK), cq].astype(F32)
                v = v_ref[0, pl.ds(r0, CHUNK), slice(h * DN_DV, (h + 1) * DN_DV)].astype(F32)
                beta = beta_all[:, h:h + 1]
                e_in = e_in_all[:, gl:gl + 1]
                k_beta = k * beta
                decay = jnp.exp(jnp.where(causal, gc_all[:, gl:gl + 1] - gc_t[gl:gl + 1, :], -jnp.inf))
                chains.append(dict(
                    c=c, h=h, r0=r0, decay=decay,
                    kq=jnp.concatenate([k_beta, q], axis=0).astype(BF16), k=k.astype(BF16),
                    rhs=jnp.concatenate([v * beta, k_beta * e_in], axis=1).astype(BF16),
                    q_dec=(q * e_in).astype(BF16),
                    k_dec_t=(k * e_out_all[:, gl:gl + 1]).T.astype(BF16)))

        for ch in chains:
            ch["kk_qk"] = _mm_nt(ch["kq"], ch["k"])
        yield
        for ch in chains:
            kk_qk = ch.pop("kk_qk")
            decay = ch.pop("decay")
            ch["lmat"] = kk_qk[0:CHUNK] * decay * tri_strict
            ch["a_qk"] = (kk_qk[CHUNK:2 * CHUNK] * decay).astype(BF16)
            ch["x"] = eye - ch["lmat"] * level_masks[0]
        for level in range(1, N_LEVELS):
            for ch in chains:
                ch["y"] = _mm(ch["lmat"] * level_masks[level], ch["x"])
            yield
            for ch in chains:
                ch["x"] = ch["x"] - _mm(ch["x"], ch.pop("y"))
            yield
        for ch in chains:
            ch["uw"] = _mm(ch.pop("x"), ch.pop("rhs"))
        for ch in chains:
            c, h, r0 = ch["c"], ch["h"], ch["r0"]
            uw = ch.pop("uw")
            u_s[pl.ds(r0, CHUNK), slice(h * DN_DV, (h + 1) * DN_DV)] = uw[:, 0:DN_DV]
            wq_s[c, h] = jnp.concatenate([uw[:, DN_DV:DN_DV + DN_DK].astype(BF16), ch["q_dec"]], axis=0)
            akd_s[c, h] = jnp.concatenate([ch["a_qk"], ch["k_dec_t"]], axis=0)
        yield

    def rec_stages(gi):
        for ci in range(group):
            c = gi * group + ci
            r0 = c * CHUNK if isinstance(c, int) else pl.multiple_of(c * CHUNK, CHUNK)
            e_last_all = el_s[c]
            rs = [_mm(wq_s[c, h], state[h]) for h in range(DN_HEADS)]
            yield
            v_new = [u_s[pl.ds(r0, CHUNK), slice(h * DN_DV, (h + 1) * DN_DV)] - rs[h][0:CHUNK]
                     for h in range(DN_HEADS)]
            av = [_mm(akd_s[c, h], v_new[h]) for h in range(DN_HEADS)]
            for h in range(DN_HEADS):
                gl = DN_HEADS + h
                o = rs[h][CHUNK:2 * CHUNK] + av[h][0:CHUNK]
                state[h] = state[h] * e_last_all[0:1, gl:gl + 1] + av[h][CHUNK:CHUNK + DN_DK]
                o_ref[0, pl.ds(r0, CHUNK), slice(h * DN_DV, (h + 1) * DN_DV)] = o.astype(o_ref.dtype)
            yield

    def run(*gens):
        gens = list(gens)
        while gens:
            for g in list(gens):
                if next(g, StopIteration) is StopIteration:
                    gens.remove(g)

    n_groups = n_chunks // group
    run(prep_stages(0))

    def both(gi, carry):
        run(prep_stages(gi), rec_stages(gi - 1))
        return carry

    lax.fori_loop(1, n_groups, both, 0)
    run(rec_stages(n_groups - 1))


def _deltanet(p, small, alog_row, dtb_row, ts, group):
    b, s, _ = p.shape
    width = DN_HEADS * DN_DK
    n_chunks = ts // CHUNK
    kern = functools.partial(_deltanet_kernel, ts=ts, group=group)
    return pl.pallas_call(
        kern,
        grid=(b, s // ts),
        in_specs=[pl.BlockSpec((1, ts, width), lambda bi, t: (bi, t, 0)),
                  pl.BlockSpec((1, ts, width), lambda bi, t: (bi, t, 1)),
                  pl.BlockSpec((1, ts, width), lambda bi, t: (bi, t, 2)),
                  pl.BlockSpec((1, ts, LANES), lambda bi, t: (bi, t, 0)),
                  pl.BlockSpec((1, LANES), lambda bi, t: (0, 0)),
                  pl.BlockSpec((1, LANES), lambda bi, t: (0, 0))],
        out_specs=pl.BlockSpec((1, ts, DN_WIDTH), lambda bi, t: (bi, t, 0)),
        out_shape=jax.ShapeDtypeStruct((b, s, DN_WIDTH), BF16),
        scratch_shapes=[pltpu.VMEM((DN_HEADS, DN_DK, DN_DV), F32),
                        pltpu.VMEM((ts, DN_WIDTH), F32),
                        pltpu.VMEM((n_chunks, DN_HEADS, 2 * CHUNK, DN_DK), BF16),
                        pltpu.VMEM((n_chunks, DN_HEADS, CHUNK + DN_DK, CHUNK), BF16),
                        pltpu.VMEM((n_chunks, 8, LANES), F32)],
        compiler_params=pltpu.CompilerParams(
            dimension_semantics=("parallel", "arbitrary"), vmem_limit_bytes=VMEM_LIMIT),
        name="gated_deltanet",
    )(p, p, p, small, alog_row, dtb_row)


ATTN_BLOCKS_PER_ITER = 4
ATTN_STRIDE = 4


def _attn_kernel(q_ref, k_ref, v_ref, o_ref, lse_ref, *scratch, dil, n_back):
    s = q_ref.shape[1]
    sub_len = s // dil
    nb = sub_len // ATTN_BLOCK
    n_pairs = ATTN_GROUP_WIDTH // LANES
    prev_block = nb > 1

    def lanes(pair):
        return slice(pair * LANES, (pair + 1) * LANES)

    if dil > 1:
        stage, tmp, qs, ks, vs, os_, ls = scratch
        assert dil in (ATTN_STRIDE, ATTN_STRIDE * ATTN_STRIDE)
        q_len = s // ATTN_STRIDE

        def classes():
            if dil == ATTN_STRIDE:
                return [(r, stage, r) for r in range(dil)]
            return [(r_lo + ATTN_STRIDE * r_hi, tmp, r_lo * q_len + r_hi)
                    for r_lo in range(ATTN_STRIDE) for r_hi in range(ATTN_STRIDE)]

        def deinterleave(dst, pair):
            if dil > ATTN_STRIDE:
                for r_lo in range(ATTN_STRIDE):
                    tmp[r_lo * q_len:(r_lo + 1) * q_len, :] = stage[pl.ds(r_lo, q_len, stride=ATTN_STRIDE), :]
            for r, buf, start in classes():
                dst[pair, r * sub_len:(r + 1) * sub_len, :] = (
                    buf[pl.ds(start, sub_len, stride=ATTN_STRIDE), :].astype(dst.dtype))

        def interleave(src, pair):
            for r, buf, start in classes():
                buf[pl.ds(start, sub_len, stride=ATTN_STRIDE), :] = src[pair, r * sub_len:(r + 1) * sub_len, :]
            if dil > ATTN_STRIDE:
                for r_lo in range(ATTN_STRIDE):
                    stage[pl.ds(r_lo, q_len, stride=ATTN_STRIDE), :] = tmp[r_lo * q_len:(r_lo + 1) * q_len, :]

        for src, dst in ((q_ref, qs), (k_ref, ks), (v_ref, vs)):
            for pair in range(n_pairs):
                stage[...] = src[0, :, lanes(pair)].astype(F32)
                deinterleave(dst, pair)

        def load(ref, src, pair, r0):
            return src[pair, pl.ds(r0, ATTN_BLOCK), :]
    else:
        qs, ks, vs = q_ref, k_ref, v_ref

        def load(ref, src, pair, r0):
            return ref[0, pl.ds(r0, ATTN_BLOCK), lanes(pair)]

    n_keys = 2 * ATTN_BLOCK if prev_block else ATTN_BLOCK
    qi = lax.broadcasted_iota(jnp.int32, (ATTN_BLOCK, n_keys), 0)
    ki = lax.broadcasted_iota(jnp.int32, (ATTN_BLOCK, n_keys), 1)
    dist = qi + (n_keys - ATTN_BLOCK) - ki
    band = (dist >= 0) & (dist <= n_back)
    bias_full = jnp.where(band, 0.0, -jnp.inf).astype(F32)
    bias_first = jnp.where(band & (ki >= ATTN_BLOCK), 0.0, -jnp.inf).astype(F32)
    lane = lax.broadcasted_iota(jnp.int32, (ATTN_BLOCK, LANES), 1)
    low_half = lane < ATTN_HEAD_DIM
    scale = ATTN_HEAD_DIM ** -0.5

    def block_body(fi, carry):
        chains = []
        for bi in range(ATTN_BLOCKS_PER_ITER):
            f = fi * ATTN_BLOCKS_PER_ITER + bi
            q0 = pl.multiple_of(f * ATTN_BLOCK, ATTN_BLOCK)
            if prev_block:
                k0 = pl.multiple_of(jnp.maximum(f - 1, 0) * ATTN_BLOCK, ATTN_BLOCK)
                bias = jnp.where(f % nb > 0, bias_full, bias_first)
            else:
                bias = bias_full
            for pair in range(n_pairs):
                qp = load(q_ref, qs, pair, q0) * scale
                kp = load(k_ref, ks, pair, q0)
                vp = load(v_ref, vs, pair, q0)
                if prev_block:
                    kp = jnp.concatenate([load(k_ref, ks, pair, k0), kp], axis=0)
                    vp = jnp.concatenate([load(v_ref, vs, pair, k0), vp], axis=0)
                for half in range(2):
                    sel = low_half if half == 0 else ~low_half
                    chains.append(dict(q0=q0, pair=pair, half=half, bias=bias, kp=kp, vp=vp,
                                       qm=jnp.where(sel, qp, jnp.zeros_like(qp))))
        for ch in chains:
            ch["sc"] = _mm_nt(ch.pop("qm"), ch.pop("kp"))
        for ch in chains:
            sc = ch.pop("sc") + ch.pop("bias")
            m = jnp.max(sc, axis=-1, keepdims=True)
            p = jnp.exp(sc - m)
            ch["denom"] = jnp.sum(p, axis=-1, keepdims=True)
            ch["m"] = m
            ch["p"] = p.astype(BF16)
        for ch in chains:
            ch["pv"] = _mm(ch.pop("p"), ch.pop("vp"))
        for c0 in range(0, len(chains), 2):
            lo, hi = chains[c0], chains[c0 + 1]
            o = jnp.where(low_half, lo["pv"] / lo["denom"], hi["pv"] / hi["denom"])
            lse = jnp.where(low_half, lo["m"] + jnp.log(lo["denom"]), hi["m"] + jnp.log(hi["denom"]))
            q0, pair = lo["q0"], lo["pair"]
            if dil > 1:
                os_[pair, pl.ds(q0, ATTN_BLOCK), :] = o
                ls[pair, pl.ds(q0, ATTN_BLOCK), :] = lse
            else:
                o_ref[0, pl.ds(q0, ATTN_BLOCK), lanes(pair)] = o.astype(o_ref.dtype)
                lse_ref[0, pl.ds(q0, ATTN_BLOCK), lanes(pair)] = lse
        return carry

    lax.fori_loop(0, dil * nb // ATTN_BLOCKS_PER_ITER, block_body, 0)

    if dil > 1:
        for pair in range(n_pairs):
            for src, dst in ((os_, o_ref), (ls, lse_ref)):
                interleave(src, pair)
                dst[0, :, lanes(pair)] = stage[...].astype(dst.dtype)


def _dilated_attention(p, group, col_base):
    b, s, _ = p.shape
    window, dil = DIL_CONFIGS[group]
    gw = ATTN_GROUP_WIDTH
    blk = col_base // gw + group
    step = ATTN_WIDTH // gw
    kern = functools.partial(_attn_kernel, dil=dil, n_back=window // dil)
    return pl.pallas_call(
        kern,
        grid=(b,),
        in_specs=[pl.BlockSpec((1, s, gw), lambda bi: (bi, 0, blk)),
                  pl.BlockSpec((1, s, gw), lambda bi: (bi, 0, blk + step)),
                  pl.BlockSpec((1, s, gw), lambda bi: (bi, 0, blk + 2 * step))],
        out_specs=[pl.BlockSpec((1, s, gw), lambda bi: (bi, 0, 0)),
                   pl.BlockSpec((1, s, gw), lambda bi: (bi, 0, 0))],
        out_shape=[jax.ShapeDtypeStruct((b, s, gw), BF16),
                   jax.ShapeDtypeStruct((b, s, gw), F32)],
        scratch_shapes=([] if dil == 1 else
                        [pltpu.VMEM((s, LANES), F32) for _ in range(2)]
                        + [pltpu.VMEM((gw // LANES, s, LANES), BF16) for _ in range(3)]
                        + [pltpu.VMEM((gw // LANES, s, LANES), F32) for _ in range(2)]),
        compiler_params=pltpu.CompilerParams(
            dimension_semantics=("parallel",), vmem_limit_bytes=VMEM_LIMIT),
        name=f"dilated_attn_g{group}",
    )(p, p, p)


def _merge_kernel(x_ref, mod_ref, oa_ref, za_ref, ga_ref, gb_ref, zb_ref,
                  ob0_ref, ob1_ref, ob2_ref, l0_ref, l1_ref, l2_ref,
                  dnw_ref, pa_ref, pb_ref, wo_ref, fw_ref, out_ref):
    def silu_of_half(h):
        return h + h * jnp.tanh(h)

    oa = oa_ref[0].astype(F32)
    za_half = za_ref[0].astype(F32)
    parts = []
    for h in range(DN_HEADS):
        cq = slice(h * DN_DV, (h + 1) * DN_DV)
        blk = oa[:, cq]
        y = blk * lax.rsqrt(jnp.mean(blk * blk, axis=-1, keepdims=True) + EPS) * dnw_ref[...]
        parts.append(y * silu_of_half(za_half[:, cq]))
    ya_half = _mm(jnp.concatenate(parts, axis=1), pa_ref[...])

    l0, l1, l2 = l0_ref[0], l1_ref[0], l2_ref[0]
    m = jnp.maximum(jnp.maximum(l0, l1), l2)
    e0, e1, e2 = jnp.exp(l0 - m), jnp.exp(l1 - m), jnp.exp(l2 - m)
    inv_den = 1.0 / (e0 + e1 + e2)
    ob = jnp.concatenate([ob0_ref[0].astype(F32) * (e0 * inv_den),
                          ob1_ref[0].astype(F32) * (e1 * inv_den),
                          ob2_ref[0].astype(F32) * (e2 * inv_den)], axis=1)
    yb_half = _mm(ob * silu_of_half(zb_ref[0].astype(F32)), pb_ref[...])

    merged = (ya_half + ya_half * jnp.tanh(ga_ref[0].astype(F32))
              + (yb_half + yb_half * jnp.tanh(gb_ref[0].astype(F32))))
    gate = mod_ref[0][2:3]
    xo = x_ref[0] + gate * _mm(merged, wo_ref[...])
    out_ref[0] = xo * lax.rsqrt(jnp.mean(xo * xo, axis=-1, keepdims=True) + EPS) * fw_ref[...]


def _merge(x, mod3, o_a, p, obs, lses, dn_norm_w, pa, pb, wo, final_w, tm, cols):
    b, s, d = x.shape
    gw = ATTN_GROUP_WIDTH

    def row_spec(width, blk):
        return pl.BlockSpec((1, tm, width), lambda bi, i: (bi, i, blk))

    def full_spec(shape):
        return pl.BlockSpec(shape, lambda bi, i: (0,) * len(shape))

    return pl.pallas_call(
        _merge_kernel,
        grid=(b, s // tm),
        in_specs=[row_spec(d, 0),
                  pl.BlockSpec((1, 3, d), lambda bi, i: (bi, 0, 0)),
                  row_spec(DN_WIDTH, 0),
                  row_spec(DN_WIDTH, cols["za"] // DN_WIDTH),
                  row_spec(d, cols["ga"] // d),
                  row_spec(d, cols["gb"] // d),
                  row_spec(ATTN_WIDTH, cols["zb"] // ATTN_WIDTH),
                  row_spec(gw, 0), row_spec(gw, 0), row_spec(gw, 0),
                  row_spec(gw, 0), row_spec(gw, 0), row_spec(gw, 0),
                  full_spec((1, DN_DV)), full_spec(pa.shape), full_spec(pb.shape), full_spec(wo.shape),
                  full_spec((1, d))],
        out_specs=row_spec(d, 0),
        out_shape=jax.ShapeDtypeStruct((b, s, d), F32),
        compiler_params=pltpu.CompilerParams(
            dimension_semantics=("parallel", "parallel"), vmem_limit_bytes=VMEM_LIMIT),
        name="merge_outproj",
    )(x, mod3, o_a, p, p, p, p, *obs, *lses, dn_norm_w.reshape(1, DN_DV), pa, pb, wo, final_w.reshape(1, d))


W_PREP_COLS = 128


def _wprep_kernel(wt_ref, wc_ref, wp_ref, ws_ref, *, conv_width, segments, small_off):
    wc_ref[...] = wt_ref[0:conv_width, :].astype(BF16)
    for src, width, dst, scale in segments:
        blk = wt_ref[src:src + width, :]
        wp_ref[dst:dst + width, :] = (blk if scale == 1.0 else blk * scale).astype(BF16)
    n_small = 2 * DN_HEADS
    ws_ref[0:n_small, :] = wt_ref[small_off:small_off + n_small, :].astype(BF16)
    ws_ref[n_small:LANES, :] = jnp.zeros((LANES - n_small, wt_ref.shape[1]), BF16)


def _weight_prep(wt, conv_width, plain_width, segments, small_off):
    n_in, d = wt.shape
    assert d % W_PREP_COLS == 0 and small_off % 16 == 0 and all(sg[0] % 16 == 0 for sg in segments)
    kern = functools.partial(_wprep_kernel, conv_width=conv_width, segments=segments, small_off=small_off)
    return pl.pallas_call(
        kern,
        grid=(d // W_PREP_COLS,),
        in_specs=[pl.BlockSpec((n_in, W_PREP_COLS), lambda i: (0, i))],
        out_specs=[pl.BlockSpec((conv_width, W_PREP_COLS), lambda i: (0, i)),
                   pl.BlockSpec((plain_width, W_PREP_COLS), lambda i: (0, i)),
                   pl.BlockSpec((LANES, W_PREP_COLS), lambda i: (0, i))],
        out_shape=[jax.ShapeDtypeStruct((conv_width, d), BF16),
                   jax.ShapeDtypeStruct((plain_width, d), BF16),
                   jax.ShapeDtypeStruct((LANES, d), BF16)],
        compiler_params=pltpu.CompilerParams(dimension_semantics=("parallel",), vmem_limit_bytes=VMEM_LIMIT),
        name="weight_prep",
    )(wt)


def _layer(x, c, norm_w, ada_w, ada_b, w_in, conv_w, a_log, dt_bias, dn_norm_w, w_proj_a, w_proj_b, w_out,
           final_norm_w):
    b, s, d = x.shape
    qk = DN_HEADS * DN_DK
    splits = (qk, qk, DN_WIDTH, DN_WIDTH, DN_HEADS, DN_HEADS, ATTN_WIDTH, ATTN_WIDTH, ATTN_WIDTH, ATTN_WIDTH, d, d)
    names = ("qa", "ka", "va", "za", "beta", "a", "qb", "kb", "vb", "zb", "ga", "gb")
    offs = dict(zip(names, np.cumsum((0,) + splits[:-1]).tolist()))
    widths = dict(zip(names, splits))
    order = ("za", "ga", "gb", "qb", "kb", "vb", "zb")
    cols, pos = {}, 0
    for n in order:
        assert pos % widths[n] == 0, (n, pos)
        cols[n] = pos
        pos += widths[n]
    assert offs["qa"] == 0 and offs["ka"] == qk and offs["va"] == 2 * qk and offs["a"] == offs["beta"] + DN_HEADS
    halved = ("za", "ga", "gb", "zb")
    segments = tuple((offs[n], widths[n], cols[n], 0.5 if n in halved else 1.0) for n in order)
    w_conv, w_plain, w_small = _weight_prep(w_in.T, 2 * qk + DN_WIDTH, pos, segments, offs["beta"])
    pad_lo = jnp.zeros((DN_HEADS,), F32)
    pad_hi = jnp.zeros((LANES - 2 * DN_HEADS,), F32)
    alog_row = jnp.concatenate([pad_lo, a_log.astype(F32), pad_hi]).reshape(1, LANES)
    dtb_row = jnp.concatenate([pad_lo, dt_bias.astype(F32), pad_hi]).reshape(1, LANES)

    mod3 = _adaln_mod(c, ada_w, ada_b).reshape(b, 3, d)
    p_conv, p, small = _inproj(x, mod3, norm_w, w_conv, w_plain, w_small, conv_w)
    o_a = _deltanet(p_conv, small, alog_row, dtb_row, ts=min(s, 1024), group=4)
    obs, lses = [], []
    for g in range(len(DIL_CONFIGS)):
        o_g, lse_g = _dilated_attention(p, g, cols["qb"])
        obs.append(o_g)
        lses.append(lse_g)
    return _merge(x, mod3, o_a, p, obs, lses, dn_norm_w, (w_proj_a * 0.5).astype(BF16),
                  (w_proj_b * 0.5).astype(BF16), w_out.astype(BF16), final_norm_w, tm=min(s, 256), cols=cols)


def kernel(x, c, norm_w, ada_w, ada_b, w_in, conv_w, a_log, dt_bias, dn_norm_w, w_proj_a, w_proj_b, w_out,
           final_norm_w):
    depth = norm_w.shape[0]
    assert depth == 1, "the final RMSNorm is fused into the single layer's output kernel"
    return _layer(x, c, norm_w[0], ada_w[0], ada_b[0], w_in[0], conv_w[0], a_log[0], dt_bias[0], dn_norm_w[0],
                  w_proj_a[0], w_proj_b[0], w_out[0], final_norm_w)
```

```python
import functools

import jax
import jax.numpy as jnp
import numpy as np
from jax import lax
from jax.experimental import pallas as pl
from jax.experimental.pallas import tpu as pltpu

F32 = jnp.float32
BF16 = jnp.bfloat16
HIGHEST = lax.Precision.HIGHEST

EPS = 1e-6
DN_HEADS = 8
DN_DK = 128
DN_DV = 128
DN_WIDTH = DN_HEADS * DN_DV
CONV_K = 4
CHUNK = 64
DIL_CONFIGS = ((128, 1), (512, 4), (2048, 16))
HEADS_PER_GROUP = 4
ATTN_HEAD_DIM = 64
ATTN_GROUP_WIDTH = HEADS_PER_GROUP * ATTN_HEAD_DIM
ATTN_WIDTH = ATTN_GROUP_WIDTH * len(DIL_CONFIGS)
ATTN_BLOCK = 128
LANES = 128
HALO = 8
VMEM_LIMIT = 56 * 1024 * 1024


def _mm(a, b):
    return jnp.dot(a.astype(BF16), b.astype(BF16), preferred_element_type=F32)


def _mm_nt(a, b):
    return lax.dot_general(a.astype(BF16), b.astype(BF16), (((1,), (1,)), ((), ())),
                           preferred_element_type=F32)


def _mm_tn(a, b):
    return lax.dot_general(a.astype(BF16), b.astype(BF16), (((0,), (0,)), ((), ())),
                           preferred_element_type=F32)


def _run_round_robin(*gens):
    gens = list(gens)
    while gens:
        for g in list(gens):
            if next(g, StopIteration) is StopIteration:
                gens.remove(g)


def _sigmoid(x):
    return 0.5 * jnp.tanh(0.5 * x) + 0.5


def _silu(x):
    return x * _sigmoid(x)


def _mod_kernel(c_ref, w_ref, b_ref, o_ref):
    sc = _silu(c_ref[...])
    o_ref[...] = jnp.dot(sc, w_ref[...], preferred_element_type=F32, precision=HIGHEST) + b_ref[...]


def _adaln_mod(c, ada_w, ada_b):
    b, d = c.shape
    return pl.pallas_call(
        _mod_kernel,
        grid=(3,),
        in_specs=[pl.BlockSpec((b, d), lambda j: (0, 0)),
                  pl.BlockSpec((d, d), lambda j: (0, j)),
                  pl.BlockSpec((1, d), lambda j: (0, j))],
        out_specs=pl.BlockSpec((b, d), lambda j: (0, j)),
        out_shape=jax.ShapeDtypeStruct((b, 3 * d), F32),
        name="adaln_mod",
    )(c, ada_w, ada_b.reshape(1, 3 * d))


CONV_ROWS = 64
CONV_SUB = 256
SUBLANES = 8
TOKEN_CHAINS = 3
TOKEN_POINT = "ss"


def _conv_unit(buf, r0, cl, half_taps, shift_masks, norm_scale, skip_norm):
    a = buf[r0:r0 + HALO + CONV_ROWS, cl]
    n = CONV_ROWS // SUBLANES
    vs = [a[SUBLANES * i:SUBLANES * (i + 1)] for i in range(n + 1)]
    acc = [vs[i + 1] * half_taps[CONV_K - 1] for i in range(n)]
    for shift in range(1, CONV_K):
        rs = [pltpu.roll(v, shift, 0) for v in vs]
        tap = half_taps[CONV_K - 1 - shift]
        for i in range(n):
            acc[i] = acc[i] + jnp.where(shift_masks[shift], rs[i], rs[i + 1]) * tap
    h = jnp.concatenate(acc, axis=0)
    y = h + h * jnp.tanh(h)
    ss = jnp.sum(y * y, axis=-1, keepdims=True)
    norm = lax.rsqrt(ss + EPS) * norm_scale
    out = y * jnp.where(skip_norm, 1.0, norm)
    token = {"silu": y, "ss": jnp.broadcast_to(ss, y.shape), "out": out}[TOKEN_POINT][CONV_ROWS - SUBLANES:]
    return out.astype(BF16), token


def _inproj_kernel(x_ref, mod_ref, nw_ref, wc_ref, wp_ref, ws_ref, cw_ref, pc_ref, pp_ref, small_ref,
                   h_scr, cbuf, tok, *, steps_per_seg):
    j = pl.program_id(1)
    s = x_ref.shape[1]

    @pl.when(j == 0)
    def _():
        x = x_ref[0]
        m = mod_ref[0]
        y = x * lax.rsqrt(jnp.mean(x * x, axis=-1, keepdims=True) + EPS) * nw_ref[...]
        h = (y * (1.0 + m[1:2]) + m[0:1]).astype(BF16)
        h_scr[...] = h
        small_ref[0] = _mm_nt(h, ws_ref[...])
        cbuf[0:HALO, :] = jnp.zeros((HALO, CONV_SUB), F32)

    cbuf[HALO:HALO + s, :] = _mm_nt(h_scr[...], wc_ref[...])
    pp_ref[0] = _mm_nt(h_scr[...], wp_ref[...]).astype(BF16)

    sub = lax.broadcasted_iota(jnp.int32, (SUBLANES, LANES), 0)
    shift_masks = [sub < shift for shift in range(CONV_K)]
    q_scale = jnp.where(j < steps_per_seg, DN_DK ** -0.5, 1.0)
    is_v = j >= 2 * steps_per_seg
    never = j < 0
    for c in range(TOKEN_CHAINS):
        tok[c] = jnp.zeros((SUBLANES, LANES), F32)
    n_unit = 0
    for c0 in range(0, CONV_SUB, DN_DK):
        cl = slice(c0, c0 + DN_DK)
        half_taps = [cw_ref[t:t + 1, cl] * 0.5 for t in range(CONV_K)]
        for r0 in range(0, s, CONV_ROWS):
            slot = n_unit % TOKEN_CHAINS
            held = tok[slot][0:1]
            taps = [jnp.where(never, held, tap) for tap in half_taps]
            out, token = _conv_unit(cbuf, r0, cl, taps, shift_masks, q_scale, is_v)
            pc_ref[0, r0:r0 + CONV_ROWS, cl] = out
            tok[slot] = token
            n_unit += 1


def _inproj(x, mod3, norm_w, w_conv, w_plain, w_small, conv_w):
    b, s, d = x.shape
    conv_width, plain_width = w_conv.shape[0], w_plain.shape[0]
    n_steps = conv_width // CONV_SUB
    tp = plain_width // n_steps
    assert n_steps % 3 == 0 and n_steps * CONV_SUB == conv_width and conv_w.shape[1] == conv_width
    assert tp * n_steps == plain_width and tp % LANES == 0 and s % CONV_ROWS == 0
    kern = functools.partial(_inproj_kernel, steps_per_seg=n_steps // 3)
    return pl.pallas_call(
        kern,
        grid=(b, n_steps),
        in_specs=[pl.BlockSpec((1, s, d), lambda bi, j: (bi, 0, 0)),
                  pl.BlockSpec((1, 3, d), lambda bi, j: (bi, 0, 0)),
                  pl.BlockSpec((1, d), lambda bi, j: (0, 0)),
                  pl.BlockSpec((CONV_SUB, d), lambda bi, j: (j, 0)),
                  pl.BlockSpec((tp, d), lambda bi, j: (j, 0)),
                  pl.BlockSpec((LANES, d), lambda bi, j: (0, 0)),
                  pl.BlockSpec((CONV_K, CONV_SUB), lambda bi, j: (0, j))],
        out_specs=[pl.BlockSpec((1, s, CONV_SUB), lambda bi, j: (bi, 0, j)),
                   pl.BlockSpec((1, s, tp), lambda bi, j: (bi, 0, j)),
                   pl.BlockSpec((1, s, LANES), lambda bi, j: (bi, 0, 0))],
        out_shape=[jax.ShapeDtypeStruct((b, s, conv_width), BF16),
                   jax.ShapeDtypeStruct((b, s, plain_width), BF16),
                   jax.ShapeDtypeStruct((b, s, LANES), F32)],
        scratch_shapes=[pltpu.VMEM((s, d), BF16),
                        pltpu.VMEM((HALO + s, CONV_SUB), F32),
                        pltpu.VMEM((TOKEN_CHAINS, SUBLANES, LANES), F32)],
        compiler_params=pltpu.CompilerParams(
            dimension_semantics=("parallel", "arbitrary"), vmem_limit_bytes=VMEM_LIMIT),
        name="norm_inproj",
    )(x, mod3, norm_w.reshape(1, d), w_conv, w_plain, w_small, conv_w)


def _level_mask(level):
    i = lax.broadcasted_iota(jnp.int32, (CHUNK, CHUNK), 0)
    j = lax.broadcasted_iota(jnp.int32, (CHUNK, CHUNK), 1)
    same_block = (i >> (level + 1)) == (j >> (level + 1))
    return same_block & (((i >> level) & 1) == 1) & (((j >> level) & 1) == 0)


N_LEVELS = CHUNK.bit_length() - 1


def _deltanet_kernel(q_ref, k_ref, v_ref, small_ref, alog_ref, dtb_ref, o_ref,
                     state, u_s, wq_s, akd_s, el_s, *, ts, group):
    @pl.when(pl.program_id(1) == 0)
    def _():
        state[...] = jnp.zeros_like(state)

    row = lax.broadcasted_iota(jnp.int32, (CHUNK, CHUNK), 0)
    col = lax.broadcasted_iota(jnp.int32, (CHUNK, CHUNK), 1)
    causal = row >= col
    tri = causal.astype(F32)
    tri_strict = (row > col).astype(F32)
    neg_a = -jnp.exp(alog_ref[...])
    dtb = dtb_ref[...]

    eye = (row == col).astype(F32)
    level_masks = [_level_mask(level).astype(F32) for level in range(N_LEVELS)]
    n_chunks = ts // CHUNK

    def prep_stages(gi):
        chains = []
        for ci in range(group):
            c = gi * group + ci
            r0 = c * CHUNK if isinstance(c, int) else pl.multiple_of(c * CHUNK, CHUNK)
            sm = small_ref[0, pl.ds(r0, CHUNK), :]
            beta_all = _sigmoid(sm)
            z = sm + dtb
            softplus = jnp.maximum(z, 0.0) + jnp.log(1.0 + jnp.exp(-jnp.abs(z)))
            g_all = neg_a * softplus
            gc_all = jnp.dot(tri, g_all, preferred_element_type=F32, precision=HIGHEST)
            gc_t = gc_all.T
            g_last = gc_all[CHUNK - 1:CHUNK, :]
            e_in_all = jnp.exp(gc_all)
            e_out_all = jnp.exp(g_last - gc_all)
            slot = (gi % 2) * group + ci
            el_s[slot] = jnp.broadcast_to(jnp.exp(g_last), (8, LANES))
            for h in range(DN_HEADS):
                gl = DN_HEADS + h
                cq = slice(h * DN_DK, (h + 1) * DN_DK)
                q = q_ref[0, pl.ds(r0, CHUNK), cq].astype(F32)
                k = k_ref[0, pl.ds(r0, CHUNK), cq].astype(F32)
                v = v_ref[0, pl.ds(r0, CHUNK), slice(h * DN_DV, (h + 1) * DN_DV)].astype(F32)
                beta = beta_all[:, h:h + 1]
                e_in = e_in_all[:, gl:gl + 1]
                k_beta = k * beta
                decay = jnp.exp(jnp.where(causal, gc_all[:, gl:gl + 1] - gc_t[gl:gl + 1, :], -jnp.inf))
                chains.append(dict(
                    slot=slot, h=h, decay=decay,
                    kq=jnp.concatenate([k_beta, q], axis=0).astype(BF16), k=k.astype(BF16),
                    rhs=jnp.concatenate([v * beta, k_beta * e_in], axis=1).astype(BF16),
                    q_dec=(q * e_in).astype(BF16),
                    k_dec_t=(k * e_out_all[:, gl:gl + 1]).T.astype(BF16)))

        for ch in chains:
            ch["kk_qk"] = _mm_nt(ch["kq"], ch["k"])
        yield
        for ch in chains:
            kk_qk = ch.pop("kk_qk")
            decay = ch.pop("decay")
            ch["lmat"] = kk_qk[0:CHUNK] * decay * tri_strict
            ch["a_qk"] = (kk_qk[CHUNK:2 * CHUNK] * decay).astype(BF16)
            ch["x"] = eye - ch["lmat"] * level_masks[0]
        for level in range(1, N_LEVELS):
            for ch in chains:
                ch["y"] = _mm(ch["lmat"] * level_masks[level], ch["x"])
            yield
            for ch in chains:
                ch["x"] = ch["x"] - _mm(ch["x"], ch.pop("y"))
            yield
        for ch in chains:
            ch["uw"] = _mm(ch.pop("x"), ch.pop("rhs"))
        for ch in chains:
            slot, h = ch["slot"], ch["h"]
            uw = ch.pop("uw")
            u_s[slot, h] = uw[:, 0:DN_DV]
            wq_s[slot, h] = jnp.concatenate([uw[:, DN_DV:DN_DV + DN_DK].astype(BF16), ch["q_dec"]], axis=0)
            akd_s[slot, h] = jnp.concatenate([ch["a_qk"], ch["k_dec_t"]], axis=0)
        yield

    def rec_stages(gi):
        for ci in range(group):
            c = gi * group + ci
            r0 = c * CHUNK if isinstance(c, int) else pl.multiple_of(c * CHUNK, CHUNK)
            slot = (gi % 2) * group + ci
            e_last_all = el_s[slot]
            rs = [_mm(wq_s[slot, h], state[h]) for h in range(DN_HEADS)]
            yield
            v_new = [u_s[slot, h] - rs[h][0:CHUNK] for h in range(DN_HEADS)]
            av = [_mm(akd_s[slot, h], v_new[h]) for h in range(DN_HEADS)]
            for h in range(DN_HEADS):
                gl = DN_HEADS + h
                o = rs[h][CHUNK:2 * CHUNK] + av[h][0:CHUNK]
                state[h] = state[h] * e_last_all[0:1, gl:gl + 1] + av[h][CHUNK:CHUNK + DN_DK]
                o_ref[0, pl.ds(r0, CHUNK), slice(h * DN_DV, (h + 1) * DN_DV)] = o.astype(o_ref.dtype)
            yield

    run = _run_round_robin

    n_groups = n_chunks // group
    run(prep_stages(0))

    def both(gi, carry):
        run(prep_stages(gi), rec_stages(gi - 1))
        return carry

    lax.fori_loop(1, n_groups, both, 0)
    run(rec_stages(n_groups - 1))


def _deltanet(p, small, alog_row, dtb_row, ts, group):
    b, s, _ = p.shape
    width = DN_HEADS * DN_DK
    n_slots = 2 * group
    assert (ts // CHUNK) % group == 0
    kern = functools.partial(_deltanet_kernel, ts=ts, group=group)
    return pl.pallas_call(
        kern,
        grid=(b, s // ts),
        in_specs=[pl.BlockSpec((1, ts, width), lambda bi, t: (bi, t, 0)),
                  pl.BlockSpec((1, ts, width), lambda bi, t: (bi, t, 1)),
                  pl.BlockSpec((1, ts, width), lambda bi, t: (bi, t, 2)),
                  pl.BlockSpec((1, ts, LANES), lambda bi, t: (bi, t, 0)),
                  pl.BlockSpec((1, LANES), lambda bi, t: (0, 0)),
                  pl.BlockSpec((1, LANES), lambda bi, t: (0, 0))],
        out_specs=pl.BlockSpec((1, ts, DN_WIDTH), lambda bi, t: (bi, t, 0)),
        out_shape=jax.ShapeDtypeStruct((b, s, DN_WIDTH), BF16),
        scratch_shapes=[pltpu.VMEM((DN_HEADS, DN_DK, DN_DV), F32),
                        pltpu.VMEM((n_slots, DN_HEADS, CHUNK, DN_DV), F32),
                        pltpu.VMEM((n_slots, DN_HEADS, 2 * CHUNK, DN_DK), BF16),
                        pltpu.VMEM((n_slots, DN_HEADS, CHUNK + DN_DK, CHUNK), BF16),
                        pltpu.VMEM((n_slots, 8, LANES), F32)],
        compiler_params=pltpu.CompilerParams(
            dimension_semantics=("parallel", "arbitrary"), vmem_limit_bytes=VMEM_LIMIT),
        name="gated_deltanet",
    )(p, p, p, small, alog_row, dtb_row)


ATTN_BLOCKS_PER_ITER = 4
ATTN_STRIDE = 4


def _attn_kernel(q_ref, k_ref, v_ref, o_ref, lse_ref, *scratch, dil, n_back):
    s = q_ref.shape[1]
    sub_len = s // dil
    nb = sub_len // ATTN_BLOCK
    n_pairs = ATTN_GROUP_WIDTH // LANES
    prev_block = nb > 1

    def lanes(pair):
        return slice(pair * LANES, (pair + 1) * LANES)

    if dil > 1:
        stage, tmp, qs, ks, vs, os_, ls = scratch
        assert dil in (ATTN_STRIDE, ATTN_STRIDE * ATTN_STRIDE)
        q_len = s // ATTN_STRIDE

        def classes():
            if dil == ATTN_STRIDE:
                return [(r, stage, r) for r in range(dil)]
            return [(r_lo + ATTN_STRIDE * r_hi, tmp, r_lo * q_len + r_hi)
                    for r_lo in range(ATTN_STRIDE) for r_hi in range(ATTN_STRIDE)]

        def deinterleave(dst, pair):
            if dil > ATTN_STRIDE:
                for r_lo in range(ATTN_STRIDE):
                    tmp[r_lo * q_len:(r_lo + 1) * q_len, :] = stage[pl.ds(r_lo, q_len, stride=ATTN_STRIDE), :]
            for r, buf, start in classes():
                dst[pair, r * sub_len:(r + 1) * sub_len, :] = (
                    buf[pl.ds(start, sub_len, stride=ATTN_STRIDE), :].astype(dst.dtype))

        def interleave(src, pair):
            for r, buf, start in classes():
                buf[pl.ds(start, sub_len, stride=ATTN_STRIDE), :] = src[pair, r * sub_len:(r + 1) * sub_len, :]
            if dil > ATTN_STRIDE:
                for r_lo in range(ATTN_STRIDE):
                    stage[pl.ds(r_lo, q_len, stride=ATTN_STRIDE), :] = tmp[r_lo * q_len:(r_lo + 1) * q_len, :]

        for src, dst in ((q_ref, qs), (k_ref, ks), (v_ref, vs)):
            for pair in range(n_pairs):
                stage[...] = src[0, :, lanes(pair)].astype(F32)
                deinterleave(dst, pair)

        def load(ref, src, pair, r0):
            return src[pair, pl.ds(r0, ATTN_BLOCK), :]
    else:
        qs, ks, vs = q_ref, k_ref, v_ref

        def load(ref, src, pair, r0):
            return ref[0, pl.ds(r0, ATTN_BLOCK), lanes(pair)]

    n_keys = 2 * ATTN_BLOCK if prev_block else ATTN_BLOCK
    qi = lax.broadcasted_iota(jnp.int32, (ATTN_BLOCK, n_keys), 0)
    ki = lax.broadcasted_iota(jnp.int32, (ATTN_BLOCK, n_keys), 1)
    dist = qi + (n_keys - ATTN_BLOCK) - ki
    band = (dist >= 0) & (dist <= n_back)
    bias_full = jnp.where(band, 0.0, -jnp.inf).astype(F32)
    bias_first = jnp.where(band & (ki >= ATTN_BLOCK), 0.0, -jnp.inf).astype(F32)
    lane = lax.broadcasted_iota(jnp.int32, (ATTN_BLOCK, LANES), 1)
    low_half = lane < ATTN_HEAD_DIM
    scale = ATTN_HEAD_DIM ** -0.5

    def block_body(fi, carry):
        chains = []
        for bi in range(ATTN_BLOCKS_PER_ITER):
            f = fi * ATTN_BLOCKS_PER_ITER + bi
            q0 = pl.multiple_of(f * ATTN_BLOCK, ATTN_BLOCK)
            if prev_block:
                k0 = pl.multiple_of(jnp.maximum(f - 1, 0) * ATTN_BLOCK, ATTN_BLOCK)
                bias = jnp.where(f % nb > 0, bias_full, bias_first)
            else:
                bias = bias_full
            for pair in range(n_pairs):
                qp = load(q_ref, qs, pair, q0) * scale
                kp = load(k_ref, ks, pair, q0)
                vp = load(v_ref, vs, pair, q0)
                if prev_block:
                    kp = jnp.concatenate([load(k_ref, ks, pair, k0), kp], axis=0)
                    vp = jnp.concatenate([load(v_ref, vs, pair, k0), vp], axis=0)
                for half in range(2):
                    sel = low_half if half == 0 else ~low_half
                    chains.append(dict(q0=q0, pair=pair, half=half, bias=bias, kp=kp, vp=vp,
                                       qm=jnp.where(sel, qp, jnp.zeros_like(qp))))
        for ch in chains:
            ch["sc"] = _mm_nt(ch.pop("qm"), ch.pop("kp"))
        for ch in chains:
            sc = ch.pop("sc") + ch.pop("bias")
            m = jnp.max(sc, axis=-1, keepdims=True)
            p = jnp.exp(sc - m)
            ch["denom"] = jnp.sum(p, axis=-1, keepdims=True)
            ch["m"] = m
            ch["p"] = p.astype(BF16)
        for ch in chains:
            ch["pv"] = _mm(ch.pop("p"), ch.pop("vp"))
        for c0 in range(0, len(chains), 2):
            lo, hi = chains[c0], chains[c0 + 1]
            o = jnp.where(low_half, lo["pv"] / lo["denom"], hi["pv"] / hi["denom"])
            lse = jnp.where(low_half, lo["m"] + jnp.log(lo["denom"]), hi["m"] + jnp.log(hi["denom"]))
            q0, pair = lo["q0"], lo["pair"]
            if dil > 1:
                os_[pair, pl.ds(q0, ATTN_BLOCK), :] = o
                ls[pair, pl.ds(q0, ATTN_BLOCK), :] = lse
            else:
                o_ref[0, pl.ds(q0, ATTN_BLOCK), lanes(pair)] = o.astype(o_ref.dtype)
                lse_ref[0, pl.ds(q0, ATTN_BLOCK), lanes(pair)] = lse
        return carry

    lax.fori_loop(0, dil * nb // ATTN_BLOCKS_PER_ITER, block_body, 0)

    if dil > 1:
        for pair in range(n_pairs):
            for src, dst in ((os_, o_ref), (ls, lse_ref)):
                interleave(src, pair)
                dst[0, :, lanes(pair)] = stage[...].astype(dst.dtype)


def _dilated_attention(p, group, col_base):
    b, s, _ = p.shape
    window, dil = DIL_CONFIGS[group]
    gw = ATTN_GROUP_WIDTH
    blk = col_base // gw + group
    step = ATTN_WIDTH // gw
    kern = functools.partial(_attn_kernel, dil=dil, n_back=window // dil)
    return pl.pallas_call(
        kern,
        grid=(b,),
        in_specs=[pl.BlockSpec((1, s, gw), lambda bi: (bi, 0, blk)),
                  pl.BlockSpec((1, s, gw), lambda bi: (bi, 0, blk + step)),
                  pl.BlockSpec((1, s, gw), lambda bi: (bi, 0, blk + 2 * step))],
        out_specs=[pl.BlockSpec((1, s, gw), lambda bi: (bi, 0, 0)),
                   pl.BlockSpec((1, s, gw), lambda bi: (bi, 0, 0))],
        out_shape=[jax.ShapeDtypeStruct((b, s, gw), BF16),
                   jax.ShapeDtypeStruct((b, s, gw), F32)],
        scratch_shapes=([] if dil == 1 else
                        [pltpu.VMEM((s, LANES), F32) for _ in range(2)]
                        + [pltpu.VMEM((gw // LANES, s, LANES), BF16) for _ in range(3)]
                        + [pltpu.VMEM((gw // LANES, s, LANES), F32) for _ in range(2)]),
        compiler_params=pltpu.CompilerParams(
            dimension_semantics=("parallel",), vmem_limit_bytes=VMEM_LIMIT),
        name=f"dilated_attn_g{group}",
    )(p, p, p)


MERGE_SUBTILES = 4


def _merge_kernel(x_ref, mod_ref, oa_ref, za_ref, ga_ref, gb_ref, zb_ref,
                  ob0_ref, ob1_ref, ob2_ref, l0_ref, l1_ref, l2_ref,
                  dnw_ref, pa_ref, pb_ref, wo_ref, fw_ref, out_ref):
    def silu_of_half(h):
        return h + h * jnp.tanh(h)

    gate = mod_ref[0][2:3]

    def stages(rows):
        oa = oa_ref[0, rows, :].astype(F32)
        za_half = za_ref[0, rows, :].astype(F32)
        parts = []
        for h in range(DN_HEADS):
            cq = slice(h * DN_DV, (h + 1) * DN_DV)
            blk = oa[:, cq]
            y = blk * lax.rsqrt(jnp.mean(blk * blk, axis=-1, keepdims=True) + EPS) * dnw_ref[...]
            parts.append(y * silu_of_half(za_half[:, cq]))
        ya_half = _mm(jnp.concatenate(parts, axis=1), pa_ref[...])
        yield

        l0, l1, l2 = l0_ref[0, rows, :], l1_ref[0, rows, :], l2_ref[0, rows, :]
        m = jnp.maximum(jnp.maximum(l0, l1), l2)
        e0, e1, e2 = jnp.exp(l0 - m), jnp.exp(l1 - m), jnp.exp(l2 - m)
        inv_den = 1.0 / (e0 + e1 + e2)
        ob = jnp.concatenate([ob0_ref[0, rows, :].astype(F32) * (e0 * inv_den),
                              ob1_ref[0, rows, :].astype(F32) * (e1 * inv_den),
                              ob2_ref[0, rows, :].astype(F32) * (e2 * inv_den)], axis=1)
        yb_half = _mm(ob * silu_of_half(zb_ref[0, rows, :].astype(F32)), pb_ref[...])
        yield

        merged = (ya_half + ya_half * jnp.tanh(ga_ref[0, rows, :].astype(F32))
                  + (yb_half + yb_half * jnp.tanh(gb_ref[0, rows, :].astype(F32))))
        delta = _mm(merged, wo_ref[...])
        yield

        xo = x_ref[0, rows, :] + gate * delta
        out_ref[0, rows, :] = xo * lax.rsqrt(jnp.mean(xo * xo, axis=-1, keepdims=True) + EPS) * fw_ref[...]
        yield

    tm = x_ref.shape[1]
    sub = tm // MERGE_SUBTILES
    _run_round_robin(*[stages(slice(i * sub, (i + 1) * sub)) for i in range(MERGE_SUBTILES)])


def _merge(x, mod3, o_a, p, obs, lses, dn_norm_w, pa, pb, wo, final_w, tm, cols):
    b, s, d = x.shape
    gw = ATTN_GROUP_WIDTH

    def row_spec(width, blk):
        return pl.BlockSpec((1, tm, width), lambda bi, i: (bi, i, blk))

    def full_spec(shape):
        return pl.BlockSpec(shape, lambda bi, i: (0,) * len(shape))

    return pl.pallas_call(
        _merge_kernel,
        grid=(b, s // tm),
        in_specs=[row_spec(d, 0),
                  pl.BlockSpec((1, 3, d), lambda bi, i: (bi, 0, 0)),
                  row_spec(DN_WIDTH, 0),
                  row_spec(DN_WIDTH, cols["za"] // DN_WIDTH),
                  row_spec(d, cols["ga"] // d),
                  row_spec(d, cols["gb"] // d),
                  row_spec(ATTN_WIDTH, cols["zb"] // ATTN_WIDTH),
                  row_spec(gw, 0), row_spec(gw, 0), row_spec(gw, 0),
                  row_spec(gw, 0), row_spec(gw, 0), row_spec(gw, 0),
                  full_spec((1, DN_DV)), full_spec(pa.shape), full_spec(pb.shape), full_spec(wo.shape),
                  full_spec((1, d))],
        out_specs=row_spec(d, 0),
        out_shape=jax.ShapeDtypeStruct((b, s, d), F32),
        compiler_params=pltpu.CompilerParams(
            dimension_semantics=("parallel", "parallel"), vmem_limit_bytes=VMEM_LIMIT),
        name="merge_outproj",
    )(x, mod3, o_a, p, p, p, p, *obs, *lses, dn_norm_w.reshape(1, DN_DV), pa, pb, wo, final_w.reshape(1, d))


W_PREP_COLS = 128


def _wprep_kernel(wt_ref, wc_ref, wp_ref, ws_ref, *, conv_width, segments, small_off):
    wc_ref[...] = wt_ref[0:conv_width, :].astype(BF16)
    for src, width, dst, scale in segments:
        blk = wt_ref[src:src + width, :]
        wp_ref[dst:dst + width, :] = (blk if scale == 1.0 else blk * scale).astype(BF16)
    n_small = 2 * DN_HEADS
    ws_ref[0:n_small, :] = wt_ref[small_off:small_off + n_small, :].astype(BF16)
    ws_ref[n_small:LANES, :] = jnp.zeros((LANES - n_small, wt_ref.shape[1]), BF16)


def _weight_prep(wt, conv_width, plain_width, segments, small_off):
    n_in, d = wt.shape
    assert d % W_PREP_COLS == 0 and small_off % 16 == 0 and all(sg[0] % 16 == 0 for sg in segments)
    kern = functools.partial(_wprep_kernel, conv_width=conv_width, segments=segments, small_off=small_off)
    return pl.pallas_call(
        kern,
        grid=(d // W_PREP_COLS,),
        in_specs=[pl.BlockSpec((n_in, W_PREP_COLS), lambda i: (0, i))],
        out_specs=[pl.BlockSpec((conv_width, W_PREP_COLS), lambda i: (0, i)),
                   pl.BlockSpec((plain_width, W_PREP_COLS), lambda i: (0, i)),
                   pl.BlockSpec((LANES, W_PREP_COLS), lambda i: (0, i))],
        out_shape=[jax.ShapeDtypeStruct((conv_width, d), BF16),
                   jax.ShapeDtypeStruct((plain_width, d), BF16),
                   jax.ShapeDtypeStruct((LANES, d), BF16)],
        compiler_params=pltpu.CompilerParams(dimension_semantics=("parallel",), vmem_limit_bytes=VMEM_LIMIT),
        name="weight_prep",
    )(wt)


def _layer(x, c, norm_w, ada_w, ada_b, w_in, conv_w, a_log, dt_bias, dn_norm_w, w_proj_a, w_proj_b, w_out,
           final_norm_w):
    b, s, d = x.shape
    qk = DN_HEADS * DN_DK
    splits = (qk, qk, DN_WIDTH, DN_WIDTH, DN_HEADS, DN_HEADS, ATTN_WIDTH, ATTN_WIDTH, ATTN_WIDTH, ATTN_WIDTH, d, d)
    names = ("qa", "ka", "va", "za", "beta", "a", "qb", "kb", "vb", "zb", "ga", "gb")
    offs = dict(zip(names, np.cumsum((0,) + splits[:-1]).tolist()))
    widths = dict(zip(names, splits))
    order = ("za", "ga", "gb", "qb", "kb", "vb", "zb")
    cols, pos = {}, 0
    for n in order:
        assert pos % widths[n] == 0, (n, pos)
        cols[n] = pos
        pos += widths[n]
    assert offs["qa"] == 0 and offs["ka"] == qk and offs["va"] == 2 * qk and offs["a"] == offs["beta"] + DN_HEADS
    halved = ("za", "ga", "gb", "zb")
    segments = tuple((offs[n], widths[n], cols[n], 0.5 if n in halved else 1.0) for n in order)
    w_conv, w_plain, w_small = _weight_prep(w_in.T, 2 * qk + DN_WIDTH, pos, segments, offs["beta"])
    pad_lo = jnp.zeros((DN_HEADS,), F32)
    pad_hi = jnp.zeros((LANES - 2 * DN_HEADS,), F32)
    alog_row = jnp.concatenate([pad_lo, a_log.astype(F32), pad_hi]).reshape(1, LANES)
    dtb_row = jnp.concatenate([pad_lo, dt_bias.astype(F32), pad_hi]).reshape(1, LANES)

    mod3 = _adaln_mod(c, ada_w, ada_b).reshape(b, 3, d)
    p_conv, p, small = _inproj(x, mod3, norm_w, w_conv, w_plain, w_small, conv_w)
    o_a = _deltanet(p_conv, small, alog_row, dtb_row, ts=min(s, 1024), group=4)
    obs, lses = [], []
    for g in range(len(DIL_CONFIGS)):
        o_g, lse_g = _dilated_attention(p, g, cols["qb"])
        obs.append(o_g)
        lses.append(lse_g)
    return _merge(x, mod3, o_a, p, obs, lses, dn_norm_w, (w_proj_a * 0.5).astype(BF16),
                  (w_proj_b * 0.5).astype(BF16), w_out.astype(BF16), final_norm_w, tm=min(s, 512), cols=cols)


def kernel(x, c, norm_w, ada_w, ada_b, w_in, conv_w, a_log, dt_bias, dn_norm_w, w_proj_a, w_proj_b, w_out,
           final_norm_w):
    depth = norm_w.shape[0]
    assert depth == 1, "the final RMSNorm is fused into the single layer's output kernel"
    return _layer(x, c, norm_w[0], ada_w[0], ada_b[0], w_in[0], conv_w[0], a_log[0], dt_bias[0], dn_norm_w[0],
                  w_proj_a[0], w_proj_b[0], w_out[0], final_norm_w)
```

```python
import functools

import jax
import jax.numpy as jnp
import numpy as np
from jax import lax
from jax.experimental import pallas as pl
from jax.experimental.pallas import tpu as pltpu

F32 = jnp.float32
BF16 = jnp.bfloat16
HIGHEST = lax.Precision.HIGHEST

EPS = 1e-6
DN_HEADS = 8
DN_DK = 128
DN_DV = 128
DN_WIDTH = DN_HEADS * DN_DV
CONV_K = 4
CHUNK = 64
DIL_CONFIGS = ((128, 1), (512, 4), (2048, 16))
HEADS_PER_GROUP = 4
ATTN_HEAD_DIM = 64
ATTN_GROUP_WIDTH = HEADS_PER_GROUP * ATTN_HEAD_DIM
ATTN_WIDTH = ATTN_GROUP_WIDTH * len(DIL_CONFIGS)
ATTN_BLOCK = 128
LANES = 128
HALO = 8
VMEM_LIMIT = 56 * 1024 * 1024


def _mm(a, b):
    return jnp.dot(a.astype(BF16), b.astype(BF16), preferred_element_type=F32)


def _mm_nt(a, b):
    return lax.dot_general(a.astype(BF16), b.astype(BF16), (((1,), (1,)), ((), ())),
                           preferred_element_type=F32)


def _mm_tn(a, b):
    return lax.dot_general(a.astype(BF16), b.astype(BF16), (((0,), (0,)), ((), ())),
                           preferred_element_type=F32)


def _run_round_robin(*gens):
    gens = list(gens)
    while gens:
        for g in list(gens):
            if next(g, StopIteration) is StopIteration:
                gens.remove(g)


def _sigmoid(x):
    return 0.5 * jnp.tanh(0.5 * x) + 0.5


def _silu(x):
    return x * _sigmoid(x)


def _mod_kernel(c_ref, w_ref, b_ref, o_ref):
    sc = _silu(c_ref[...])
    o_ref[...] = jnp.dot(sc, w_ref[...], preferred_element_type=F32, precision=HIGHEST) + b_ref[...]


def _adaln_mod(c, ada_w, ada_b):
    b, d = c.shape
    return pl.pallas_call(
        _mod_kernel,
        grid=(3,),
        in_specs=[pl.BlockSpec((b, d), lambda j: (0, 0)),
                  pl.BlockSpec((d, d), lambda j: (0, j)),
                  pl.BlockSpec((1, d), lambda j: (0, j))],
        out_specs=pl.BlockSpec((b, d), lambda j: (0, j)),
        out_shape=jax.ShapeDtypeStruct((b, 3 * d), F32),
        name="adaln_mod",
    )(c, ada_w, ada_b.reshape(1, 3 * d))


CONV_ROWS = 64
CONV_SUB = 256
SUBLANES = 8
TOKEN_CHAINS = 3
TOKEN_POINT = "ss"


def _conv_unit(buf, r0, cl, half_taps, shift_masks, norm_scale, skip_norm):
    a = buf[r0:r0 + HALO + CONV_ROWS, cl]
    n = CONV_ROWS // SUBLANES
    vs = [a[SUBLANES * i:SUBLANES * (i + 1)] for i in range(n + 1)]
    acc = [vs[i + 1] * half_taps[CONV_K - 1] for i in range(n)]
    for shift in range(1, CONV_K):
        rs = [pltpu.roll(v, shift, 0) for v in vs]
        tap = half_taps[CONV_K - 1 - shift]
        for i in range(n):
            acc[i] = acc[i] + jnp.where(shift_masks[shift], rs[i], rs[i + 1]) * tap
    h = jnp.concatenate(acc, axis=0)
    y = h + h * jnp.tanh(h)
    ss = jnp.sum(y * y, axis=-1, keepdims=True)
    norm = lax.rsqrt(ss + EPS) * norm_scale
    out = y * jnp.where(skip_norm, 1.0, norm)
    token = {"silu": y, "ss": jnp.broadcast_to(ss, y.shape), "out": out}[TOKEN_POINT][CONV_ROWS - SUBLANES:]
    return out.astype(BF16), token


def _inproj_kernel(x_ref, mod_ref, nw_ref, wc_ref, wp_ref, ws_ref, cw_ref, pc_ref, pp_ref, small_ref,
                   h_scr, cbuf, tok, *, steps_per_seg):
    j = pl.program_id(1)
    s = x_ref.shape[1]

    @pl.when(j == 0)
    def _():
        x = x_ref[0]
        m = mod_ref[0]
        y = x * lax.rsqrt(jnp.mean(x * x, axis=-1, keepdims=True) + EPS) * nw_ref[...]
        h = (y * (1.0 + m[1:2]) + m[0:1]).astype(BF16)
        h_scr[...] = h
        small_ref[0] = _mm_nt(h, ws_ref[...])
        cbuf[0:HALO, :] = jnp.zeros((HALO, CONV_SUB), F32)

    cbuf[HALO:HALO + s, :] = _mm_nt(h_scr[...], wc_ref[...])
    pp_ref[0] = _mm_nt(h_scr[...], wp_ref[...]).astype(BF16)

    sub = lax.broadcasted_iota(jnp.int32, (SUBLANES, LANES), 0)
    shift_masks = [sub < shift for shift in range(CONV_K)]
    q_scale = jnp.where(j < steps_per_seg, DN_DK ** -0.5, 1.0)
    is_v = j >= 2 * steps_per_seg
    never = j < 0
    for c in range(TOKEN_CHAINS):
        tok[c] = jnp.zeros((SUBLANES, LANES), F32)
    n_unit = 0
    for c0 in range(0, CONV_SUB, DN_DK):
        cl = slice(c0, c0 + DN_DK)
        half_taps = [cw_ref[t:t + 1, cl] * 0.5 for t in range(CONV_K)]
        for r0 in range(0, s, CONV_ROWS):
            slot = n_unit % TOKEN_CHAINS
            held = tok[slot][0:1]
            taps = [jnp.where(never, held, tap) for tap in half_taps]
            out, token = _conv_unit(cbuf, r0, cl, taps, shift_masks, q_scale, is_v)
            pc_ref[0, r0:r0 + CONV_ROWS, cl] = out
            tok[slot] = token
            n_unit += 1


def _inproj(x, mod3, norm_w, w_conv, w_plain, w_small, conv_w):
    b, s, d = x.shape
    conv_width, plain_width = w_conv.shape[0], w_plain.shape[0]
    n_steps = conv_width // CONV_SUB
    tp = plain_width // n_steps
    assert n_steps % 3 == 0 and n_steps * CONV_SUB == conv_width and conv_w.shape[1] == conv_width
    assert tp * n_steps == plain_width and tp % LANES == 0 and s % CONV_ROWS == 0
    kern = functools.partial(_inproj_kernel, steps_per_seg=n_steps // 3)
    return pl.pallas_call(
        kern,
        grid=(b, n_steps),
        in_specs=[pl.BlockSpec((1, s, d), lambda bi, j: (bi, 0, 0)),
                  pl.BlockSpec((1, 3, d), lambda bi, j: (bi, 0, 0)),
                  pl.BlockSpec((1, d), lambda bi, j: (0, 0)),
                  pl.BlockSpec((CONV_SUB, d), lambda bi, j: (j, 0)),
                  pl.BlockSpec((tp, d), lambda bi, j: (j, 0)),
                  pl.BlockSpec((LANES, d), lambda bi, j: (0, 0)),
                  pl.BlockSpec((CONV_K, CONV_SUB), lambda bi, j: (0, j))],
        out_specs=[pl.BlockSpec((1, s, CONV_SUB), lambda bi, j: (bi, 0, j)),
                   pl.BlockSpec((1, s, tp), lambda bi, j: (bi, 0, j)),
                   pl.BlockSpec((1, s, LANES), lambda bi, j: (bi, 0, 0))],
        out_shape=[jax.ShapeDtypeStruct((b, s, conv_width), BF16),
                   jax.ShapeDtypeStruct((b, s, plain_width), BF16),
                   jax.ShapeDtypeStruct((b, s, LANES), F32)],
        scratch_shapes=[pltpu.VMEM((s, d), BF16),
                        pltpu.VMEM((HALO + s, CONV_SUB), F32),
                        pltpu.VMEM((TOKEN_CHAINS, SUBLANES, LANES), F32)],
        compiler_params=pltpu.CompilerParams(
            dimension_semantics=("parallel", "arbitrary"), vmem_limit_bytes=VMEM_LIMIT),
        name="norm_inproj",
    )(x, mod3, norm_w.reshape(1, d), w_conv, w_plain, w_small, conv_w)


def _level_mask(level, i, j):
    same_block = (i >> (level + 1)) == (j >> (level + 1))
    return same_block & (((i >> level) & 1) == 1) & (((j >> level) & 1) == 0)


N_LEVELS = CHUNK.bit_length() - 1


def _deltanet_kernel(q_ref, k_ref, v_ref, small_ref, alog_ref, dtb_ref, o_ref,
                     state, u_s, wq_s, akd_s, el_s, *, ts, group):
    @pl.when(pl.program_id(1) == 0)
    def _():
        state[...] = jnp.zeros_like(state)

    row = lax.broadcasted_iota(jnp.int32, (CHUNK, 2 * CHUNK), 0)
    lane = lax.broadcasted_iota(jnp.int32, (CHUNK, 2 * CHUNK), 1)
    col = lane & (CHUNK - 1)
    in_a = lane < CHUNK
    causal = row >= col
    tri_strict = (row > col).astype(F32)
    eye = (row == col).astype(F32)
    row_c = lax.broadcasted_iota(jnp.int32, (CHUNK, CHUNK), 0)
    col_c = lax.broadcasted_iota(jnp.int32, (CHUNK, CHUNK), 1)
    tri = (row_c >= col_c).astype(F32)
    level_masks = [_level_mask(level, row, col).astype(F32) for level in range(N_LEVELS)]
    neg_a = -jnp.exp(alog_ref[...])
    dtb = dtb_ref[...]
    n_chunks = ts // CHUNK
    n_pairs = DN_HEADS // 2
    zeros_wide = jnp.zeros((CHUNK, DN_DV + DN_DK), BF16)
    zeros_head = jnp.zeros((CHUNK, DN_DK), BF16)

    def blockdiag(m):
        return jnp.concatenate([jnp.where(in_a, m, 0.0), jnp.where(in_a, 0.0, m)], axis=0).astype(BF16)

    def prep_stages(gi):
        chains = []
        for ci in range(group):
            c = gi * group + ci
            r0 = c * CHUNK if isinstance(c, int) else pl.multiple_of(c * CHUNK, CHUNK)
            sm = small_ref[0, pl.ds(r0, CHUNK), :]
            beta_all = _sigmoid(sm)
            z = sm + dtb
            softplus = jnp.maximum(z, 0.0) + jnp.log(1.0 + jnp.exp(-jnp.abs(z)))
            g_all = neg_a * softplus
            gc_all = jnp.dot(tri, g_all, preferred_element_type=F32, precision=HIGHEST)
            gc_t = jnp.concatenate([gc_all, gc_all], axis=0).T
            g_last = gc_all[CHUNK - 1:CHUNK, :]
            e_in_all = jnp.exp(gc_all)
            e_out_all = jnp.exp(g_last - gc_all)
            slot = (gi % 2) * group + ci
            el_s[slot] = jnp.broadcast_to(jnp.exp(g_last), (8, LANES))
            for p in range(n_pairs):
                heads = (2 * p, 2 * p + 1)
                q, k, k_beta, rhs, k_dec = [], [], [], [], []
                for h in heads:
                    gl = DN_HEADS + h
                    cq = slice(h * DN_DK, (h + 1) * DN_DK)
                    q_h = q_ref[0, pl.ds(r0, CHUNK), cq].astype(F32)
                    k_h = k_ref[0, pl.ds(r0, CHUNK), cq].astype(F32)
                    v_h = v_ref[0, pl.ds(r0, CHUNK), slice(h * DN_DV, (h + 1) * DN_DV)].astype(F32)
                    beta = beta_all[:, h:h + 1]
                    e_in = e_in_all[:, gl:gl + 1]
                    kb_h = k_h * beta
                    q.append(q_h)
                    k.append(k_h.astype(BF16))
                    k_beta.append(kb_h)
                    rhs.append(jnp.concatenate([v_h * beta, kb_h * e_in], axis=1).astype(BF16))
                    k_dec.append(k_h * e_out_all[:, gl:gl + 1])
                    wq_s[slot, h, CHUNK:2 * CHUNK, :] = (q_h * e_in).astype(BF16)
                ga, gb = DN_HEADS + heads[0], DN_HEADS + heads[1]
                gc_col = jnp.where(in_a, gc_all[:, ga:ga + 1], gc_all[:, gb:gb + 1])
                gc_row = jnp.where(in_a[0:1], gc_t[ga:ga + 1, :], gc_t[gb:gb + 1, :])
                chains.append(dict(
                    slot=slot, p=p,
                    decay=jnp.exp(jnp.where(causal, gc_col - gc_row, -jnp.inf)),
                    kq=jnp.concatenate([jnp.concatenate(k_beta, axis=1), jnp.concatenate(q, axis=1)],
                                       axis=0).astype(BF16),
                    kk=jnp.concatenate([jnp.concatenate([k[0], zeros_head], axis=1),
                                        jnp.concatenate([zeros_head, k[1]], axis=1)], axis=0),
                    rhs=jnp.concatenate([jnp.concatenate([rhs[0], zeros_wide], axis=1),
                                         jnp.concatenate([zeros_wide, rhs[1]], axis=1)], axis=0),
                    k_dec_t=jnp.concatenate(k_dec, axis=0).T.astype(BF16)))

        for ch in chains:
            ch["kk_qk"] = _mm_nt(ch.pop("kq"), ch.pop("kk"))
        yield
        for ch in chains:
            kk_qk = ch.pop("kk_qk")
            decay = ch.pop("decay")
            ch["lmat"] = kk_qk[0:CHUNK] * decay * tri_strict
            ch["a_qk"] = (kk_qk[CHUNK:2 * CHUNK] * decay).astype(BF16)
            ch["x"] = eye - ch["lmat"] * level_masks[0]
        for level in range(1, N_LEVELS):
            for ch in chains:
                ch["y"] = _mm(ch["lmat"] * level_masks[level], blockdiag(ch["x"]))
            yield
            for ch in chains:
                ch["x"] = ch["x"] - _mm(ch["x"], blockdiag(ch.pop("y")))
            yield
        for ch in chains:
            ch["uw"] = _mm(ch.pop("x"), ch.pop("rhs"))
        for ch in chains:
            slot, p = ch["slot"], ch["p"]
            uw = ch.pop("uw")
            for i, h in enumerate((2 * p, 2 * p + 1)):
                base = i * (DN_DV + DN_DK)
                u_s[slot, h] = uw[:, base:base + DN_DV]
                wq_s[slot, h, 0:CHUNK, :] = uw[:, base + DN_DV:base + DN_DV + DN_DK].astype(BF16)
            akd_s[slot, p] = jnp.concatenate([ch["a_qk"], ch["k_dec_t"]], axis=0)
        yield

    def rec_stages(gi):
        for ci in range(group):
            c = gi * group + ci
            r0 = c * CHUNK if isinstance(c, int) else pl.multiple_of(c * CHUNK, CHUNK)
            slot = (gi % 2) * group + ci
            e_last_all = el_s[slot]
            rs = [_mm(wq_s[slot, h], state[h]) for h in range(DN_HEADS)]
            yield
            v_new = [(u_s[slot, h] - rs[h][0:CHUNK]).astype(BF16) for h in range(DN_HEADS)]
            zeros_v = jnp.zeros((CHUNK, DN_DV), BF16)
            av = [_mm(akd_s[slot, p],
                      jnp.concatenate([jnp.concatenate([v_new[2 * p], zeros_v], axis=1),
                                       jnp.concatenate([zeros_v, v_new[2 * p + 1]], axis=1)], axis=0))
                  for p in range(n_pairs)]
            for h in range(DN_HEADS):
                gl = DN_HEADS + h
                lanes_h = slice((h % 2) * DN_DV, (h % 2 + 1) * DN_DV)
                o = rs[h][CHUNK:2 * CHUNK] + av[h // 2][0:CHUNK, lanes_h]
                state[h] = state[h] * e_last_all[0:1, gl:gl + 1] + av[h // 2][CHUNK:CHUNK + DN_DK, lanes_h]
                o_ref[0, pl.ds(r0, CHUNK), slice(h * DN_DV, (h + 1) * DN_DV)] = o.astype(o_ref.dtype)
            yield

    run = _run_round_robin

    n_groups = n_chunks // group
    run(prep_stages(0))

    def both(gi, carry):
        run(prep_stages(gi), rec_stages(gi - 1))
        return carry

    lax.fori_loop(1, n_groups, both, 0)
    run(rec_stages(n_groups - 1))


def _deltanet(p, small, alog_row, dtb_row, ts, group):
    b, s, _ = p.shape
    width = DN_HEADS * DN_DK
    n_slots = 2 * group
    assert (ts // CHUNK) % group == 0
    kern = functools.partial(_deltanet_kernel, ts=ts, group=group)
    return pl.pallas_call(
        kern,
        grid=(b, s // ts),
        in_specs=[pl.BlockSpec((1, ts, width), lambda bi, t: (bi, t, 0)),
                  pl.BlockSpec((1, ts, width), lambda bi, t: (bi, t, 1)),
                  pl.BlockSpec((1, ts, width), lambda bi, t: (bi, t, 2)),
                  pl.BlockSpec((1, ts, LANES), lambda bi, t: (bi, t, 0)),
                  pl.BlockSpec((1, LANES), lambda bi, t: (0, 0)),
                  pl.BlockSpec((1, LANES), lambda bi, t: (0, 0))],
        out_specs=pl.BlockSpec((1, ts, DN_WIDTH), lambda bi, t: (bi, t, 0)),
        out_shape=jax.ShapeDtypeStruct((b, s, DN_WIDTH), BF16),
        scratch_shapes=[pltpu.VMEM((DN_HEADS, DN_DK, DN_DV), F32),
                        pltpu.VMEM((n_slots, DN_HEADS, CHUNK, DN_DV), F32),
                        pltpu.VMEM((n_slots, DN_HEADS, 2 * CHUNK, DN_DK), BF16),
                        pltpu.VMEM((n_slots, DN_HEADS // 2, CHUNK + DN_DK, 2 * CHUNK), BF16),
                        pltpu.VMEM((n_slots, 8, LANES), F32)],
        compiler_params=pltpu.CompilerParams(
            dimension_semantics=("parallel", "arbitrary"), vmem_limit_bytes=VMEM_LIMIT),
        name="gated_deltanet",
    )(p, p, p, small, alog_row, dtb_row)


ATTN_BLOCKS_PER_ITER = 4
ATTN_STRIDE = 4


def _attn_kernel(q_ref, k_ref, v_ref, o_ref, lse_ref, *scratch, dil, n_back):
    s = q_ref.shape[1]
    sub_len = s // dil
    nb = sub_len // ATTN_BLOCK
    n_pairs = ATTN_GROUP_WIDTH // LANES
    prev_block = nb > 1

    def lanes(pair):
        return slice(pair * LANES, (pair + 1) * LANES)

    if dil > 1:
        stage, tmp, qs, ks, vs, os_, ls = scratch
        assert dil in (ATTN_STRIDE, ATTN_STRIDE * ATTN_STRIDE)
        q_len = s // ATTN_STRIDE

        def classes():
            if dil == ATTN_STRIDE:
                return [(r, stage, r) for r in range(dil)]
            return [(r_lo + ATTN_STRIDE * r_hi, tmp, r_lo * q_len + r_hi)
                    for r_lo in range(ATTN_STRIDE) for r_hi in range(ATTN_STRIDE)]

        def deinterleave(dst, pair):
            if dil > ATTN_STRIDE:
                for r_lo in range(ATTN_STRIDE):
                    tmp[r_lo * q_len:(r_lo + 1) * q_len, :] = stage[pl.ds(r_lo, q_len, stride=ATTN_STRIDE), :]
            for r, buf, start in classes():
                dst[pair, r * sub_len:(r + 1) * sub_len, :] = (
                    buf[pl.ds(start, sub_len, stride=ATTN_STRIDE), :].astype(dst.dtype))

        def interleave(src, pair):
            for r, buf, start in classes():
                buf[pl.ds(start, sub_len, stride=ATTN_STRIDE), :] = src[pair, r * sub_len:(r + 1) * sub_len, :]
            if dil > ATTN_STRIDE:
                for r_lo in range(ATTN_STRIDE):
                    stage[pl.ds(r_lo, q_len, stride=ATTN_STRIDE), :] = tmp[r_lo * q_len:(r_lo + 1) * q_len, :]

        for src, dst in ((q_ref, qs), (k_ref, ks), (v_ref, vs)):
            for pair in range(n_pairs):
                stage[...] = src[0, :, lanes(pair)].astype(F32)
                deinterleave(dst, pair)

        def load(ref, src, pair, r0):
            return src[pair, pl.ds(r0, ATTN_BLOCK), :]
    else:
        qs, ks, vs = q_ref, k_ref, v_ref

        def load(ref, src, pair, r0):
            return ref[0, pl.ds(r0, ATTN_BLOCK), lanes(pair)]

    n_keys = 2 * ATTN_BLOCK if prev_block else ATTN_BLOCK
    qi = lax.broadcasted_iota(jnp.int32, (ATTN_BLOCK, n_keys), 0)
    ki = lax.broadcasted_iota(jnp.int32, (ATTN_BLOCK, n_keys), 1)
    dist = qi + (n_keys - ATTN_BLOCK) - ki
    band = (dist >= 0) & (dist <= n_back)
    bias_full = jnp.where(band, 0.0, -jnp.inf).astype(F32)
    bias_first = jnp.where(band & (ki >= ATTN_BLOCK), 0.0, -jnp.inf).astype(F32)
    lane = lax.broadcasted_iota(jnp.int32, (ATTN_BLOCK, LANES), 1)
    low_half = lane < ATTN_HEAD_DIM
    scale = ATTN_HEAD_DIM ** -0.5

    def block_body(fi, carry):
        chains = []
        for bi in range(ATTN_BLOCKS_PER_ITER):
            f = fi * ATTN_BLOCKS_PER_ITER + bi
            q0 = pl.multiple_of(f * ATTN_BLOCK, ATTN_BLOCK)
            if prev_block:
                k0 = pl.multiple_of(jnp.maximum(f - 1, 0) * ATTN_BLOCK, ATTN_BLOCK)
                bias = jnp.where(f % nb > 0, bias_full, bias_first)
            else:
                bias = bias_full
            for pair in range(n_pairs):
                qp = load(q_ref, qs, pair, q0) * scale
                kp = load(k_ref, ks, pair, q0)
                vp = load(v_ref, vs, pair, q0)
                if prev_block:
                    kp = jnp.concatenate([load(k_ref, ks, pair, k0), kp], axis=0)
                    vp = jnp.concatenate([load(v_ref, vs, pair, k0), vp], axis=0)
                for half in range(2):
                    sel = low_half if half == 0 else ~low_half
                    chains.append(dict(q0=q0, pair=pair, half=half, bias=bias, kp=kp, vp=vp,
                                       qm=jnp.where(sel, qp, jnp.zeros_like(qp))))
        for ch in chains:
            ch["sc"] = _mm_nt(ch.pop("qm"), ch.pop("kp"))
        for ch in chains:
            sc = ch.pop("sc") + ch.pop("bias")
            m = jnp.max(sc, axis=-1, keepdims=True)
            p = jnp.exp(sc - m)
            ch["denom"] = jnp.sum(p, axis=-1, keepdims=True)
            ch["m"] = m
            ch["p"] = p.astype(BF16)
        for ch in chains:
            ch["pv"] = _mm(ch.pop("p"), ch.pop("vp"))
        for c0 in range(0, len(chains), 2):
            lo, hi = chains[c0], chains[c0 + 1]
            o = jnp.where(low_half, lo["pv"] / lo["denom"], hi["pv"] / hi["denom"])
            lse = jnp.where(low_half, lo["m"] + jnp.log(lo["denom"]), hi["m"] + jnp.log(hi["denom"]))
            q0, pair = lo["q0"], lo["pair"]
            if dil > 1:
                os_[pair, pl.ds(q0, ATTN_BLOCK), :] = o
                ls[pair, pl.ds(q0, ATTN_BLOCK), :] = lse
            else:
                o_ref[0, pl.ds(q0, ATTN_BLOCK), lanes(pair)] = o.astype(o_ref.dtype)
                lse_ref[0, pl.ds(q0, ATTN_BLOCK), lanes(pair)] = lse
        return carry

    lax.fori_loop(0, dil * nb // ATTN_BLOCKS_PER_ITER, block_body, 0)

    if dil > 1:
        for pair in range(n_pairs):
            for src, dst in ((os_, o_ref), (ls, lse_ref)):
                interleave(src, pair)
                dst[0, :, lanes(pair)] = stage[...].astype(dst.dtype)


def _dilated_attention(p, group, col_base):
    b, s, _ = p.shape
    window, dil = DIL_CONFIGS[group]
    gw = ATTN_GROUP_WIDTH
    blk = col_base // gw + group
    step = ATTN_WIDTH // gw
    kern = functools.partial(_attn_kernel, dil=dil, n_back=window // dil)
    return pl.pallas_call(
        kern,
        grid=(b,),
        in_specs=[pl.BlockSpec((1, s, gw), lambda bi: (bi, 0, blk)),
                  pl.BlockSpec((1, s, gw), lambda bi: (bi, 0, blk + step)),
                  pl.BlockSpec((1, s, gw), lambda bi: (bi, 0, blk + 2 * step))],
        out_specs=[pl.BlockSpec((1, s, gw), lambda bi: (bi, 0, 0)),
                   pl.BlockSpec((1, s, gw), lambda bi: (bi, 0, 0))],
        out_shape=[jax.ShapeDtypeStruct((b, s, gw), BF16),
                   jax.ShapeDtypeStruct((b, s, gw), F32)],
        scratch_shapes=([] if dil == 1 else
                        [pltpu.VMEM((s, LANES), F32) for _ in range(2)]
                        + [pltpu.VMEM((gw // LANES, s, LANES), BF16) for _ in range(3)]
                        + [pltpu.VMEM((gw // LANES, s, LANES), F32) for _ in range(2)]),
        compiler_params=pltpu.CompilerParams(
            dimension_semantics=("parallel",), vmem_limit_bytes=VMEM_LIMIT),
        name=f"dilated_attn_g{group}",
    )(p, p, p)


MERGE_SUBTILES = 4


def _merge_kernel(x_ref, mod_ref, oa_ref, za_ref, ga_ref, gb_ref, zb_ref,
                  ob0_ref, ob1_ref, ob2_ref, l0_ref, l1_ref, l2_ref,
                  dnw_ref, pa_ref, pb_ref, wo_ref, fw_ref, out_ref):
    def silu_of_half(h):
        return h + h * jnp.tanh(h)

    gate = mod_ref[0][2:3]

    def stages(rows):
        oa = oa_ref[0, rows, :].astype(F32)
        za_half = za_ref[0, rows, :].astype(F32)
        parts = []
        for h in range(DN_HEADS):
            cq = slice(h * DN_DV, (h + 1) * DN_DV)
            blk = oa[:, cq]
            y = blk * lax.rsqrt(jnp.mean(blk * blk, axis=-1, keepdims=True) + EPS) * dnw_ref[...]
            parts.append(y * silu_of_half(za_half[:, cq]))
        ya_half = _mm(jnp.concatenate(parts, axis=1), pa_ref[...])
        yield

        l0, l1, l2 = l0_ref[0, rows, :], l1_ref[0, rows, :], l2_ref[0, rows, :]
        m = jnp.maximum(jnp.maximum(l0, l1), l2)
        e0, e1, e2 = jnp.exp(l0 - m), jnp.exp(l1 - m), jnp.exp(l2 - m)
        inv_den = 1.0 / (e0 + e1 + e2)
        ob = jnp.concatenate([ob0_ref[0, rows, :].astype(F32) * (e0 * inv_den),
                              ob1_ref[0, rows, :].astype(F32) * (e1 * inv_den),
                              ob2_ref[0, rows, :].astype(F32) * (e2 * inv_den)], axis=1)
        yb_half = _mm(ob * silu_of_half(zb_ref[0, rows, :].astype(F32)), pb_ref[...])
        yield

        merged = (ya_half + ya_half * jnp.tanh(ga_ref[0, rows, :].astype(F32))
                  + (yb_half + yb_half * jnp.tanh(gb_ref[0, rows, :].astype(F32))))
        delta = _mm(merged, wo_ref[...])
        yield

        xo = x_ref[0, rows, :] + gate * delta
        out_ref[0, rows, :] = xo * lax.rsqrt(jnp.mean(xo * xo, axis=-1, keepdims=True) + EPS) * fw_ref[...]
        yield

    tm = x_ref.shape[1]
    sub = tm // MERGE_SUBTILES
    _run_round_robin(*[stages(slice(i * sub, (i + 1) * sub)) for i in range(MERGE_SUBTILES)])


def _merge(x, mod3, o_a, p, obs, lses, dn_norm_w, pa, pb, wo, final_w, tm, cols):
    b, s, d = x.shape
    gw = ATTN_GROUP_WIDTH

    def row_spec(width, blk):
        return pl.BlockSpec((1, tm, width), lambda bi, i: (bi, i, blk))

    def full_spec(shape):
        return pl.BlockSpec(shape, lambda bi, i: (0,) * len(shape))

    return pl.pallas_call(
        _merge_kernel,
        grid=(b, s // tm),
        in_specs=[row_spec(d, 0),
                  pl.BlockSpec((1, 3, d), lambda bi, i: (bi, 0, 0)),
                  row_spec(DN_WIDTH, 0),
                  row_spec(DN_WIDTH, cols["za"] // DN_WIDTH),
                  row_spec(d, cols["ga"] // d),
                  row_spec(d, cols["gb"] // d),
                  row_spec(ATTN_WIDTH, cols["zb"] // ATTN_WIDTH),
                  row_spec(gw, 0), row_spec(gw, 0), row_spec(gw, 0),
                  row_spec(gw, 0), row_spec(gw, 0), row_spec(gw, 0),
                  full_spec((1, DN_DV)), full_spec(pa.shape), full_spec(pb.shape), full_spec(wo.shape),
                  full_spec((1, d))],
        out_specs=row_spec(d, 0),
        out_shape=jax.ShapeDtypeStruct((b, s, d), F32),
        compiler_params=pltpu.CompilerParams(
            dimension_semantics=("parallel", "parallel"), vmem_limit_bytes=VMEM_LIMIT),
        name="merge_outproj",
    )(x, mod3, o_a, p, p, p, p, *obs, *lses, dn_norm_w.reshape(1, DN_DV), pa, pb, wo, final_w.reshape(1, d))


W_PREP_COLS = 128


def _wprep_kernel(wt_ref, wc_ref, wp_ref, ws_ref, *, conv_width, segments, small_off):
    wc_ref[...] = wt_ref[0:conv_width, :].astype(BF16)
    for src, width, dst, scale in segments:
        blk = wt_ref[src:src + width, :]
        wp_ref[dst:dst + width, :] = (blk if scale == 1.0 else blk * scale).astype(BF16)
    n_small = 2 * DN_HEADS
    ws_ref[0:n_small, :] = wt_ref[small_off:small_off + n_small, :].astype(BF16)
    ws_ref[n_small:LANES, :] = jnp.zeros((LANES - n_small, wt_ref.shape[1]), BF16)


def _weight_prep(wt, conv_width, plain_width, segments, small_off):
    n_in, d = wt.shape
    assert d % W_PREP_COLS == 0 and small_off % 16 == 0 and all(sg[0] % 16 == 0 for sg in segments)
    kern = functools.partial(_wprep_kernel, conv_width=conv_width, segments=segments, small_off=small_off)
    return pl.pallas_call(
        kern,
        grid=(d // W_PREP_COLS,),
        in_specs=[pl.BlockSpec((n_in, W_PREP_COLS), lambda i: (0, i))],
        out_specs=[pl.BlockSpec((conv_width, W_PREP_COLS), lambda i: (0, i)),
                   pl.BlockSpec((plain_width, W_PREP_COLS), lambda i: (0, i)),
                   pl.BlockSpec((LANES, W_PREP_COLS), lambda i: (0, i))],
        out_shape=[jax.ShapeDtypeStruct((conv_width, d), BF16),
                   jax.ShapeDtypeStruct((plain_width, d), BF16),
                   jax.ShapeDtypeStruct((LANES, d), BF16)],
        compiler_params=pltpu.CompilerParams(dimension_semantics=("parallel",), vmem_limit_bytes=VMEM_LIMIT),
        name="weight_prep",
    )(wt)


def _layer(x, c, norm_w, ada_w, ada_b, w_in, conv_w, a_log, dt_bias, dn_norm_w, w_proj_a, w_proj_b, w_out,
           final_norm_w):
    b, s, d = x.shape
    qk = DN_HEADS * DN_DK
    splits = (qk, qk, DN_WIDTH, DN_WIDTH, DN_HEADS, DN_HEADS, ATTN_WIDTH, ATTN_WIDTH, ATTN_WIDTH, ATTN_WIDTH, d, d)
    names = ("qa", "ka", "va", "za", "beta", "a", "qb", "kb", "vb", "zb", "ga", "gb")
    offs = dict(zip(names, np.cumsum((0,) + splits[:-1]).tolist()))
    widths = dict(zip(names, splits))
    order = ("za", "ga", "gb", "qb", "kb", "vb", "zb")
    cols, pos = {}, 0
    for n in order:
        assert pos % widths[n] == 0, (n, pos)
        cols[n] = pos
        pos += widths[n]
    assert offs["qa"] == 0 and offs["ka"] == qk and offs["va"] == 2 * qk and offs["a"] == offs["beta"] + DN_HEADS
    halved = ("za", "ga", "gb", "zb")
    segments = tuple((offs[n], widths[n], cols[n], 0.5 if n in halved else 1.0) for n in order)
    w_conv, w_plain, w_small = _weight_prep(w_in.T, 2 * qk + DN_WIDTH, pos, segments, offs["beta"])
    pad_lo = jnp.zeros((DN_HEADS,), F32)
    pad_hi = jnp.zeros((LANES - 2 * DN_HEADS,), F32)
    alog_row = jnp.concatenate([pad_lo, a_log.astype(F32), pad_hi]).reshape(1, LANES)
    dtb_row = jnp.concatenate([pad_lo, dt_bias.astype(F32), pad_hi]).reshape(1, LANES)

    mod3 = _adaln_mod(c, ada_w, ada_b).reshape(b, 3, d)
    p_conv, p, small = _inproj(x, mod3, norm_w, w_conv, w_plain, w_small, conv_w)
    o_a = _deltanet(p_conv, small, alog_row, dtb_row, ts=min(s, 1024), group=8)
    obs, lses = [], []
    for g in range(len(DIL_CONFIGS)):
        o_g, lse_g = _dilated_attention(p, g, cols["qb"])
        obs.append(o_g)
        lses.append(lse_g)
    return _merge(x, mod3, o_a, p, obs, lses, dn_norm_w, (w_proj_a * 0.5).astype(BF16),
                  (w_proj_b * 0.5).astype(BF16), w_out.astype(BF16), final_norm_w, tm=min(s, 512), cols=cols)


def kernel(x, c, norm_w, ada_w, ada_b, w_in, conv_w, a_log, dt_bias, dn_norm_w, w_proj_a, w_proj_b, w_out,
           final_norm_w):
    depth = norm_w.shape[0]
    assert depth == 1, "the final RMSNorm is fused into the single layer's output kernel"
    return _layer(x, c, norm_w[0], ada_w[0], ada_b[0], w_in[0], conv_w[0], a_log[0], dt_bias[0], dn_norm_w[0],
                  w_proj_a[0], w_proj_b[0], w_out[0], final_norm_w)
```

```python
import functools

import jax
import jax.numpy as jnp
import numpy as np
from jax import lax
from jax.experimental import pallas as pl
from jax.experimental.pallas import tpu as pltpu

F32 = jnp.float32
BF16 = jnp.bfloat16
HIGHEST = lax.Precision.HIGHEST

EPS = 1e-6
DN_HEADS = 8
DN_DK = 128
DN_DV = 128
DN_WIDTH = DN_HEADS * DN_DV
CONV_K = 4
CHUNK = 64
DIL_CONFIGS = ((128, 1), (512, 4), (2048, 16))
HEADS_PER_GROUP = 4
ATTN_HEAD_DIM = 64
ATTN_GROUP_WIDTH = HEADS_PER_GROUP * ATTN_HEAD_DIM
ATTN_WIDTH = ATTN_GROUP_WIDTH * len(DIL_CONFIGS)
ATTN_BLOCK = 128
LANES = 128
HALO = 8
VMEM_LIMIT = 56 * 1024 * 1024


def _mm(a, b):
    return jnp.dot(a.astype(BF16), b.astype(BF16), preferred_element_type=F32)


def _mm_nt(a, b):
    return lax.dot_general(a.astype(BF16), b.astype(BF16), (((1,), (1,)), ((), ())),
                           preferred_element_type=F32)


def _mm_tn(a, b):
    return lax.dot_general(a.astype(BF16), b.astype(BF16), (((0,), (0,)), ((), ())),
                           preferred_element_type=F32)


def _run_round_robin(*gens):
    gens = list(gens)
    while gens:
        for g in list(gens):
            if next(g, StopIteration) is StopIteration:
                gens.remove(g)


def _sigmoid(x):
    return 0.5 * jnp.tanh(0.5 * x) + 0.5


def _silu(x):
    return x * _sigmoid(x)


def _mod_kernel(c_ref, w_ref, b_ref, o_ref):
    sc = _silu(c_ref[...])
    o_ref[...] = jnp.dot(sc, w_ref[...], preferred_element_type=F32, precision=HIGHEST) + b_ref[...]


def _adaln_mod(c, ada_w, ada_b):
    b, d = c.shape
    return pl.pallas_call(
        _mod_kernel,
        grid=(3,),
        in_specs=[pl.BlockSpec((b, d), lambda j: (0, 0)),
                  pl.BlockSpec((d, d), lambda j: (0, j)),
                  pl.BlockSpec((1, d), lambda j: (0, j))],
        out_specs=pl.BlockSpec((b, d), lambda j: (0, j)),
        out_shape=jax.ShapeDtypeStruct((b, 3 * d), F32),
        name="adaln_mod",
    )(c, ada_w, ada_b.reshape(1, 3 * d))


CONV_ROWS = 64
CONV_SUB = 256
SUBLANES = 8
TOKEN_CHAINS = 3


def _conv_unit(buf, r0, cl, half_taps, shift_masks, norm_scale, skip_norm):
    a = buf[r0:r0 + HALO + CONV_ROWS, cl]
    n = CONV_ROWS // SUBLANES
    vs = [a[SUBLANES * i:SUBLANES * (i + 1)] for i in range(n + 1)]
    acc = [vs[i + 1] * half_taps[CONV_K - 1] for i in range(n)]
    for shift in range(1, CONV_K):
        rs = [pltpu.roll(v, shift, 0) for v in vs]
        tap = half_taps[CONV_K - 1 - shift]
        for i in range(n):
            acc[i] = acc[i] + jnp.where(shift_masks[shift], rs[i], rs[i + 1]) * tap
    h = jnp.concatenate(acc, axis=0)
    y = h + h * jnp.tanh(h)
    ss = jnp.sum(y * y, axis=-1, keepdims=True)
    out = y * jnp.where(skip_norm, 1.0, lax.rsqrt(ss + EPS) * norm_scale)
    token = jnp.broadcast_to(ss[CONV_ROWS - SUBLANES:], (SUBLANES, LANES))
    return out.astype(BF16), token


def _inproj_kernel(x_ref, mod_ref, nw_ref, wc_ref, wp_ref, ws_ref, cw_ref, pc_ref, pp_ref, small_ref,
                   h_scr, cbuf, tok, *, steps_per_seg):
    j = pl.program_id(1)
    s = x_ref.shape[1]

    @pl.when(j == 0)
    def _():
        x = x_ref[0]
        m = mod_ref[0]
        y = x * lax.rsqrt(jnp.mean(x * x, axis=-1, keepdims=True) + EPS) * nw_ref[...]
        h = (y * (1.0 + m[1:2]) + m[0:1]).astype(BF16)
        h_scr[...] = h
        small_ref[0] = _mm_nt(h, ws_ref[...])
        cbuf[0:HALO, :] = jnp.zeros((HALO, CONV_SUB), F32)

    cbuf[HALO:HALO + s, :] = _mm_nt(h_scr[...], wc_ref[...])
    pp_ref[0] = _mm_nt(h_scr[...], wp_ref[...]).astype(BF16)

    sub = lax.broadcasted_iota(jnp.int32, (SUBLANES, LANES), 0)
    shift_masks = [sub < shift for shift in range(CONV_K)]
    q_scale = jnp.where(j < steps_per_seg, DN_DK ** -0.5, 1.0)
    is_v = j >= 2 * steps_per_seg
    never = j < 0
    for c in range(TOKEN_CHAINS):
        tok[c] = jnp.zeros((SUBLANES, LANES), F32)
    n_unit = 0
    for c0 in range(0, CONV_SUB, DN_DK):
        cl = slice(c0, c0 + DN_DK)
        half_taps = [cw_ref[t:t + 1, cl] * 0.5 for t in range(CONV_K)]
        for r0 in range(0, s, CONV_ROWS):
            slot = n_unit % TOKEN_CHAINS
            held = tok[slot][0:1]
            taps = [jnp.where(never, held, tap) for tap in half_taps]
            out, token = _conv_unit(cbuf, r0, cl, taps, shift_masks, q_scale, is_v)
            pc_ref[0, r0:r0 + CONV_ROWS, cl] = out
            tok[slot] = token
            n_unit += 1


def _inproj(x, mod3, norm_w, w_conv, w_plain, w_small, conv_w):
    b, s, d = x.shape
    conv_width, plain_width = w_conv.shape[0], w_plain.shape[0]
    n_steps = conv_width // CONV_SUB
    tp = plain_width // n_steps
    assert n_steps % 3 == 0 and n_steps * CONV_SUB == conv_width and conv_w.shape[1] == conv_width
    assert tp * n_steps == plain_width and tp % LANES == 0 and s % CONV_ROWS == 0
    kern = functools.partial(_inproj_kernel, steps_per_seg=n_steps // 3)
    return pl.pallas_call(
        kern,
        grid=(b, n_steps),
        in_specs=[pl.BlockSpec((1, s, d), lambda bi, j: (bi, 0, 0)),
                  pl.BlockSpec((1, 3, d), lambda bi, j: (bi, 0, 0)),
                  pl.BlockSpec((1, d), lambda bi, j: (0, 0)),
                  pl.BlockSpec((CONV_SUB, d), lambda bi, j: (j, 0)),
                  pl.BlockSpec((tp, d), lambda bi, j: (j, 0)),
                  pl.BlockSpec((LANES, d), lambda bi, j: (0, 0)),
                  pl.BlockSpec((CONV_K, CONV_SUB), lambda bi, j: (0, j))],
        out_specs=[pl.BlockSpec((1, s, CONV_SUB), lambda bi, j: (bi, 0, j)),
                   pl.BlockSpec((1, s, tp), lambda bi, j: (bi, 0, j)),
                   pl.BlockSpec((1, s, LANES), lambda bi, j: (bi, 0, 0))],
        out_shape=[jax.ShapeDtypeStruct((b, s, conv_width), BF16),
                   jax.ShapeDtypeStruct((b, s, plain_width), BF16),
                   jax.ShapeDtypeStruct((b, s, LANES), F32)],
        scratch_shapes=[pltpu.VMEM((s, d), BF16),
                        pltpu.VMEM((HALO + s, CONV_SUB), F32),
                        pltpu.VMEM((TOKEN_CHAINS, SUBLANES, LANES), F32)],
        compiler_params=pltpu.CompilerParams(
            dimension_semantics=("parallel", "arbitrary"), vmem_limit_bytes=VMEM_LIMIT),
        name="norm_inproj",
    )(x, mod3, norm_w.reshape(1, d), w_conv, w_plain, w_small, conv_w)


def _level_mask(level, i, j):
    same_block = (i >> (level + 1)) == (j >> (level + 1))
    return same_block & (((i >> level) & 1) == 1) & (((j >> level) & 1) == 0)


N_LEVELS = CHUNK.bit_length() - 1


def _deltanet_kernel(q_ref, k_ref, v_ref, small_ref, alog_ref, dtb_ref, o_ref,
                     state, u_s, wq_s, akd_s, el_s, *, ts, group):
    @pl.when(pl.program_id(1) == 0)
    def _():
        state[...] = jnp.zeros_like(state)

    row = lax.broadcasted_iota(jnp.int32, (CHUNK, 2 * CHUNK), 0)
    lane = lax.broadcasted_iota(jnp.int32, (CHUNK, 2 * CHUNK), 1)
    col = lane & (CHUNK - 1)
    in_a = lane < CHUNK
    causal = row >= col
    tri_strict = (row > col).astype(F32)
    eye = (row == col).astype(F32)
    row_c = lax.broadcasted_iota(jnp.int32, (CHUNK, CHUNK), 0)
    col_c = lax.broadcasted_iota(jnp.int32, (CHUNK, CHUNK), 1)
    tri = (row_c >= col_c).astype(F32)
    level_masks = [_level_mask(level, row, col).astype(F32) for level in range(N_LEVELS)]
    neg_a = -jnp.exp(alog_ref[...])
    dtb = dtb_ref[...]
    n_chunks = ts // CHUNK
    n_pairs = DN_HEADS // 2
    zeros_wide = jnp.zeros((CHUNK, DN_DV + DN_DK), BF16)
    zeros_head = jnp.zeros((CHUNK, DN_DK), BF16)

    def blockdiag(m):
        return jnp.concatenate([jnp.where(in_a, m, 0.0), jnp.where(in_a, 0.0, m)], axis=0).astype(BF16)

    def prep_stages(gi):
        chains = []
        for ci in range(group):
            c = gi * group + ci
            r0 = c * CHUNK if isinstance(c, int) else pl.multiple_of(c * CHUNK, CHUNK)
            sm = small_ref[0, pl.ds(r0, CHUNK), :]
            beta_all = _sigmoid(sm)
            z = sm + dtb
            softplus = jnp.maximum(z, 0.0) + jnp.log(1.0 + jnp.exp(-jnp.abs(z)))
            g_all = neg_a * softplus
            gc_all = jnp.dot(tri, g_all, preferred_element_type=F32, precision=HIGHEST)
            gc_t = jnp.concatenate([gc_all, gc_all], axis=0).T
            g_last = gc_all[CHUNK - 1:CHUNK, :]
            e_in_all = jnp.exp(gc_all)
            e_out_all = jnp.exp(g_last - gc_all)
            slot = (gi % 2) * group + ci
            el_s[slot] = jnp.broadcast_to(jnp.exp(g_last), (8, LANES))
            for p in range(n_pairs):
                heads = (2 * p, 2 * p + 1)
                q, k, k_beta, rhs, k_dec = [], [], [], [], []
                for h in heads:
                    gl = DN_HEADS + h
                    cq = slice(h * DN_DK, (h + 1) * DN_DK)
                    q_h = q_ref[0, pl.ds(r0, CHUNK), cq].astype(F32)
                    k_h = k_ref[0, pl.ds(r0, CHUNK), cq].astype(F32)
                    v_h = v_ref[0, pl.ds(r0, CHUNK), slice(h * DN_DV, (h + 1) * DN_DV)].astype(F32)
                    beta = beta_all[:, h:h + 1]
                    e_in = e_in_all[:, gl:gl + 1]
                    kb_h = k_h * beta
                    q.append(q_h)
                    k.append(k_h.astype(BF16))
                    k_beta.append(kb_h)
                    rhs.append(jnp.concatenate([v_h * beta, kb_h * e_in], axis=1).astype(BF16))
                    k_dec.append(k_h * e_out_all[:, gl:gl + 1])
                    wq_s[slot, h, CHUNK:2 * CHUNK, :] = (q_h * e_in).astype(BF16)
                ga, gb = DN_HEADS + heads[0], DN_HEADS + heads[1]
                gc_col = jnp.where(in_a, gc_all[:, ga:ga + 1], gc_all[:, gb:gb + 1])
                gc_row = jnp.where(in_a[0:1], gc_t[ga:ga + 1, :], gc_t[gb:gb + 1, :])
                chains.append(dict(
                    slot=slot, p=p,
                    decay=jnp.exp(jnp.where(causal, gc_col - gc_row, -jnp.inf)),
                    kq=jnp.concatenate([jnp.concatenate(k_beta, axis=1), jnp.concatenate(q, axis=1)],
                                       axis=0).astype(BF16),
                    kk=jnp.concatenate([jnp.concatenate([k[0], zeros_head], axis=1),
                                        jnp.concatenate([zeros_head, k[1]], axis=1)], axis=0),
                    rhs=jnp.concatenate([jnp.concatenate([rhs[0], zeros_wide], axis=1),
                                         jnp.concatenate([zeros_wide, rhs[1]], axis=1)], axis=0),
                    k_dec_t=jnp.concatenate(k_dec, axis=0).T.astype(BF16)))

        for ch in chains:
            ch["kk_qk"] = _mm_nt(ch.pop("kq"), ch.pop("kk"))
        yield
        for ch in chains:
            kk_qk = ch.pop("kk_qk")
            decay = ch.pop("decay")
            ch["lmat"] = kk_qk[0:CHUNK] * decay * tri_strict
            ch["a_qk"] = (kk_qk[CHUNK:2 * CHUNK] * decay).astype(BF16)
            ch["x"] = eye - ch["lmat"] * level_masks[0]
        for level in range(1, N_LEVELS):
            for ch in chains:
                ch["y"] = _mm(ch["lmat"] * level_masks[level], blockdiag(ch["x"]))
            yield
            for ch in chains:
                ch["x"] = ch["x"] - _mm(ch["x"], blockdiag(ch.pop("y")))
            yield
        for ch in chains:
            ch["uw"] = _mm(ch.pop("x"), ch.pop("rhs"))
        for ch in chains:
            slot, p = ch["slot"], ch["p"]
            uw = ch.pop("uw")
            for i, h in enumerate((2 * p, 2 * p + 1)):
                base = i * (DN_DV + DN_DK)
                u_s[slot, h] = uw[:, base:base + DN_DV]
                wq_s[slot, h, 0:CHUNK, :] = uw[:, base + DN_DV:base + DN_DV + DN_DK].astype(BF16)
            akd_s[slot, p] = jnp.concatenate([ch["a_qk"], ch["k_dec_t"]], axis=0)
        yield

    def rec_stages(gi):
        for ci in range(group):
            c = gi * group + ci
            r0 = c * CHUNK if isinstance(c, int) else pl.multiple_of(c * CHUNK, CHUNK)
            slot = (gi % 2) * group + ci
            e_last_all = el_s[slot]
            rs = [_mm(wq_s[slot, h], state[h]) for h in range(DN_HEADS)]
            yield
            v_new = [(u_s[slot, h] - rs[h][0:CHUNK]).astype(BF16) for h in range(DN_HEADS)]
            zeros_v = jnp.zeros((CHUNK, DN_DV), BF16)
            av = [_mm(akd_s[slot, p],
                      jnp.concatenate([jnp.concatenate([v_new[2 * p], zeros_v], axis=1),
                                       jnp.concatenate([zeros_v, v_new[2 * p + 1]], axis=1)], axis=0))
                  for p in range(n_pairs)]
            for h in range(DN_HEADS):
                gl = DN_HEADS + h
                lanes_h = slice((h % 2) * DN_DV, (h % 2 + 1) * DN_DV)
                o = rs[h][CHUNK:2 * CHUNK] + av[h // 2][0:CHUNK, lanes_h]
                state[h] = state[h] * e_last_all[0:1, gl:gl + 1] + av[h // 2][CHUNK:CHUNK + DN_DK, lanes_h]
                o_ref[0, pl.ds(r0, CHUNK), slice(h * DN_DV, (h + 1) * DN_DV)] = o.astype(o_ref.dtype)
            yield

    run = _run_round_robin

    n_groups = n_chunks // group
    run(prep_stages(0))

    def both(gi, carry):
        run(prep_stages(gi), rec_stages(gi - 1))
        return carry

    lax.fori_loop(1, n_groups, both, 0)
    run(rec_stages(n_groups - 1))


def _deltanet(p, small, alog_row, dtb_row, ts, group):
    b, s, _ = p.shape
    width = DN_HEADS * DN_DK
    n_slots = 2 * group
    assert (ts // CHUNK) % group == 0
    kern = functools.partial(_deltanet_kernel, ts=ts, group=group)
    return pl.pallas_call(
        kern,
        grid=(b, s // ts),
        in_specs=[pl.BlockSpec((1, ts, width), lambda bi, t: (bi, t, 0)),
                  pl.BlockSpec((1, ts, width), lambda bi, t: (bi, t, 1)),
                  pl.BlockSpec((1, ts, width), lambda bi, t: (bi, t, 2)),
                  pl.BlockSpec((1, ts, LANES), lambda bi, t: (bi, t, 0)),
                  pl.BlockSpec((1, LANES), lambda bi, t: (0, 0)),
                  pl.BlockSpec((1, LANES), lambda bi, t: (0, 0))],
        out_specs=pl.BlockSpec((1, ts, DN_WIDTH), lambda bi, t: (bi, t, 0)),
        out_shape=jax.ShapeDtypeStruct((b, s, DN_WIDTH), BF16),
        scratch_shapes=[pltpu.VMEM((DN_HEADS, DN_DK, DN_DV), F32),
                        pltpu.VMEM((n_slots, DN_HEADS, CHUNK, DN_DV), F32),
                        pltpu.VMEM((n_slots, DN_HEADS, 2 * CHUNK, DN_DK), BF16),
                        pltpu.VMEM((n_slots, DN_HEADS // 2, CHUNK + DN_DK, 2 * CHUNK), BF16),
                        pltpu.VMEM((n_slots, 8, LANES), F32)],
        compiler_params=pltpu.CompilerParams(
            dimension_semantics=("parallel", "arbitrary"), vmem_limit_bytes=VMEM_LIMIT),
        name="gated_deltanet",
    )(p, p, p, small, alog_row, dtb_row)


ATTN_BLOCKS_PER_ITER = 4
ATTN_STRIDE = 4


def _attn_kernel(q_ref, k_ref, v_ref, o_ref, lse_ref, *scratch, dil, n_back):
    s = q_ref.shape[1]
    sub_len = s // dil
    nb = sub_len // ATTN_BLOCK
    n_pairs = ATTN_GROUP_WIDTH // LANES
    prev_block = nb > 1

    def lanes(pair):
        return slice(pair * LANES, (pair + 1) * LANES)

    if dil > 1:
        stage, tmp, qs, ks, vs, os_, ls = scratch
        assert dil in (ATTN_STRIDE, ATTN_STRIDE * ATTN_STRIDE)
        q_len = s // ATTN_STRIDE

        def classes():
            if dil == ATTN_STRIDE:
                return [(r, stage, r) for r in range(dil)]
            return [(r_lo + ATTN_STRIDE * r_hi, tmp, r_lo * q_len + r_hi)
                    for r_lo in range(ATTN_STRIDE) for r_hi in range(ATTN_STRIDE)]

        def deinterleave(dst, pair):
            if dil > ATTN_STRIDE:
                for r_lo in range(ATTN_STRIDE):
                    tmp[r_lo * q_len:(r_lo + 1) * q_len, :] = stage[pl.ds(r_lo, q_len, stride=ATTN_STRIDE), :]
            for r, buf, start in classes():
                dst[pair, r * sub_len:(r + 1) * sub_len, :] = (
                    buf[pl.ds(start, sub_len, stride=ATTN_STRIDE), :].astype(dst.dtype))

        def interleave(src, pair):
            for r, buf, start in classes():
                buf[pl.ds(start, sub_len, stride=ATTN_STRIDE), :] = src[pair, r * sub_len:(r + 1) * sub_len, :]
            if dil > ATTN_STRIDE:
                for r_lo in range(ATTN_STRIDE):
                    stage[pl.ds(r_lo, q_len, stride=ATTN_STRIDE), :] = tmp[r_lo * q_len:(r_lo + 1) * q_len, :]

        for src, dst in ((q_ref, qs), (k_ref, ks), (v_ref, vs)):
            for pair in range(n_pairs):
                stage[...] = src[0, :, lanes(pair)].astype(F32)
                deinterleave(dst, pair)

        def load(ref, src, pair, r0):
            return src[pair, pl.ds(r0, ATTN_BLOCK), :]
    else:
        qs, ks, vs = q_ref, k_ref, v_ref

        def load(ref, src, pair, r0):
            return ref[0, pl.ds(r0, ATTN_BLOCK), lanes(pair)]

    n_keys = 2 * ATTN_BLOCK if prev_block else ATTN_BLOCK
    qi = lax.broadcasted_iota(jnp.int32, (ATTN_BLOCK, n_keys), 0)
    ki = lax.broadcasted_iota(jnp.int32, (ATTN_BLOCK, n_keys), 1)
    dist = qi + (n_keys - ATTN_BLOCK) - ki
    band = (dist >= 0) & (dist <= n_back)
    bias_full = jnp.where(band, 0.0, -jnp.inf).astype(F32)
    bias_first = jnp.where(band & (ki >= ATTN_BLOCK), 0.0, -jnp.inf).astype(F32)
    lane = lax.broadcasted_iota(jnp.int32, (ATTN_BLOCK, LANES), 1)
    low_half = lane < ATTN_HEAD_DIM
    scale = ATTN_HEAD_DIM ** -0.5

    def block_body(fi, carry):
        chains = []
        for bi in range(ATTN_BLOCKS_PER_ITER):
            f = fi * ATTN_BLOCKS_PER_ITER + bi
            q0 = pl.multiple_of(f * ATTN_BLOCK, ATTN_BLOCK)
            if prev_block:
                k0 = pl.multiple_of(jnp.maximum(f - 1, 0) * ATTN_BLOCK, ATTN_BLOCK)
                bias = jnp.where(f % nb > 0, bias_full, bias_first)
            else:
                bias = bias_full
            for pair in range(n_pairs):
                qp = load(q_ref, qs, pair, q0) * scale
                kp = load(k_ref, ks, pair, q0)
                vp = load(v_ref, vs, pair, q0)
                if prev_block:
                    kp = jnp.concatenate([load(k_ref, ks, pair, k0), kp], axis=0)
                    vp = jnp.concatenate([load(v_ref, vs, pair, k0), vp], axis=0)
                for half in range(2):
                    sel = low_half if half == 0 else ~low_half
                    chains.append(dict(q0=q0, pair=pair, half=half, bias=bias, kp=kp, vp=vp,
                                       qm=jnp.where(sel, qp, jnp.zeros_like(qp))))
        for ch in chains:
            ch["sc"] = _mm_nt(ch.pop("qm"), ch.pop("kp"))
        for ch in chains:
            sc = ch.pop("sc") + ch.pop("bias")
            m = jnp.max(sc, axis=-1, keepdims=True)
            p = jnp.exp(sc - m)
            ch["denom"] = jnp.sum(p, axis=-1, keepdims=True)
            ch["m"] = m
            ch["p"] = p.astype(BF16)
        for ch in chains:
            ch["pv"] = _mm(ch.pop("p"), ch.pop("vp"))
        for c0 in range(0, len(chains), 2):
            lo, hi = chains[c0], chains[c0 + 1]
            o = jnp.where(low_half, lo["pv"] / lo["denom"], hi["pv"] / hi["denom"])
            lse = jnp.where(low_half, lo["m"] + jnp.log(lo["denom"]), hi["m"] + jnp.log(hi["denom"]))
            q0, pair = lo["q0"], lo["pair"]
            if dil > 1:
                os_[pair, pl.ds(q0, ATTN_BLOCK), :] = o
                ls[pair, pl.ds(q0, ATTN_BLOCK), :] = lse
            else:
                o_ref[0, pl.ds(q0, ATTN_BLOCK), lanes(pair)] = o.astype(o_ref.dtype)
                lse_ref[0, pl.ds(q0, ATTN_BLOCK), lanes(pair)] = lse
        return carry

    lax.fori_loop(0, dil * nb // ATTN_BLOCKS_PER_ITER, block_body, 0)

    if dil > 1:
        for pair in range(n_pairs):
            for src, dst in ((os_, o_ref), (ls, lse_ref)):
                interleave(src, pair)
                dst[0, :, lanes(pair)] = stage[...].astype(dst.dtype)


def _dilated_attention(p, group, col_base):
    b, s, _ = p.shape
    window, dil = DIL_CONFIGS[group]
    gw = ATTN_GROUP_WIDTH
    blk = col_base // gw + group
    step = ATTN_WIDTH // gw
    kern = functools.partial(_attn_kernel, dil=dil, n_back=window // dil)
    return pl.pallas_call(
        kern,
        grid=(b,),
        in_specs=[pl.BlockSpec((1, s, gw), lambda bi: (bi, 0, blk)),
                  pl.BlockSpec((1, s, gw), lambda bi: (bi, 0, blk + step)),
                  pl.BlockSpec((1, s, gw), lambda bi: (bi, 0, blk + 2 * step))],
        out_specs=[pl.BlockSpec((1, s, gw), lambda bi: (bi, 0, 0)),
                   pl.BlockSpec((1, s, gw), lambda bi: (bi, 0, 0))],
        out_shape=[jax.ShapeDtypeStruct((b, s, gw), BF16),
                   jax.ShapeDtypeStruct((b, s, gw), F32)],
        scratch_shapes=([] if dil == 1 else
                        [pltpu.VMEM((s, LANES), F32) for _ in range(2)]
                        + [pltpu.VMEM((gw // LANES, s, LANES), BF16) for _ in range(3)]
                        + [pltpu.VMEM((gw // LANES, s, LANES), F32) for _ in range(2)]),
        compiler_params=pltpu.CompilerParams(
            dimension_semantics=("parallel",), vmem_limit_bytes=VMEM_LIMIT),
        name=f"dilated_attn_g{group}",
    )(p, p, p)


MERGE_SUBTILES = 4


def _merge_kernel(x_ref, mod_ref, oa_ref, za_ref, ga_ref, gb_ref, zb_ref,
                  ob0_ref, ob1_ref, ob2_ref, l0_ref, l1_ref, l2_ref,
                  dnw_ref, pa_ref, pb_ref, wo_ref, fw_ref, out_ref):
    def silu_of_half(h):
        return h + h * jnp.tanh(h)

    gate = mod_ref[0][2:3]

    def stages(rows):
        oa = oa_ref[0, rows, :].astype(F32)
        za_half = za_ref[0, rows, :].astype(F32)
        parts = []
        for h in range(DN_HEADS):
            cq = slice(h * DN_DV, (h + 1) * DN_DV)
            blk = oa[:, cq]
            y = blk * lax.rsqrt(jnp.mean(blk * blk, axis=-1, keepdims=True) + EPS) * dnw_ref[...]
            parts.append(y * silu_of_half(za_half[:, cq]))
        ya_half = _mm(jnp.concatenate(parts, axis=1), pa_ref[...])
        yield

        l0, l1, l2 = l0_ref[0, rows, :], l1_ref[0, rows, :], l2_ref[0, rows, :]
        m = jnp.maximum(jnp.maximum(l0, l1), l2)
        e0, e1, e2 = jnp.exp(l0 - m), jnp.exp(l1 - m), jnp.exp(l2 - m)
        inv_den = 1.0 / (e0 + e1 + e2)
        ob = jnp.concatenate([ob0_ref[0, rows, :].astype(F32) * (e0 * inv_den),
                              ob1_ref[0, rows, :].astype(F32) * (e1 * inv_den),
                              ob2_ref[0, rows, :].astype(F32) * (e2 * inv_den)], axis=1)
        yb_half = _mm(ob * silu_of_half(zb_ref[0, rows, :].astype(F32)), pb_ref[...])
        yield

        merged = (ya_half + ya_half * jnp.tanh(ga_ref[0, rows, :].astype(F32))
                  + (yb_half + yb_half * jnp.tanh(gb_ref[0, rows, :].astype(F32))))
        delta = _mm(merged, wo_ref[...])
        yield

        xo = x_ref[0, rows, :] + gate * delta
        out_ref[0, rows, :] = xo * lax.rsqrt(jnp.mean(xo * xo, axis=-1, keepdims=True) + EPS) * fw_ref[...]
        yield

    tm = x_ref.shape[1]
    sub = tm // MERGE_SUBTILES
    _run_round_robin(*[stages(slice(i * sub, (i + 1) * sub)) for i in range(MERGE_SUBTILES)])


def _merge(x, mod3, o_a, p, obs, lses, dn_norm_w, pa, pb, wo, final_w, tm, cols):
    b, s, d = x.shape
    gw = ATTN_GROUP_WIDTH

    def row_spec(width, blk):
        return pl.BlockSpec((1, tm, width), lambda bi, i: (bi, i, blk))

    def full_spec(shape):
        return pl.BlockSpec(shape, lambda bi, i: (0,) * len(shape))

    return pl.pallas_call(
        _merge_kernel,
        grid=(b, s // tm),
        in_specs=[row_spec(d, 0),
                  pl.BlockSpec((1, 3, d), lambda bi, i: (bi, 0, 0)),
                  row_spec(DN_WIDTH, 0),
                  row_spec(DN_WIDTH, cols["za"] // DN_WIDTH),
                  row_spec(d, cols["ga"] // d),
                  row_spec(d, cols["gb"] // d),
                  row_spec(ATTN_WIDTH, cols["zb"] // ATTN_WIDTH),
                  row_spec(gw, 0), row_spec(gw, 0), row_spec(gw, 0),
                  row_spec(gw, 0), row_spec(gw, 0), row_spec(gw, 0),
                  full_spec((1, DN_DV)), full_spec(pa.shape), full_spec(pb.shape), full_spec(wo.shape),
                  full_spec((1, d))],
        out_specs=row_spec(d, 0),
        out_shape=jax.ShapeDtypeStruct((b, s, d), F32),
        compiler_params=pltpu.CompilerParams(
            dimension_semantics=("parallel", "parallel"), vmem_limit_bytes=VMEM_LIMIT),
        name="merge_outproj",
    )(x, mod3, o_a, p, p, p, p, *obs, *lses, dn_norm_w.reshape(1, DN_DV), pa, pb, wo, final_w.reshape(1, d))


W_PREP_COLS = 128


def _wprep_kernel(wt_ref, wc_ref, wp_ref, ws_ref, *, conv_width, segments, small_off):
    wc_ref[...] = wt_ref[0:conv_width, :].astype(BF16)
    for src, width, dst, scale in segments:
        blk = wt_ref[src:src + width, :]
        wp_ref[dst:dst + width, :] = (blk if scale == 1.0 else blk * scale).astype(BF16)
    n_small = 2 * DN_HEADS
    ws_ref[0:n_small, :] = wt_ref[small_off:small_off + n_small, :].astype(BF16)
    ws_ref[n_small:LANES, :] = jnp.zeros((LANES - n_small, wt_ref.shape[1]), BF16)


def _weight_prep(wt, conv_width, plain_width, segments, small_off):
    n_in, d = wt.shape
    assert d % W_PREP_COLS == 0 and small_off % 16 == 0 and all(sg[0] % 16 == 0 for sg in segments)
    kern = functools.partial(_wprep_kernel, conv_width=conv_width, segments=segments, small_off=small_off)
    return pl.pallas_call(
        kern,
        grid=(d // W_PREP_COLS,),
        in_specs=[pl.BlockSpec((n_in, W_PREP_COLS), lambda i: (0, i))],
        out_specs=[pl.BlockSpec((conv_width, W_PREP_COLS), lambda i: (0, i)),
                   pl.BlockSpec((plain_width, W_PREP_COLS), lambda i: (0, i)),
                   pl.BlockSpec((LANES, W_PREP_COLS), lambda i: (0, i))],
        out_shape=[jax.ShapeDtypeStruct((conv_width, d), BF16),
                   jax.ShapeDtypeStruct((plain_width, d), BF16),
                   jax.ShapeDtypeStruct((LANES, d), BF16)],
        compiler_params=pltpu.CompilerParams(dimension_semantics=("parallel",), vmem_limit_bytes=VMEM_LIMIT),
        name="weight_prep",
    )(wt)


def _layer(x, c, norm_w, ada_w, ada_b, w_in, conv_w, a_log, dt_bias, dn_norm_w, w_proj_a, w_proj_b, w_out,
           final_norm_w):
    b, s, d = x.shape
    qk = DN_HEADS * DN_DK
    splits = (qk, qk, DN_WIDTH, DN_WIDTH, DN_HEADS, DN_HEADS, ATTN_WIDTH, ATTN_WIDTH, ATTN_WIDTH, ATTN_WIDTH, d, d)
    names = ("qa", "ka", "va", "za", "beta", "a", "qb", "kb", "vb", "zb", "ga", "gb")
    offs = dict(zip(names, np.cumsum((0,) + splits[:-1]).tolist()))
    widths = dict(zip(names, splits))
    order = ("za", "ga", "gb", "qb", "kb", "vb", "zb")
    cols, pos = {}, 0
    for n in order:
        assert pos % widths[n] == 0, (n, pos)
        cols[n] = pos
        pos += widths[n]
    assert offs["qa"] == 0 and offs["ka"] == qk and offs["va"] == 2 * qk and offs["a"] == offs["beta"] + DN_HEADS
    halved = ("za", "ga", "gb", "zb")
    segments = tuple((offs[n], widths[n], cols[n], 0.5 if n in halved else 1.0) for n in order)
    w_conv, w_plain, w_small = _weight_prep(w_in.T, 2 * qk + DN_WIDTH, pos, segments, offs["beta"])
    pad_lo = jnp.zeros((DN_HEADS,), F32)
    pad_hi = jnp.zeros((LANES - 2 * DN_HEADS,), F32)
    alog_row = jnp.concatenate([pad_lo, a_log.astype(F32), pad_hi]).reshape(1, LANES)
    dtb_row = jnp.concatenate([pad_lo, dt_bias.astype(F32), pad_hi]).reshape(1, LANES)

    mod3 = _adaln_mod(c, ada_w, ada_b).reshape(b, 3, d)
    p_conv, p, small = _inproj(x, mod3, norm_w, w_conv, w_plain, w_small, conv_w)
    o_a = _deltanet(p_conv, small, alog_row, dtb_row, ts=min(s, 1024), group=8)
    obs, lses = [], []
    for g in range(len(DIL_CONFIGS)):
        o_g, lse_g = _dilated_attention(p, g, cols["qb"])
        obs.append(o_g)
        lses.append(lse_g)
    return _merge(x, mod3, o_a, p, obs, lses, dn_norm_w, (w_proj_a * 0.5).astype(BF16),
                  (w_proj_b * 0.5).astype(BF16), w_out.astype(BF16), final_norm_w, tm=min(s, 512), cols=cols)


def kernel(x, c, norm_w, ada_w, ada_b, w_in, conv_w, a_log, dt_bias, dn_norm_w, w_proj_a, w_proj_b, w_out,
           final_norm_w):
    depth = norm_w.shape[0]
    assert depth == 1, "the final RMSNorm is fused into the single layer's output kernel"
    return _layer(x, c, norm_w[0], ada_w[0], ada_b[0], w_in[0], conv_w[0], a_log[0], dt_bias[0], dn_norm_w[0],
                  w_proj_a[0], w_proj_b[0], w_out[0], final_norm_w)
```

```python
import functools

import jax
import jax.numpy as jnp
import numpy as np
from jax import lax
from jax.experimental import pallas as pl
from jax.experimental.pallas import tpu as pltpu

F32 = jnp.float32
BF16 = jnp.bfloat16
HIGHEST = lax.Precision.HIGHEST

EPS = 1e-6
DN_HEADS = 8
DN_DK = 128
DN_DV = 128
DN_WIDTH = DN_HEADS * DN_DV
CONV_K = 4
CHUNK = 64
DIL_CONFIGS = ((128, 1), (512, 4), (2048, 16))
HEADS_PER_GROUP = 4
ATTN_HEAD_DIM = 64
ATTN_GROUP_WIDTH = HEADS_PER_GROUP * ATTN_HEAD_DIM
ATTN_WIDTH = ATTN_GROUP_WIDTH * len(DIL_CONFIGS)
ATTN_BLOCK = 128
LANES = 128
HALO = 8
VMEM_LIMIT = 56 * 1024 * 1024


def _mm(a, b):
    return jnp.dot(a.astype(BF16), b.astype(BF16), preferred_element_type=F32)


def _mm_nt(a, b):
    return lax.dot_general(a.astype(BF16), b.astype(BF16), (((1,), (1,)), ((), ())),
                           preferred_element_type=F32)


def _mm_tn(a, b):
    return lax.dot_general(a.astype(BF16), b.astype(BF16), (((0,), (0,)), ((), ())),
                           preferred_element_type=F32)


def _run_round_robin(*gens):
    gens = list(gens)
    while gens:
        for g in list(gens):
            if next(g, StopIteration) is StopIteration:
                gens.remove(g)


def _sigmoid(x):
    return 0.5 * jnp.tanh(0.5 * x) + 0.5


def _silu(x):
    return x * _sigmoid(x)


def _mod_kernel(c_ref, w_ref, b_ref, o_ref):
    sc = _silu(c_ref[...])
    o_ref[...] = jnp.dot(sc, w_ref[...], preferred_element_type=F32, precision=HIGHEST) + b_ref[...]


def _adaln_mod(c, ada_w, ada_b):
    b, d = c.shape
    return pl.pallas_call(
        _mod_kernel,
        grid=(3,),
        in_specs=[pl.BlockSpec((b, d), lambda j: (0, 0)),
                  pl.BlockSpec((d, d), lambda j: (0, j)),
                  pl.BlockSpec((1, d), lambda j: (0, j))],
        out_specs=pl.BlockSpec((b, d), lambda j: (0, j)),
        out_shape=jax.ShapeDtypeStruct((b, 3 * d), F32),
        name="adaln_mod",
    )(c, ada_w, ada_b.reshape(1, 3 * d))


CONV_ROWS = 64
CONV_SUB = 256
SUBLANES = 8
TOKEN_CHAINS = 3


def _conv_unit(buf, r0, cl, half_taps, shift_masks, norm_scale, skip_norm):
    a = buf[r0:r0 + HALO + CONV_ROWS, cl]
    n = CONV_ROWS // SUBLANES
    vs = [a[SUBLANES * i:SUBLANES * (i + 1)] for i in range(n + 1)]
    acc = [vs[i + 1] * half_taps[CONV_K - 1] for i in range(n)]
    for shift in range(1, CONV_K):
        rs = [pltpu.roll(v, shift, 0) for v in vs]
        tap = half_taps[CONV_K - 1 - shift]
        for i in range(n):
            acc[i] = acc[i] + jnp.where(shift_masks[shift], rs[i], rs[i + 1]) * tap
    h = jnp.concatenate(acc, axis=0)
    y = h + h * jnp.tanh(h)
    ss = jnp.sum(y * y, axis=-1, keepdims=True)
    out = y * jnp.where(skip_norm, 1.0, lax.rsqrt(ss + EPS) * norm_scale)
    token = jnp.broadcast_to(ss[CONV_ROWS - SUBLANES:], (SUBLANES, LANES))
    return out.astype(BF16), token


def _inproj_kernel(x_ref, mod_ref, nw_ref, wc_ref, wp_ref, ws_ref, cw_ref, pc_ref, pp_ref, small_ref,
                   h_scr, cbuf, tok, *, steps_per_seg):
    j = pl.program_id(1)
    s = x_ref.shape[1]

    @pl.when(j == 0)
    def _():
        x = x_ref[0]
        m = mod_ref[0]
        y = x * lax.rsqrt(jnp.mean(x * x, axis=-1, keepdims=True) + EPS) * nw_ref[...]
        h = (y * (1.0 + m[1:2]) + m[0:1]).astype(BF16)
        h_scr[...] = h
        small_ref[0] = _mm_nt(h, ws_ref[...])
        cbuf[0:HALO, :] = jnp.zeros((HALO, CONV_SUB), F32)

    cbuf[HALO:HALO + s, :] = _mm_nt(h_scr[...], wc_ref[...])
    pp_ref[0] = _mm_nt(h_scr[...], wp_ref[...]).astype(BF16)

    sub = lax.broadcasted_iota(jnp.int32, (SUBLANES, LANES), 0)
    shift_masks = [sub < shift for shift in range(CONV_K)]
    q_scale = jnp.where(j < steps_per_seg, DN_DK ** -0.5, 1.0)
    is_v = j >= 2 * steps_per_seg
    never = j < 0
    for c in range(TOKEN_CHAINS):
        tok[c] = jnp.zeros((SUBLANES, LANES), F32)
    n_unit = 0
    for c0 in range(0, CONV_SUB, DN_DK):
        cl = slice(c0, c0 + DN_DK)
        half_taps = [cw_ref[t:t + 1, cl] * 0.5 for t in range(CONV_K)]
        for r0 in range(0, s, CONV_ROWS):
            slot = n_unit % TOKEN_CHAINS
            held = tok[slot][0:1]
            taps = [jnp.where(never, held, tap) for tap in half_taps]
            out, token = _conv_unit(cbuf, r0, cl, taps, shift_masks, q_scale, is_v)
            pc_ref[0, r0:r0 + CONV_ROWS, cl] = out
            tok[slot] = token
            n_unit += 1


def _inproj(x, mod3, norm_w, w_conv, w_plain, w_small, conv_w):
    b, s, d = x.shape
    conv_width, plain_width = w_conv.shape[0], w_plain.shape[0]
    n_steps = conv_width // CONV_SUB
    tp = plain_width // n_steps
    assert n_steps % 3 == 0 and n_steps * CONV_SUB == conv_width and conv_w.shape[1] == conv_width
    assert tp * n_steps == plain_width and tp % LANES == 0 and s % CONV_ROWS == 0
    kern = functools.partial(_inproj_kernel, steps_per_seg=n_steps // 3)
    return pl.pallas_call(
        kern,
        grid=(b, n_steps),
        in_specs=[pl.BlockSpec((1, s, d), lambda bi, j: (bi, 0, 0)),
                  pl.BlockSpec((1, 3, d), lambda bi, j: (bi, 0, 0)),
                  pl.BlockSpec((1, d), lambda bi, j: (0, 0)),
                  pl.BlockSpec((CONV_SUB, d), lambda bi, j: (j, 0)),
                  pl.BlockSpec((tp, d), lambda bi, j: (j, 0)),
                  pl.BlockSpec((LANES, d), lambda bi, j: (0, 0)),
                  pl.BlockSpec((CONV_K, CONV_SUB), lambda bi, j: (0, j))],
        out_specs=[pl.BlockSpec((1, s, CONV_SUB), lambda bi, j: (bi, 0, j)),
                   pl.BlockSpec((1, s, tp), lambda bi, j: (bi, 0, j)),
                   pl.BlockSpec((1, s, LANES), lambda bi, j: (bi, 0, 0))],
        out_shape=[jax.ShapeDtypeStruct((b, s, conv_width), BF16),
                   jax.ShapeDtypeStruct((b, s, plain_width), BF16),
                   jax.ShapeDtypeStruct((b, s, LANES), F32)],
        scratch_shapes=[pltpu.VMEM((s, d), BF16),
                        pltpu.VMEM((HALO + s + SUBLANES, CONV_SUB), F32),
                        pltpu.VMEM((TOKEN_CHAINS + 1, SUBLANES, LANES), F32)],
        compiler_params=pltpu.CompilerParams(
            dimension_semantics=("parallel", "arbitrary"), vmem_limit_bytes=VMEM_LIMIT),
        name="norm_inproj",
    )(x, mod3, norm_w.reshape(1, d), w_conv, w_plain, w_small, conv_w)


def _level_mask(level, i, j):
    same_block = (i >> (level + 1)) == (j >> (level + 1))
    return same_block & (((i >> level) & 1) == 1) & (((j >> level) & 1) == 0)


N_LEVELS = CHUNK.bit_length() - 1


def _deltanet_kernel(q_ref, k_ref, v_ref, small_ref, alog_ref, dtb_ref, o_ref,
                     state, u_s, wq_s, akd_s, el_s, *, ts, group):
    @pl.when(pl.program_id(1) == 0)
    def _():
        state[...] = jnp.zeros_like(state)

    row = lax.broadcasted_iota(jnp.int32, (CHUNK, 2 * CHUNK), 0)
    lane = lax.broadcasted_iota(jnp.int32, (CHUNK, 2 * CHUNK), 1)
    col = lane & (CHUNK - 1)
    in_a = lane < CHUNK
    causal = row >= col
    tri_strict = (row > col).astype(F32)
    eye = (row == col).astype(F32)
    row_c = lax.broadcasted_iota(jnp.int32, (CHUNK, CHUNK), 0)
    col_c = lax.broadcasted_iota(jnp.int32, (CHUNK, CHUNK), 1)
    tri = (row_c >= col_c).astype(F32)
    level_masks = [_level_mask(level, row, col).astype(F32) for level in range(N_LEVELS)]
    neg_a = -jnp.exp(alog_ref[...])
    dtb = dtb_ref[...]
    n_chunks = ts // CHUNK
    n_pairs = DN_HEADS // 2
    zeros_wide = jnp.zeros((CHUNK, DN_DV + DN_DK), BF16)
    zeros_head = jnp.zeros((CHUNK, DN_DK), BF16)

    def blockdiag(m):
        return jnp.concatenate([jnp.where(in_a, m, 0.0), jnp.where(in_a, 0.0, m)], axis=0).astype(BF16)

    def prep_stages(gi):
        chains = []
        for ci in range(group):
            c = gi * group + ci
            r0 = c * CHUNK if isinstance(c, int) else pl.multiple_of(c * CHUNK, CHUNK)
            sm = small_ref[0, pl.ds(r0, CHUNK), :]
            beta_all = _sigmoid(sm)
            z = sm + dtb
            softplus = jnp.maximum(z, 0.0) + jnp.log(1.0 + jnp.exp(-jnp.abs(z)))
            g_all = neg_a * softplus
            gc_all = jnp.dot(tri, g_all, preferred_element_type=F32, precision=HIGHEST)
            gc_t = jnp.concatenate([gc_all, gc_all], axis=0).T
            g_last = gc_all[CHUNK - 1:CHUNK, :]
            e_in_all = jnp.exp(gc_all)
            e_out_all = jnp.exp(g_last - gc_all)
            slot = (gi % 2) * group + ci
            el_s[slot] = jnp.broadcast_to(jnp.exp(g_last), (8, LANES))
            for p in range(n_pairs):
                heads = (2 * p, 2 * p + 1)
                q, k, k_beta, rhs, k_dec = [], [], [], [], []
                for h in heads:
                    gl = DN_HEADS + h
                    cq = slice(h * DN_DK, (h + 1) * DN_DK)
                    q_h = q_ref[0, pl.ds(r0, CHUNK), cq].astype(F32)
                    k_h = k_ref[0, pl.ds(r0, CHUNK), cq].astype(F32)
                    v_h = v_ref[0, pl.ds(r0, CHUNK), slice(h * DN_DV, (h + 1) * DN_DV)].astype(F32)
                    beta = beta_all[:, h:h + 1]
                    e_in = e_in_all[:, gl:gl + 1]
                    kb_h = k_h * beta
                    q.append(q_h)
                    k.append(k_h.astype(BF16))
                    k_beta.append(kb_h)
                    rhs.append(jnp.concatenate([v_h * beta, kb_h * e_in], axis=1).astype(BF16))
                    k_dec.append(k_h * e_out_all[:, gl:gl + 1])
                    wq_s[slot, h, CHUNK:2 * CHUNK, :] = (q_h * e_in).astype(BF16)
                ga, gb = DN_HEADS + heads[0], DN_HEADS + heads[1]
                gc_col = jnp.where(in_a, gc_all[:, ga:ga + 1], gc_all[:, gb:gb + 1])
                gc_row = jnp.where(in_a[0:1], gc_t[ga:ga + 1, :], gc_t[gb:gb + 1, :])
                chains.append(dict(
                    slot=slot, p=p,
                    decay=jnp.exp(jnp.where(causal, gc_col - gc_row, -jnp.inf)),
                    kq=jnp.concatenate([jnp.concatenate(k_beta, axis=1), jnp.concatenate(q, axis=1)],
                                       axis=0).astype(BF16),
                    kk=jnp.concatenate([jnp.concatenate([k[0], zeros_head], axis=1),
                                        jnp.concatenate([zeros_head, k[1]], axis=1)], axis=0),
                    rhs=jnp.concatenate([jnp.concatenate([rhs[0], zeros_wide], axis=1),
                                         jnp.concatenate([zeros_wide, rhs[1]], axis=1)], axis=0),
                    k_dec_t=jnp.concatenate(k_dec, axis=0).T.astype(BF16)))

        for ch in chains:
            ch["kk_qk"] = _mm_nt(ch.pop("kq"), ch.pop("kk"))
        yield
        for ch in chains:
            kk_qk = ch.pop("kk_qk")
            decay = ch.pop("decay")
            ch["lmat"] = kk_qk[0:CHUNK] * decay * tri_strict
            ch["a_qk"] = (kk_qk[CHUNK:2 * CHUNK] * decay).astype(BF16)
            ch["x"] = eye - ch["lmat"] * level_masks[0]
        for level in range(1, N_LEVELS):
            for ch in chains:
                ch["y"] = _mm(ch["lmat"] * level_masks[level], blockdiag(ch["x"]))
            yield
            for ch in chains:
                ch["x"] = ch["x"] - _mm(ch["x"], blockdiag(ch.pop("y")))
            yield
        for ch in chains:
            ch["uw"] = _mm(ch.pop("x"), ch.pop("rhs"))
        for ch in chains:
            slot, p = ch["slot"], ch["p"]
            uw = ch.pop("uw")
            for i, h in enumerate((2 * p, 2 * p + 1)):
                base = i * (DN_DV + DN_DK)
                u_s[slot, h] = uw[:, base:base + DN_DV]
                wq_s[slot, h, 0:CHUNK, :] = uw[:, base + DN_DV:base + DN_DV + DN_DK].astype(BF16)
            akd_s[slot, p] = jnp.concatenate([ch["a_qk"], ch["k_dec_t"]], axis=0)
        yield

    def rec_stages(gi):
        for ci in range(group):
            c = gi * group + ci
            r0 = c * CHUNK if isinstance(c, int) else pl.multiple_of(c * CHUNK, CHUNK)
            slot = (gi % 2) * group + ci
            e_last_all = el_s[slot]
            rs = [_mm(wq_s[slot, h], state[h]) for h in range(DN_HEADS)]
            yield
            v_new = [(u_s[slot, h] - rs[h][0:CHUNK]).astype(BF16) for h in range(DN_HEADS)]
            zeros_v = jnp.zeros((CHUNK, DN_DV), BF16)
            av = [_mm(akd_s[slot, p],
                      jnp.concatenate([jnp.concatenate([v_new[2 * p], zeros_v], axis=1),
                                       jnp.concatenate([zeros_v, v_new[2 * p + 1]], axis=1)], axis=0))
                  for p in range(n_pairs)]
            for h in range(DN_HEADS):
                gl = DN_HEADS + h
                lanes_h = slice((h % 2) * DN_DV, (h % 2 + 1) * DN_DV)
                o = rs[h][CHUNK:2 * CHUNK] + av[h // 2][0:CHUNK, lanes_h]
                state[h] = state[h] * e_last_all[0:1, gl:gl + 1] + av[h // 2][CHUNK:CHUNK + DN_DK, lanes_h]
                o_ref[0, pl.ds(r0, CHUNK), slice(h * DN_DV, (h + 1) * DN_DV)] = o.astype(o_ref.dtype)
            yield

    run = _run_round_robin

    n_groups = n_chunks // group
    run(prep_stages(0))

    def both(gi, carry):
        run(prep_stages(gi), rec_stages(gi - 1))
        return carry

    lax.fori_loop(1, n_groups, both, 0)
    run(rec_stages(n_groups - 1))


def _deltanet(p, small, alog_row, dtb_row, ts, group):
    b, s, _ = p.shape
    width = DN_HEADS * DN_DK
    n_slots = 2 * group
    assert (ts // CHUNK) % group == 0
    kern = functools.partial(_deltanet_kernel, ts=ts, group=group)
    return pl.pallas_call(
        kern,
        grid=(b, s // ts),
        in_specs=[pl.BlockSpec((1, ts, width), lambda bi, t: (bi, t, 0)),
                  pl.BlockSpec((1, ts, width), lambda bi, t: (bi, t, 1)),
                  pl.BlockSpec((1, ts, width), lambda bi, t: (bi, t, 2)),
                  pl.BlockSpec((1, ts, LANES), lambda bi, t: (bi, t, 0)),
                  pl.BlockSpec((1, LANES), lambda bi, t: (0, 0)),
                  pl.BlockSpec((1, LANES), lambda bi, t: (0, 0))],
        out_specs=pl.BlockSpec((1, ts, DN_WIDTH), lambda bi, t: (bi, t, 0)),
        out_shape=jax.ShapeDtypeStruct((b, s, DN_WIDTH), BF16),
        scratch_shapes=[pltpu.VMEM((DN_HEADS, DN_DK, DN_DV), F32),
                        pltpu.VMEM((n_slots, DN_HEADS, CHUNK, DN_DV), F32),
                        pltpu.VMEM((n_slots, DN_HEADS, 2 * CHUNK, DN_DK), BF16),
                        pltpu.VMEM((n_slots, DN_HEADS // 2, CHUNK + DN_DK, 2 * CHUNK), BF16),
                        pltpu.VMEM((n_slots, 8, LANES), F32)],
        compiler_params=pltpu.CompilerParams(
            dimension_semantics=("parallel", "arbitrary"), vmem_limit_bytes=VMEM_LIMIT),
        name="gated_deltanet",
    )(p, p, p, small, alog_row, dtb_row)


ATTN_BLOCKS_PER_ITER = 4
ATTN_STRIDE = 4


def _attn_kernel(q_ref, k_ref, v_ref, o_ref, lse_ref, *scratch, dil, n_back):
    s = q_ref.shape[1]
    sub_len = s // dil
    nb = sub_len // ATTN_BLOCK
    n_pairs = ATTN_GROUP_WIDTH // LANES
    prev_block = nb > 1

    def lanes(pair):
        return slice(pair * LANES, (pair + 1) * LANES)

    if dil > 1:
        stage, tmp, qs, ks, vs, os_, ls = scratch
        assert dil in (ATTN_STRIDE, ATTN_STRIDE * ATTN_STRIDE)
        q_len = s // ATTN_STRIDE

        def classes():
            if dil == ATTN_STRIDE:
                return [(r, stage, r) for r in range(dil)]
            return [(r_lo + ATTN_STRIDE * r_hi, tmp, r_lo * q_len + r_hi)
                    for r_lo in range(ATTN_STRIDE) for r_hi in range(ATTN_STRIDE)]

        def deinterleave(dst, pair):
            if dil > ATTN_STRIDE:
                for r_lo in range(ATTN_STRIDE):
                    tmp[r_lo * q_len:(r_lo + 1) * q_len, :] = stage[pl.ds(r_lo, q_len, stride=ATTN_STRIDE), :]
            for r, buf, start in classes():
                dst[pair, r * sub_len:(r + 1) * sub_len, :] = (
                    buf[pl.ds(start, sub_len, stride=ATTN_STRIDE), :].astype(dst.dtype))

        def interleave(src, pair):
            for r, buf, start in classes():
                buf[pl.ds(start, sub_len, stride=ATTN_STRIDE), :] = src[pair, r * sub_len:(r + 1) * sub_len, :]
            if dil > ATTN_STRIDE:
                for r_lo in range(ATTN_STRIDE):
                    stage[pl.ds(r_lo, q_len, stride=ATTN_STRIDE), :] = tmp[r_lo * q_len:(r_lo + 1) * q_len, :]

        for src, dst in ((q_ref, qs), (k_ref, ks), (v_ref, vs)):
            for pair in range(n_pairs):
                stage[...] = src[0, :, lanes(pair)].astype(F32)
                deinterleave(dst, pair)

        def load(ref, src, pair, r0):
            return src[pair, pl.ds(r0, ATTN_BLOCK), :]
    else:
        qs, ks, vs = q_ref, k_ref, v_ref

        def load(ref, src, pair, r0):
            return ref[0, pl.ds(r0, ATTN_BLOCK), lanes(pair)]

    n_keys = 2 * ATTN_BLOCK if prev_block else ATTN_BLOCK
    qi = lax.broadcasted_iota(jnp.int32, (ATTN_BLOCK, n_keys), 0)
    ki = lax.broadcasted_iota(jnp.int32, (ATTN_BLOCK, n_keys), 1)
    dist = qi + (n_keys - ATTN_BLOCK) - ki
    band = (dist >= 0) & (dist <= n_back)
    bias_full = jnp.where(band, 0.0, -jnp.inf).astype(F32)
    bias_first = jnp.where(band & (ki >= ATTN_BLOCK), 0.0, -jnp.inf).astype(F32)
    lane = lax.broadcasted_iota(jnp.int32, (ATTN_BLOCK, LANES), 1)
    low_half = lane < ATTN_HEAD_DIM
    scale = ATTN_HEAD_DIM ** -0.5

    def block_body(fi, carry):
        chains = []
        for bi in range(ATTN_BLOCKS_PER_ITER):
            f = fi * ATTN_BLOCKS_PER_ITER + bi
            q0 = pl.multiple_of(f * ATTN_BLOCK, ATTN_BLOCK)
            if prev_block:
                k0 = pl.multiple_of(jnp.maximum(f - 1, 0) * ATTN_BLOCK, ATTN_BLOCK)
                bias = jnp.where(f % nb > 0, bias_full, bias_first)
            else:
                bias = bias_full
            for pair in range(n_pairs):
                qp = load(q_ref, qs, pair, q0) * scale
                kp = load(k_ref, ks, pair, q0)
                vp = load(v_ref, vs, pair, q0)
                if prev_block:
                    kp = jnp.concatenate([load(k_ref, ks, pair, k0), kp], axis=0)
                    vp = jnp.concatenate([load(v_ref, vs, pair, k0), vp], axis=0)
                for half in range(2):
                    sel = low_half if half == 0 else ~low_half
                    chains.append(dict(q0=q0, pair=pair, half=half, bias=bias, kp=kp, vp=vp,
                                       qm=jnp.where(sel, qp, jnp.zeros_like(qp))))
        for ch in chains:
            ch["sc"] = _mm_nt(ch.pop("qm"), ch.pop("kp"))
        for ch in chains:
            sc = ch.pop("sc") + ch.pop("bias")
            m = jnp.max(sc, axis=-1, keepdims=True)
            p = jnp.exp(sc - m)
            ch["denom"] = jnp.sum(p, axis=-1, keepdims=True)
            ch["m"] = m
            ch["p"] = p.astype(BF16)
        for ch in chains:
            ch["pv"] = _mm(ch.pop("p"), ch.pop("vp"))
        for c0 in range(0, len(chains), 2):
            lo, hi = chains[c0], chains[c0 + 1]
            o = jnp.where(low_half, lo["pv"] / lo["denom"], hi["pv"] / hi["denom"])
            lse = jnp.where(low_half, lo["m"] + jnp.log(lo["denom"]), hi["m"] + jnp.log(hi["denom"]))
            q0, pair = lo["q0"], lo["pair"]
            if dil > 1:
                os_[pair, pl.ds(q0, ATTN_BLOCK), :] = o
                ls[pair, pl.ds(q0, ATTN_BLOCK), :] = lse
            else:
                o_ref[0, pl.ds(q0, ATTN_BLOCK), lanes(pair)] = o.astype(o_ref.dtype)
                lse_ref[0, pl.ds(q0, ATTN_BLOCK), lanes(pair)] = lse
        return carry

    lax.fori_loop(0, dil * nb // ATTN_BLOCKS_PER_ITER, block_body, 0)

    if dil > 1:
        for pair in range(n_pairs):
            for src, dst in ((os_, o_ref), (ls, lse_ref)):
                interleave(src, pair)
                dst[0, :, lanes(pair)] = stage[...].astype(dst.dtype)


def _dilated_attention(p, group, col_base):
    b, s, _ = p.shape
    window, dil = DIL_CONFIGS[group]
    gw = ATTN_GROUP_WIDTH
    blk = col_base // gw + group
    step = ATTN_WIDTH // gw
    kern = functools.partial(_attn_kernel, dil=dil, n_back=window // dil)
    return pl.pallas_call(
        kern,
        grid=(b,),
        in_specs=[pl.BlockSpec((1, s, gw), lambda bi: (bi, 0, blk)),
                  pl.BlockSpec((1, s, gw), lambda bi: (bi, 0, blk + step)),
                  pl.BlockSpec((1, s, gw), lambda bi: (bi, 0, blk + 2 * step))],
        out_specs=[pl.BlockSpec((1, s, gw), lambda bi: (bi, 0, 0)),
                   pl.BlockSpec((1, s, gw), lambda bi: (bi, 0, 0))],
        out_shape=[jax.ShapeDtypeStruct((b, s, gw), BF16),
                   jax.ShapeDtypeStruct((b, s, gw), F32)],
        scratch_shapes=([] if dil == 1 else
                        [pltpu.VMEM((s, LANES), F32) for _ in range(2)]
                        + [pltpu.VMEM((gw // LANES, s, LANES), BF16) for _ in range(3)]
                        + [pltpu.VMEM((gw // LANES, s, LANES), F32) for _ in range(2)]),
        compiler_params=pltpu.CompilerParams(
            dimension_semantics=("parallel",), vmem_limit_bytes=VMEM_LIMIT),
        name=f"dilated_attn_g{group}",
    )(p, p, p)


MERGE_SUBTILES = 4


def _merge_kernel(x_ref, mod_ref, oa_ref, za_ref, ga_ref, gb_ref, zb_ref,
                  ob0_ref, ob1_ref, ob2_ref, l0_ref, l1_ref, l2_ref,
                  dnw_ref, pa_ref, pb_ref, wo_ref, fw_ref, out_ref):
    def silu_of_half(h):
        return h + h * jnp.tanh(h)

    gate = mod_ref[0][2:3]

    def stages(rows):
        oa = oa_ref[0, rows, :].astype(F32)
        za_half = za_ref[0, rows, :].astype(F32)
        parts = []
        for h in range(DN_HEADS):
            cq = slice(h * DN_DV, (h + 1) * DN_DV)
            blk = oa[:, cq]
            y = blk * lax.rsqrt(jnp.mean(blk * blk, axis=-1, keepdims=True) + EPS) * dnw_ref[...]
            parts.append(y * silu_of_half(za_half[:, cq]))
        ya_half = _mm(jnp.concatenate(parts, axis=1), pa_ref[...])
        yield

        l0, l1, l2 = l0_ref[0, rows, :], l1_ref[0, rows, :], l2_ref[0, rows, :]
        m = jnp.maximum(jnp.maximum(l0, l1), l2)
        e0, e1, e2 = jnp.exp(l0 - m), jnp.exp(l1 - m), jnp.exp(l2 - m)
        inv_den = 1.0 / (e0 + e1 + e2)
        ob = jnp.concatenate([ob0_ref[0, rows, :].astype(F32) * (e0 * inv_den),
                              ob1_ref[0, rows, :].astype(F32) * (e1 * inv_den),
                              ob2_ref[0, rows, :].astype(F32) * (e2 * inv_den)], axis=1)
        yb_half = _mm(ob * silu_of_half(zb_ref[0, rows, :].astype(F32)), pb_ref[...])
        yield

        merged = (ya_half + ya_half * jnp.tanh(ga_ref[0, rows, :].astype(F32))
                  + (yb_half + yb_half * jnp.tanh(gb_ref[0, rows, :].astype(F32))))
        delta = _mm(merged, wo_ref[...])
        yield

        xo = x_ref[0, rows, :] + gate * delta
        out_ref[0, rows, :] = xo * lax.rsqrt(jnp.mean(xo * xo, axis=-1, keepdims=True) + EPS) * fw_ref[...]
        yield

    tm = x_ref.shape[1]
    sub = tm // MERGE_SUBTILES
    _run_round_robin(*[stages(slice(i * sub, (i + 1) * sub)) for i in range(MERGE_SUBTILES)])


def _merge(x, mod3, o_a, p, obs, lses, dn_norm_w, pa, pb, wo, final_w, tm, cols):
    b, s, d = x.shape
    gw = ATTN_GROUP_WIDTH

    def row_spec(width, blk):
        return pl.BlockSpec((1, tm, width), lambda bi, i: (bi, i, blk))

    def full_spec(shape):
        return pl.BlockSpec(shape, lambda bi, i: (0,) * len(shape))

    return pl.pallas_call(
        _merge_kernel,
        grid=(b, s // tm),
        in_specs=[row_spec(d, 0),
                  pl.BlockSpec((1, 3, d), lambda bi, i: (bi, 0, 0)),
                  row_spec(DN_WIDTH, 0),
                  row_spec(DN_WIDTH, cols["za"] // DN_WIDTH),
                  row_spec(d, cols["ga"] // d),
                  row_spec(d, cols["gb"] // d),
                  row_spec(ATTN_WIDTH, cols["zb"] // ATTN_WIDTH),
                  row_spec(gw, 0), row_spec(gw, 0), row_spec(gw, 0),
                  row_spec(gw, 0), row_spec(gw, 0), row_spec(gw, 0),
                  full_spec((1, DN_DV)), full_spec(pa.shape), full_spec(pb.shape), full_spec(wo.shape),
                  full_spec((1, d))],
        out_specs=row_spec(d, 0),
        out_shape=jax.ShapeDtypeStruct((b, s, d), F32),
        compiler_params=pltpu.CompilerParams(
            dimension_semantics=("parallel", "parallel"), vmem_limit_bytes=VMEM_LIMIT),
        name="merge_outproj",
    )(x, mod3, o_a, p, p, p, p, *obs, *lses, dn_norm_w.reshape(1, DN_DV), pa, pb, wo, final_w.reshape(1, d))


W_PREP_COLS = 128


def _wprep_kernel(wt_ref, wc_ref, wp_ref, ws_ref, *, conv_width, segments, small_off):
    wc_ref[...] = wt_ref[0:conv_width, :].astype(BF16)
    for src, width, dst, scale in segments:
        blk = wt_ref[src:src + width, :]
        wp_ref[dst:dst + width, :] = (blk if scale == 1.0 else blk * scale).astype(BF16)
    n_small = 2 * DN_HEADS
    ws_ref[0:n_small, :] = wt_ref[small_off:small_off + n_small, :].astype(BF16)
    ws_ref[n_small:LANES, :] = jnp.zeros((LANES - n_small, wt_ref.shape[1]), BF16)


def _weight_prep(wt, conv_width, plain_width, segments, small_off):
    n_in, d = wt.shape
    assert d % W_PREP_COLS == 0 and small_off % 16 == 0 and all(sg[0] % 16 == 0 for sg in segments)
    kern = functools.partial(_wprep_kernel, conv_width=conv_width, segments=segments, small_off=small_off)
    return pl.pallas_call(
        kern,
        grid=(d // W_PREP_COLS,),
        in_specs=[pl.BlockSpec((n_in, W_PREP_COLS), lambda i: (0, i))],
        out_specs=[pl.BlockSpec((conv_width, W_PREP_COLS), lambda i: (0, i)),
                   pl.BlockSpec((plain_width, W_PREP_COLS), lambda i: (0, i)),
                   pl.BlockSpec((LANES, W_PREP_COLS), lambda i: (0, i))],
        out_shape=[jax.ShapeDtypeStruct((conv_width, d), BF16),
                   jax.ShapeDtypeStruct((plain_width, d), BF16),
                   jax.ShapeDtypeStruct((LANES, d), BF16)],
        compiler_params=pltpu.CompilerParams(dimension_semantics=("parallel",), vmem_limit_bytes=VMEM_LIMIT),
        name="weight_prep",
    )(wt)


def _layer(x, c, norm_w, ada_w, ada_b, w_in, conv_w, a_log, dt_bias, dn_norm_w, w_proj_a, w_proj_b, w_out,
           final_norm_w):
    b, s, d = x.shape
    qk = DN_HEADS * DN_DK
    splits = (qk, qk, DN_WIDTH, DN_WIDTH, DN_HEADS, DN_HEADS, ATTN_WIDTH, ATTN_WIDTH, ATTN_WIDTH, ATTN_WIDTH, d, d)
    names = ("qa", "ka", "va", "za", "beta", "a", "qb", "kb", "vb", "zb", "ga", "gb")
    offs = dict(zip(names, np.cumsum((0,) + splits[:-1]).tolist()))
    widths = dict(zip(names, splits))
    order = ("za", "ga", "gb", "qb", "kb", "vb", "zb")
    cols, pos = {}, 0
    for n in order:
        assert pos % widths[n] == 0, (n, pos)
        cols[n] = pos
        pos += widths[n]
    assert offs["qa"] == 0 and offs["ka"] == qk and offs["va"] == 2 * qk and offs["a"] == offs["beta"] + DN_HEADS
    halved = ("za", "ga", "gb", "zb")
    segments = tuple((offs[n], widths[n], cols[n], 0.5 if n in halved else 1.0) for n in order)
    w_conv, w_plain, w_small = _weight_prep(w_in.T, 2 * qk + DN_WIDTH, pos, segments, offs["beta"])
    pad_lo = jnp.zeros((DN_HEADS,), F32)
    pad_hi = jnp.zeros((LANES - 2 * DN_HEADS,), F32)
    alog_row = jnp.concatenate([pad_lo, a_log.astype(F32), pad_hi]).reshape(1, LANES)
    dtb_row = jnp.concatenate([pad_lo, dt_bias.astype(F32), pad_hi]).reshape(1, LANES)

    mod3 = _adaln_mod(c, ada_w, ada_b).reshape(b, 3, d)
    p_conv, p, small = _inproj(x, mod3, norm_w, w_conv, w_plain, w_small, conv_w)
    o_a = _deltanet(p_conv, small, alog_row, dtb_row, ts=min(s, 1024), group=8)
    obs, lses = [], []
    for g in range(len(DIL_CONFIGS)):
        o_g, lse_g = _dilated_attention(p, g, cols["qb"])
        obs.append(o_g)
        lses.append(lse_g)
    return _merge(x, mod3, o_a, p, obs, lses, dn_norm_w, (w_proj_a * 0.5).astype(BF16),
                  (w_proj_b * 0.5).astype(BF16), w_out.astype(BF16), final_norm_w, tm=min(s, 512), cols=cols)


def kernel(x, c, norm_w, ada_w, ada_b, w_in, conv_w, a_log, dt_bias, dn_norm_w, w_proj_a, w_proj_b, w_out,
           final_norm_w):
    depth = norm_w.shape[0]
    assert depth == 1, "the final RMSNorm is fused into the single layer's output kernel"
    return _layer(x, c, norm_w[0], ada_w[0], ada_b[0], w_in[0], conv_w[0], a_log[0], dt_bias[0], dn_norm_w[0],
                  w_proj_a[0], w_proj_b[0], w_out[0], final_norm_w)
```

```python
import functools

import jax
import jax.numpy as jnp
import numpy as np
from jax import lax
from jax.experimental import pallas as pl
from jax.experimental.pallas import tpu as pltpu

F32 = jnp.float32
BF16 = jnp.bfloat16
HIGHEST = lax.Precision.HIGHEST

EPS = 1e-6
DN_HEADS = 8
DN_DK = 128
DN_DV = 128
DN_WIDTH = DN_HEADS * DN_DV
CONV_K = 4
CHUNK = 64
DIL_CONFIGS = ((128, 1), (512, 4), (2048, 16))
HEADS_PER_GROUP = 4
ATTN_HEAD_DIM = 64
ATTN_GROUP_WIDTH = HEADS_PER_GROUP * ATTN_HEAD_DIM
ATTN_WIDTH = ATTN_GROUP_WIDTH * len(DIL_CONFIGS)
ATTN_BLOCK = 128
LANES = 128
HALO = 8
VMEM_LIMIT = 56 * 1024 * 1024


def _mm(a, b):
    return jnp.dot(a.astype(BF16), b.astype(BF16), preferred_element_type=F32)


def _mm_nt(a, b):
    return lax.dot_general(a.astype(BF16), b.astype(BF16), (((1,), (1,)), ((), ())),
                           preferred_element_type=F32)


def _mm_tn(a, b):
    return lax.dot_general(a.astype(BF16), b.astype(BF16), (((0,), (0,)), ((), ())),
                           preferred_element_type=F32)


def _run_round_robin(*gens):
    gens = list(gens)
    while gens:
        for g in list(gens):
            if next(g, StopIteration) is StopIteration:
                gens.remove(g)


def _sigmoid(x):
    return 0.5 * jnp.tanh(0.5 * x) + 0.5


def _silu(x):
    return x * _sigmoid(x)


def _mod_kernel(c_ref, w_ref, b_ref, o_ref):
    sc = _silu(c_ref[...])
    o_ref[...] = jnp.dot(sc, w_ref[...], preferred_element_type=F32, precision=HIGHEST) + b_ref[...]


def _adaln_mod(c, ada_w, ada_b):
    b, d = c.shape
    return pl.pallas_call(
        _mod_kernel,
        grid=(3,),
        in_specs=[pl.BlockSpec((b, d), lambda j: (0, 0)),
                  pl.BlockSpec((d, d), lambda j: (0, j)),
                  pl.BlockSpec((1, d), lambda j: (0, j))],
        out_specs=pl.BlockSpec((b, d), lambda j: (0, j)),
        out_shape=jax.ShapeDtypeStruct((b, 3 * d), F32),
        name="adaln_mod",
    )(c, ada_w, ada_b.reshape(1, 3 * d))


CONV_ROWS = 64
CONV_SUB = 256
SUBLANES = 8
TOKEN_CHAINS = 3


def _conv_unit(buf, r0, cl, half_taps, shift_masks, norm_scale, skip_norm):
    a = buf[r0:r0 + HALO + CONV_ROWS, cl]
    n = CONV_ROWS // SUBLANES
    vs = [a[SUBLANES * i:SUBLANES * (i + 1)] for i in range(n + 1)]
    acc = [vs[i + 1] * half_taps[CONV_K - 1] for i in range(n)]
    for shift in range(1, CONV_K):
        rs = [pltpu.roll(v, shift, 0) for v in vs]
        tap = half_taps[CONV_K - 1 - shift]
        for i in range(n):
            acc[i] = acc[i] + jnp.where(shift_masks[shift], rs[i], rs[i + 1]) * tap
    h = jnp.concatenate(acc, axis=0)
    y = h + h * jnp.tanh(h)
    ss = jnp.sum(y * y, axis=-1, keepdims=True)
    out = y * jnp.where(skip_norm, 1.0, lax.rsqrt(ss + EPS) * norm_scale)
    token = jnp.broadcast_to(ss[CONV_ROWS - SUBLANES:], (SUBLANES, LANES))
    return out.astype(BF16), token


def _inproj_kernel(x_ref, mod_ref, nw_ref, wc_ref, wp_ref, ws_ref, cw_ref, pc_ref, pp_ref, small_ref,
                   h_scr, cbuf, tok, *, steps_per_seg):
    j = pl.program_id(1)
    s = x_ref.shape[1]

    @pl.when(j == 0)
    def _():
        x = x_ref[0]
        m = mod_ref[0]
        y = x * lax.rsqrt(jnp.mean(x * x, axis=-1, keepdims=True) + EPS) * nw_ref[...]
        h = (y * (1.0 + m[1:2]) + m[0:1]).astype(BF16)
        h_scr[...] = h
        small_ref[0] = _mm_nt(h, ws_ref[...])
        cbuf[0:HALO, :] = jnp.zeros((HALO, CONV_SUB), F32)

    cbuf[HALO:HALO + s, :] = _mm_nt(h_scr[...], wc_ref[...])
    pp_ref[0] = _mm_nt(h_scr[...], wp_ref[...]).astype(BF16)

    sub = lax.broadcasted_iota(jnp.int32, (SUBLANES, LANES), 0)
    shift_masks = [sub < shift for shift in range(CONV_K)]
    q_scale = jnp.where(j < steps_per_seg, DN_DK ** -0.5, 1.0)
    is_v = j >= 2 * steps_per_seg
    never = j < 0
    for c in range(TOKEN_CHAINS):
        tok[c] = jnp.zeros((SUBLANES, LANES), F32)
    n_unit = 0
    for c0 in range(0, CONV_SUB, DN_DK):
        cl = slice(c0, c0 + DN_DK)
        half_taps = [cw_ref[t:t + 1, cl] * 0.5 for t in range(CONV_K)]
        for r0 in range(0, s, CONV_ROWS):
            slot = n_unit % TOKEN_CHAINS
            held = tok[slot][0:1]
            taps = [jnp.where(never, held, tap) for tap in half_taps]
            out, token = _conv_unit(cbuf, r0, cl, taps, shift_masks, q_scale, is_v)
            pc_ref[0, r0:r0 + CONV_ROWS, cl] = out
            tok[slot] = token
            n_unit += 1


def _inproj(x, mod3, norm_w, w_conv, w_plain, w_small, conv_w):
    b, s, d = x.shape
    conv_width, plain_width = w_conv.shape[0], w_plain.shape[0]
    n_steps = conv_width // CONV_SUB
    tp = plain_width // n_steps
    assert n_steps % 3 == 0 and n_steps * CONV_SUB == conv_width and conv_w.shape[1] == conv_width
    assert tp * n_steps == plain_width and tp % LANES == 0 and s % CONV_ROWS == 0
    kern = functools.partial(_inproj_kernel, steps_per_seg=n_steps // 3)
    return pl.pallas_call(
        kern,
        grid=(b, n_steps),
        in_specs=[pl.BlockSpec((1, s, d), lambda bi, j: (bi, 0, 0)),
                  pl.BlockSpec((1, 3, d), lambda bi, j: (bi, 0, 0)),
                  pl.BlockSpec((1, d), lambda bi, j: (0, 0)),
                  pl.BlockSpec((CONV_SUB, d), lambda bi, j: (j, 0)),
                  pl.BlockSpec((tp, d), lambda bi, j: (j, 0)),
                  pl.BlockSpec((LANES, d), lambda bi, j: (0, 0)),
                  pl.BlockSpec((CONV_K, CONV_SUB), lambda bi, j: (0, j))],
        out_specs=[pl.BlockSpec((1, s, CONV_SUB), lambda bi, j: (bi, 0, j)),
                   pl.BlockSpec((1, s, tp), lambda bi, j: (bi, 0, j)),
                   pl.BlockSpec((1, s, LANES), lambda bi, j: (bi, 0, 0))],
        out_shape=[jax.ShapeDtypeStruct((b, s, conv_width), BF16),
                   jax.ShapeDtypeStruct((b, s, plain_width), BF16),
                   jax.ShapeDtypeStruct((b, s, LANES), F32)],
        scratch_shapes=[pltpu.VMEM((s, d), BF16),
                        pltpu.VMEM((HALO + s, CONV_SUB), F32),
                        pltpu.VMEM((TOKEN_CHAINS, SUBLANES, LANES), F32)],
        compiler_params=pltpu.CompilerParams(
            dimension_semantics=("parallel", "arbitrary"), vmem_limit_bytes=VMEM_LIMIT),
        name="norm_inproj",
    )(x, mod3, norm_w.reshape(1, d), w_conv, w_plain, w_small, conv_w)


def _level_mask(level, i, j):
    same_block = (i >> (level + 1)) == (j >> (level + 1))
    return same_block & (((i >> level) & 1) == 1) & (((j >> level) & 1) == 0)


N_LEVELS = CHUNK.bit_length() - 1


def _deltanet_kernel(q_ref, k_ref, v_ref, small_ref, alog_ref, dtb_ref, o_ref,
                     state, u_s, wq_s, akd_s, el_s, *, ts, group):
    @pl.when(pl.program_id(1) == 0)
    def _():
        state[...] = jnp.zeros_like(state)

    row = lax.broadcasted_iota(jnp.int32, (CHUNK, 2 * CHUNK), 0)
    lane = lax.broadcasted_iota(jnp.int32, (CHUNK, 2 * CHUNK), 1)
    col = lane & (CHUNK - 1)
    in_a = lane < CHUNK
    causal = row >= col
    tri_strict = (row > col).astype(F32)
    eye = (row == col).astype(F32)
    row_c = lax.broadcasted_iota(jnp.int32, (CHUNK, CHUNK), 0)
    col_c = lax.broadcasted_iota(jnp.int32, (CHUNK, CHUNK), 1)
    tri = (row_c >= col_c).astype(F32)
    level_masks = [_level_mask(level, row, col).astype(F32) for level in range(N_LEVELS)]
    neg_a = -jnp.exp(alog_ref[...])
    dtb = dtb_ref[...]
    n_chunks = ts // CHUNK
    n_pairs = DN_HEADS // 2
    zeros_wide = jnp.zeros((CHUNK, DN_DV + DN_DK), BF16)
    zeros_head = jnp.zeros((CHUNK, DN_DK), BF16)

    def blockdiag(m):
        return jnp.concatenate([jnp.where(in_a, m, 0.0), jnp.where(in_a, 0.0, m)], axis=0).astype(BF16)

    def prep_stages(gi):
        chains = []
        for ci in range(group):
            c = gi * group + ci
            r0 = c * CHUNK if isinstance(c, int) else pl.multiple_of(c * CHUNK, CHUNK)
            sm = small_ref[0, pl.ds(r0, CHUNK), :]
            beta_all = _sigmoid(sm)
            z = sm + dtb
            softplus = jnp.maximum(z, 0.0) + jnp.log(1.0 + jnp.exp(-jnp.abs(z)))
            g_all = neg_a * softplus
            gc_all = jnp.dot(tri, g_all, preferred_element_type=F32, precision=HIGHEST)
            gc_t = jnp.concatenate([gc_all, gc_all], axis=0).T
            g_last = gc_all[CHUNK - 1:CHUNK, :]
            e_in_all = jnp.exp(gc_all)
            e_out_all = jnp.exp(g_last - gc_all)
            slot = (gi % 2) * group + ci
            el_s[slot] = jnp.broadcast_to(jnp.exp(g_last), (8, LANES))
            for p in range(n_pairs):
                heads = (2 * p, 2 * p + 1)
                q, k, k_beta, rhs, k_dec = [], [], [], [], []
                for h in heads:
                    gl = DN_HEADS + h
                    cq = slice(h * DN_DK, (h + 1) * DN_DK)
                    q_h = q_ref[0, pl.ds(r0, CHUNK), cq].astype(F32)
                    k_h = k_ref[0, pl.ds(r0, CHUNK), cq].astype(F32)
                    v_h = v_ref[0, pl.ds(r0, CHUNK), slice(h * DN_DV, (h + 1) * DN_DV)].astype(F32)
                    beta = beta_all[:, h:h + 1]
                    e_in = e_in_all[:, gl:gl + 1]
                    kb_h = k_h * beta
                    q.append(q_h)
                    k.append(k_h.astype(BF16))
                    k_beta.append(kb_h)
                    rhs.append(jnp.concatenate([v_h * beta, kb_h * e_in], axis=1).astype(BF16))
                    k_dec.append(k_h * e_out_all[:, gl:gl + 1])
                    wq_s[slot, h, CHUNK:2 * CHUNK, :] = (q_h * e_in).astype(BF16)
                ga, gb = DN_HEADS + heads[0], DN_HEADS + heads[1]
                gc_col = jnp.where(in_a, gc_all[:, ga:ga + 1], gc_all[:, gb:gb + 1])
                gc_row = jnp.where(in_a[0:1], gc_t[ga:ga + 1, :], gc_t[gb:gb + 1, :])
                chains.append(dict(
                    slot=slot, p=p,
                    decay=jnp.exp(jnp.where(causal, gc_col - gc_row, -jnp.inf)),
                    kq=jnp.concatenate([jnp.concatenate(k_beta, axis=1), jnp.concatenate(q, axis=1)],
                                       axis=0).astype(BF16),
                    kk=jnp.concatenate([jnp.concatenate([k[0], zeros_head], axis=1),
                                        jnp.concatenate([zeros_head, k[1]], axis=1)], axis=0),
                    rhs=jnp.concatenate([jnp.concatenate([rhs[0], zeros_wide], axis=1),
                                         jnp.concatenate([zeros_wide, rhs[1]], axis=1)], axis=0),
                    k_dec_t=jnp.concatenate(k_dec, axis=0).T.astype(BF16)))

        for ch in chains:
            ch["kk_qk"] = _mm_nt(ch.pop("kq"), ch.pop("kk"))
        yield
        for ch in chains:
            kk_qk = ch.pop("kk_qk")
            decay = ch.pop("decay")
            ch["lmat"] = kk_qk[0:CHUNK] * decay * tri_strict
            ch["a_qk"] = (kk_qk[CHUNK:2 * CHUNK] * decay).astype(BF16)
            ch["x"] = eye - ch["lmat"] * level_masks[0]
        for level in range(1, N_LEVELS):
            for ch in chains:
                ch["y"] = _mm(ch["lmat"] * level_masks[level], blockdiag(ch["x"]))
            yield
            for ch in chains:
                ch["x"] = ch["x"] - _mm(ch["x"], blockdiag(ch.pop("y")))
            yield
        for ch in chains:
            ch["uw"] = _mm(ch.pop("x"), ch.pop("rhs"))
        for ch in chains:
            slot, p = ch["slot"], ch["p"]
            uw = ch.pop("uw")
            for i, h in enumerate((2 * p, 2 * p + 1)):
                base = i * (DN_DV + DN_DK)
                u_s[slot, h] = uw[:, base:base + DN_DV]
                wq_s[slot, h, 0:CHUNK, :] = uw[:, base + DN_DV:base + DN_DV + DN_DK].astype(BF16)
            akd_s[slot, p] = jnp.concatenate([ch["a_qk"], ch["k_dec_t"]], axis=0)
        yield

    def rec_stages(gi):
        for ci in range(group):
            c = gi * group + ci
            r0 = c * CHUNK if isinstance(c, int) else pl.multiple_of(c * CHUNK, CHUNK)
            slot = (gi % 2) * group + ci
            e_last_all = el_s[slot]
            rs = [_mm(wq_s[slot, h], state[h]) for h in range(DN_HEADS)]
            yield
            v_new = [(u_s[slot, h] - rs[h][0:CHUNK]).astype(BF16) for h in range(DN_HEADS)]
            zeros_v = jnp.zeros((CHUNK, DN_DV), BF16)
            av = [_mm(akd_s[slot, p],
                      jnp.concatenate([jnp.concatenate([v_new[2 * p], zeros_v], axis=1),
                                       jnp.concatenate([zeros_v, v_new[2 * p + 1]], axis=1)], axis=0))
                  for p in range(n_pairs)]
            for h in range(DN_HEADS):
                gl = DN_HEADS + h
                lanes_h = slice((h % 2) * DN_DV, (h % 2 + 1) * DN_DV)
                o = rs[h][CHUNK:2 * CHUNK] + av[h // 2][0:CHUNK, lanes_h]
                state[h] = state[h] * e_last_all[0:1, gl:gl + 1] + av[h // 2][CHUNK:CHUNK + DN_DK, lanes_h]
                o_ref[0, pl.ds(r0, CHUNK), slice(h * DN_DV, (h + 1) * DN_DV)] = o.astype(o_ref.dtype)
            yield

    run = _run_round_robin

    n_groups = n_chunks // group
    run(prep_stages(0))

    def both(gi, carry):
        run(prep_stages(gi), rec_stages(gi - 1))
        return carry

    lax.fori_loop(1, n_groups, both, 0)
    run(rec_stages(n_groups - 1))


def _deltanet(p, small, alog_row, dtb_row, ts, group):
    b, s, _ = p.shape
    width = DN_HEADS * DN_DK
    n_slots = 2 * group
    assert (ts // CHUNK) % group == 0
    kern = functools.partial(_deltanet_kernel, ts=ts, group=group)
    return pl.pallas_call(
        kern,
        grid=(b, s // ts),
        in_specs=[pl.BlockSpec((1, ts, width), lambda bi, t: (bi, t, 0)),
                  pl.BlockSpec((1, ts, width), lambda bi, t: (bi, t, 1)),
                  pl.BlockSpec((1, ts, width), lambda bi, t: (bi, t, 2)),
                  pl.BlockSpec((1, ts, LANES), lambda bi, t: (bi, t, 0)),
                  pl.BlockSpec((1, LANES), lambda bi, t: (0, 0)),
                  pl.BlockSpec((1, LANES), lambda bi, t: (0, 0))],
        out_specs=pl.BlockSpec((1, ts, DN_WIDTH), lambda bi, t: (bi, t, 0)),
        out_shape=jax.ShapeDtypeStruct((b, s, DN_WIDTH), BF16),
        scratch_shapes=[pltpu.VMEM((DN_HEADS, DN_DK, DN_DV), F32),
                        pltpu.VMEM((n_slots, DN_HEADS, CHUNK, DN_DV), F32),
                        pltpu.VMEM((n_slots, DN_HEADS, 2 * CHUNK, DN_DK), BF16),
                        pltpu.VMEM((n_slots, DN_HEADS // 2, CHUNK + DN_DK, 2 * CHUNK), BF16),
                        pltpu.VMEM((n_slots, 8, LANES), F32)],
        compiler_params=pltpu.CompilerParams(
            dimension_semantics=("parallel", "arbitrary"), vmem_limit_bytes=VMEM_LIMIT),
        name="gated_deltanet",
    )(p, p, p, small, alog_row, dtb_row)


ATTN_BLOCKS_PER_ITER = 4
ATTN_STRIDE = 4


def _attn_kernel(q_ref, k_ref, v_ref, o_ref, lse_ref, *scratch, dil, n_back):
    s = q_ref.shape[1]
    sub_len = s // dil
    nb = sub_len // ATTN_BLOCK
    n_pairs = ATTN_GROUP_WIDTH // LANES
    prev_block = nb > 1

    def lanes(pair):
        return slice(pair * LANES, (pair + 1) * LANES)

    if dil > 1:
        stage, tmp, qs, ks, vs, os_, ls = scratch
        assert dil in (ATTN_STRIDE, ATTN_STRIDE * ATTN_STRIDE)
        q_len = s // ATTN_STRIDE

        def classes():
            if dil == ATTN_STRIDE:
                return [(r, stage, r) for r in range(dil)]
            return [(r_lo + ATTN_STRIDE * r_hi, tmp, r_lo * q_len + r_hi)
                    for r_lo in range(ATTN_STRIDE) for r_hi in range(ATTN_STRIDE)]

        def deinterleave(dst, pair):
            if dil > ATTN_STRIDE:
                for r_lo in range(ATTN_STRIDE):
                    tmp[r_lo * q_len:(r_lo + 1) * q_len, :] = stage[pl.ds(r_lo, q_len, stride=ATTN_STRIDE), :]
            for r, buf, start in classes():
                dst[pair, r * sub_len:(r + 1) * sub_len, :] = (
                    buf[pl.ds(start, sub_len, stride=ATTN_STRIDE), :].astype(dst.dtype))

        def interleave(src, pair):
            for r, buf, start in classes():
                buf[pl.ds(start, sub_len, stride=ATTN_STRIDE), :] = src[pair, r * sub_len:(r + 1) * sub_len, :]
            if dil > ATTN_STRIDE:
                for r_lo in range(ATTN_STRIDE):
                    stage[pl.ds(r_lo, q_len, stride=ATTN_STRIDE), :] = tmp[r_lo * q_len:(r_lo + 1) * q_len, :]

        for src, dst in ((q_ref, qs), (k_ref, ks), (v_ref, vs)):
            for pair in range(n_pairs):
                stage[...] = src[0, :, lanes(pair)].astype(F32)
                deinterleave(dst, pair)

        def load(ref, src, pair, r0):
            return src[pair, pl.ds(r0, ATTN_BLOCK), :]
    else:
        qs, ks, vs = q_ref, k_ref, v_ref

        def load(ref, src, pair, r0):
            return ref[0, pl.ds(r0, ATTN_BLOCK), lanes(pair)]

    n_keys = 2 * ATTN_BLOCK if prev_block else ATTN_BLOCK
    qi = lax.broadcasted_iota(jnp.int32, (ATTN_BLOCK, n_keys), 0)
    ki = lax.broadcasted_iota(jnp.int32, (ATTN_BLOCK, n_keys), 1)
    dist = qi + (n_keys - ATTN_BLOCK) - ki
    band = (dist >= 0) & (dist <= n_back)
    bias_full = jnp.where(band, 0.0, -jnp.inf).astype(F32)
    bias_first = jnp.where(band & (ki >= ATTN_BLOCK), 0.0, -jnp.inf).astype(F32)
    lane = lax.broadcasted_iota(jnp.int32, (ATTN_BLOCK, LANES), 1)
    low_half = lane < ATTN_HEAD_DIM
    scale = ATTN_HEAD_DIM ** -0.5

    def block_body(fi, carry):
        chains = []
        for bi in range(ATTN_BLOCKS_PER_ITER):
            f = fi * ATTN_BLOCKS_PER_ITER + bi
            q0 = pl.multiple_of(f * ATTN_BLOCK, ATTN_BLOCK)
            if prev_block:
                k0 = pl.multiple_of(jnp.maximum(f - 1, 0) * ATTN_BLOCK, ATTN_BLOCK)
                bias = jnp.where(f % nb > 0, bias_full, bias_first)
            else:
                bias = bias_full
            for pair in range(n_pairs):
                qp = load(q_ref, qs, pair, q0) * scale
                kp = load(k_ref, ks, pair, q0)
                vp = load(v_ref, vs, pair, q0)
                if prev_block:
                    kp = jnp.concatenate([load(k_ref, ks, pair, k0), kp], axis=0)
                    vp = jnp.concatenate([load(v_ref, vs, pair, k0), vp], axis=0)
                for half in range(2):
                    sel = low_half if half == 0 else ~low_half
                    chains.append(dict(q0=q0, pair=pair, half=half, bias=bias, kp=kp, vp=vp,
                                       qm=jnp.where(sel, qp, jnp.zeros_like(qp))))
        for ch in chains:
            ch["sc"] = _mm_nt(ch.pop("qm"), ch.pop("kp"))
        for ch in chains:
            sc = ch.pop("sc") + ch.pop("bias")
            m = jnp.max(sc, axis=-1, keepdims=True)
            p = jnp.exp(sc - m)
            ch["denom"] = jnp.sum(p, axis=-1, keepdims=True)
            ch["m"] = m
            ch["p"] = p.astype(BF16)
        for ch in chains:
            ch["pv"] = _mm(ch.pop("p"), ch.pop("vp"))
        for c0 in range(0, len(chains), 2):
            lo, hi = chains[c0], chains[c0 + 1]
            o = jnp.where(low_half, lo["pv"] / lo["denom"], hi["pv"] / hi["denom"])
            lse = jnp.where(low_half, lo["m"] + jnp.log(lo["denom"]), hi["m"] + jnp.log(hi["denom"]))
            q0, pair = lo["q0"], lo["pair"]
            if dil > 1:
                os_[pair, pl.ds(q0, ATTN_BLOCK), :] = o
                ls[pair, pl.ds(q0, ATTN_BLOCK), :] = lse
            else:
                o_ref[0, pl.ds(q0, ATTN_BLOCK), lanes(pair)] = o.astype(o_ref.dtype)
                lse_ref[0, pl.ds(q0, ATTN_BLOCK), lanes(pair)] = lse
        return carry

    lax.fori_loop(0, dil * nb // ATTN_BLOCKS_PER_ITER, block_body, 0)

    if dil > 1:
        for pair in range(n_pairs):
            for src, dst in ((os_, o_ref), (ls, lse_ref)):
                interleave(src, pair)
                dst[0, :, lanes(pair)] = stage[...].astype(dst.dtype)


def _dilated_attention(p, group, col_base):
    b, s, _ = p.shape
    window, dil = DIL_CONFIGS[group]
    gw = ATTN_GROUP_WIDTH
    blk = col_base // gw + group
    step = ATTN_WIDTH // gw
    kern = functools.partial(_attn_kernel, dil=dil, n_back=window // dil)
    return pl.pallas_call(
        kern,
        grid=(b,),
        in_specs=[pl.BlockSpec((1, s, gw), lambda bi: (bi, 0, blk)),
                  pl.BlockSpec((1, s, gw), lambda bi: (bi, 0, blk + step)),
                  pl.BlockSpec((1, s, gw), lambda bi: (bi, 0, blk + 2 * step))],
        out_specs=[pl.BlockSpec((1, s, gw), lambda bi: (bi, 0, 0)),
                   pl.BlockSpec((1, s, gw), lambda bi: (bi, 0, 0))],
        out_shape=[jax.ShapeDtypeStruct((b, s, gw), BF16),
                   jax.ShapeDtypeStruct((b, s, gw), F32)],
        scratch_shapes=([] if dil == 1 else
                        [pltpu.VMEM((s, LANES), F32) for _ in range(2)]
                        + [pltpu.VMEM((gw // LANES, s, LANES), BF16) for _ in range(3)]
                        + [pltpu.VMEM((gw // LANES, s, LANES), F32) for _ in range(2)]),
        compiler_params=pltpu.CompilerParams(
            dimension_semantics=("parallel",), vmem_limit_bytes=VMEM_LIMIT),
        name=f"dilated_attn_g{group}",
    )(p, p, p)


MERGE_SUBTILES = 8


def _merge_kernel(x_ref, mod_ref, oa_ref, za_ref, ga_ref, gb_ref, zb_ref,
                  ob0_ref, ob1_ref, ob2_ref, l0_ref, l1_ref, l2_ref,
                  dnw_ref, pa_ref, pb_ref, wo_ref, fw_ref, out_ref):
    def silu_of_half(h):
        return h + h * jnp.tanh(h)

    gate = mod_ref[0][2:3]

    def stages(rows):
        oa = oa_ref[0, rows, :].astype(F32)
        za_half = za_ref[0, rows, :].astype(F32)
        parts = []
        for h in range(DN_HEADS):
            cq = slice(h * DN_DV, (h + 1) * DN_DV)
            blk = oa[:, cq]
            y = blk * lax.rsqrt(jnp.mean(blk * blk, axis=-1, keepdims=True) + EPS) * dnw_ref[...]
            parts.append(y * silu_of_half(za_half[:, cq]))
        ya_half = _mm(jnp.concatenate(parts, axis=1), pa_ref[...])
        yield

        l0, l1, l2 = l0_ref[0, rows, :], l1_ref[0, rows, :], l2_ref[0, rows, :]
        m = jnp.maximum(jnp.maximum(l0, l1), l2)
        e0, e1, e2 = jnp.exp(l0 - m), jnp.exp(l1 - m), jnp.exp(l2 - m)
        inv_den = 1.0 / (e0 + e1 + e2)
        ob = jnp.concatenate([ob0_ref[0, rows, :].astype(F32) * (e0 * inv_den),
                              ob1_ref[0, rows, :].astype(F32) * (e1 * inv_den),
                              ob2_ref[0, rows, :].astype(F32) * (e2 * inv_den)], axis=1)
        yb_half = _mm(ob * silu_of_half(zb_ref[0, rows, :].astype(F32)), pb_ref[...])
        yield

        merged = (ya_half + ya_half * jnp.tanh(ga_ref[0, rows, :].astype(F32))
                  + (yb_half + yb_half * jnp.tanh(gb_ref[0, rows, :].astype(F32))))
        delta = _mm(merged, wo_ref[...])
        yield

        xo = x_ref[0, rows, :] + gate * delta
        out_ref[0, rows, :] = xo * lax.rsqrt(jnp.mean(xo * xo, axis=-1, keepdims=True) + EPS) * fw_ref[...]
        yield

    tm = x_ref.shape[1]
    sub = tm // MERGE_SUBTILES
    _run_round_robin(*[stages(slice(i * sub, (i + 1) * sub)) for i in range(MERGE_SUBTILES)])


def _merge(x, mod3, o_a, p, obs, lses, dn_norm_w, pa, pb, wo, final_w, tm, cols):
    b, s, d = x.shape
    gw = ATTN_GROUP_WIDTH

    def row_spec(width, blk):
        return pl.BlockSpec((1, tm, width), lambda bi, i: (bi, i, blk))

    def full_spec(shape):
        return pl.BlockSpec(shape, lambda bi, i: (0,) * len(shape), pipeline_mode=pl.Buffered(1))

    return pl.pallas_call(
        _merge_kernel,
        grid=(b, s // tm),
        in_specs=[row_spec(d, 0),
                  pl.BlockSpec((1, 3, d), lambda bi, i: (bi, 0, 0)),
                  row_spec(DN_WIDTH, 0),
                  row_spec(DN_WIDTH, cols["za"] // DN_WIDTH),
                  row_spec(d, cols["ga"] // d),
                  row_spec(d, cols["gb"] // d),
                  row_spec(ATTN_WIDTH, cols["zb"] // ATTN_WIDTH),
                  row_spec(gw, 0), row_spec(gw, 0), row_spec(gw, 0),
                  row_spec(gw, 0), row_spec(gw, 0), row_spec(gw, 0),
                  full_spec((1, DN_DV)), full_spec(pa.shape), full_spec(pb.shape), full_spec(wo.shape),
                  full_spec((1, d))],
        out_specs=row_spec(d, 0),
        out_shape=jax.ShapeDtypeStruct((b, s, d), F32),
        compiler_params=pltpu.CompilerParams(
            dimension_semantics=("parallel", "parallel"), vmem_limit_bytes=VMEM_LIMIT),
        name="merge_outproj",
    )(x, mod3, o_a, p, p, p, p, *obs, *lses, dn_norm_w.reshape(1, DN_DV), pa, pb, wo, final_w.reshape(1, d))


W_PREP_COLS = 128


def _wprep_kernel(wt_ref, wc_ref, wp_ref, ws_ref, *, conv_width, segments, small_off):
    wc_ref[...] = wt_ref[0:conv_width, :].astype(BF16)
    for src, width, dst, scale in segments:
        blk = wt_ref[src:src + width, :]
        wp_ref[dst:dst + width, :] = (blk if scale == 1.0 else blk * scale).astype(BF16)
    n_small = 2 * DN_HEADS
    ws_ref[0:n_small, :] = wt_ref[small_off:small_off + n_small, :].astype(BF16)
    ws_ref[n_small:LANES, :] = jnp.zeros((LANES - n_small, wt_ref.shape[1]), BF16)


def _weight_prep(wt, conv_width, plain_width, segments, small_off):
    n_in, d = wt.shape
    assert d % W_PREP_COLS == 0 and small_off % 16 == 0 and all(sg[0] % 16 == 0 for sg in segments)
    kern = functools.partial(_wprep_kernel, conv_width=conv_width, segments=segments, small_off=small_off)
    return pl.pallas_call(
        kern,
        grid=(d // W_PREP_COLS,),
        in_specs=[pl.BlockSpec((n_in, W_PREP_COLS), lambda i: (0, i))],
        out_specs=[pl.BlockSpec((conv_width, W_PREP_COLS), lambda i: (0, i)),
                   pl.BlockSpec((plain_width, W_PREP_COLS), lambda i: (0, i)),
                   pl.BlockSpec((LANES, W_PREP_COLS), lambda i: (0, i))],
        out_shape=[jax.ShapeDtypeStruct((conv_width, d), BF16),
                   jax.ShapeDtypeStruct((plain_width, d), BF16),
                   jax.ShapeDtypeStruct((LANES, d), BF16)],
        compiler_params=pltpu.CompilerParams(dimension_semantics=("parallel",), vmem_limit_bytes=VMEM_LIMIT),
        name="weight_prep",
    )(wt)


def _layer(x, c, norm_w, ada_w, ada_b, w_in, conv_w, a_log, dt_bias, dn_norm_w, w_proj_a, w_proj_b, w_out,
           final_norm_w):
    b, s, d = x.shape
    qk = DN_HEADS * DN_DK
    splits = (qk, qk, DN_WIDTH, DN_WIDTH, DN_HEADS, DN_HEADS, ATTN_WIDTH, ATTN_WIDTH, ATTN_WIDTH, ATTN_WIDTH, d, d)
    names = ("qa", "ka", "va", "za", "beta", "a", "qb", "kb", "vb", "zb", "ga", "gb")
    offs = dict(zip(names, np.cumsum((0,) + splits[:-1]).tolist()))
    widths = dict(zip(names, splits))
    order = ("za", "ga", "gb", "qb", "kb", "vb", "zb")
    cols, pos = {}, 0
    for n in order:
        assert pos % widths[n] == 0, (n, pos)
        cols[n] = pos
        pos += widths[n]
    assert offs["qa"] == 0 and offs["ka"] == qk and offs["va"] == 2 * qk and offs["a"] == offs["beta"] + DN_HEADS
    halved = ("za", "ga", "gb", "zb")
    segments = tuple((offs[n], widths[n], cols[n], 0.5 if n in halved else 1.0) for n in order)
    w_conv, w_plain, w_small = _weight_prep(w_in.T, 2 * qk + DN_WIDTH, pos, segments, offs["beta"])
    pad_lo = jnp.zeros((DN_HEADS,), F32)
    pad_hi = jnp.zeros((LANES - 2 * DN_HEADS,), F32)
    alog_row = jnp.concatenate([pad_lo, a_log.astype(F32), pad_hi]).reshape(1, LANES)
    dtb_row = jnp.concatenate([pad_lo, dt_bias.astype(F32), pad_hi]).reshape(1, LANES)

    mod3 = _adaln_mod(c, ada_w, ada_b).reshape(b, 3, d)
    p_conv, p, small = _inproj(x, mod3, norm_w, w_conv, w_plain, w_small, conv_w)
    o_a = _deltanet(p_conv, small, alog_row, dtb_row, ts=min(s, 1024), group=8)
    obs, lses = [], []
    for g in range(len(DIL_CONFIGS)):
        o_g, lse_g = _dilated_attention(p, g, cols["qb"])
        obs.append(o_g)
        lses.append(lse_g)
    return _merge(x, mod3, o_a, p, obs, lses, dn_norm_w, (w_proj_a * 0.5).astype(BF16),
                  (w_proj_b * 0.5).astype(BF16), w_out.astype(BF16), final_norm_w, tm=min(s, 1024), cols=cols)


def kernel(x, c, norm_w, ada_w, ada_b, w_in, conv_w, a_log, dt_bias, dn_norm_w, w_proj_a, w_proj_b, w_out,
           final_norm_w):
    depth = norm_w.shape[0]
    assert depth == 1, "the final RMSNorm is fused into the single layer's output kernel"
    return _layer(x, c, norm_w[0], ada_w[0], ada_b[0], w_in[0], conv_w[0], a_log[0], dt_bias[0], dn_norm_w[0],
                  w_proj_a[0], w_proj_b[0], w_out[0], final_norm_w)
```

```python
import functools

import jax
import jax.numpy as jnp
import numpy as np
from jax import lax
from jax.experimental import pallas as pl
from jax.experimental.pallas import tpu as pltpu

F32 = jnp.float32
BF16 = jnp.bfloat16
HIGHEST = lax.Precision.HIGHEST

EPS = 1e-6
DN_HEADS = 8
DN_DK = 128
DN_DV = 128
DN_WIDTH = DN_HEADS * DN_DV
CONV_K = 4
CHUNK = 64
DIL_CONFIGS = ((128, 1), (512, 4), (2048, 16))
HEADS_PER_GROUP = 4
ATTN_HEAD_DIM = 64
ATTN_GROUP_WIDTH = HEADS_PER_GROUP * ATTN_HEAD_DIM
ATTN_WIDTH = ATTN_GROUP_WIDTH * len(DIL_CONFIGS)
ATTN_BLOCK = 128
LANES = 128
HALO = 8
VMEM_LIMIT = 56 * 1024 * 1024


def _mm(a, b):
    return jnp.dot(a.astype(BF16), b.astype(BF16), preferred_element_type=F32)


def _mm_nt(a, b):
    return lax.dot_general(a.astype(BF16), b.astype(BF16), (((1,), (1,)), ((), ())),
                           preferred_element_type=F32)


def _mm_tn(a, b):
    return lax.dot_general(a.astype(BF16), b.astype(BF16), (((0,), (0,)), ((), ())),
                           preferred_element_type=F32)


def _run_round_robin(*gens):
    gens = list(gens)
    while gens:
        for g in list(gens):
            if next(g, StopIteration) is StopIteration:
                gens.remove(g)


def _sigmoid(x):
    return 0.5 * jnp.tanh(0.5 * x) + 0.5


def _silu(x):
    return x * _sigmoid(x)


def _mod_kernel(c_ref, w_ref, b_ref, o_ref):
    sc = _silu(c_ref[...])
    o_ref[...] = jnp.dot(sc, w_ref[...], preferred_element_type=F32, precision=HIGHEST) + b_ref[...]


def _adaln_mod(c, ada_w, ada_b):
    b, d = c.shape
    return pl.pallas_call(
        _mod_kernel,
        grid=(3,),
        in_specs=[pl.BlockSpec((b, d), lambda j: (0, 0)),
                  pl.BlockSpec((d, d), lambda j: (0, j)),
                  pl.BlockSpec((1, d), lambda j: (0, j))],
        out_specs=pl.BlockSpec((b, d), lambda j: (0, j)),
        out_shape=jax.ShapeDtypeStruct((b, 3 * d), F32),
        name="adaln_mod",
    )(c, ada_w, ada_b.reshape(1, 3 * d))


CONV_ROWS = 64
CONV_SUB = 256
SUBLANES = 8
TOKEN_CHAINS = 3


def _conv_unit(buf, r0, cl, half_taps, shift_masks, norm_scale, skip_norm):
    a = buf[r0:r0 + HALO + CONV_ROWS, cl]
    n = CONV_ROWS // SUBLANES
    vs = [a[SUBLANES * i:SUBLANES * (i + 1)] for i in range(n + 1)]
    acc = [vs[i + 1] * half_taps[CONV_K - 1] for i in range(n)]
    for shift in range(1, CONV_K):
        rs = [pltpu.roll(v, shift, 0) for v in vs]
        tap = half_taps[CONV_K - 1 - shift]
        for i in range(n):
            acc[i] = acc[i] + jnp.where(shift_masks[shift], rs[i], rs[i + 1]) * tap
    h = jnp.concatenate(acc, axis=0)
    y = h + h * jnp.tanh(h)
    ss = jnp.sum(y * y, axis=-1, keepdims=True)
    out = y * jnp.where(skip_norm, 1.0, lax.rsqrt(ss + EPS) * norm_scale)
    token = jnp.broadcast_to(ss[CONV_ROWS - SUBLANES:], (SUBLANES, LANES))
    return out.astype(BF16), token


def _inproj_kernel(x_ref, mod_ref, nw_ref, wc_ref, wp_ref, ws_ref, cw_ref, pc_ref, pp_ref, small_ref,
                   h_scr, cbuf, tok, *, steps_per_seg):
    j = pl.program_id(1)
    s = x_ref.shape[1]

    @pl.when(j == 0)
    def _():
        x = x_ref[0]
        m = mod_ref[0]
        y = x * lax.rsqrt(jnp.mean(x * x, axis=-1, keepdims=True) + EPS) * nw_ref[...]
        h = (y * (1.0 + m[1:2]) + m[0:1]).astype(BF16)
        h_scr[...] = h
        small_ref[0] = _mm_nt(h, ws_ref[...])
        cbuf[0:HALO, :] = jnp.zeros((HALO, CONV_SUB), F32)

    cbuf[HALO:HALO + s, :] = _mm_nt(h_scr[...], wc_ref[...])
    pp_ref[0] = _mm_nt(h_scr[...], wp_ref[...]).astype(BF16)

    sub = lax.broadcasted_iota(jnp.int32, (SUBLANES, LANES), 0)
    shift_masks = [sub < shift for shift in range(CONV_K)]
    q_scale = jnp.where(j < steps_per_seg, DN_DK ** -0.5, 1.0)
    is_v = j >= 2 * steps_per_seg
    never = j < 0
    for c in range(TOKEN_CHAINS):
        tok[c] = jnp.zeros((SUBLANES, LANES), F32)
    n_unit = 0
    for c0 in range(0, CONV_SUB, DN_DK):
        cl = slice(c0, c0 + DN_DK)
        half_taps = [cw_ref[t:t + 1, cl] * 0.5 for t in range(CONV_K)]
        for r0 in range(0, s, CONV_ROWS):
            slot = n_unit % TOKEN_CHAINS
            held = tok[slot][0:1]
            taps = [jnp.where(never, held, tap) for tap in half_taps]
            out, token = _conv_unit(cbuf, r0, cl, taps, shift_masks, q_scale, is_v)
            pc_ref[0, r0:r0 + CONV_ROWS, cl] = out
            tok[slot] = token
            n_unit += 1


def _inproj(x, mod3, norm_w, w_conv, w_plain, w_small, conv_w):
    b, s, d = x.shape
    conv_width, plain_width = w_conv.shape[0], w_plain.shape[0]
    n_steps = conv_width // CONV_SUB
    tp = plain_width // n_steps
    assert n_steps % 3 == 0 and n_steps * CONV_SUB == conv_width and conv_w.shape[1] == conv_width
    assert tp * n_steps == plain_width and tp % LANES == 0 and s % CONV_ROWS == 0
    kern = functools.partial(_inproj_kernel, steps_per_seg=n_steps // 3)
    return pl.pallas_call(
        kern,
        grid=(b, n_steps),
        in_specs=[pl.BlockSpec((1, s, d), lambda bi, j: (bi, 0, 0)),
                  pl.BlockSpec((1, 3, d), lambda bi, j: (bi, 0, 0)),
                  pl.BlockSpec((1, d), lambda bi, j: (0, 0)),
                  pl.BlockSpec((CONV_SUB, d), lambda bi, j: (j, 0)),
                  pl.BlockSpec((tp, d), lambda bi, j: (j, 0)),
                  pl.BlockSpec((LANES, d), lambda bi, j: (0, 0)),
                  pl.BlockSpec((CONV_K, CONV_SUB), lambda bi, j: (0, j))],
        out_specs=[pl.BlockSpec((1, s, CONV_SUB), lambda bi, j: (bi, 0, j)),
                   pl.BlockSpec((1, s, tp), lambda bi, j: (bi, 0, j)),
                   pl.BlockSpec((1, s, LANES), lambda bi, j: (bi, 0, 0))],
        out_shape=[jax.ShapeDtypeStruct((b, s, conv_width), BF16),
                   jax.ShapeDtypeStruct((b, s, plain_width), BF16),
                   jax.ShapeDtypeStruct((b, s, LANES), F32)],
        scratch_shapes=[pltpu.VMEM((s, d), BF16),
                        pltpu.VMEM((HALO + s, CONV_SUB), F32),
                        pltpu.VMEM((TOKEN_CHAINS, SUBLANES, LANES), F32)],
        compiler_params=pltpu.CompilerParams(
            dimension_semantics=("parallel", "arbitrary"), vmem_limit_bytes=VMEM_LIMIT),
        name="norm_inproj",
    )(x, mod3, norm_w.reshape(1, d), w_conv, w_plain, w_small, conv_w)


def _level_mask(level, i, j):
    same_block = (i >> (level + 1)) == (j >> (level + 1))
    return same_block & (((i >> level) & 1) == 1) & (((j >> level) & 1) == 0)


N_LEVELS = CHUNK.bit_length() - 1


def _deltanet_kernel(q_ref, k_ref, v_ref, small_ref, alog_ref, dtb_ref, o_ref,
                     state, u_s, wq_s, akd_s, el_s, *, ts, group):
    @pl.when(pl.program_id(1) == 0)
    def _():
        state[...] = jnp.zeros_like(state)

    row = lax.broadcasted_iota(jnp.int32, (CHUNK, 2 * CHUNK), 0)
    lane = lax.broadcasted_iota(jnp.int32, (CHUNK, 2 * CHUNK), 1)
    col = lane & (CHUNK - 1)
    in_a = lane < CHUNK
    causal = row >= col
    tri_strict = (row > col).astype(F32)
    eye = (row == col).astype(F32)
    row_c = lax.broadcasted_iota(jnp.int32, (CHUNK, CHUNK), 0)
    col_c = lax.broadcasted_iota(jnp.int32, (CHUNK, CHUNK), 1)
    tri = (row_c >= col_c).astype(F32)
    level_masks = [_level_mask(level, row, col).astype(F32) for level in range(N_LEVELS)]
    neg_a = -jnp.exp(alog_ref[...])
    dtb = dtb_ref[...]
    n_chunks = ts // CHUNK
    n_pairs = DN_HEADS // 2
    zeros_wide = jnp.zeros((CHUNK, DN_DV + DN_DK), BF16)
    zeros_head = jnp.zeros((CHUNK, DN_DK), BF16)

    def blockdiag(m):
        return jnp.concatenate([jnp.where(in_a, m, 0.0), jnp.where(in_a, 0.0, m)], axis=0).astype(BF16)

    def prep_stages(gi):
        chains = []
        for ci in range(group):
            r0 = (gi * group + ci) * CHUNK
            sm = small_ref[0, pl.ds(r0, CHUNK), :]
            beta_all = _sigmoid(sm)
            z = sm + dtb
            softplus = jnp.maximum(z, 0.0) + jnp.log(1.0 + jnp.exp(-jnp.abs(z)))
            g_all = neg_a * softplus
            gc_all = jnp.dot(tri, g_all, preferred_element_type=F32, precision=HIGHEST)
            gc_t = jnp.concatenate([gc_all, gc_all], axis=0).T
            g_last = gc_all[CHUNK - 1:CHUNK, :]
            e_in_all = jnp.exp(gc_all)
            e_out_all = jnp.exp(g_last - gc_all)
            slot = (gi % 2) * group + ci
            el_s[slot] = jnp.broadcast_to(jnp.exp(g_last), (8, LANES))
            for p in range(n_pairs):
                heads = (2 * p, 2 * p + 1)
                q, k, k_beta, rhs, k_dec = [], [], [], [], []
                for h in heads:
                    gl = DN_HEADS + h
                    cq = slice(h * DN_DK, (h + 1) * DN_DK)
                    q_h = q_ref[0, pl.ds(r0, CHUNK), cq].astype(F32)
                    k_h = k_ref[0, pl.ds(r0, CHUNK), cq].astype(F32)
                    v_h = v_ref[0, pl.ds(r0, CHUNK), slice(h * DN_DV, (h + 1) * DN_DV)].astype(F32)
                    beta = beta_all[:, h:h + 1]
                    e_in = e_in_all[:, gl:gl + 1]
                    kb_h = k_h * beta
                    q.append(q_h)
                    k.append(k_h.astype(BF16))
                    k_beta.append(kb_h)
                    rhs.append(jnp.concatenate([v_h * beta, kb_h * e_in], axis=1).astype(BF16))
                    k_dec.append(k_h * e_out_all[:, gl:gl + 1])
                    wq_s[slot, h, CHUNK:2 * CHUNK, :] = (q_h * e_in).astype(BF16)
                ga, gb = DN_HEADS + heads[0], DN_HEADS + heads[1]
                gc_col = jnp.where(in_a, gc_all[:, ga:ga + 1], gc_all[:, gb:gb + 1])
                gc_row = jnp.where(in_a[0:1], gc_t[ga:ga + 1, :], gc_t[gb:gb + 1, :])
                chains.append(dict(
                    slot=slot, p=p,
                    decay=jnp.exp(jnp.where(causal, gc_col - gc_row, -jnp.inf)),
                    kq=jnp.concatenate([jnp.concatenate(k_beta, axis=1), jnp.concatenate(q, axis=1)],
                                       axis=0).astype(BF16),
                    kk=jnp.concatenate([jnp.concatenate([k[0], zeros_head], axis=1),
                                        jnp.concatenate([zeros_head, k[1]], axis=1)], axis=0),
                    rhs=jnp.concatenate([jnp.concatenate([rhs[0], zeros_wide], axis=1),
                                         jnp.concatenate([zeros_wide, rhs[1]], axis=1)], axis=0),
                    k_dec_t=jnp.concatenate(k_dec, axis=0).T.astype(BF16)))

        for ch in chains:
            ch["kk_qk"] = _mm_nt(ch.pop("kq"), ch.pop("kk"))
        yield
        for ch in chains:
            kk_qk = ch.pop("kk_qk")
            decay = ch.pop("decay")
            ch["lmat"] = kk_qk[0:CHUNK] * decay * tri_strict
            ch["a_qk"] = (kk_qk[CHUNK:2 * CHUNK] * decay).astype(BF16)
            ch["x"] = eye - ch["lmat"] * level_masks[0]
        for level in range(1, N_LEVELS):
            for ch in chains:
                ch["y"] = _mm(ch["lmat"] * level_masks[level], blockdiag(ch["x"]))
            yield
            for ch in chains:
                ch["x"] = ch["x"] - _mm(ch["x"], blockdiag(ch.pop("y")))
            yield
        for ch in chains:
            ch["uw"] = _mm(ch.pop("x"), ch.pop("rhs"))
        for ch in chains:
            slot, p = ch["slot"], ch["p"]
            uw = ch.pop("uw")
            for i, h in enumerate((2 * p, 2 * p + 1)):
                base = i * (DN_DV + DN_DK)
                u_s[slot, h] = uw[:, base:base + DN_DV]
                wq_s[slot, h, 0:CHUNK, :] = uw[:, base + DN_DV:base + DN_DV + DN_DK].astype(BF16)
            akd_s[slot, p] = jnp.concatenate([ch["a_qk"], ch["k_dec_t"]], axis=0)
        yield

    def rec_stages(gi):
        for ci in range(group):
            r0 = (gi * group + ci) * CHUNK
            slot = (gi % 2) * group + ci
            e_last_all = el_s[slot]
            rs = [_mm(wq_s[slot, h], state[h]) for h in range(DN_HEADS)]
            yield
            v_new = [(u_s[slot, h] - rs[h][0:CHUNK]).astype(BF16) for h in range(DN_HEADS)]
            zeros_v = jnp.zeros((CHUNK, DN_DV), BF16)
            av = [_mm(akd_s[slot, p],
                      jnp.concatenate([jnp.concatenate([v_new[2 * p], zeros_v], axis=1),
                                       jnp.concatenate([zeros_v, v_new[2 * p + 1]], axis=1)], axis=0))
                  for p in range(n_pairs)]
            for h in range(DN_HEADS):
                gl = DN_HEADS + h
                lanes_h = slice((h % 2) * DN_DV, (h % 2 + 1) * DN_DV)
                o = rs[h][CHUNK:2 * CHUNK] + av[h // 2][0:CHUNK, lanes_h]
                state[h] = state[h] * e_last_all[0:1, gl:gl + 1] + av[h // 2][CHUNK:CHUNK + DN_DK, lanes_h]
                o_ref[0, pl.ds(r0, CHUNK), slice(h * DN_DV, (h + 1) * DN_DV)] = o.astype(o_ref.dtype)
            yield

    run = _run_round_robin

    n_groups = n_chunks // group
    run(prep_stages(0))
    for gi in range(1, n_groups):
        run(prep_stages(gi), rec_stages(gi - 1))
    run(rec_stages(n_groups - 1))


def _deltanet(p, small, alog_row, dtb_row, ts, group):
    b, s, _ = p.shape
    width = DN_HEADS * DN_DK
    n_slots = 2 * group
    assert (ts // CHUNK) % group == 0
    kern = functools.partial(_deltanet_kernel, ts=ts, group=group)
    return pl.pallas_call(
        kern,
        grid=(b, s // ts),
        in_specs=[pl.BlockSpec((1, ts, width), lambda bi, t: (bi, t, 0)),
                  pl.BlockSpec((1, ts, width), lambda bi, t: (bi, t, 1)),
                  pl.BlockSpec((1, ts, width), lambda bi, t: (bi, t, 2)),
                  pl.BlockSpec((1, ts, LANES), lambda bi, t: (bi, t, 0)),
                  pl.BlockSpec((1, LANES), lambda bi, t: (0, 0)),
                  pl.BlockSpec((1, LANES), lambda bi, t: (0, 0))],
        out_specs=pl.BlockSpec((1, ts, DN_WIDTH), lambda bi, t: (bi, t, 0)),
        out_shape=jax.ShapeDtypeStruct((b, s, DN_WIDTH), BF16),
        scratch_shapes=[pltpu.VMEM((DN_HEADS, DN_DK, DN_DV), F32),
                        pltpu.VMEM((n_slots, DN_HEADS, CHUNK, DN_DV), F32),
                        pltpu.VMEM((n_slots, DN_HEADS, 2 * CHUNK, DN_DK), BF16),
                        pltpu.VMEM((n_slots, DN_HEADS // 2, CHUNK + DN_DK, 2 * CHUNK), BF16),
                        pltpu.VMEM((n_slots, 8, LANES), F32)],
        compiler_params=pltpu.CompilerParams(
            dimension_semantics=("parallel", "arbitrary"), vmem_limit_bytes=VMEM_LIMIT),
        name="gated_deltanet",
    )(p, p, p, small, alog_row, dtb_row)


ATTN_BLOCKS_PER_ITER = 4
ATTN_STRIDE = 4


def _attn_kernel(q_ref, k_ref, v_ref, o_ref, lse_ref, *scratch, dil, n_back):
    s = q_ref.shape[1]
    sub_len = s // dil
    nb = sub_len // ATTN_BLOCK
    n_pairs = ATTN_GROUP_WIDTH // LANES
    prev_block = nb > 1

    def lanes(pair):
        return slice(pair * LANES, (pair + 1) * LANES)

    if dil > 1:
        stage, tmp, qs, ks, vs, os_, ls = scratch
        assert dil in (ATTN_STRIDE, ATTN_STRIDE * ATTN_STRIDE)
        q_len = s // ATTN_STRIDE

        def classes():
            if dil == ATTN_STRIDE:
                return [(r, stage, r) for r in range(dil)]
            return [(r_lo + ATTN_STRIDE * r_hi, tmp, r_lo * q_len + r_hi)
                    for r_lo in range(ATTN_STRIDE) for r_hi in range(ATTN_STRIDE)]

        def deinterleave(dst, pair):
            if dil > ATTN_STRIDE:
                for r_lo in range(ATTN_STRIDE):
                    tmp[r_lo * q_len:(r_lo + 1) * q_len, :] = stage[pl.ds(r_lo, q_len, stride=ATTN_STRIDE), :]
            for r, buf, start in classes():
                dst[pair, r * sub_len:(r + 1) * sub_len, :] = (
                    buf[pl.ds(start, sub_len, stride=ATTN_STRIDE), :].astype(dst.dtype))

        def interleave(src, pair):
            for r, buf, start in classes():
                buf[pl.ds(start, sub_len, stride=ATTN_STRIDE), :] = src[pair, r * sub_len:(r + 1) * sub_len, :]
            if dil > ATTN_STRIDE:
                for r_lo in range(ATTN_STRIDE):
                    stage[pl.ds(r_lo, q_len, stride=ATTN_STRIDE), :] = tmp[r_lo * q_len:(r_lo + 1) * q_len, :]

        for src, dst in ((q_ref, qs), (k_ref, ks), (v_ref, vs)):
            for pair in range(n_pairs):
                stage[...] = src[0, :, lanes(pair)].astype(F32)
                deinterleave(dst, pair)

        def load(ref, src, pair, r0):
            return src[pair, pl.ds(r0, ATTN_BLOCK), :]
    else:
        qs, ks, vs = q_ref, k_ref, v_ref

        def load(ref, src, pair, r0):
            return ref[0, pl.ds(r0, ATTN_BLOCK), lanes(pair)]

    n_keys = 2 * ATTN_BLOCK if prev_block else ATTN_BLOCK
    qi = lax.broadcasted_iota(jnp.int32, (ATTN_BLOCK, n_keys), 0)
    ki = lax.broadcasted_iota(jnp.int32, (ATTN_BLOCK, n_keys), 1)
    dist = qi + (n_keys - ATTN_BLOCK) - ki
    band = (dist >= 0) & (dist <= n_back)
    bias_full = jnp.where(band, 0.0, -jnp.inf).astype(F32)
    bias_first = jnp.where(band & (ki >= ATTN_BLOCK), 0.0, -jnp.inf).astype(F32)
    lane = lax.broadcasted_iota(jnp.int32, (ATTN_BLOCK, LANES), 1)
    low_half = lane < ATTN_HEAD_DIM
    scale = ATTN_HEAD_DIM ** -0.5

    def block_body(fi, carry):
        chains = []
        for bi in range(ATTN_BLOCKS_PER_ITER):
            f = fi * ATTN_BLOCKS_PER_ITER + bi
            q0 = pl.multiple_of(f * ATTN_BLOCK, ATTN_BLOCK)
            if prev_block:
                k0 = pl.multiple_of(jnp.maximum(f - 1, 0) * ATTN_BLOCK, ATTN_BLOCK)
                bias = jnp.where(f % nb > 0, bias_full, bias_first)
            else:
                bias = bias_full
            for pair in range(n_pairs):
                qp = load(q_ref, qs, pair, q0) * scale
                kp = load(k_ref, ks, pair, q0)
                vp = load(v_ref, vs, pair, q0)
                if prev_block:
                    kp = jnp.concatenate([load(k_ref, ks, pair, k0), kp], axis=0)
                    vp = jnp.concatenate([load(v_ref, vs, pair, k0), vp], axis=0)
                for half in range(2):
                    sel = low_half if half == 0 else ~low_half
                    chains.append(dict(q0=q0, pair=pair, half=half, bias=bias, kp=kp, vp=vp,
                                       qm=jnp.where(sel, qp, jnp.zeros_like(qp))))
        for ch in chains:
            ch["sc"] = _mm_nt(ch.pop("qm"), ch.pop("kp"))
        for ch in chains:
            sc = ch.pop("sc") + ch.pop("bias")
            m = jnp.max(sc, axis=-1, keepdims=True)
            p = jnp.exp(sc - m)
            ch["denom"] = jnp.sum(p, axis=-1, keepdims=True)
            ch["m"] = m
            ch["p"] = p.astype(BF16)
        for ch in chains:
            ch["pv"] = _mm(ch.pop("p"), ch.pop("vp"))
        for c0 in range(0, len(chains), 2):
            lo, hi = chains[c0], chains[c0 + 1]
            o = jnp.where(low_half, lo["pv"] / lo["denom"], hi["pv"] / hi["denom"])
            lse = jnp.where(low_half, lo["m"] + jnp.log(lo["denom"]), hi["m"] + jnp.log(hi["denom"]))
            q0, pair = lo["q0"], lo["pair"]
            if dil > 1:
                os_[pair, pl.ds(q0, ATTN_BLOCK), :] = o
                ls[pair, pl.ds(q0, ATTN_BLOCK), :] = lse
            else:
                o_ref[0, pl.ds(q0, ATTN_BLOCK), lanes(pair)] = o.astype(o_ref.dtype)
                lse_ref[0, pl.ds(q0, ATTN_BLOCK), lanes(pair)] = lse
        return carry

    lax.fori_loop(0, dil * nb // ATTN_BLOCKS_PER_ITER, block_body, 0)

    if dil > 1:
        for pair in range(n_pairs):
            for src, dst in ((os_, o_ref), (ls, lse_ref)):
                interleave(src, pair)
                dst[0, :, lanes(pair)] = stage[...].astype(dst.dtype)


def _dilated_attention(p, group, col_base):
    b, s, _ = p.shape
    window, dil = DIL_CONFIGS[group]
    gw = ATTN_GROUP_WIDTH
    blk = col_base // gw + group
    step = ATTN_WIDTH // gw
    kern = functools.partial(_attn_kernel, dil=dil, n_back=window // dil)
    return pl.pallas_call(
        kern,
        grid=(b,),
        in_specs=[pl.BlockSpec((1, s, gw), lambda bi: (bi, 0, blk)),
                  pl.BlockSpec((1, s, gw), lambda bi: (bi, 0, blk + step)),
                  pl.BlockSpec((1, s, gw), lambda bi: (bi, 0, blk + 2 * step))],
        out_specs=[pl.BlockSpec((1, s, gw), lambda bi: (bi, 0, 0)),
                   pl.BlockSpec((1, s, gw), lambda bi: (bi, 0, 0))],
        out_shape=[jax.ShapeDtypeStruct((b, s, gw), BF16),
                   jax.ShapeDtypeStruct((b, s, gw), F32)],
        scratch_shapes=([] if dil == 1 else
                        [pltpu.VMEM((s, LANES), F32) for _ in range(2)]
                        + [pltpu.VMEM((gw // LANES, s, LANES), BF16) for _ in range(3)]
                        + [pltpu.VMEM((gw // LANES, s, LANES), F32) for _ in range(2)]),
        compiler_params=pltpu.CompilerParams(
            dimension_semantics=("parallel",), vmem_limit_bytes=VMEM_LIMIT),
        name=f"dilated_attn_g{group}",
    )(p, p, p)


MERGE_SUBTILES = 8


def _merge_kernel(x_ref, mod_ref, oa_ref, za_ref, ga_ref, gb_ref, zb_ref,
                  ob0_ref, ob1_ref, ob2_ref, l0_ref, l1_ref, l2_ref,
                  dnw_ref, pa_ref, pb_ref, wo_ref, fw_ref, out_ref):
    def silu_of_half(h):
        return h + h * jnp.tanh(h)

    gate = mod_ref[0][2:3]

    def stages(rows):
        oa = oa_ref[0, rows, :].astype(F32)
        za_half = za_ref[0, rows, :].astype(F32)
        parts = []
        for h in range(DN_HEADS):
            cq = slice(h * DN_DV, (h + 1) * DN_DV)
            blk = oa[:, cq]
            y = blk * lax.rsqrt(jnp.mean(blk * blk, axis=-1, keepdims=True) + EPS) * dnw_ref[...]
            parts.append(y * silu_of_half(za_half[:, cq]))
        ya_half = _mm(jnp.concatenate(parts, axis=1), pa_ref[...])
        yield

        l0, l1, l2 = l0_ref[0, rows, :], l1_ref[0, rows, :], l2_ref[0, rows, :]
        m = jnp.maximum(jnp.maximum(l0, l1), l2)
        e0, e1, e2 = jnp.exp(l0 - m), jnp.exp(l1 - m), jnp.exp(l2 - m)
        inv_den = 1.0 / (e0 + e1 + e2)
        ob = jnp.concatenate([ob0_ref[0, rows, :].astype(F32) * (e0 * inv_den),
                              ob1_ref[0, rows, :].astype(F32) * (e1 * inv_den),
                              ob2_ref[0, rows, :].astype(F32) * (e2 * inv_den)], axis=1)
        yb_half = _mm(ob * silu_of_half(zb_ref[0, rows, :].astype(F32)), pb_ref[...])
        yield

        merged = (ya_half + ya_half * jnp.tanh(ga_ref[0, rows, :].astype(F32))
                  + (yb_half + yb_half * jnp.tanh(gb_ref[0, rows, :].astype(F32))))
        delta = _mm(merged, wo_ref[...])
        yield

        xo = x_ref[0, rows, :] + gate * delta
        out_ref[0, rows, :] = xo * lax.rsqrt(jnp.mean(xo * xo, axis=-1, keepdims=True) + EPS) * fw_ref[...]
        yield

    tm = x_ref.shape[1]
    sub = tm // MERGE_SUBTILES
    _run_round_robin(*[stages(slice(i * sub, (i + 1) * sub)) for i in range(MERGE_SUBTILES)])


def _merge(x, mod3, o_a, p, obs, lses, dn_norm_w, pa, pb, wo, final_w, tm, cols):
    b, s, d = x.shape
    gw = ATTN_GROUP_WIDTH

    def row_spec(width, blk):
        return pl.BlockSpec((1, tm, width), lambda bi, i: (bi, i, blk))

    def full_spec(shape):
        return pl.BlockSpec(shape, lambda bi, i: (0,) * len(shape), pipeline_mode=pl.Buffered(1))

    return pl.pallas_call(
        _merge_kernel,
        grid=(b, s // tm),
        in_specs=[row_spec(d, 0),
                  pl.BlockSpec((1, 3, d), lambda bi, i: (bi, 0, 0)),
                  row_spec(DN_WIDTH, 0),
                  row_spec(DN_WIDTH, cols["za"] // DN_WIDTH),
                  row_spec(d, cols["ga"] // d),
                  row_spec(d, cols["gb"] // d),
                  row_spec(ATTN_WIDTH, cols["zb"] // ATTN_WIDTH),
                  row_spec(gw, 0), row_spec(gw, 0), row_spec(gw, 0),
                  row_spec(gw, 0), row_spec(gw, 0), row_spec(gw, 0),
                  full_spec((1, DN_DV)), full_spec(pa.shape), full_spec(pb.shape), full_spec(wo.shape),
                  full_spec((1, d))],
        out_specs=row_spec(d, 0),
        out_shape=jax.ShapeDtypeStruct((b, s, d), F32),
        compiler_params=pltpu.CompilerParams(
            dimension_semantics=("parallel", "parallel"), vmem_limit_bytes=VMEM_LIMIT),
        name="merge_outproj",
    )(x, mod3, o_a, p, p, p, p, *obs, *lses, dn_norm_w.reshape(1, DN_DV), pa, pb, wo, final_w.reshape(1, d))


W_PREP_COLS = 128


def _wprep_kernel(wt_ref, wc_ref, wp_ref, ws_ref, *, conv_width, segments, small_off):
    wc_ref[...] = wt_ref[0:conv_width, :].astype(BF16)
    for src, width, dst, scale in segments:
        blk = wt_ref[src:src + width, :]
        wp_ref[dst:dst + width, :] = (blk if scale == 1.0 else blk * scale).astype(BF16)
    n_small = 2 * DN_HEADS
    ws_ref[0:n_small, :] = wt_ref[small_off:small_off + n_small, :].astype(BF16)
    ws_ref[n_small:LANES, :] = jnp.zeros((LANES - n_small, wt_ref.shape[1]), BF16)


def _weight_prep(wt, conv_width, plain_width, segments, small_off):
    n_in, d = wt.shape
    assert d % W_PREP_COLS == 0 and small_off % 16 == 0 and all(sg[0] % 16 == 0 for sg in segments)
    kern = functools.partial(_wprep_kernel, conv_width=conv_width, segments=segments, small_off=small_off)
    return pl.pallas_call(
        kern,
        grid=(d // W_PREP_COLS,),
        in_specs=[pl.BlockSpec((n_in, W_PREP_COLS), lambda i: (0, i))],
        out_specs=[pl.BlockSpec((conv_width, W_PREP_COLS), lambda i: (0, i)),
                   pl.BlockSpec((plain_width, W_PREP_COLS), lambda i: (0, i)),
                   pl.BlockSpec((LANES, W_PREP_COLS), lambda i: (0, i))],
        out_shape=[jax.ShapeDtypeStruct((conv_width, d), BF16),
                   jax.ShapeDtypeStruct((plain_width, d), BF16),
                   jax.ShapeDtypeStruct((LANES, d), BF16)],
        compiler_params=pltpu.CompilerParams(dimension_semantics=("parallel",), vmem_limit_bytes=VMEM_LIMIT),
        name="weight_prep",
    )(wt)


def _layer(x, c, norm_w, ada_w, ada_b, w_in, conv_w, a_log, dt_bias, dn_norm_w, w_proj_a, w_proj_b, w_out,
           final_norm_w):
    b, s, d = x.shape
    qk = DN_HEADS * DN_DK
    splits = (qk, qk, DN_WIDTH, DN_WIDTH, DN_HEADS, DN_HEADS, ATTN_WIDTH, ATTN_WIDTH, ATTN_WIDTH, ATTN_WIDTH, d, d)
    names = ("qa", "ka", "va", "za", "beta", "a", "qb", "kb", "vb", "zb", "ga", "gb")
    offs = dict(zip(names, np.cumsum((0,) + splits[:-1]).tolist()))
    widths = dict(zip(names, splits))
    order = ("za", "ga", "gb", "qb", "kb", "vb", "zb")
    cols, pos = {}, 0
    for n in order:
        assert pos % widths[n] == 0, (n, pos)
        cols[n] = pos
        pos += widths[n]
    assert offs["qa"] == 0 and offs["ka"] == qk and offs["va"] == 2 * qk and offs["a"] == offs["beta"] + DN_HEADS
    halved = ("za", "ga", "gb", "zb")
    segments = tuple((offs[n], widths[n], cols[n], 0.5 if n in halved else 1.0) for n in order)
    w_conv, w_plain, w_small = _weight_prep(w_in.T, 2 * qk + DN_WIDTH, pos, segments, offs["beta"])
    pad_lo = jnp.zeros((DN_HEADS,), F32)
    pad_hi = jnp.zeros((LANES - 2 * DN_HEADS,), F32)
    alog_row = jnp.concatenate([pad_lo, a_log.astype(F32), pad_hi]).reshape(1, LANES)
    dtb_row = jnp.concatenate([pad_lo, dt_bias.astype(F32), pad_hi]).reshape(1, LANES)

    mod3 = _adaln_mod(c, ada_w, ada_b).reshape(b, 3, d)
    p_conv, p, small = _inproj(x, mod3, norm_w, w_conv, w_plain, w_small, conv_w)
    o_a = _deltanet(p_conv, small, alog_row, dtb_row, ts=min(s, 2048), group=4)
    obs, lses = [], []
    for g in range(len(DIL_CONFIGS)):
        o_g, lse_g = _dilated_attention(p, g, cols["qb"])
        obs.append(o_g)
        lses.append(lse_g)
    return _merge(x, mod3, o_a, p, obs, lses, dn_norm_w, (w_proj_a * 0.5).astype(BF16),
                  (w_proj_b * 0.5).astype(BF16), w_out.astype(BF16), final_norm_w, tm=min(s, 1024), cols=cols)


def kernel(x, c, norm_w, ada_w, ada_b, w_in, conv_w, a_log, dt_bias, dn_norm_w, w_proj_a, w_proj_b, w_out,
           final_norm_w):
    depth = norm_w.shape[0]
    assert depth == 1, "the final RMSNorm is fused into the single layer's output kernel"
    return _layer(x, c, norm_w[0], ada_w[0], ada_b[0], w_in[0], conv_w[0], a_log[0], dt_bias[0], dn_norm_w[0],
                  w_proj_a[0], w_proj_b[0], w_out[0], final_norm_w)
```

```python
import functools

import jax
import jax.numpy as jnp
import numpy as np
from jax import lax
from jax.experimental import pallas as pl
from jax.experimental.pallas import tpu as pltpu

F32 = jnp.float32
BF16 = jnp.bfloat16
HIGHEST = lax.Precision.HIGHEST

EPS = 1e-6
DN_HEADS = 8
DN_DK = 128
DN_DV = 128
DN_WIDTH = DN_HEADS * DN_DV
CONV_K = 4
CHUNK = 64
DIL_CONFIGS = ((128, 1), (512, 4), (2048, 16))
HEADS_PER_GROUP = 4
ATTN_HEAD_DIM = 64
ATTN_GROUP_WIDTH = HEADS_PER_GROUP * ATTN_HEAD_DIM
ATTN_WIDTH = ATTN_GROUP_WIDTH * len(DIL_CONFIGS)
ATTN_BLOCK = 128
LANES = 128
HALO = 8
VMEM_LIMIT = 56 * 1024 * 1024


def _mm(a, b):
    return jnp.dot(a.astype(BF16), b.astype(BF16), preferred_element_type=F32)


def _mm_nt(a, b):
    return lax.dot_general(a.astype(BF16), b.astype(BF16), (((1,), (1,)), ((), ())),
                           preferred_element_type=F32)


def _mm_tn(a, b):
    return lax.dot_general(a.astype(BF16), b.astype(BF16), (((0,), (0,)), ((), ())),
                           preferred_element_type=F32)


def _run_round_robin(*gens):
    gens = list(gens)
    while gens:
        for g in list(gens):
            if next(g, StopIteration) is StopIteration:
                gens.remove(g)


def _sigmoid(x):
    return 0.5 * jnp.tanh(0.5 * x) + 0.5


def _silu(x):
    return x * _sigmoid(x)


def _mod_kernel(c_ref, w_ref, b_ref, o_ref):
    sc = _silu(c_ref[...])
    o_ref[...] = jnp.dot(sc, w_ref[...], preferred_element_type=F32, precision=HIGHEST) + b_ref[...]


def _adaln_mod(c, ada_w, ada_b):
    b, d = c.shape
    return pl.pallas_call(
        _mod_kernel,
        grid=(3,),
        in_specs=[pl.BlockSpec((b, d), lambda j: (0, 0)),
                  pl.BlockSpec((d, d), lambda j: (0, j)),
                  pl.BlockSpec((1, d), lambda j: (0, j))],
        out_specs=pl.BlockSpec((b, d), lambda j: (0, j)),
        out_shape=jax.ShapeDtypeStruct((b, 3 * d), F32),
        name="adaln_mod",
    )(c, ada_w, ada_b.reshape(1, 3 * d))


CONV_ROWS = 64
CONV_SUB = 256
SUBLANES = 8
TOKEN_CHAINS = 3


def _conv_unit(buf, r0, cl, half_taps, shift_masks, norm_scale, skip_norm):
    a = buf[r0:r0 + HALO + CONV_ROWS, cl]
    n = CONV_ROWS // SUBLANES
    vs = [a[SUBLANES * i:SUBLANES * (i + 1)] for i in range(n + 1)]
    acc = [vs[i + 1] * half_taps[CONV_K - 1] for i in range(n)]
    for shift in range(1, CONV_K):
        rs = [pltpu.roll(v, shift, 0) for v in vs]
        tap = half_taps[CONV_K - 1 - shift]
        for i in range(n):
            acc[i] = acc[i] + jnp.where(shift_masks[shift], rs[i], rs[i + 1]) * tap
    h = jnp.concatenate(acc, axis=0)
    y = h + h * jnp.tanh(h)
    ss = jnp.sum(y * y, axis=-1, keepdims=True)
    out = y * jnp.where(skip_norm, 1.0, lax.rsqrt(ss + EPS) * norm_scale)
    token = jnp.broadcast_to(ss[CONV_ROWS - SUBLANES:], (SUBLANES, LANES))
    return out.astype(BF16), token


def _inproj_kernel(x_ref, mod_ref, nw_ref, wc_ref, wp_ref, ws_ref, cw_ref, pc_ref, pp_ref, small_ref,
                   h_scr, cbuf, tok, *, steps_per_seg):
    j = pl.program_id(1)
    s = x_ref.shape[1]

    @pl.when(j == 0)
    def _():
        x = x_ref[0]
        m = mod_ref[0]
        y = x * lax.rsqrt(jnp.mean(x * x, axis=-1, keepdims=True) + EPS) * nw_ref[...]
        h = (y * (1.0 + m[1:2]) + m[0:1]).astype(BF16)
        h_scr[...] = h
        small_ref[0] = _mm_nt(h, ws_ref[...])
        cbuf[0:HALO, :] = jnp.zeros((HALO, CONV_SUB), F32)

    cbuf[HALO:HALO + s, :] = _mm_nt(h_scr[...], wc_ref[...])
    pp_ref[0] = _mm_nt(h_scr[...], wp_ref[...]).astype(BF16)

    sub = lax.broadcasted_iota(jnp.int32, (SUBLANES, LANES), 0)
    shift_masks = [sub < shift for shift in range(CONV_K)]
    q_scale = jnp.where(j < steps_per_seg, DN_DK ** -0.5, 1.0)
    is_v = j >= 2 * steps_per_seg
    never = j < 0
    for c in range(TOKEN_CHAINS):
        tok[c] = jnp.zeros((SUBLANES, LANES), F32)
    n_unit = 0
    for c0 in range(0, CONV_SUB, DN_DK):
        cl = slice(c0, c0 + DN_DK)
        half_taps = [cw_ref[t:t + 1, cl] * 0.5 for t in range(CONV_K)]
        for r0 in range(0, s, CONV_ROWS):
            slot = n_unit % TOKEN_CHAINS
            held = tok[slot][0:1]
            taps = [jnp.where(never, held, tap) for tap in half_taps]
            out, token = _conv_unit(cbuf, r0, cl, taps, shift_masks, q_scale, is_v)
            pc_ref[0, r0:r0 + CONV_ROWS, cl] = out
            tok[slot] = token
            n_unit += 1


def _inproj(x, mod3, norm_w, w_conv, w_plain, w_small, conv_w):
    b, s, d = x.shape
    conv_width, plain_width = w_conv.shape[0], w_plain.shape[0]
    n_steps = conv_width // CONV_SUB
    tp = plain_width // n_steps
    assert n_steps % 3 == 0 and n_steps * CONV_SUB == conv_width and conv_w.shape[1] == conv_width
    assert tp * n_steps == plain_width and tp % LANES == 0 and s % CONV_ROWS == 0
    kern = functools.partial(_inproj_kernel, steps_per_seg=n_steps // 3)
    return pl.pallas_call(
        kern,
        grid=(b, n_steps),
        in_specs=[pl.BlockSpec((1, s, d), lambda bi, j: (bi, 0, 0)),
                  pl.BlockSpec((1, 3, d), lambda bi, j: (bi, 0, 0)),
                  pl.BlockSpec((1, d), lambda bi, j: (0, 0)),
                  pl.BlockSpec((CONV_SUB, d), lambda bi, j: (j, 0)),
                  pl.BlockSpec((tp, d), lambda bi, j: (j, 0)),
                  pl.BlockSpec((LANES, d), lambda bi, j: (0, 0)),
                  pl.BlockSpec((CONV_K, CONV_SUB), lambda bi, j: (0, j))],
        out_specs=[pl.BlockSpec((1, s, CONV_SUB), lambda bi, j: (bi, 0, j)),
                   pl.BlockSpec((1, s, tp), lambda bi, j: (bi, 0, j)),
                   pl.BlockSpec((1, s, LANES), lambda bi, j: (bi, 0, 0))],
        out_shape=[jax.ShapeDtypeStruct((b, s, conv_width), BF16),
                   jax.ShapeDtypeStruct((b, s, plain_width), BF16),
                   jax.ShapeDtypeStruct((b, s, LANES), F32)],
        scratch_shapes=[pltpu.VMEM((s, d), BF16),
                        pltpu.VMEM((HALO + s, CONV_SUB), F32),
                        pltpu.VMEM((TOKEN_CHAINS, SUBLANES, LANES), F32)],
        compiler_params=pltpu.CompilerParams(
            dimension_semantics=("parallel", "arbitrary"), vmem_limit_bytes=VMEM_LIMIT),
        name="norm_inproj",
    )(x, mod3, norm_w.reshape(1, d), w_conv, w_plain, w_small, conv_w)


def _level_mask(level, i, j):
    same_block = (i >> (level + 1)) == (j >> (level + 1))
    return same_block & (((i >> level) & 1) == 1) & (((j >> level) & 1) == 0)


N_LEVELS = CHUNK.bit_length() - 1


def _deltanet_kernel(q_ref, k_ref, v_ref, small_ref, alog_ref, dtb_ref, o_ref,
                     state, u_s, wq_s, akd_s, el_s, *, ts, group):
    @pl.when(pl.program_id(1) == 0)
    def _():
        state[...] = jnp.zeros_like(state)

    row = lax.broadcasted_iota(jnp.int32, (CHUNK, 2 * CHUNK), 0)
    lane = lax.broadcasted_iota(jnp.int32, (CHUNK, 2 * CHUNK), 1)
    col = lane & (CHUNK - 1)
    in_a = lane < CHUNK
    causal = row >= col
    tri_strict = (row > col).astype(F32)
    eye = (row == col).astype(F32)
    row_c = lax.broadcasted_iota(jnp.int32, (CHUNK, CHUNK), 0)
    col_c = lax.broadcasted_iota(jnp.int32, (CHUNK, CHUNK), 1)
    tri = (row_c >= col_c).astype(F32)
    level_masks = [_level_mask(level, row, col).astype(F32) for level in range(N_LEVELS)]
    neg_a = -jnp.exp(alog_ref[...])
    dtb = dtb_ref[...]
    n_chunks = ts // CHUNK
    n_pairs = DN_HEADS // 2
    zeros_wide = jnp.zeros((CHUNK, DN_DV + DN_DK), BF16)
    zeros_head = jnp.zeros((CHUNK, DN_DK), BF16)

    def blockdiag(m):
        return jnp.concatenate([jnp.where(in_a, m, 0.0), jnp.where(in_a, 0.0, m)], axis=0).astype(BF16)

    def prep_stages(gi):
        chains = []
        for ci in range(group):
            c = gi * group + ci
            r0 = c * CHUNK if isinstance(c, int) else pl.multiple_of(c * CHUNK, CHUNK)
            sm = small_ref[0, pl.ds(r0, CHUNK), :]
            beta_all = _sigmoid(sm)
            z = sm + dtb
            softplus = jnp.maximum(z, 0.0) + jnp.log(1.0 + jnp.exp(-jnp.abs(z)))
            g_all = neg_a * softplus
            gc_all = jnp.dot(tri, g_all, preferred_element_type=F32, precision=HIGHEST)
            gc_t = jnp.concatenate([gc_all, gc_all], axis=0).T
            g_last = gc_all[CHUNK - 1:CHUNK, :]
            e_in_all = jnp.exp(gc_all)
            e_out_all = jnp.exp(g_last - gc_all)
            slot = (gi % 2) * group + ci
            el_s[slot] = jnp.broadcast_to(jnp.exp(g_last), (8, LANES))
            for p in range(n_pairs):
                heads = (2 * p, 2 * p + 1)
                q, k, k_beta, rhs, k_dec = [], [], [], [], []
                for h in heads:
                    gl = DN_HEADS + h
                    cq = slice(h * DN_DK, (h + 1) * DN_DK)
                    q_h = q_ref[0, pl.ds(r0, CHUNK), cq].astype(F32)
                    k_h = k_ref[0, pl.ds(r0, CHUNK), cq].astype(F32)
                    v_h = v_ref[0, pl.ds(r0, CHUNK), slice(h * DN_DV, (h + 1) * DN_DV)].astype(F32)
                    beta = beta_all[:, h:h + 1]
                    e_in = e_in_all[:, gl:gl + 1]
                    kb_h = k_h * beta
                    q.append(q_h)
                    k.append(k_h.astype(BF16))
                    k_beta.append(kb_h)
                    rhs.append(jnp.concatenate([v_h * beta, kb_h * e_in], axis=1).astype(BF16))
                    k_dec.append(k_h * e_out_all[:, gl:gl + 1])
                    wq_s[slot, h, CHUNK:2 * CHUNK, :] = (q_h * e_in).astype(BF16)
                ga, gb = DN_HEADS + heads[0], DN_HEADS + heads[1]
                gc_col = jnp.where(in_a, gc_all[:, ga:ga + 1], gc_all[:, gb:gb + 1])
                gc_row = jnp.where(in_a[0:1], gc_t[ga:ga + 1, :], gc_t[gb:gb + 1, :])
                chains.append(dict(
                    slot=slot, p=p,
                    decay=jnp.exp(jnp.where(causal, gc_col - gc_row, -jnp.inf)),
                    kq=jnp.concatenate([jnp.concatenate(k_beta, axis=1), jnp.concatenate(q, axis=1)],
                                       axis=0).astype(BF16),
                    kk=jnp.concatenate([jnp.concatenate([k[0], zeros_head], axis=1),
                                        jnp.concatenate([zeros_head, k[1]], axis=1)], axis=0),
                    rhs=jnp.concatenate([jnp.concatenate([rhs[0], zeros_wide], axis=1),
                                         jnp.concatenate([zeros_wide, rhs[1]], axis=1)], axis=0),
                    k_dec_t=jnp.concatenate(k_dec, axis=0).T.astype(BF16)))

        for ch in chains:
            ch["kk_qk"] = _mm_nt(ch.pop("kq"), ch.pop("kk"))
        yield
        for ch in chains:
            kk_qk = ch.pop("kk_qk")
            decay = ch.pop("decay")
            ch["lmat"] = kk_qk[0:CHUNK] * decay * tri_strict
            ch["a_qk"] = (kk_qk[CHUNK:2 * CHUNK] * decay).astype(BF16)
            ch["x"] = eye - ch["lmat"] * level_masks[0]
        for level in range(1, N_LEVELS):
            for ch in chains:
                ch["y"] = _mm(ch["lmat"] * level_masks[level], blockdiag(ch["x"]))
            yield
            for ch in chains:
                ch["x"] = ch["x"] - _mm(ch["x"], blockdiag(ch.pop("y")))
            yield
        for ch in chains:
            ch["uw"] = _mm(ch.pop("x"), ch.pop("rhs"))
        for ch in chains:
            slot, p = ch["slot"], ch["p"]
            uw = ch.pop("uw")
            for i, h in enumerate((2 * p, 2 * p + 1)):
                base = i * (DN_DV + DN_DK)
                u_s[slot, h] = uw[:, base:base + DN_DV]
                wq_s[slot, h, 0:CHUNK, :] = uw[:, base + DN_DV:base + DN_DV + DN_DK].astype(BF16)
            akd_s[slot, p] = jnp.concatenate([ch["a_qk"], ch["k_dec_t"]], axis=0)
        yield

    def rec_stages(gi):
        for ci in range(group):
            c = gi * group + ci
            r0 = c * CHUNK if isinstance(c, int) else pl.multiple_of(c * CHUNK, CHUNK)
            slot = (gi % 2) * group + ci
            e_last_all = el_s[slot]
            rs = [_mm(wq_s[slot, h], state[h]) for h in range(DN_HEADS)]
            yield
            v_new = [(u_s[slot, h] - rs[h][0:CHUNK]).astype(BF16) for h in range(DN_HEADS)]
            zeros_v = jnp.zeros((CHUNK, DN_DV), BF16)
            av = [_mm(akd_s[slot, p],
                      jnp.concatenate([jnp.concatenate([v_new[2 * p], zeros_v], axis=1),
                                       jnp.concatenate([zeros_v, v_new[2 * p + 1]], axis=1)], axis=0))
                  for p in range(n_pairs)]
            for h in range(DN_HEADS):
                gl = DN_HEADS + h
                lanes_h = slice((h % 2) * DN_DV, (h % 2 + 1) * DN_DV)
                o = rs[h][CHUNK:2 * CHUNK] + av[h // 2][0:CHUNK, lanes_h]
                state[h] = state[h] * e_last_all[0:1, gl:gl + 1] + av[h // 2][CHUNK:CHUNK + DN_DK, lanes_h]
                o_ref[0, pl.ds(r0, CHUNK), slice(h * DN_DV, (h + 1) * DN_DV)] = o.astype(o_ref.dtype)
            yield

    run = _run_round_robin

    n_groups = n_chunks // group
    run(prep_stages(0))

    def both(gi, carry):
        run(prep_stages(gi), rec_stages(gi - 1))
        return carry

    lax.fori_loop(1, n_groups, both, 0)
    run(rec_stages(n_groups - 1))


def _deltanet(p, small, alog_row, dtb_row, ts, group):
    b, s, _ = p.shape
    width = DN_HEADS * DN_DK
    n_slots = 2 * group
    assert (ts // CHUNK) % group == 0
    kern = functools.partial(_deltanet_kernel, ts=ts, group=group)
    return pl.pallas_call(
        kern,
        grid=(b, s // ts),
        in_specs=[pl.BlockSpec((1, ts, width), lambda bi, t: (bi, t, 0)),
                  pl.BlockSpec((1, ts, width), lambda bi, t: (bi, t, 1)),
                  pl.BlockSpec((1, ts, width), lambda bi, t: (bi, t, 2)),
                  pl.BlockSpec((1, ts, LANES), lambda bi, t: (bi, t, 0)),
                  pl.BlockSpec((1, LANES), lambda bi, t: (0, 0)),
                  pl.BlockSpec((1, LANES), lambda bi, t: (0, 0))],
        out_specs=pl.BlockSpec((1, ts, DN_WIDTH), lambda bi, t: (bi, t, 0)),
        out_shape=jax.ShapeDtypeStruct((b, s, DN_WIDTH), BF16),
        scratch_shapes=[pltpu.VMEM((DN_HEADS, DN_DK, DN_DV), F32),
                        pltpu.VMEM((n_slots, DN_HEADS, CHUNK, DN_DV), F32),
                        pltpu.VMEM((n_slots, DN_HEADS, 2 * CHUNK, DN_DK), BF16),
                        pltpu.VMEM((n_slots, DN_HEADS // 2, CHUNK + DN_DK, 2 * CHUNK), BF16),
                        pltpu.VMEM((n_slots, 8, LANES), F32)],
        compiler_params=pltpu.CompilerParams(
            dimension_semantics=("parallel", "arbitrary"), vmem_limit_bytes=VMEM_LIMIT),
        name="gated_deltanet",
    )(p, p, p, small, alog_row, dtb_row)


ATTN_BLOCKS_PER_ITER = 4
ATTN_STRIDE = 4


def _attn_kernel(q_ref, k_ref, v_ref, o_ref, lse_ref, *scratch, dil, n_back):
    s = q_ref.shape[1]
    sub_len = s // dil
    nb = sub_len // ATTN_BLOCK
    n_pairs = ATTN_GROUP_WIDTH // LANES
    prev_block = nb > 1

    def lanes(pair):
        return slice(pair * LANES, (pair + 1) * LANES)

    if dil > 1:
        stage, tmp, qs, ks, vs, os_, ls = scratch
        assert dil in (ATTN_STRIDE, ATTN_STRIDE * ATTN_STRIDE)
        q_len = s // ATTN_STRIDE

        def classes():
            if dil == ATTN_STRIDE:
                return [(r, stage, r) for r in range(dil)]
            return [(r_lo + ATTN_STRIDE * r_hi, tmp, r_lo * q_len + r_hi)
                    for r_lo in range(ATTN_STRIDE) for r_hi in range(ATTN_STRIDE)]

        def deinterleave(dst, pair):
            if dil > ATTN_STRIDE:
                for r_lo in range(ATTN_STRIDE):
                    tmp[r_lo * q_len:(r_lo + 1) * q_len, :] = stage[pl.ds(r_lo, q_len, stride=ATTN_STRIDE), :]
            for r, buf, start in classes():
                dst[pair, r * sub_len:(r + 1) * sub_len, :] = (
                    buf[pl.ds(start, sub_len, stride=ATTN_STRIDE), :].astype(dst.dtype))

        def interleave(src, pair):
            for r, buf, start in classes():
                buf[pl.ds(start, sub_len, stride=ATTN_STRIDE), :] = src[pair, r * sub_len:(r + 1) * sub_len, :]
            if dil > ATTN_STRIDE:
                for r_lo in range(ATTN_STRIDE):
                    stage[pl.ds(r_lo, q_len, stride=ATTN_STRIDE), :] = tmp[r_lo * q_len:(r_lo + 1) * q_len, :]

        for src, dst in ((q_ref, qs), (k_ref, ks), (v_ref, vs)):
            for pair in range(n_pairs):
                stage[...] = src[0, :, lanes(pair)].astype(F32)
                deinterleave(dst, pair)

        def load(ref, src, pair, r0):
            return src[pair, pl.ds(r0, ATTN_BLOCK), :]
    else:
        qs, ks, vs = q_ref, k_ref, v_ref

        def load(ref, src, pair, r0):
            return ref[0, pl.ds(r0, ATTN_BLOCK), lanes(pair)]

    n_keys = 2 * ATTN_BLOCK if prev_block else ATTN_BLOCK
    qi = lax.broadcasted_iota(jnp.int32, (ATTN_BLOCK, n_keys), 0)
    ki = lax.broadcasted_iota(jnp.int32, (ATTN_BLOCK, n_keys), 1)
    dist = qi + (n_keys - ATTN_BLOCK) - ki
    band = (dist >= 0) & (dist <= n_back)
    bias_full = jnp.where(band, 0.0, -jnp.inf).astype(F32)
    bias_first = jnp.where(band & (ki >= ATTN_BLOCK), 0.0, -jnp.inf).astype(F32)
    lane = lax.broadcasted_iota(jnp.int32, (ATTN_BLOCK, LANES), 1)
    low_half = lane < ATTN_HEAD_DIM
    scale = ATTN_HEAD_DIM ** -0.5

    def block_body(fi, carry):
        chains = []
        for bi in range(ATTN_BLOCKS_PER_ITER):
            f = fi * ATTN_BLOCKS_PER_ITER + bi
            q0 = pl.multiple_of(f * ATTN_BLOCK, ATTN_BLOCK)
            if prev_block:
                k0 = pl.multiple_of(jnp.maximum(f - 1, 0) * ATTN_BLOCK, ATTN_BLOCK)
                bias = jnp.where(f % nb > 0, bias_full, bias_first)
            else:
                bias = bias_full
            for pair in range(n_pairs):
                qp = load(q_ref, qs, pair, q0) * scale
                kp = load(k_ref, ks, pair, q0)
                vp = load(v_ref, vs, pair, q0)
                if prev_block:
                    kp = jnp.concatenate([load(k_ref, ks, pair, k0), kp], axis=0)
                    vp = jnp.concatenate([load(v_ref, vs, pair, k0), vp], axis=0)
                for half in range(2):
                    sel = low_half if half == 0 else ~low_half
                    chains.append(dict(q0=q0, pair=pair, half=half, bias=bias, kp=kp, vp=vp,
                                       qm=jnp.where(sel, qp, jnp.zeros_like(qp))))
        for ch in chains:
            ch["sc"] = _mm_nt(ch.pop("qm"), ch.pop("kp"))
        for ch in chains:
            sc = ch.pop("sc") + ch.pop("bias")
            m = jnp.max(sc, axis=-1, keepdims=True)
            p = jnp.exp(sc - m)
            ch["denom"] = jnp.sum(p, axis=-1, keepdims=True)
            ch["m"] = m
            ch["p"] = p.astype(BF16)
        for ch in chains:
            ch["pv"] = _mm(ch.pop("p"), ch.pop("vp"))
        for c0 in range(0, len(chains), 2):
            lo, hi = chains[c0], chains[c0 + 1]
            o = jnp.where(low_half, lo["pv"] / lo["denom"], hi["pv"] / hi["denom"])
            lse = jnp.where(low_half, lo["m"] + jnp.log(lo["denom"]), hi["m"] + jnp.log(hi["denom"]))
            q0, pair = lo["q0"], lo["pair"]
            if dil > 1:
                os_[pair, pl.ds(q0, ATTN_BLOCK), :] = o
                ls[pair, pl.ds(q0, ATTN_BLOCK), :] = lse
            else:
                o_ref[0, pl.ds(q0, ATTN_BLOCK), lanes(pair)] = o.astype(o_ref.dtype)
                lse_ref[0, pl.ds(q0, ATTN_BLOCK), lanes(pair)] = lse
        return carry

    lax.fori_loop(0, dil * nb // ATTN_BLOCKS_PER_ITER, block_body, 0)

    if dil > 1:
        for pair in range(n_pairs):
            for src, dst in ((os_, o_ref), (ls, lse_ref)):
                interleave(src, pair)
                dst[0, :, lanes(pair)] = stage[...].astype(dst.dtype)


def _dilated_attention(p, group, col_base):
    b, s, _ = p.shape
    window, dil = DIL_CONFIGS[group]
    gw = ATTN_GROUP_WIDTH
    blk = col_base // gw + group
    step = ATTN_WIDTH // gw
    kern = functools.partial(_attn_kernel, dil=dil, n_back=window // dil)
    return pl.pallas_call(
        kern,
        grid=(b,),
        in_specs=[pl.BlockSpec((1, s, gw), lambda bi: (bi, 0, blk)),
                  pl.BlockSpec((1, s, gw), lambda bi: (bi, 0, blk + step)),
                  pl.BlockSpec((1, s, gw), lambda bi: (bi, 0, blk + 2 * step))],
        out_specs=[pl.BlockSpec((1, s, gw), lambda bi: (bi, 0, 0)),
                   pl.BlockSpec((1, s, gw), lambda bi: (bi, 0, 0))],
        out_shape=[jax.ShapeDtypeStruct((b, s, gw), BF16),
                   jax.ShapeDtypeStruct((b, s, gw), F32)],
        scratch_shapes=([] if dil == 1 else
                        [pltpu.VMEM((s, LANES), F32) for _ in range(2)]
                        + [pltpu.VMEM((gw // LANES, s, LANES), BF16) for _ in range(3)]
                        + [pltpu.VMEM((gw // LANES, s, LANES), F32) for _ in range(2)]),
        compiler_params=pltpu.CompilerParams(
            dimension_semantics=("parallel",), vmem_limit_bytes=VMEM_LIMIT),
        name=f"dilated_attn_g{group}",
    )(p, p, p)


MERGE_SUBTILES = 8


def _merge_kernel(x_ref, mod_ref, oa_ref, za_ref, ga_ref, gb_ref, zb_ref,
                  ob0_ref, ob1_ref, ob2_ref, l0_ref, l1_ref, l2_ref,
                  dnw_ref, pa_ref, pb_ref, wo_ref, fw_ref, out_ref):
    def silu_of_half(h):
        return h + h * jnp.tanh(h)

    gate = mod_ref[0][2:3]

    def stages(rows):
        oa = oa_ref[0, rows, :].astype(F32)
        za_half = za_ref[0, rows, :].astype(F32)
        parts = []
        for h in range(DN_HEADS):
            cq = slice(h * DN_DV, (h + 1) * DN_DV)
            blk = oa[:, cq]
            y = blk * lax.rsqrt(jnp.mean(blk * blk, axis=-1, keepdims=True) + EPS) * dnw_ref[...]
            parts.append(y * silu_of_half(za_half[:, cq]))
        ya_half = _mm(jnp.concatenate(parts, axis=1), pa_ref[...])
        yield

        l0, l1, l2 = l0_ref[0, rows, :], l1_ref[0, rows, :], l2_ref[0, rows, :]
        m = jnp.maximum(jnp.maximum(l0, l1), l2)
        e0, e1, e2 = jnp.exp(l0 - m), jnp.exp(l1 - m), jnp.exp(l2 - m)
        inv_den = 1.0 / (e0 + e1 + e2)
        ob = jnp.concatenate([ob0_ref[0, rows, :].astype(F32) * (e0 * inv_den),
                              ob1_ref[0, rows, :].astype(F32) * (e1 * inv_den),
                              ob2_ref[0, rows, :].astype(F32) * (e2 * inv_den)], axis=1)
        yb_half = _mm(ob * silu_of_half(zb_ref[0, rows, :].astype(F32)), pb_ref[...])
        yield

        merged = (ya_half + ya_half * jnp.tanh(ga_ref[0, rows, :].astype(F32))
                  + (yb_half + yb_half * jnp.tanh(gb_ref[0, rows, :].astype(F32))))
        delta = _mm(merged, wo_ref[...])
        yield

        xo = x_ref[0, rows, :] + gate * delta
        out_ref[0, rows, :] = xo * lax.rsqrt(jnp.mean(xo * xo, axis=-1, keepdims=True) + EPS) * fw_ref[...]
        yield

    tm = x_ref.shape[1]
    sub = tm // MERGE_SUBTILES
    _run_round_robin(*[stages(slice(i * sub, (i + 1) * sub)) for i in range(MERGE_SUBTILES)])


def _merge(x, mod3, o_a, p, obs, lses, dn_norm_w, pa, pb, wo, final_w, tm, cols):
    b, s, d = x.shape
    gw = ATTN_GROUP_WIDTH

    def row_spec(width, blk):
        return pl.BlockSpec((1, tm, width), lambda bi, i: (bi, i, blk))

    def full_spec(shape):
        return pl.BlockSpec(shape, lambda bi, i: (0,) * len(shape), pipeline_mode=pl.Buffered(1))

    return pl.pallas_call(
        _merge_kernel,
        grid=(b, s // tm),
        in_specs=[row_spec(d, 0),
                  pl.BlockSpec((1, 3, d), lambda bi, i: (bi, 0, 0)),
                  row_spec(DN_WIDTH, 0),
                  row_spec(DN_WIDTH, cols["za"] // DN_WIDTH),
                  row_spec(d, cols["ga"] // d),
                  row_spec(d, cols["gb"] // d),
                  row_spec(ATTN_WIDTH, cols["zb"] // ATTN_WIDTH),
                  row_spec(gw, 0), row_spec(gw, 0), row_spec(gw, 0),
                  row_spec(gw, 0), row_spec(gw, 0), row_spec(gw, 0),
                  full_spec((1, DN_DV)), full_spec(pa.shape), full_spec(pb.shape), full_spec(wo.shape),
                  full_spec((1, d))],
        out_specs=row_spec(d, 0),
        out_shape=jax.ShapeDtypeStruct((b, s, d), F32),
        compiler_params=pltpu.CompilerParams(
            dimension_semantics=("parallel", "parallel"), vmem_limit_bytes=VMEM_LIMIT),
        name="merge_outproj",
    )(x, mod3, o_a, p, p, p, p, *obs, *lses, dn_norm_w.reshape(1, DN_DV), pa, pb, wo, final_w.reshape(1, d))


W_PREP_COLS = 128


def _wprep_kernel(wt_ref, wc_ref, wp_ref, ws_ref, *, conv_width, segments, small_off):
    wc_ref[...] = wt_ref[0:conv_width, :].astype(BF16)
    for src, width, dst, scale in segments:
        blk = wt_ref[src:src + width, :]
        wp_ref[dst:dst + width, :] = (blk if scale == 1.0 else blk * scale).astype(BF16)
    n_small = 2 * DN_HEADS
    ws_ref[0:n_small, :] = wt_ref[small_off:small_off + n_small, :].astype(BF16)
    ws_ref[n_small:LANES, :] = jnp.zeros((LANES - n_small, wt_ref.shape[1]), BF16)


def _weight_prep(wt, conv_width, plain_width, segments, small_off):
    n_in, d = wt.shape
    assert d % W_PREP_COLS == 0 and small_off % 16 == 0 and all(sg[0] % 16 == 0 for sg in segments)
    kern = functools.partial(_wprep_kernel, conv_width=conv_width, segments=segments, small_off=small_off)
    return pl.pallas_call(
        kern,
        grid=(d // W_PREP_COLS,),
        in_specs=[pl.BlockSpec((n_in, W_PREP_COLS), lambda i: (0, i))],
        out_specs=[pl.BlockSpec((conv_width, W_PREP_COLS), lambda i: (0, i)),
                   pl.BlockSpec((plain_width, W_PREP_COLS), lambda i: (0, i)),
                   pl.BlockSpec((LANES, W_PREP_COLS), lambda i: (0, i))],
        out_shape=[jax.ShapeDtypeStruct((conv_width, d), BF16),
                   jax.ShapeDtypeStruct((plain_width, d), BF16),
                   jax.ShapeDtypeStruct((LANES, d), BF16)],
        compiler_params=pltpu.CompilerParams(dimension_semantics=("parallel",), vmem_limit_bytes=VMEM_LIMIT),
        name="weight_prep",
    )(wt)


def _layer(x, c, norm_w, ada_w, ada_b, w_in, conv_w, a_log, dt_bias, dn_norm_w, w_proj_a, w_proj_b, w_out,
           final_norm_w):
    b, s, d = x.shape
    qk = DN_HEADS * DN_DK
    splits = (qk, qk, DN_WIDTH, DN_WIDTH, DN_HEADS, DN_HEADS, ATTN_WIDTH, ATTN_WIDTH, ATTN_WIDTH, ATTN_WIDTH, d, d)
    names = ("qa", "ka", "va", "za", "beta", "a", "qb", "kb", "vb", "zb", "ga", "gb")
    offs = dict(zip(names, np.cumsum((0,) + splits[:-1]).tolist()))
    widths = dict(zip(names, splits))
    order = ("za", "ga", "gb", "qb", "kb", "vb", "zb")
    cols, pos = {}, 0
    for n in order:
        assert pos % widths[n] == 0, (n, pos)
        cols[n] = pos
        pos += widths[n]
    assert offs["qa"] == 0 and offs["ka"] == qk and offs["va"] == 2 * qk and offs["a"] == offs["beta"] + DN_HEADS
    halved = ("za", "ga", "gb", "zb")
    segments = tuple((offs[n], widths[n], cols[n], 0.5 if n in halved else 1.0) for n in order)
    w_conv, w_plain, w_small = _weight_prep(w_in.T, 2 * qk + DN_WIDTH, pos, segments, offs["beta"])
    pad_lo = jnp.zeros((DN_HEADS,), F32)
    pad_hi = jnp.zeros((LANES - 2 * DN_HEADS,), F32)
    alog_row = jnp.concatenate([pad_lo, a_log.astype(F32), pad_hi]).reshape(1, LANES)
    dtb_row = jnp.concatenate([pad_lo, dt_bias.astype(F32), pad_hi]).reshape(1, LANES)

    mod3 = _adaln_mod(c, ada_w, ada_b).reshape(b, 3, d)
    p_conv, p, small = _inproj(x, mod3, norm_w, w_conv, w_plain, w_small, conv_w)
    o_a = _deltanet(p_conv, small, alog_row, dtb_row, ts=min(s, 2048), group=4)
    obs, lses = [], []
    for g in range(len(DIL_CONFIGS)):
        o_g, lse_g = _dilated_attention(p, g, cols["qb"])
        obs.append(o_g)
        lses.append(lse_g)
    return _merge(x, mod3, o_a, p, obs, lses, dn_norm_w, (w_proj_a * 0.5).astype(BF16),
                  (w_proj_b * 0.5).astype(BF16), w_out.astype(BF16), final_norm_w, tm=min(s, 1024), cols=cols)


def kernel(x, c, norm_w, ada_w, ada_b, w_in, conv_w, a_log, dt_bias, dn_norm_w, w_proj_a, w_proj_b, w_out,
           final_norm_w):
    depth = norm_w.shape[0]
    assert depth == 1, "the final RMSNorm is fused into the single layer's output kernel"
    return _layer(x, c, norm_w[0], ada_w[0], ada_b[0], w_in[0], conv_w[0], a_log[0], dt_bias[0], dn_norm_w[0],
                  w_proj_a[0], w_proj_b[0], w_out[0], final_norm_w)
```

```python
import functools

import jax
import jax.numpy as jnp
import numpy as np
from jax import lax
from jax.experimental import pallas as pl
from jax.experimental.pallas import tpu as pltpu

F32 = jnp.float32
BF16 = jnp.bfloat16
HIGHEST = lax.Precision.HIGHEST

EPS = 1e-6
DN_HEADS = 8
DN_DK = 128
DN_DV = 128
DN_WIDTH = DN_HEADS * DN_DV
CONV_K = 4
CHUNK = 64
DIL_CONFIGS = ((128, 1), (512, 4), (2048, 16))
HEADS_PER_GROUP = 4
ATTN_HEAD_DIM = 64
ATTN_GROUP_WIDTH = HEADS_PER_GROUP * ATTN_HEAD_DIM
ATTN_WIDTH = ATTN_GROUP_WIDTH * len(DIL_CONFIGS)
ATTN_BLOCK = 128
LANES = 128
HALO = 8
VMEM_LIMIT = 56 * 1024 * 1024


def _mm(a, b):
    return jnp.dot(a.astype(BF16), b.astype(BF16), preferred_element_type=F32)


def _mm_nt(a, b):
    return lax.dot_general(a.astype(BF16), b.astype(BF16), (((1,), (1,)), ((), ())),
                           preferred_element_type=F32)


def _mm_tn(a, b):
    return lax.dot_general(a.astype(BF16), b.astype(BF16), (((0,), (0,)), ((), ())),
                           preferred_element_type=F32)


def _run_round_robin(*gens):
    gens = list(gens)
    while gens:
        for g in list(gens):
            if next(g, StopIteration) is StopIteration:
                gens.remove(g)


def _sigmoid(x):
    return 0.5 * jnp.tanh(0.5 * x) + 0.5


def _silu(x):
    return x * _sigmoid(x)


def _mod_kernel(c_ref, w_ref, b_ref, o_ref):
    sc = _silu(c_ref[...])
    o_ref[...] = jnp.dot(sc, w_ref[...], preferred_element_type=F32, precision=HIGHEST) + b_ref[...]


def _adaln_mod(c, ada_w, ada_b):
    b, d = c.shape
    return pl.pallas_call(
        _mod_kernel,
        grid=(3,),
        in_specs=[pl.BlockSpec((b, d), lambda j: (0, 0)),
                  pl.BlockSpec((d, d), lambda j: (0, j)),
                  pl.BlockSpec((1, d), lambda j: (0, j))],
        out_specs=pl.BlockSpec((b, d), lambda j: (0, j)),
        out_shape=jax.ShapeDtypeStruct((b, 3 * d), F32),
        name="adaln_mod",
    )(c, ada_w, ada_b.reshape(1, 3 * d))


CONV_ROWS = 64
CONV_SUB = 256
SUBLANES = 8
TOKEN_CHAINS = 3


def _conv_unit(buf, r0, cl, half_taps, shift_masks, norm_scale, skip_norm):
    a = buf[r0:r0 + HALO + CONV_ROWS, cl]
    n = CONV_ROWS // SUBLANES
    vs = [a[SUBLANES * i:SUBLANES * (i + 1)] for i in range(n + 1)]
    acc = [vs[i + 1] * half_taps[CONV_K - 1] for i in range(n)]
    for shift in range(1, CONV_K):
        rs = [pltpu.roll(v, shift, 0) for v in vs]
        tap = half_taps[CONV_K - 1 - shift]
        for i in range(n):
            acc[i] = acc[i] + jnp.where(shift_masks[shift], rs[i], rs[i + 1]) * tap
    h = jnp.concatenate(acc, axis=0)
    y = h + h * jnp.tanh(h)
    ss = jnp.sum(y * y, axis=-1, keepdims=True)
    out = y * jnp.where(skip_norm, 1.0, lax.rsqrt(ss + EPS) * norm_scale)
    token = jnp.broadcast_to(ss[CONV_ROWS - SUBLANES:], (SUBLANES, LANES))
    return out.astype(BF16), token


def _inproj_kernel(x_ref, mod_ref, nw_ref, wc_ref, wp_ref, ws_ref, cw_ref, pc_ref, pp_ref, small_ref,
                   h_scr, cbuf, tok, *, steps_per_seg):
    j = pl.program_id(1)
    s = x_ref.shape[1]

    @pl.when(j == 0)
    def _():
        x = x_ref[0]
        m = mod_ref[0]
        y = x * lax.rsqrt(jnp.mean(x * x, axis=-1, keepdims=True) + EPS) * nw_ref[...]
        h = (y * (1.0 + m[1:2]) + m[0:1]).astype(BF16)
        h_scr[...] = h
        small_ref[0] = _mm_nt(h, ws_ref[...])
        cbuf[0:HALO, :] = jnp.zeros((HALO, CONV_SUB), F32)

    cbuf[HALO:HALO + s, :] = _mm_nt(h_scr[...], wc_ref[...])
    pp_ref[0] = _mm_nt(h_scr[...], wp_ref[...]).astype(BF16)

    sub = lax.broadcasted_iota(jnp.int32, (SUBLANES, LANES), 0)
    shift_masks = [sub < shift for shift in range(CONV_K)]
    q_scale = jnp.where(j < steps_per_seg, DN_DK ** -0.5, 1.0)
    is_v = j >= 2 * steps_per_seg
    never = j < 0
    for c in range(TOKEN_CHAINS):
        tok[c] = jnp.zeros((SUBLANES, LANES), F32)
    n_unit = 0
    for c0 in range(0, CONV_SUB, DN_DK):
        cl = slice(c0, c0 + DN_DK)
        half_taps = [cw_ref[t:t + 1, cl] * 0.5 for t in range(CONV_K)]
        for r0 in range(0, s, CONV_ROWS):
            slot = n_unit % TOKEN_CHAINS
            held = tok[slot][0:1]
            taps = [jnp.where(never, held, tap) for tap in half_taps]
            out, token = _conv_unit(cbuf, r0, cl, taps, shift_masks, q_scale, is_v)
            pc_ref[0, r0:r0 + CONV_ROWS, cl] = out
            tok[slot] = token
            n_unit += 1


def _inproj(x, mod3, norm_w, w_conv, w_plain, w_small, conv_w):
    b, s, d = x.shape
    conv_width, plain_width = w_conv.shape[0], w_plain.shape[0]
    n_steps = conv_width // CONV_SUB
    tp = plain_width // n_steps
    assert n_steps % 3 == 0 and n_steps * CONV_SUB == conv_width and conv_w.shape[1] == conv_width
    assert tp * n_steps == plain_width and tp % LANES == 0 and s % CONV_ROWS == 0
    kern = functools.partial(_inproj_kernel, steps_per_seg=n_steps // 3)
    return pl.pallas_call(
        kern,
        grid=(b, n_steps),
        in_specs=[pl.BlockSpec((1, s, d), lambda bi, j: (bi, 0, 0)),
                  pl.BlockSpec((1, 3, d), lambda bi, j: (bi, 0, 0)),
                  pl.BlockSpec((1, d), lambda bi, j: (0, 0)),
                  pl.BlockSpec((CONV_SUB, d), lambda bi, j: (j, 0)),
                  pl.BlockSpec((tp, d), lambda bi, j: (j, 0)),
                  pl.BlockSpec((LANES, d), lambda bi, j: (0, 0)),
                  pl.BlockSpec((CONV_K, CONV_SUB), lambda bi, j: (0, j))],
        out_specs=[pl.BlockSpec((1, s, CONV_SUB), lambda bi, j: (bi, 0, j)),
                   pl.BlockSpec((1, s, tp), lambda bi, j: (bi, 0, j)),
                   pl.BlockSpec((1, s, LANES), lambda bi, j: (bi, 0, 0))],
        out_shape=[jax.ShapeDtypeStruct((b, s, conv_width), BF16),
                   jax.ShapeDtypeStruct((b, s, plain_width), BF16),
                   jax.ShapeDtypeStruct((b, s, LANES), F32)],
        scratch_shapes=[pltpu.VMEM((s, d), BF16),
                        pltpu.VMEM((HALO + s, CONV_SUB), F32),
                        pltpu.VMEM((TOKEN_CHAINS, SUBLANES, LANES), F32)],
        compiler_params=pltpu.CompilerParams(
            dimension_semantics=("parallel", "arbitrary"), vmem_limit_bytes=VMEM_LIMIT),
        name="norm_inproj",
    )(x, mod3, norm_w.reshape(1, d), w_conv, w_plain, w_small, conv_w)


def _level_mask(level, i, j):
    same_block = (i >> (level + 1)) == (j >> (level + 1))
    return same_block & (((i >> level) & 1) == 1) & (((j >> level) & 1) == 0)


N_LEVELS = CHUNK.bit_length() - 1


def _deltanet_kernel(q_ref, k_ref, v_ref, small_ref, alog_ref, dtb_ref, o_ref,
                     state, u_s, wq_s, akd_s, el_s, *, ts, group):
    @pl.when(pl.program_id(1) == 0)
    def _():
        state[...] = jnp.zeros_like(state)

    row = lax.broadcasted_iota(jnp.int32, (CHUNK, 2 * CHUNK), 0)
    lane = lax.broadcasted_iota(jnp.int32, (CHUNK, 2 * CHUNK), 1)
    col = lane & (CHUNK - 1)
    in_a = lane < CHUNK
    causal = row >= col
    tri_strict = (row > col).astype(F32)
    eye = (row == col).astype(F32)
    row_c = lax.broadcasted_iota(jnp.int32, (CHUNK, CHUNK), 0)
    col_c = lax.broadcasted_iota(jnp.int32, (CHUNK, CHUNK), 1)
    tri = (row_c >= col_c).astype(F32)
    level_masks = [_level_mask(level, row, col).astype(F32) for level in range(N_LEVELS)]
    neg_a = -jnp.exp(alog_ref[...])
    dtb = dtb_ref[...]
    n_chunks = ts // CHUNK
    n_pairs = DN_HEADS // 2
    zeros_wide = jnp.zeros((CHUNK, DN_DV + DN_DK), BF16)
    zeros_head = jnp.zeros((CHUNK, DN_DK), BF16)

    def blockdiag(m):
        return jnp.concatenate([jnp.where(in_a, m, 0.0), jnp.where(in_a, 0.0, m)], axis=0).astype(BF16)

    def prep_stages(gi):
        chains = []
        for ci in range(group):
            c = gi * group + ci
            r0 = c * CHUNK if isinstance(c, int) else pl.multiple_of(c * CHUNK, CHUNK)
            sm = small_ref[0, pl.ds(r0, CHUNK), :]
            beta_all = _sigmoid(sm)
            z = sm + dtb
            softplus = jnp.maximum(z, 0.0) + jnp.log(1.0 + jnp.exp(-jnp.abs(z)))
            g_all = neg_a * softplus
            gc_all = jnp.dot(tri, g_all, preferred_element_type=F32, precision=HIGHEST)
            gc_t = jnp.concatenate([gc_all, gc_all], axis=0).T
            g_last = gc_all[CHUNK - 1:CHUNK, :]
            e_in_all = jnp.exp(gc_all)
            e_out_all = jnp.exp(g_last - gc_all)
            slot = (gi % 2) * group + ci
            el_s[slot] = jnp.broadcast_to(jnp.exp(g_last), (8, LANES))
            for p in range(n_pairs):
                heads = (2 * p, 2 * p + 1)
                q, k, k_beta, rhs, k_dec = [], [], [], [], []
                for h in heads:
                    gl = DN_HEADS + h
                    cq = slice(h * DN_DK, (h + 1) * DN_DK)
                    q_h = q_ref[0, pl.ds(r0, CHUNK), cq].astype(F32)
                    k_h = k_ref[0, pl.ds(r0, CHUNK), cq].astype(F32)
                    v_h = v_ref[0, pl.ds(r0, CHUNK), slice(h * DN_DV, (h + 1) * DN_DV)].astype(F32)
                    beta = beta_all[:, h:h + 1]
                    e_in = e_in_all[:, gl:gl + 1]
                    kb_h = k_h * beta
                    q.append(q_h)
                    k.append(k_h.astype(BF16))
                    k_beta.append(kb_h)
                    rhs.append(jnp.concatenate([v_h * beta, kb_h * e_in], axis=1).astype(BF16))
                    k_dec.append(k_h * e_out_all[:, gl:gl + 1])
                    wq_s[slot, h, CHUNK:2 * CHUNK, :] = (q_h * e_in).astype(BF16)
                ga, gb = DN_HEADS + heads[0], DN_HEADS + heads[1]
                gc_col = jnp.where(in_a, gc_all[:, ga:ga + 1], gc_all[:, gb:gb + 1])
                gc_row = jnp.where(in_a[0:1], gc_t[ga:ga + 1, :], gc_t[gb:gb + 1, :])
                chains.append(dict(
                    slot=slot, p=p,
                    decay=jnp.exp(jnp.where(causal, gc_col - gc_row, -jnp.inf)),
                    kq=jnp.concatenate([jnp.concatenate(k_beta, axis=1), jnp.concatenate(q, axis=1)],
                                       axis=0).astype(BF16),
                    kk=jnp.concatenate([jnp.concatenate([k[0], zeros_head], axis=1),
                                        jnp.concatenate([zeros_head, k[1]], axis=1)], axis=0),
                    rhs=jnp.concatenate([jnp.concatenate([rhs[0], zeros_wide], axis=1),
                                         jnp.concatenate([zeros_wide, rhs[1]], axis=1)], axis=0),
                    k_dec_t=jnp.concatenate(k_dec, axis=0).T.astype(BF16)))

        for ch in chains:
            ch["kk_qk"] = _mm_nt(ch.pop("kq"), ch.pop("kk"))
        yield
        for ch in chains:
            kk_qk = ch.pop("kk_qk")
            decay = ch.pop("decay")
            ch["lmat"] = kk_qk[0:CHUNK] * decay * tri_strict
            ch["a_qk"] = (kk_qk[CHUNK:2 * CHUNK] * decay).astype(BF16)
            ch["x"] = eye - ch["lmat"] * level_masks[0]
        for level in range(1, N_LEVELS):
            for ch in chains:
                ch["y"] = _mm(ch["lmat"] * level_masks[level], blockdiag(ch["x"]))
            yield
            for ch in chains:
                ch["x"] = ch["x"] - _mm(ch["x"], blockdiag(ch.pop("y")))
            yield
        for ch in chains:
            ch["uw"] = _mm(ch.pop("x"), ch.pop("rhs"))
        for ch in chains:
            slot, p = ch["slot"], ch["p"]
            uw = ch.pop("uw")
            for i, h in enumerate((2 * p, 2 * p + 1)):
                base = i * (DN_DV + DN_DK)
                u_s[slot, h] = uw[:, base:base + DN_DV]
                wq_s[slot, h, 0:CHUNK, :] = uw[:, base + DN_DV:base + DN_DV + DN_DK].astype(BF16)
            akd_s[slot, p] = jnp.concatenate([ch["a_qk"], ch["k_dec_t"]], axis=0)
        yield

    def rec_stages(gi):
        for ci in range(group):
            c = gi * group + ci
            r0 = c * CHUNK if isinstance(c, int) else pl.multiple_of(c * CHUNK, CHUNK)
            slot = (gi % 2) * group + ci
            e_last_all = el_s[slot]
            rs = [_mm(wq_s[slot, h], state[h]) for h in range(DN_HEADS)]
            yield
            v_new = [(u_s[slot, h] - rs[h][0:CHUNK]).astype(BF16) for h in range(DN_HEADS)]
            zeros_v = jnp.zeros((CHUNK, DN_DV), BF16)
            av = [_mm(akd_s[slot, p],
                      jnp.concatenate([jnp.concatenate([v_new[2 * p], zeros_v], axis=1),
                                       jnp.concatenate([zeros_v, v_new[2 * p + 1]], axis=1)], axis=0))
                  for p in range(n_pairs)]
            for h in range(DN_HEADS):
                gl = DN_HEADS + h
                lanes_h = slice((h % 2) * DN_DV, (h % 2 + 1) * DN_DV)
                o = rs[h][CHUNK:2 * CHUNK] + av[h // 2][0:CHUNK, lanes_h]
                state[h] = state[h] * e_last_all[0:1, gl:gl + 1] + av[h // 2][CHUNK:CHUNK + DN_DK, lanes_h]
                o_ref[0, pl.ds(r0, CHUNK), slice(h * DN_DV, (h + 1) * DN_DV)] = o.astype(o_ref.dtype)
            yield

    run = _run_round_robin

    n_groups = n_chunks // group
    run(prep_stages(0))
    for gi in range(1, n_groups):
        run(prep_stages(gi), rec_stages(gi - 1))
    run(rec_stages(n_groups - 1))


def _deltanet(p, small, alog_row, dtb_row, ts, group):
    b, s, _ = p.shape
    width = DN_HEADS * DN_DK
    n_slots = 2 * group
    assert (ts // CHUNK) % group == 0
    kern = functools.partial(_deltanet_kernel, ts=ts, group=group)
    return pl.pallas_call(
        kern,
        grid=(b, s // ts),
        in_specs=[pl.BlockSpec((1, ts, width), lambda bi, t: (bi, t, 0)),
                  pl.BlockSpec((1, ts, width), lambda bi, t: (bi, t, 1)),
                  pl.BlockSpec((1, ts, width), lambda bi, t: (bi, t, 2)),
                  pl.BlockSpec((1, ts, LANES), lambda bi, t: (bi, t, 0)),
                  pl.BlockSpec((1, LANES), lambda bi, t: (0, 0)),
                  pl.BlockSpec((1, LANES), lambda bi, t: (0, 0))],
        out_specs=pl.BlockSpec((1, ts, DN_WIDTH), lambda bi, t: (bi, t, 0)),
        out_shape=jax.ShapeDtypeStruct((b, s, DN_WIDTH), BF16),
        scratch_shapes=[pltpu.VMEM((DN_HEADS, DN_DK, DN_DV), F32),
                        pltpu.VMEM((n_slots, DN_HEADS, CHUNK, DN_DV), F32),
                        pltpu.VMEM((n_slots, DN_HEADS, 2 * CHUNK, DN_DK), BF16),
                        pltpu.VMEM((n_slots, DN_HEADS // 2, CHUNK + DN_DK, 2 * CHUNK), BF16),
                        pltpu.VMEM((n_slots, 8, LANES), F32)],
        compiler_params=pltpu.CompilerParams(
            dimension_semantics=("parallel", "arbitrary"), vmem_limit_bytes=VMEM_LIMIT),
        name="gated_deltanet",
    )(p, p, p, small, alog_row, dtb_row)


ATTN_BLOCKS_PER_ITER = 4
ATTN_STRIDE = 4


def _attn_kernel(q_ref, k_ref, v_ref, o_ref, lse_ref, *scratch, dil, n_back):
    s = q_ref.shape[1]
    sub_len = s // dil
    nb = sub_len // ATTN_BLOCK
    n_pairs = ATTN_GROUP_WIDTH // LANES
    prev_block = nb > 1

    def lanes(pair):
        return slice(pair * LANES, (pair + 1) * LANES)

    if dil > 1:
        stage, tmp, qs, ks, vs, os_, ls = scratch
        assert dil in (ATTN_STRIDE, ATTN_STRIDE * ATTN_STRIDE)
        q_len = s // ATTN_STRIDE

        def classes():
            if dil == ATTN_STRIDE:
                return [(r, stage, r) for r in range(dil)]
            return [(r_lo + ATTN_STRIDE * r_hi, tmp, r_lo * q_len + r_hi)
                    for r_lo in range(ATTN_STRIDE) for r_hi in range(ATTN_STRIDE)]

        def deinterleave(dst, pair):
            if dil > ATTN_STRIDE:
                for r_lo in range(ATTN_STRIDE):
                    tmp[r_lo * q_len:(r_lo + 1) * q_len, :] = stage[pl.ds(r_lo, q_len, stride=ATTN_STRIDE), :]
            for r, buf, start in classes():
                dst[pair, r * sub_len:(r + 1) * sub_len, :] = (
                    buf[pl.ds(start, sub_len, stride=ATTN_STRIDE), :].astype(dst.dtype))

        def interleave(src, pair):
            for r, buf, start in classes():
                buf[pl.ds(start, sub_len, stride=ATTN_STRIDE), :] = src[pair, r * sub_len:(r + 1) * sub_len, :]
            if dil > ATTN_STRIDE:
                for r_lo in range(ATTN_STRIDE):
                    stage[pl.ds(r_lo, q_len, stride=ATTN_STRIDE), :] = tmp[r_lo * q_len:(r_lo + 1) * q_len, :]

        for src, dst in ((q_ref, qs), (k_ref, ks), (v_ref, vs)):
            for pair in range(n_pairs):
                stage[...] = src[0, :, lanes(pair)].astype(F32)
                deinterleave(dst, pair)

        def load(ref, src, pair, r0):
            return src[pair, pl.ds(r0, ATTN_BLOCK), :]
    else:
        qs, ks, vs = q_ref, k_ref, v_ref

        def load(ref, src, pair, r0):
            return ref[0, pl.ds(r0, ATTN_BLOCK), lanes(pair)]

    n_keys = 2 * ATTN_BLOCK if prev_block else ATTN_BLOCK
    qi = lax.broadcasted_iota(jnp.int32, (ATTN_BLOCK, n_keys), 0)
    ki = lax.broadcasted_iota(jnp.int32, (ATTN_BLOCK, n_keys), 1)
    dist = qi + (n_keys - ATTN_BLOCK) - ki
    band = (dist >= 0) & (dist <= n_back)
    bias_full = jnp.where(band, 0.0, -jnp.inf).astype(F32)
    bias_first = jnp.where(band & (ki >= ATTN_BLOCK), 0.0, -jnp.inf).astype(F32)
    lane = lax.broadcasted_iota(jnp.int32, (ATTN_BLOCK, LANES), 1)
    low_half = lane < ATTN_HEAD_DIM
    scale = ATTN_HEAD_DIM ** -0.5

    def block_body(fi, carry):
        chains = []
        for bi in range(ATTN_BLOCKS_PER_ITER):
            f = fi * ATTN_BLOCKS_PER_ITER + bi
            q0 = pl.multiple_of(f * ATTN_BLOCK, ATTN_BLOCK)
            if prev_block:
                k0 = pl.multiple_of(jnp.maximum(f - 1, 0) * ATTN_BLOCK, ATTN_BLOCK)
                bias = jnp.where(f % nb > 0, bias_full, bias_first)
            else:
                bias = bias_full
            for pair in range(n_pairs):
                qp = load(q_ref, qs, pair, q0) * scale
                kp = load(k_ref, ks, pair, q0)
                vp = load(v_ref, vs, pair, q0)
                if prev_block:
                    kp = jnp.concatenate([load(k_ref, ks, pair, k0), kp], axis=0)
                    vp = jnp.concatenate([load(v_ref, vs, pair, k0), vp], axis=0)
                for half in range(2):
                    sel = low_half if half == 0 else ~low_half
                    chains.append(dict(q0=q0, pair=pair, half=half, bias=bias, kp=kp, vp=vp,
                                       qm=jnp.where(sel, qp, jnp.zeros_like(qp))))
        for ch in chains:
            ch["sc"] = _mm_nt(ch.pop("qm"), ch.pop("kp"))
        for ch in chains:
            sc = ch.pop("sc") + ch.pop("bias")
            m = jnp.max(sc, axis=-1, keepdims=True)
            p = jnp.exp(sc - m)
            ch["denom"] = jnp.sum(p, axis=-1, keepdims=True)
            ch["m"] = m
            ch["p"] = p.astype(BF16)
        for ch in chains:
            ch["pv"] = _mm(ch.pop("p"), ch.pop("vp"))
        for c0 in range(0, len(chains), 2):
            lo, hi = chains[c0], chains[c0 + 1]
            o = jnp.where(low_half, lo["pv"] / lo["denom"], hi["pv"] / hi["denom"])
            lse = jnp.where(low_half, lo["m"] + jnp.log(lo["denom"]), hi["m"] + jnp.log(hi["denom"]))
            q0, pair = lo["q0"], lo["pair"]
            if dil > 1:
                os_[pair, pl.ds(q0, ATTN_BLOCK), :] = o
                ls[pair, pl.ds(q0, ATTN_BLOCK), :] = lse
            else:
                o_ref[0, pl.ds(q0, ATTN_BLOCK), lanes(pair)] = o.astype(o_ref.dtype)
                lse_ref[0, pl.ds(q0, ATTN_BLOCK), lanes(pair)] = lse
        return carry

    lax.fori_loop(0, dil * nb // ATTN_BLOCKS_PER_ITER, block_body, 0)

    if dil > 1:
        for pair in range(n_pairs):
            for src, dst in ((os_, o_ref), (ls, lse_ref)):
                interleave(src, pair)
                dst[0, :, lanes(pair)] = stage[...].astype(dst.dtype)


def _dilated_attention(p, group, col_base):
    b, s, _ = p.shape
    window, dil = DIL_CONFIGS[group]
    gw = ATTN_GROUP_WIDTH
    blk = col_base // gw + group
    step = ATTN_WIDTH // gw
    kern = functools.partial(_attn_kernel, dil=dil, n_back=window // dil)
    return pl.pallas_call(
        kern,
        grid=(b,),
        in_specs=[pl.BlockSpec((1, s, gw), lambda bi: (bi, 0, blk)),
                  pl.BlockSpec((1, s, gw), lambda bi: (bi, 0, blk + step)),
                  pl.BlockSpec((1, s, gw), lambda bi: (bi, 0, blk + 2 * step))],
        out_specs=[pl.BlockSpec((1, s, gw), lambda bi: (bi, 0, 0)),
                   pl.BlockSpec((1, s, gw), lambda bi: (bi, 0, 0))],
        out_shape=[jax.ShapeDtypeStruct((b, s, gw), BF16),
                   jax.ShapeDtypeStruct((b, s, gw), F32)],
        scratch_shapes=([] if dil == 1 else
                        [pltpu.VMEM((s, LANES), F32) for _ in range(2)]
                        + [pltpu.VMEM((gw // LANES, s, LANES), BF16) for _ in range(3)]
                        + [pltpu.VMEM((gw // LANES, s, LANES), F32) for _ in range(2)]),
        compiler_params=pltpu.CompilerParams(
            dimension_semantics=("parallel",), vmem_limit_bytes=VMEM_LIMIT),
        name=f"dilated_attn_g{group}",
    )(p, p, p)


MERGE_SUBTILES = 8


def _merge_kernel(x_ref, mod_ref, oa_ref, za_ref, ga_ref, gb_ref, zb_ref,
                  ob0_ref, ob1_ref, ob2_ref, l0_ref, l1_ref, l2_ref,
                  dnw_ref, pa_ref, pb_ref, wo_ref, fw_ref, out_ref):
    def silu_of_half(h):
        return h + h * jnp.tanh(h)

    gate = mod_ref[0][2:3]

    def stages(rows):
        oa = oa_ref[0, rows, :].astype(F32)
        za_half = za_ref[0, rows, :].astype(F32)
        parts = []
        for h in range(DN_HEADS):
            cq = slice(h * DN_DV, (h + 1) * DN_DV)
            blk = oa[:, cq]
            y = blk * lax.rsqrt(jnp.mean(blk * blk, axis=-1, keepdims=True) + EPS) * dnw_ref[...]
            parts.append(y * silu_of_half(za_half[:, cq]))
        ya_half = _mm(jnp.concatenate(parts, axis=1), pa_ref[...])
        yield

        l0, l1, l2 = l0_ref[0, rows, :], l1_ref[0, rows, :], l2_ref[0, rows, :]
        m = jnp.maximum(jnp.maximum(l0, l1), l2)
        e0, e1, e2 = jnp.exp(l0 - m), jnp.exp(l1 - m), jnp.exp(l2 - m)
        inv_den = 1.0 / (e0 + e1 + e2)
        ob = jnp.concatenate([ob0_ref[0, rows, :].astype(F32) * (e0 * inv_den),
                              ob1_ref[0, rows, :].astype(F32) * (e1 * inv_den),
                              ob2_ref[0, rows, :].astype(F32) * (e2 * inv_den)], axis=1)
        yb_half = _mm(ob * silu_of_half(zb_ref[0, rows, :].astype(F32)), pb_ref[...])
        yield

        merged = (ya_half + ya_half * jnp.tanh(ga_ref[0, rows, :].astype(F32))
                  + (yb_half + yb_half * jnp.tanh(gb_ref[0, rows, :].astype(F32))))
        delta = _mm(merged, wo_ref[...])
        yield

        xo = x_ref[0, rows, :] + gate * delta
        out_ref[0, rows, :] = xo * lax.rsqrt(jnp.mean(xo * xo, axis=-1, keepdims=True) + EPS) * fw_ref[...]
        yield

    tm = x_ref.shape[1]
    sub = tm // MERGE_SUBTILES
    _run_round_robin(*[stages(slice(i * sub, (i + 1) * sub)) for i in range(MERGE_SUBTILES)])


def _merge(x, mod3, o_a, p, obs, lses, dn_norm_w, pa, pb, wo, final_w, tm, cols):
    b, s, d = x.shape
    gw = ATTN_GROUP_WIDTH

    def row_spec(width, blk):
        return pl.BlockSpec((1, tm, width), lambda bi, i: (bi, i, blk))

    def full_spec(shape):
        return pl.BlockSpec(shape, lambda bi, i: (0,) * len(shape), pipeline_mode=pl.Buffered(1))

    return pl.pallas_call(
        _merge_kernel,
        grid=(b, s // tm),
        in_specs=[row_spec(d, 0),
                  pl.BlockSpec((1, 3, d), lambda bi, i: (bi, 0, 0)),
                  row_spec(DN_WIDTH, 0),
                  row_spec(DN_WIDTH, cols["za"] // DN_WIDTH),
                  row_spec(d, cols["ga"] // d),
                  row_spec(d, cols["gb"] // d),
                  row_spec(ATTN_WIDTH, cols["zb"] // ATTN_WIDTH),
                  row_spec(gw, 0), row_spec(gw, 0), row_spec(gw, 0),
                  row_spec(gw, 0), row_spec(gw, 0), row_spec(gw, 0),
                  full_spec((1, DN_DV)), full_spec(pa.shape), full_spec(pb.shape), full_spec(wo.shape),
                  full_spec((1, d))],
        out_specs=row_spec(d, 0),
        out_shape=jax.ShapeDtypeStruct((b, s, d), F32),
        compiler_params=pltpu.CompilerParams(
            dimension_semantics=("parallel", "parallel"), vmem_limit_bytes=VMEM_LIMIT),
        name="merge_outproj",
    )(x, mod3, o_a, p, p, p, p, *obs, *lses, dn_norm_w.reshape(1, DN_DV), pa, pb, wo, final_w.reshape(1, d))


W_PREP_COLS = 128


def _wprep_kernel(wt_ref, wc_ref, wp_ref, ws_ref, *, conv_width, segments, small_off):
    wc_ref[...] = wt_ref[0:conv_width, :].astype(BF16)
    for src, width, dst, scale in segments:
        blk = wt_ref[src:src + width, :]
        wp_ref[dst:dst + width, :] = (blk if scale == 1.0 else blk * scale).astype(BF16)
    n_small = 2 * DN_HEADS
    ws_ref[0:n_small, :] = wt_ref[small_off:small_off + n_small, :].astype(BF16)
    ws_ref[n_small:LANES, :] = jnp.zeros((LANES - n_small, wt_ref.shape[1]), BF16)


def _weight_prep(wt, conv_width, plain_width, segments, small_off):
    n_in, d = wt.shape
    assert d % W_PREP_COLS == 0 and small_off % 16 == 0 and all(sg[0] % 16 == 0 for sg in segments)
    kern = functools.partial(_wprep_kernel, conv_width=conv_width, segments=segments, small_off=small_off)
    return pl.pallas_call(
        kern,
        grid=(d // W_PREP_COLS,),
        in_specs=[pl.BlockSpec((n_in, W_PREP_COLS), lambda i: (0, i))],
        out_specs=[pl.BlockSpec((conv_width, W_PREP_COLS), lambda i: (0, i)),
                   pl.BlockSpec((plain_width, W_PREP_COLS), lambda i: (0, i)),
                   pl.BlockSpec((LANES, W_PREP_COLS), lambda i: (0, i))],
        out_shape=[jax.ShapeDtypeStruct((conv_width, d), BF16),
                   jax.ShapeDtypeStruct((plain_width, d), BF16),
                   jax.ShapeDtypeStruct((LANES, d), BF16)],
        compiler_params=pltpu.CompilerParams(dimension_semantics=("parallel",), vmem_limit_bytes=VMEM_LIMIT),
        name="weight_prep",
    )(wt)


def _layer(x, c, norm_w, ada_w, ada_b, w_in, conv_w, a_log, dt_bias, dn_norm_w, w_proj_a, w_proj_b, w_out,
           final_norm_w):
    b, s, d = x.shape
    qk = DN_HEADS * DN_DK
    splits = (qk, qk, DN_WIDTH, DN_WIDTH, DN_HEADS, DN_HEADS, ATTN_WIDTH, ATTN_WIDTH, ATTN_WIDTH, ATTN_WIDTH, d, d)
    names = ("qa", "ka", "va", "za", "beta", "a", "qb", "kb", "vb", "zb", "ga", "gb")
    offs = dict(zip(names, np.cumsum((0,) + splits[:-1]).tolist()))
    widths = dict(zip(names, splits))
    order = ("za", "ga", "gb", "qb", "kb", "vb", "zb")
    cols, pos = {}, 0
    for n in order:
        assert pos % widths[n] == 0, (n, pos)
        cols[n] = pos
        pos += widths[n]
    assert offs["qa"] == 0 and offs["ka"] == qk and offs["va"] == 2 * qk and offs["a"] == offs["beta"] + DN_HEADS
    halved = ("za", "ga", "gb", "zb")
    segments = tuple((offs[n], widths[n], cols[n], 0.5 if n in halved else 1.0) for n in order)
    w_conv, w_plain, w_small = _weight_prep(w_in.T, 2 * qk + DN_WIDTH, pos, segments, offs["beta"])
    pad_lo = jnp.zeros((DN_HEADS,), F32)
    pad_hi = jnp.zeros((LANES - 2 * DN_HEADS,), F32)
    alog_row = jnp.concatenate([pad_lo, a_log.astype(F32), pad_hi]).reshape(1, LANES)
    dtb_row = jnp.concatenate([pad_lo, dt_bias.astype(F32), pad_hi]).reshape(1, LANES)

    mod3 = _adaln_mod(c, ada_w, ada_b).reshape(b, 3, d)
    p_conv, p, small = _inproj(x, mod3, norm_w, w_conv, w_plain, w_small, conv_w)
    o_a = _deltanet(p_conv, small, alog_row, dtb_row, ts=min(s, 1024), group=4)
    obs, lses = [], []
    for g in range(len(DIL_CONFIGS)):
        o_g, lse_g = _dilated_attention(p, g, cols["qb"])
        obs.append(o_g)
        lses.append(lse_g)
    return _merge(x, mod3, o_a, p, obs, lses, dn_norm_w, (w_proj_a * 0.5).astype(BF16),
                  (w_proj_b * 0.5).astype(BF16), w_out.astype(BF16), final_norm_w, tm=min(s, 1024), cols=cols)


def kernel(x, c, norm_w, ada_w, ada_b, w_in, conv_w, a_log, dt_bias, dn_norm_w, w_proj_a, w_proj_b, w_out,
           final_norm_w):
    depth = norm_w.shape[0]
    assert depth == 1, "the final RMSNorm is fused into the single layer's output kernel"
    return _layer(x, c, norm_w[0], ada_w[0], ada_b[0], w_in[0], conv_w[0], a_log[0], dt_bias[0], dn_norm_w[0],
                  w_proj_a[0], w_proj_b[0], w_out[0], final_norm_w)
```

```python
import functools

import jax
import jax.numpy as jnp
import numpy as np
from jax import lax
from jax.experimental import pallas as pl
from jax.experimental.pallas import tpu as pltpu

F32 = jnp.float32
BF16 = jnp.bfloat16
HIGHEST = lax.Precision.HIGHEST

EPS = 1e-6
DN_HEADS = 8
DN_DK = 128
DN_DV = 128
DN_WIDTH = DN_HEADS * DN_DV
CONV_K = 4
CHUNK = 64
DIL_CONFIGS = ((128, 1), (512, 4), (2048, 16))
HEADS_PER_GROUP = 4
ATTN_HEAD_DIM = 64
ATTN_GROUP_WIDTH = HEADS_PER_GROUP * ATTN_HEAD_DIM
ATTN_WIDTH = ATTN_GROUP_WIDTH * len(DIL_CONFIGS)
ATTN_BLOCK = 128
LANES = 128
HALO = 8
VMEM_LIMIT = 56 * 1024 * 1024


def _mm(a, b):
    return jnp.dot(a.astype(BF16), b.astype(BF16), preferred_element_type=F32)


def _mm_nt(a, b):
    return lax.dot_general(a.astype(BF16), b.astype(BF16), (((1,), (1,)), ((), ())),
                           preferred_element_type=F32)


def _mm_tn(a, b):
    return lax.dot_general(a.astype(BF16), b.astype(BF16), (((0,), (0,)), ((), ())),
                           preferred_element_type=F32)


def _run_round_robin(*gens):
    gens = list(gens)
    while gens:
        for g in list(gens):
            if next(g, StopIteration) is StopIteration:
                gens.remove(g)


def _sigmoid(x):
    return 0.5 * jnp.tanh(0.5 * x) + 0.5


def _silu(x):
    return x * _sigmoid(x)


def _mod_kernel(c_ref, w_ref, b_ref, o_ref):
    sc = _silu(c_ref[...])
    o_ref[...] = jnp.dot(sc, w_ref[...], preferred_element_type=F32, precision=HIGHEST) + b_ref[...]


def _adaln_mod(c, ada_w, ada_b):
    b, d = c.shape
    return pl.pallas_call(
        _mod_kernel,
        grid=(3,),
        in_specs=[pl.BlockSpec((b, d), lambda j: (0, 0)),
                  pl.BlockSpec((d, d), lambda j: (0, j)),
                  pl.BlockSpec((1, d), lambda j: (0, j))],
        out_specs=pl.BlockSpec((b, d), lambda j: (0, j)),
        out_shape=jax.ShapeDtypeStruct((b, 3 * d), F32),
        name="adaln_mod",
    )(c, ada_w, ada_b.reshape(1, 3 * d))


CONV_ROWS = 64
CONV_SUB = 256
SUBLANES = 8
TOKEN_CHAINS = 3


def _conv_unit(buf, r0, cl, half_taps, shift_masks, norm_scale, skip_norm):
    a = buf[r0:r0 + HALO + CONV_ROWS, cl]
    n = CONV_ROWS // SUBLANES
    vs = [a[SUBLANES * i:SUBLANES * (i + 1)] for i in range(n + 1)]
    acc = [vs[i + 1] * half_taps[CONV_K - 1] for i in range(n)]
    for shift in range(1, CONV_K):
        rs = [pltpu.roll(v, shift, 0) for v in vs]
        tap = half_taps[CONV_K - 1 - shift]
        for i in range(n):
            acc[i] = acc[i] + jnp.where(shift_masks[shift], rs[i], rs[i + 1]) * tap
    h = jnp.concatenate(acc, axis=0)
    y = h + h * jnp.tanh(h)
    ss = jnp.sum(y * y, axis=-1, keepdims=True)
    out = y * jnp.where(skip_norm, 1.0, lax.rsqrt(ss + EPS) * norm_scale)
    token = jnp.broadcast_to(ss[CONV_ROWS - SUBLANES:], (SUBLANES, LANES))
    return out.astype(BF16), token


def _inproj_kernel(x_ref, mod_ref, nw_ref, wc_ref, wp_ref, ws_ref, cw_ref, pc_ref, pp_ref, small_ref,
                   h_scr, cbuf, tok, *, steps_per_seg):
    j = pl.program_id(1)
    s = x_ref.shape[1]

    @pl.when(j == 0)
    def _():
        x = x_ref[0]
        m = mod_ref[0]
        y = x * lax.rsqrt(jnp.mean(x * x, axis=-1, keepdims=True) + EPS) * nw_ref[...]
        h = (y * (1.0 + m[1:2]) + m[0:1]).astype(BF16)
        h_scr[...] = h
        small_ref[0] = _mm_nt(h, ws_ref[...])
        cbuf[0:HALO, :] = jnp.zeros((HALO, CONV_SUB), F32)

    cbuf[HALO:HALO + s, :] = _mm_nt(h_scr[...], wc_ref[...])
    pp_ref[0] = _mm_nt(h_scr[...], wp_ref[...]).astype(BF16)

    sub = lax.broadcasted_iota(jnp.int32, (SUBLANES, LANES), 0)
    shift_masks = [sub < shift for shift in range(CONV_K)]
    q_scale = jnp.where(j < steps_per_seg, DN_DK ** -0.5, 1.0)
    is_v = j >= 2 * steps_per_seg
    never = j < 0
    for c in range(TOKEN_CHAINS):
        tok[c] = jnp.zeros((SUBLANES, LANES), F32)
    n_unit = 0
    for c0 in range(0, CONV_SUB, DN_DK):
        cl = slice(c0, c0 + DN_DK)
        half_taps = [cw_ref[t:t + 1, cl] * 0.5 for t in range(CONV_K)]
        for r0 in range(0, s, CONV_ROWS):
            slot = n_unit % TOKEN_CHAINS
            held = tok[slot][0:1]
            taps = [jnp.where(never, held, tap) for tap in half_taps]
            out, token = _conv_unit(cbuf, r0, cl, taps, shift_masks, q_scale, is_v)
            pc_ref[0, r0:r0 + CONV_ROWS, cl] = out
            tok[slot] = token
            n_unit += 1


def _inproj(x, mod3, norm_w, w_conv, w_plain, w_small, conv_w):
    b, s, d = x.shape
    conv_width, plain_width = w_conv.shape[0], w_plain.shape[0]
    n_steps = conv_width // CONV_SUB
    tp = plain_width // n_steps
    assert n_steps % 3 == 0 and n_steps * CONV_SUB == conv_width and conv_w.shape[1] == conv_width
    assert tp * n_steps == plain_width and tp % LANES == 0 and s % CONV_ROWS == 0
    kern = functools.partial(_inproj_kernel, steps_per_seg=n_steps // 3)
    return pl.pallas_call(
        kern,
        grid=(b, n_steps),
        in_specs=[pl.BlockSpec((1, s, d), lambda bi, j: (bi, 0, 0)),
                  pl.BlockSpec((1, 3, d), lambda bi, j: (bi, 0, 0)),
                  pl.BlockSpec((1, d), lambda bi, j: (0, 0)),
                  pl.BlockSpec((CONV_SUB, d), lambda bi, j: (j, 0)),
                  pl.BlockSpec((tp, d), lambda bi, j: (j, 0)),
                  pl.BlockSpec((LANES, d), lambda bi, j: (0, 0)),
                  pl.BlockSpec((CONV_K, CONV_SUB), lambda bi, j: (0, j))],
        out_specs=[pl.BlockSpec((1, s, CONV_SUB), lambda bi, j: (bi, 0, j)),
                   pl.BlockSpec((1, s, tp), lambda bi, j: (bi, 0, j)),
                   pl.BlockSpec((1, s, LANES), lambda bi, j: (bi, 0, 0))],
        out_shape=[jax.ShapeDtypeStruct((b, s, conv_width), BF16),
                   jax.ShapeDtypeStruct((b, s, plain_width), BF16),
                   jax.ShapeDtypeStruct((b, s, LANES), F32)],
        scratch_shapes=[pltpu.VMEM((s, d), BF16),
                        pltpu.VMEM((HALO + s, CONV_SUB), F32),
                        pltpu.VMEM((TOKEN_CHAINS, SUBLANES, LANES), F32)],
        compiler_params=pltpu.CompilerParams(
            dimension_semantics=("parallel", "arbitrary"), vmem_limit_bytes=VMEM_LIMIT),
        name="norm_inproj",
    )(x, mod3, norm_w.reshape(1, d), w_conv, w_plain, w_small, conv_w)


def _level_mask(level, i, j):
    same_block = (i >> (level + 1)) == (j >> (level + 1))
    return same_block & (((i >> level) & 1) == 1) & (((j >> level) & 1) == 0)


N_LEVELS = CHUNK.bit_length() - 1


def _deltanet_kernel(q_ref, k_ref, v_ref, small_ref, alog_ref, dtb_ref, o_ref,
                     state, u_s, wq_s, akd_s, el_s, *, ts, group):
    @pl.when(pl.program_id(1) == 0)
    def _():
        state[...] = jnp.zeros_like(state)

    row = lax.broadcasted_iota(jnp.int32, (CHUNK, 2 * CHUNK), 0)
    lane = lax.broadcasted_iota(jnp.int32, (CHUNK, 2 * CHUNK), 1)
    col = lane & (CHUNK - 1)
    in_a = lane < CHUNK
    causal = row >= col
    tri_strict = (row > col).astype(F32)
    eye = (row == col).astype(F32)
    row_c = lax.broadcasted_iota(jnp.int32, (CHUNK, CHUNK), 0)
    col_c = lax.broadcasted_iota(jnp.int32, (CHUNK, CHUNK), 1)
    tri = (row_c >= col_c).astype(F32)
    level_masks = [_level_mask(level, row, col).astype(F32) for level in range(N_LEVELS)]
    neg_a = -jnp.exp(alog_ref[...])
    dtb = dtb_ref[...]
    n_chunks = ts // CHUNK
    n_pairs = DN_HEADS // 2
    zeros_wide = jnp.zeros((CHUNK, DN_DV + DN_DK), BF16)
    zeros_head = jnp.zeros((CHUNK, DN_DK), BF16)

    def blockdiag(m):
        return jnp.concatenate([jnp.where(in_a, m, 0.0), jnp.where(in_a, 0.0, m)], axis=0).astype(BF16)

    def prep_stages(gi):
        chains = []
        for ci in range(group):
            c = gi * group + ci
            r0 = c * CHUNK if isinstance(c, int) else pl.multiple_of(c * CHUNK, CHUNK)
            sm = small_ref[0, pl.ds(r0, CHUNK), :]
            beta_all = _sigmoid(sm)
            z = sm + dtb
            softplus = jnp.maximum(z, 0.0) + jnp.log(1.0 + jnp.exp(-jnp.abs(z)))
            g_all = neg_a * softplus
            gc_all = jnp.dot(tri, g_all, preferred_element_type=F32, precision=HIGHEST)
            gc_t = jnp.concatenate([gc_all, gc_all], axis=0).T
            g_last = gc_all[CHUNK - 1:CHUNK, :]
            e_in_all = jnp.exp(gc_all)
            e_out_all = jnp.exp(g_last - gc_all)
            slot = (gi % 2) * group + ci
            el_s[slot] = jnp.broadcast_to(jnp.exp(g_last), (8, LANES))
            for p in range(n_pairs):
                heads = (2 * p, 2 * p + 1)
                q, k, k_beta, rhs, k_dec = [], [], [], [], []
                for h in heads:
                    gl = DN_HEADS + h
                    cq = slice(h * DN_DK, (h + 1) * DN_DK)
                    q_h = q_ref[0, pl.ds(r0, CHUNK), cq].astype(F32)
                    k_h = k_ref[0, pl.ds(r0, CHUNK), cq].astype(F32)
                    v_h = v_ref[0, pl.ds(r0, CHUNK), slice(h * DN_DV, (h + 1) * DN_DV)].astype(F32)
                    beta = beta_all[:, h:h + 1]
                    e_in = e_in_all[:, gl:gl + 1]
                    kb_h = k_h * beta
                    q.append(q_h)
                    k.append(k_h.astype(BF16))
                    k_beta.append(kb_h)
                    rhs.append(jnp.concatenate([v_h * beta, kb_h * e_in], axis=1).astype(BF16))
                    k_dec.append(k_h * e_out_all[:, gl:gl + 1])
                    wq_s[slot, h, CHUNK:2 * CHUNK, :] = (q_h * e_in).astype(BF16)
                ga, gb = DN_HEADS + heads[0], DN_HEADS + heads[1]
                gc_col = jnp.where(in_a, gc_all[:, ga:ga + 1], gc_all[:, gb:gb + 1])
                gc_row = jnp.where(in_a[0:1], gc_t[ga:ga + 1, :], gc_t[gb:gb + 1, :])
                chains.append(dict(
                    slot=slot, p=p,
                    decay=jnp.exp(jnp.where(causal, gc_col - gc_row, -jnp.inf)),
                    kq=jnp.concatenate([jnp.concatenate(k_beta, axis=1), jnp.concatenate(q, axis=1)],
                                       axis=0).astype(BF16),
                    kk=jnp.concatenate([jnp.concatenate([k[0], zeros_head], axis=1),
                                        jnp.concatenate([zeros_head, k[1]], axis=1)], axis=0),
                    rhs=jnp.concatenate([jnp.concatenate([rhs[0], zeros_wide], axis=1),
                                         jnp.concatenate([zeros_wide, rhs[1]], axis=1)], axis=0),
                    k_dec_t=jnp.concatenate(k_dec, axis=0).T.astype(BF16)))

        for ch in chains:
            ch["kk_qk"] = _mm_nt(ch.pop("kq"), ch.pop("kk"))
        yield
        for ch in chains:
            kk_qk = ch.pop("kk_qk")
            decay = ch.pop("decay")
            ch["lmat"] = kk_qk[0:CHUNK] * decay * tri_strict
            ch["a_qk"] = (kk_qk[CHUNK:2 * CHUNK] * decay).astype(BF16)
            ch["x"] = eye - ch["lmat"] * level_masks[0]
        for level in range(1, N_LEVELS):
            for ch in chains:
                ch["y"] = _mm(ch["lmat"] * level_masks[level], blockdiag(ch["x"]))
            yield
            for ch in chains:
                ch["x"] = ch["x"] - _mm(ch["x"], blockdiag(ch.pop("y")))
            yield
        for ch in chains:
            ch["uw"] = _mm(ch.pop("x"), ch.pop("rhs"))
        for ch in chains:
            slot, p = ch["slot"], ch["p"]
            uw = ch.pop("uw")
            for i, h in enumerate((2 * p, 2 * p + 1)):
                base = i * (DN_DV + DN_DK)
                u_s[slot, h] = uw[:, base:base + DN_DV]
                wq_s[slot, h, 0:CHUNK, :] = uw[:, base + DN_DV:base + DN_DV + DN_DK].astype(BF16)
            akd_s[slot, p] = jnp.concatenate([ch["a_qk"], ch["k_dec_t"]], axis=0)
        yield

    def rec_stages(gi):
        for ci in range(group):
            c = gi * group + ci
            r0 = c * CHUNK if isinstance(c, int) else pl.multiple_of(c * CHUNK, CHUNK)
            slot = (gi % 2) * group + ci
            e_last_all = el_s[slot]
            rs = [_mm(wq_s[slot, h], state[h]) for h in range(DN_HEADS)]
            yield
            v_new = [(u_s[slot, h] - rs[h][0:CHUNK]).astype(BF16) for h in range(DN_HEADS)]
            zeros_v = jnp.zeros((CHUNK, DN_DV), BF16)
            av = [_mm(akd_s[slot, p],
                      jnp.concatenate([jnp.concatenate([v_new[2 * p], zeros_v], axis=1),
                                       jnp.concatenate([zeros_v, v_new[2 * p + 1]], axis=1)], axis=0))
                  for p in range(n_pairs)]
            for h in range(DN_HEADS):
                gl = DN_HEADS + h
                lanes_h = slice((h % 2) * DN_DV, (h % 2 + 1) * DN_DV)
                o = rs[h][CHUNK:2 * CHUNK] + av[h // 2][0:CHUNK, lanes_h]
                state[h] = state[h] * e_last_all[0:1, gl:gl + 1] + av[h // 2][CHUNK:CHUNK + DN_DK, lanes_h]
                o_ref[0, pl.ds(r0, CHUNK), slice(h * DN_DV, (h + 1) * DN_DV)] = o.astype(o_ref.dtype)
            yield

    run = _run_round_robin

    n_groups = n_chunks // group
    run(prep_stages(0))
    for gi in range(1, n_groups):
        run(prep_stages(gi), rec_stages(gi - 1))
    run(rec_stages(n_groups - 1))


def _deltanet(p, small, alog_row, dtb_row, ts, group):
    b, s, _ = p.shape
    width = DN_HEADS * DN_DK
    n_slots = 2 * group
    assert (ts // CHUNK) % group == 0
    kern = functools.partial(_deltanet_kernel, ts=ts, group=group)
    return pl.pallas_call(
        kern,
        grid=(b, s // ts),
        in_specs=[pl.BlockSpec((1, ts, width), lambda bi, t: (bi, t, 0)),
                  pl.BlockSpec((1, ts, width), lambda bi, t: (bi, t, 1)),
                  pl.BlockSpec((1, ts, width), lambda bi, t: (bi, t, 2)),
                  pl.BlockSpec((1, ts, LANES), lambda bi, t: (bi, t, 0)),
                  pl.BlockSpec((1, LANES), lambda bi, t: (0, 0)),
                  pl.BlockSpec((1, LANES), lambda bi, t: (0, 0))],
        out_specs=pl.BlockSpec((1, ts, DN_WIDTH), lambda bi, t: (bi, t, 0)),
        out_shape=jax.ShapeDtypeStruct((b, s, DN_WIDTH), BF16),
        scratch_shapes=[pltpu.VMEM((DN_HEADS, DN_DK, DN_DV), F32),
                        pltpu.VMEM((n_slots, DN_HEADS, CHUNK, DN_DV), F32),
                        pltpu.VMEM((n_slots, DN_HEADS, 2 * CHUNK, DN_DK), BF16),
                        pltpu.VMEM((n_slots, DN_HEADS // 2, CHUNK + DN_DK, 2 * CHUNK), BF16),
                        pltpu.VMEM((n_slots, 8, LANES), F32)],
        compiler_params=pltpu.CompilerParams(
            dimension_semantics=("parallel", "arbitrary"), vmem_limit_bytes=VMEM_LIMIT),
        name="gated_deltanet",
    )(p, p, p, small, alog_row, dtb_row)


ATTN_BLOCKS_PER_ITER = 4
ATTN_STRIDE = 4


def _attn_kernel(q_ref, k_ref, v_ref, o_ref, lse_ref, *scratch, dil, n_back):
    s = q_ref.shape[1]
    sub_len = s // dil
    nb = sub_len // ATTN_BLOCK
    n_pairs = ATTN_GROUP_WIDTH // LANES
    prev_block = nb > 1

    def lanes(pair):
        return slice(pair * LANES, (pair + 1) * LANES)

    if dil > 1:
        stage, tmp, qs, ks, vs, os_, ls = scratch
        assert dil in (ATTN_STRIDE, ATTN_STRIDE * ATTN_STRIDE)
        q_len = s // ATTN_STRIDE

        def classes():
            if dil == ATTN_STRIDE:
                return [(r, stage, r) for r in range(dil)]
            return [(r_lo + ATTN_STRIDE * r_hi, tmp, r_lo * q_len + r_hi)
                    for r_lo in range(ATTN_STRIDE) for r_hi in range(ATTN_STRIDE)]

        def deinterleave(dst, pair):
            if dil > ATTN_STRIDE:
                for r_lo in range(ATTN_STRIDE):
                    tmp[r_lo * q_len:(r_lo + 1) * q_len, :] = stage[pl.ds(r_lo, q_len, stride=ATTN_STRIDE), :]
            for r, buf, start in classes():
                dst[pair, r * sub_len:(r + 1) * sub_len, :] = (
                    buf[pl.ds(start, sub_len, stride=ATTN_STRIDE), :].astype(dst.dtype))

        def interleave(src, pair):
            for r, buf, start in classes():
                buf[pl.ds(start, sub_len, stride=ATTN_STRIDE), :] = src[pair, r * sub_len:(r + 1) * sub_len, :]
            if dil > ATTN_STRIDE:
                for r_lo in range(ATTN_STRIDE):
                    stage[pl.ds(r_lo, q_len, stride=ATTN_STRIDE), :] = tmp[r_lo * q_len:(r_lo + 1) * q_len, :]

        for src, dst in ((q_ref, qs), (k_ref, ks), (v_ref, vs)):
            for pair in range(n_pairs):
                stage[...] = src[0, :, lanes(pair)].astype(F32)
                deinterleave(dst, pair)

        def load(ref, src, pair, r0):
            return src[pair, pl.ds(r0, ATTN_BLOCK), :]
    else:
        qs, ks, vs = q_ref, k_ref, v_ref

        def load(ref, src, pair, r0):
            return ref[0, pl.ds(r0, ATTN_BLOCK), lanes(pair)]

    n_keys = 2 * ATTN_BLOCK if prev_block else ATTN_BLOCK
    qi = lax.broadcasted_iota(jnp.int32, (ATTN_BLOCK, n_keys), 0)
    ki = lax.broadcasted_iota(jnp.int32, (ATTN_BLOCK, n_keys), 1)
    dist = qi + (n_keys - ATTN_BLOCK) - ki
    band = (dist >= 0) & (dist <= n_back)
    bias_full = jnp.where(band, 0.0, -jnp.inf).astype(F32)
    bias_first = jnp.where(band & (ki >= ATTN_BLOCK), 0.0, -jnp.inf).astype(F32)
    lane = lax.broadcasted_iota(jnp.int32, (ATTN_BLOCK, LANES), 1)
    low_half = lane < ATTN_HEAD_DIM
    scale = ATTN_HEAD_DIM ** -0.5

    def block_body(fi, carry):
        chains = []
        for bi in range(ATTN_BLOCKS_PER_ITER):
            f = fi * ATTN_BLOCKS_PER_ITER + bi
            q0 = f * ATTN_BLOCK if isinstance(f, int) else pl.multiple_of(f * ATTN_BLOCK, ATTN_BLOCK)
            if prev_block:
                k0 = max(f - 1, 0) * ATTN_BLOCK
                bias = bias_full if f % nb > 0 else bias_first
            else:
                bias = bias_full
            for pair in range(n_pairs):
                qp = load(q_ref, qs, pair, q0) * scale
                kp = load(k_ref, ks, pair, q0)
                vp = load(v_ref, vs, pair, q0)
                if prev_block:
                    kp = jnp.concatenate([load(k_ref, ks, pair, k0), kp], axis=0)
                    vp = jnp.concatenate([load(v_ref, vs, pair, k0), vp], axis=0)
                for half in range(2):
                    sel = low_half if half == 0 else ~low_half
                    chains.append(dict(q0=q0, pair=pair, half=half, bias=bias, kp=kp, vp=vp,
                                       qm=jnp.where(sel, qp, jnp.zeros_like(qp))))
        for ch in chains:
            ch["sc"] = _mm_nt(ch.pop("qm"), ch.pop("kp"))
        for ch in chains:
            sc = ch.pop("sc") + ch.pop("bias")
            m = jnp.max(sc, axis=-1, keepdims=True)
            p = jnp.exp(sc - m)
            ch["denom"] = jnp.sum(p, axis=-1, keepdims=True)
            ch["m"] = m
            ch["p"] = p.astype(BF16)
        for ch in chains:
            ch["pv"] = _mm(ch.pop("p"), ch.pop("vp"))
        for c0 in range(0, len(chains), 2):
            lo, hi = chains[c0], chains[c0 + 1]
            o = jnp.where(low_half, lo["pv"] / lo["denom"], hi["pv"] / hi["denom"])
            lse = jnp.where(low_half, lo["m"] + jnp.log(lo["denom"]), hi["m"] + jnp.log(hi["denom"]))
            q0, pair = lo["q0"], lo["pair"]
            if dil > 1:
                os_[pair, pl.ds(q0, ATTN_BLOCK), :] = o
                ls[pair, pl.ds(q0, ATTN_BLOCK), :] = lse
            else:
                o_ref[0, pl.ds(q0, ATTN_BLOCK), lanes(pair)] = o.astype(o_ref.dtype)
                lse_ref[0, pl.ds(q0, ATTN_BLOCK), lanes(pair)] = lse
        return carry

    n_iters = dil * nb // ATTN_BLOCKS_PER_ITER
    if prev_block:
        for fi in range(n_iters):
            block_body(fi, 0)
    else:
        lax.fori_loop(0, n_iters, block_body, 0)

    if dil > 1:
        for pair in range(n_pairs):
            for src, dst in ((os_, o_ref), (ls, lse_ref)):
                interleave(src, pair)
                dst[0, :, lanes(pair)] = stage[...].astype(dst.dtype)


def _dilated_attention(p, group, col_base):
    b, s, _ = p.shape
    window, dil = DIL_CONFIGS[group]
    gw = ATTN_GROUP_WIDTH
    blk = col_base // gw + group
    step = ATTN_WIDTH // gw
    kern = functools.partial(_attn_kernel, dil=dil, n_back=window // dil)
    return pl.pallas_call(
        kern,
        grid=(b,),
        in_specs=[pl.BlockSpec((1, s, gw), lambda bi: (bi, 0, blk)),
                  pl.BlockSpec((1, s, gw), lambda bi: (bi, 0, blk + step)),
                  pl.BlockSpec((1, s, gw), lambda bi: (bi, 0, blk + 2 * step))],
        out_specs=[pl.BlockSpec((1, s, gw), lambda bi: (bi, 0, 0)),
                   pl.BlockSpec((1, s, gw), lambda bi: (bi, 0, 0))],
        out_shape=[jax.ShapeDtypeStruct((b, s, gw), BF16),
                   jax.ShapeDtypeStruct((b, s, gw), F32)],
        scratch_shapes=([] if dil == 1 else
                        [pltpu.VMEM((s, LANES), F32) for _ in range(2)]
                        + [pltpu.VMEM((gw // LANES, s, LANES), BF16) for _ in range(3)]
                        + [pltpu.VMEM((gw // LANES, s, LANES), F32) for _ in range(2)]),
        compiler_params=pltpu.CompilerParams(
            dimension_semantics=("parallel",), vmem_limit_bytes=VMEM_LIMIT),
        name=f"dilated_attn_g{group}",
    )(p, p, p)


MERGE_SUBTILES = 8


def _merge_kernel(x_ref, mod_ref, oa_ref, za_ref, ga_ref, gb_ref, zb_ref,
                  ob0_ref, ob1_ref, ob2_ref, l0_ref, l1_ref, l2_ref,
                  dnw_ref, pa_ref, pb_ref, wo_ref, fw_ref, out_ref):
    def silu_of_half(h):
        return h + h * jnp.tanh(h)

    gate = mod_ref[0][2:3]

    def stages(rows):
        oa = oa_ref[0, rows, :].astype(F32)
        za_half = za_ref[0, rows, :].astype(F32)
        parts = []
        for h in range(DN_HEADS):
            cq = slice(h * DN_DV, (h + 1) * DN_DV)
            blk = oa[:, cq]
            y = blk * lax.rsqrt(jnp.mean(blk * blk, axis=-1, keepdims=True) + EPS) * dnw_ref[...]
            parts.append(y * silu_of_half(za_half[:, cq]))
        ya_half = _mm(jnp.concatenate(parts, axis=1), pa_ref[...])
        yield

        l0, l1, l2 = l0_ref[0, rows, :], l1_ref[0, rows, :], l2_ref[0, rows, :]
        m = jnp.maximum(jnp.maximum(l0, l1), l2)
        e0, e1, e2 = jnp.exp(l0 - m), jnp.exp(l1 - m), jnp.exp(l2 - m)
        inv_den = 1.0 / (e0 + e1 + e2)
        ob = jnp.concatenate([ob0_ref[0, rows, :].astype(F32) * (e0 * inv_den),
                              ob1_ref[0, rows, :].astype(F32) * (e1 * inv_den),
                              ob2_ref[0, rows, :].astype(F32) * (e2 * inv_den)], axis=1)
        yb_half = _mm(ob * silu_of_half(zb_ref[0, rows, :].astype(F32)), pb_ref[...])
        yield

        merged = (ya_half + ya_half * jnp.tanh(ga_ref[0, rows, :].astype(F32))
                  + (yb_half + yb_half * jnp.tanh(gb_ref[0, rows, :].astype(F32))))
        delta = _mm(merged, wo_ref[...])
        yield

        xo = x_ref[0, rows, :] + gate * delta
        out_ref[0, rows, :] = xo * lax.rsqrt(jnp.mean(xo * xo, axis=-1, keepdims=True) + EPS) * fw_ref[...]
        yield

    tm = x_ref.shape[1]
    sub = tm // MERGE_SUBTILES
    _run_round_robin(*[stages(slice(i * sub, (i + 1) * sub)) for i in range(MERGE_SUBTILES)])


def _merge(x, mod3, o_a, p, obs, lses, dn_norm_w, pa, pb, wo, final_w, tm, cols):
    b, s, d = x.shape
    gw = ATTN_GROUP_WIDTH

    def row_spec(width, blk):
        return pl.BlockSpec((1, tm, width), lambda bi, i: (bi, i, blk))

    def full_spec(shape):
        return pl.BlockSpec(shape, lambda bi, i: (0,) * len(shape), pipeline_mode=pl.Buffered(1))

    return pl.pallas_call(
        _merge_kernel,
        grid=(b, s // tm),
        in_specs=[row_spec(d, 0),
                  pl.BlockSpec((1, 3, d), lambda bi, i: (bi, 0, 0)),
                  row_spec(DN_WIDTH, 0),
                  row_spec(DN_WIDTH, cols["za"] // DN_WIDTH),
                  row_spec(d, cols["ga"] // d),
                  row_spec(d, cols["gb"] // d),
                  row_spec(ATTN_WIDTH, cols["zb"] // ATTN_WIDTH),
                  row_spec(gw, 0), row_spec(gw, 0), row_spec(gw, 0),
                  row_spec(gw, 0), row_spec(gw, 0), row_spec(gw, 0),
                  full_spec((1, DN_DV)), full_spec(pa.shape), full_spec(pb.shape), full_spec(wo.shape),
                  full_spec((1, d))],
        out_specs=row_spec(d, 0),
        out_shape=jax.ShapeDtypeStruct((b, s, d), F32),
        compiler_params=pltpu.CompilerParams(
            dimension_semantics=("parallel", "parallel"), vmem_limit_bytes=VMEM_LIMIT),
        name="merge_outproj",
    )(x, mod3, o_a, p, p, p, p, *obs, *lses, dn_norm_w.reshape(1, DN_DV), pa, pb, wo, final_w.reshape(1, d))


W_PREP_COLS = 128


def _wprep_kernel(wt_ref, wc_ref, wp_ref, ws_ref, *, conv_width, segments, small_off):
    wc_ref[...] = wt_ref[0:conv_width, :].astype(BF16)
    for src, width, dst, scale in segments:
        blk = wt_ref[src:src + width, :]
        wp_ref[dst:dst + width, :] = (blk if scale == 1.0 else blk * scale).astype(BF16)
    n_small = 2 * DN_HEADS
    ws_ref[0:n_small, :] = wt_ref[small_off:small_off + n_small, :].astype(BF16)
    ws_ref[n_small:LANES, :] = jnp.zeros((LANES - n_small, wt_ref.shape[1]), BF16)


def _weight_prep(wt, conv_width, plain_width, segments, small_off):
    n_in, d = wt.shape
    assert d % W_PREP_COLS == 0 and small_off % 16 == 0 and all(sg[0] % 16 == 0 for sg in segments)
    kern = functools.partial(_wprep_kernel, conv_width=conv_width, segments=segments, small_off=small_off)
    return pl.pallas_call(
        kern,
        grid=(d // W_PREP_COLS,),
        in_specs=[pl.BlockSpec((n_in, W_PREP_COLS), lambda i: (0, i))],
        out_specs=[pl.BlockSpec((conv_width, W_PREP_COLS), lambda i: (0, i)),
                   pl.BlockSpec((plain_width, W_PREP_COLS), lambda i: (0, i)),
                   pl.BlockSpec((LANES, W_PREP_COLS), lambda i: (0, i))],
        out_shape=[jax.ShapeDtypeStruct((conv_width, d), BF16),
                   jax.ShapeDtypeStruct((plain_width, d), BF16),
                   jax.ShapeDtypeStruct((LANES, d), BF16)],
        compiler_params=pltpu.CompilerParams(dimension_semantics=("parallel",), vmem_limit_bytes=VMEM_LIMIT),
        name="weight_prep",
    )(wt)


def _layer(x, c, norm_w, ada_w, ada_b, w_in, conv_w, a_log, dt_bias, dn_norm_w, w_proj_a, w_proj_b, w_out,
           final_norm_w):
    b, s, d = x.shape
    qk = DN_HEADS * DN_DK
    splits = (qk, qk, DN_WIDTH, DN_WIDTH, DN_HEADS, DN_HEADS, ATTN_WIDTH, ATTN_WIDTH, ATTN_WIDTH, ATTN_WIDTH, d, d)
    names = ("qa", "ka", "va", "za", "beta", "a", "qb", "kb", "vb", "zb", "ga", "gb")
    offs = dict(zip(names, np.cumsum((0,) + splits[:-1]).tolist()))
    widths = dict(zip(names, splits))
    order = ("za", "ga", "gb", "qb", "kb", "vb", "zb")
    cols, pos = {}, 0
    for n in order:
        assert pos % widths[n] == 0, (n, pos)
        cols[n] = pos
        pos += widths[n]
    assert offs["qa"] == 0 and offs["ka"] == qk and offs["va"] == 2 * qk and offs["a"] == offs["beta"] + DN_HEADS
    halved = ("za", "ga", "gb", "zb")
    segments = tuple((offs[n], widths[n], cols[n], 0.5 if n in halved else 1.0) for n in order)
    w_conv, w_plain, w_small = _weight_prep(w_in.T, 2 * qk + DN_WIDTH, pos, segments, offs["beta"])
    pad_lo = jnp.zeros((DN_HEADS,), F32)
    pad_hi = jnp.zeros((LANES - 2 * DN_HEADS,), F32)
    alog_row = jnp.concatenate([pad_lo, a_log.astype(F32), pad_hi]).reshape(1, LANES)
    dtb_row = jnp.concatenate([pad_lo, dt_bias.astype(F32), pad_hi]).reshape(1, LANES)

    mod3 = _adaln_mod(c, ada_w, ada_b).reshape(b, 3, d)
    p_conv, p, small = _inproj(x, mod3, norm_w, w_conv, w_plain, w_small, conv_w)
    o_a = _deltanet(p_conv, small, alog_row, dtb_row, ts=min(s, 1024), group=4)
    obs, lses = [], []
    for g in range(len(DIL_CONFIGS)):
        o_g, lse_g = _dilated_attention(p, g, cols["qb"])
        obs.append(o_g)
        lses.append(lse_g)
    return _merge(x, mod3, o_a, p, obs, lses, dn_norm_w, (w_proj_a * 0.5).astype(BF16),
                  (w_proj_b * 0.5).astype(BF16), w_out.astype(BF16), final_norm_w, tm=min(s, 1024), cols=cols)


def kernel(x, c, norm_w, ada_w, ada_b, w_in, conv_w, a_log, dt_bias, dn_norm_w, w_proj_a, w_proj_b, w_out,
           final_norm_w):
    depth = norm_w.shape[0]
    assert depth == 1, "the final RMSNorm is fused into the single layer's output kernel"
    return _layer(x, c, norm_w[0], ada_w[0], ada_b[0], w_in[0], conv_w[0], a_log[0], dt_bias[0], dn_norm_w[0],
                  w_proj_a[0], w_proj_b[0], w_out[0], final_norm_w)
```

```python
import functools

import jax
import jax.numpy as jnp
import numpy as np
from jax import lax
from jax.experimental import pallas as pl
from jax.experimental.pallas import tpu as pltpu

F32 = jnp.float32
BF16 = jnp.bfloat16
HIGHEST = lax.Precision.HIGHEST

EPS = 1e-6
DN_HEADS = 8
DN_DK = 128
DN_DV = 128
DN_WIDTH = DN_HEADS * DN_DV
CONV_K = 4
CHUNK = 64
DIL_CONFIGS = ((128, 1), (512, 4), (2048, 16))
HEADS_PER_GROUP = 4
ATTN_HEAD_DIM = 64
ATTN_GROUP_WIDTH = HEADS_PER_GROUP * ATTN_HEAD_DIM
ATTN_WIDTH = ATTN_GROUP_WIDTH * len(DIL_CONFIGS)
ATTN_BLOCK = 128
LANES = 128
HALO = 8
VMEM_LIMIT = 56 * 1024 * 1024


def _mm(a, b):
    return jnp.dot(a.astype(BF16), b.astype(BF16), preferred_element_type=F32)


def _mm_nt(a, b):
    return lax.dot_general(a.astype(BF16), b.astype(BF16), (((1,), (1,)), ((), ())),
                           preferred_element_type=F32)


def _mm_tn(a, b):
    return lax.dot_general(a.astype(BF16), b.astype(BF16), (((0,), (0,)), ((), ())),
                           preferred_element_type=F32)


def _run_round_robin(*gens):
    gens = list(gens)
    while gens:
        for g in list(gens):
            if next(g, StopIteration) is StopIteration:
                gens.remove(g)


def _sigmoid(x):
    return 0.5 * jnp.tanh(0.5 * x) + 0.5


def _silu(x):
    return x * _sigmoid(x)


def _mod_kernel(c_ref, w_ref, b_ref, o_ref):
    sc = _silu(c_ref[...])
    o_ref[...] = jnp.dot(sc, w_ref[...], preferred_element_type=F32, precision=HIGHEST) + b_ref[...]


def _adaln_mod(c, ada_w, ada_b):
    b, d = c.shape
    return pl.pallas_call(
        _mod_kernel,
        grid=(3,),
        in_specs=[pl.BlockSpec((b, d), lambda j: (0, 0)),
                  pl.BlockSpec((d, d), lambda j: (0, j)),
                  pl.BlockSpec((1, d), lambda j: (0, j))],
        out_specs=pl.BlockSpec((b, d), lambda j: (0, j)),
        out_shape=jax.ShapeDtypeStruct((b, 3 * d), F32),
        name="adaln_mod",
    )(c, ada_w, ada_b.reshape(1, 3 * d))


CONV_ROWS = 64
CONV_SUB = 256
SUBLANES = 8
TOKEN_CHAINS = 3


def _conv_unit(buf, r0, cl, half_taps, shift_masks, norm_scale, skip_norm):
    a = buf[r0:r0 + HALO + CONV_ROWS, cl]
    n = CONV_ROWS // SUBLANES
    vs = [a[SUBLANES * i:SUBLANES * (i + 1)] for i in range(n + 1)]
    acc = [vs[i + 1] * half_taps[CONV_K - 1] for i in range(n)]
    for shift in range(1, CONV_K):
        rs = [pltpu.roll(v, shift, 0) for v in vs]
        tap = half_taps[CONV_K - 1 - shift]
        for i in range(n):
            acc[i] = acc[i] + jnp.where(shift_masks[shift], rs[i], rs[i + 1]) * tap
    h = jnp.concatenate(acc, axis=0)
    y = h + h * jnp.tanh(h)
    ss = jnp.sum(y * y, axis=-1, keepdims=True)
    out = y * jnp.where(skip_norm, 1.0, lax.rsqrt(ss + EPS) * norm_scale)
    token = jnp.broadcast_to(ss[CONV_ROWS - SUBLANES:], (SUBLANES, LANES))
    return out.astype(BF16), token


def _inproj_kernel(x_ref, mod_ref, nw_ref, wc_ref, wp_ref, ws_ref, cw_ref, pc_ref, pp_ref, small_ref,
                   h_scr, cbuf, tok, *, steps_per_seg):
    j = pl.program_id(1)
    s = x_ref.shape[1]

    @pl.when(j == 0)
    def _():
        x = x_ref[0]
        m = mod_ref[0]
        y = x * lax.rsqrt(jnp.mean(x * x, axis=-1, keepdims=True) + EPS) * nw_ref[...]
        h = (y * (1.0 + m[1:2]) + m[0:1]).astype(BF16)
        h_scr[...] = h
        small_ref[0] = _mm_nt(h, ws_ref[...])
        cbuf[0:HALO, :] = jnp.zeros((HALO, CONV_SUB), F32)

    cbuf[HALO:HALO + s, :] = _mm_nt(h_scr[...], wc_ref[...])
    pp_ref[0] = _mm_nt(h_scr[...], wp_ref[...]).astype(BF16)

    sub = lax.broadcasted_iota(jnp.int32, (SUBLANES, LANES), 0)
    shift_masks = [sub < shift for shift in range(CONV_K)]
    q_scale = jnp.where(j < steps_per_seg, DN_DK ** -0.5, 1.0)
    is_v = j >= 2 * steps_per_seg
    never = j < 0
    for c in range(TOKEN_CHAINS):
        tok[c] = jnp.zeros((SUBLANES, LANES), F32)
    n_unit = 0
    for c0 in range(0, CONV_SUB, DN_DK):
        cl = slice(c0, c0 + DN_DK)
        half_taps = [cw_ref[t:t + 1, cl] * 0.5 for t in range(CONV_K)]
        for r0 in range(0, s, CONV_ROWS):
            slot = n_unit % TOKEN_CHAINS
            held = tok[slot][0:1]
            taps = [jnp.where(never, held, tap) for tap in half_taps]
            out, token = _conv_unit(cbuf, r0, cl, taps, shift_masks, q_scale, is_v)
            pc_ref[0, r0:r0 + CONV_ROWS, cl] = out
            tok[slot] = token
            n_unit += 1


def _inproj(x, mod3, norm_w, w_conv, w_plain, w_small, conv_w):
    b, s, d = x.shape
    conv_width, plain_width = w_conv.shape[0], w_plain.shape[0]
    n_steps = conv_width // CONV_SUB
    tp = plain_width // n_steps
    assert n_steps % 3 == 0 and n_steps * CONV_SUB == conv_width and conv_w.shape[1] == conv_width
    assert tp * n_steps == plain_width and tp % LANES == 0 and s % CONV_ROWS == 0
    kern = functools.partial(_inproj_kernel, steps_per_seg=n_steps // 3)
    return pl.pallas_call(
        kern,
        grid=(b, n_steps),
        in_specs=[pl.BlockSpec((1, s, d), lambda bi, j: (bi, 0, 0)),
                  pl.BlockSpec((1, 3, d), lambda bi, j: (bi, 0, 0)),
                  pl.BlockSpec((1, d), lambda bi, j: (0, 0)),
                  pl.BlockSpec((CONV_SUB, d), lambda bi, j: (j, 0)),
                  pl.BlockSpec((tp, d), lambda bi, j: (j, 0)),
                  pl.BlockSpec((LANES, d), lambda bi, j: (0, 0)),
                  pl.BlockSpec((CONV_K, CONV_SUB), lambda bi, j: (0, j))],
        out_specs=[pl.BlockSpec((1, s, CONV_SUB), lambda bi, j: (bi, 0, j)),
                   pl.BlockSpec((1, s, tp), lambda bi, j: (bi, 0, j)),
                   pl.BlockSpec((1, s, LANES), lambda bi, j: (bi, 0, 0))],
        out_shape=[jax.ShapeDtypeStruct((b, s, conv_width), BF16),
                   jax.ShapeDtypeStruct((b, s, plain_width), BF16),
                   jax.ShapeDtypeStruct((b, s, LANES), F32)],
        scratch_shapes=[pltpu.VMEM((s, d), BF16),
                        pltpu.VMEM((HALO + s, CONV_SUB), F32),
                        pltpu.VMEM((TOKEN_CHAINS, SUBLANES, LANES), F32)],
        compiler_params=pltpu.CompilerParams(
            dimension_semantics=("parallel", "arbitrary"), vmem_limit_bytes=VMEM_LIMIT),
        name="norm_inproj",
    )(x, mod3, norm_w.reshape(1, d), w_conv, w_plain, w_small, conv_w)


def _level_mask(level, i, j):
    same_block = (i >> (level + 1)) == (j >> (level + 1))
    return same_block & (((i >> level) & 1) == 1) & (((j >> level) & 1) == 0)


N_LEVELS = CHUNK.bit_length() - 1


def _deltanet_kernel(q_ref, k_ref, v_ref, small_ref, alog_ref, dtb_ref, o_ref,
                     state, u_s, wq_s, akd_s, el_s, *, ts, group):
    @pl.when(pl.program_id(1) == 0)
    def _():
        state[...] = jnp.zeros_like(state)

    row = lax.broadcasted_iota(jnp.int32, (CHUNK, 2 * CHUNK), 0)
    lane = lax.broadcasted_iota(jnp.int32, (CHUNK, 2 * CHUNK), 1)
    col = lane & (CHUNK - 1)
    in_a = lane < CHUNK
    causal = row >= col
    tri_strict = (row > col).astype(F32)
    eye = (row == col).astype(F32)
    row_c = lax.broadcasted_iota(jnp.int32, (CHUNK, CHUNK), 0)
    col_c = lax.broadcasted_iota(jnp.int32, (CHUNK, CHUNK), 1)
    tri = (row_c >= col_c).astype(F32)
    level_masks = [_level_mask(level, row, col).astype(F32) for level in range(N_LEVELS)]
    neg_a = -jnp.exp(alog_ref[...])
    dtb = dtb_ref[...]
    n_chunks = ts // CHUNK
    n_pairs = DN_HEADS // 2
    zeros_wide = jnp.zeros((CHUNK, DN_DV + DN_DK), BF16)
    zeros_head = jnp.zeros((CHUNK, DN_DK), BF16)

    def blockdiag(m):
        return jnp.concatenate([jnp.where(in_a, m, 0.0), jnp.where(in_a, 0.0, m)], axis=0).astype(BF16)

    def prep_stages(gi):
        chains = []
        for ci in range(group):
            c = gi * group + ci
            r0 = c * CHUNK if isinstance(c, int) else pl.multiple_of(c * CHUNK, CHUNK)
            sm = small_ref[0, pl.ds(r0, CHUNK), :]
            beta_all = _sigmoid(sm)
            z = sm + dtb
            softplus = jnp.maximum(z, 0.0) + jnp.log(1.0 + jnp.exp(-jnp.abs(z)))
            g_all = neg_a * softplus
            gc_all = jnp.dot(tri, g_all, preferred_element_type=F32, precision=HIGHEST)
            gc_t = jnp.concatenate([gc_all, gc_all], axis=0).T
            g_last = gc_all[CHUNK - 1:CHUNK, :]
            e_in_all = jnp.exp(gc_all)
            e_out_all = jnp.exp(g_last - gc_all)
            slot = (gi % 2) * group + ci
            el_s[slot] = jnp.broadcast_to(jnp.exp(g_last), (8, LANES))
            for p in range(n_pairs):
                heads = (2 * p, 2 * p + 1)
                q, k, k_beta, rhs, k_dec = [], [], [], [], []
                for h in heads:
                    gl = DN_HEADS + h
                    cq = slice(h * DN_DK, (h + 1) * DN_DK)
                    q_h = q_ref[0, pl.ds(r0, CHUNK), cq].astype(F32)
                    k_h = k_ref[0, pl.ds(r0, CHUNK), cq].astype(F32)
                    v_h = v_ref[0, pl.ds(r0, CHUNK), slice(h * DN_DV, (h + 1) * DN_DV)].astype(F32)
                    beta = beta_all[:, h:h + 1]
                    e_in = e_in_all[:, gl:gl + 1]
                    kb_h = k_h * beta
                    q.append(q_h)
                    k.append(k_h.astype(BF16))
                    k_beta.append(kb_h)
                    rhs.append(jnp.concatenate([v_h * beta, kb_h * e_in], axis=1).astype(BF16))
                    k_dec.append(k_h * e_out_all[:, gl:gl + 1])
                    wq_s[slot, h, CHUNK:2 * CHUNK, :] = (q_h * e_in).astype(BF16)
                ga, gb = DN_HEADS + heads[0], DN_HEADS + heads[1]
                gc_col = jnp.where(in_a, gc_all[:, ga:ga + 1], gc_all[:, gb:gb + 1])
                gc_row = jnp.where(in_a[0:1], gc_t[ga:ga + 1, :], gc_t[gb:gb + 1, :])
                chains.append(dict(
                    slot=slot, p=p,
                    decay=jnp.exp(jnp.where(causal, gc_col - gc_row, -jnp.inf)),
                    kq=jnp.concatenate([jnp.concatenate(k_beta, axis=1), jnp.concatenate(q, axis=1)],
                                       axis=0).astype(BF16),
                    kk=jnp.concatenate([jnp.concatenate([k[0], zeros_head], axis=1),
                                        jnp.concatenate([zeros_head, k[1]], axis=1)], axis=0),
                    rhs=jnp.concatenate([jnp.concatenate([rhs[0], zeros_wide], axis=1),
                                         jnp.concatenate([zeros_wide, rhs[1]], axis=1)], axis=0),
                    k_dec_t=jnp.concatenate(k_dec, axis=0).T.astype(BF16)))

        for ch in chains:
            ch["kk_qk"] = _mm_nt(ch.pop("kq"), ch.pop("kk"))
        yield
        for ch in chains:
            kk_qk = ch.pop("kk_qk")
            decay = ch.pop("decay")
            ch["lmat"] = kk_qk[0:CHUNK] * decay * tri_strict
            ch["a_qk"] = (kk_qk[CHUNK:2 * CHUNK] * decay).astype(BF16)
            ch["x"] = eye - ch["lmat"] * level_masks[0]
        for level in range(1, N_LEVELS):
            for ch in chains:
                ch["y"] = _mm(ch["lmat"] * level_masks[level], blockdiag(ch["x"]))
            yield
            for ch in chains:
                ch["x"] = ch["x"] - _mm(ch["x"], blockdiag(ch.pop("y")))
            yield
        for ch in chains:
            ch["uw"] = _mm(ch.pop("x"), ch.pop("rhs"))
        for ch in chains:
            slot, p = ch["slot"], ch["p"]
            uw = ch.pop("uw")
            for i, h in enumerate((2 * p, 2 * p + 1)):
                base = i * (DN_DV + DN_DK)
                u_s[slot, h] = uw[:, base:base + DN_DV]
                wq_s[slot, h, 0:CHUNK, :] = uw[:, base + DN_DV:base + DN_DV + DN_DK].astype(BF16)
            akd_s[slot, p] = jnp.concatenate([ch["a_qk"], ch["k_dec_t"]], axis=0)
        yield

    def rec_stages(gi):
        for ci in range(group):
            c = gi * group + ci
            r0 = c * CHUNK if isinstance(c, int) else pl.multiple_of(c * CHUNK, CHUNK)
            slot = (gi % 2) * group + ci
            e_last_all = el_s[slot]
            rs = [_mm(wq_s[slot, h], state[h]) for h in range(DN_HEADS)]
            yield
            v_new = [(u_s[slot, h] - rs[h][0:CHUNK]).astype(BF16) for h in range(DN_HEADS)]
            zeros_v = jnp.zeros((CHUNK, DN_DV), BF16)
            av = [_mm(akd_s[slot, p],
                      jnp.concatenate([jnp.concatenate([v_new[2 * p], zeros_v], axis=1),
                                       jnp.concatenate([zeros_v, v_new[2 * p + 1]], axis=1)], axis=0))
                  for p in range(n_pairs)]
            for h in range(DN_HEADS):
                gl = DN_HEADS + h
                lanes_h = slice((h % 2) * DN_DV, (h % 2 + 1) * DN_DV)
                o = rs[h][CHUNK:2 * CHUNK] + av[h // 2][0:CHUNK, lanes_h]
                state[h] = state[h] * e_last_all[0:1, gl:gl + 1] + av[h // 2][CHUNK:CHUNK + DN_DK, lanes_h]
                o_ref[0, pl.ds(r0, CHUNK), slice(h * DN_DV, (h + 1) * DN_DV)] = o.astype(o_ref.dtype)
            yield

    run = _run_round_robin

    n_groups = n_chunks // group
    run(prep_stages(0))
    for gi in range(1, n_groups):
        run(prep_stages(gi), rec_stages(gi - 1))
    run(rec_stages(n_groups - 1))


def _deltanet(p, small, alog_row, dtb_row, ts, group):
    b, s, _ = p.shape
    width = DN_HEADS * DN_DK
    n_slots = 2 * group
    assert (ts // CHUNK) % group == 0
    kern = functools.partial(_deltanet_kernel, ts=ts, group=group)
    return pl.pallas_call(
        kern,
        grid=(b, s // ts),
        in_specs=[pl.BlockSpec((1, ts, width), lambda bi, t: (bi, t, 0)),
                  pl.BlockSpec((1, ts, width), lambda bi, t: (bi, t, 1)),
                  pl.BlockSpec((1, ts, width), lambda bi, t: (bi, t, 2)),
                  pl.BlockSpec((1, ts, LANES), lambda bi, t: (bi, t, 0)),
                  pl.BlockSpec((1, LANES), lambda bi, t: (0, 0)),
                  pl.BlockSpec((1, LANES), lambda bi, t: (0, 0))],
        out_specs=pl.BlockSpec((1, ts, DN_WIDTH), lambda bi, t: (bi, t, 0)),
        out_shape=jax.ShapeDtypeStruct((b, s, DN_WIDTH), BF16),
        scratch_shapes=[pltpu.VMEM((DN_HEADS, DN_DK, DN_DV), F32),
                        pltpu.VMEM((n_slots, DN_HEADS, CHUNK, DN_DV), F32),
                        pltpu.VMEM((n_slots, DN_HEADS, 2 * CHUNK, DN_DK), BF16),
                        pltpu.VMEM((n_slots, DN_HEADS // 2, CHUNK + DN_DK, 2 * CHUNK), BF16),
                        pltpu.VMEM((n_slots, 8, LANES), F32)],
        compiler_params=pltpu.CompilerParams(
            dimension_semantics=("parallel", "arbitrary"), vmem_limit_bytes=VMEM_LIMIT),
        name="gated_deltanet",
    )(p, p, p, small, alog_row, dtb_row)


ATTN_BLOCKS_UNROLLED = 2
ATTN_BLOCKS_ROLLED = 4
ATTN_STRIDE = 4


def _attn_kernel(q_ref, k_ref, v_ref, o_ref, lse_ref, *scratch, dil, n_back):
    s = q_ref.shape[1]
    sub_len = s // dil
    nb = sub_len // ATTN_BLOCK
    n_pairs = ATTN_GROUP_WIDTH // LANES
    prev_block = nb > 1
    blocks_per_iter = ATTN_BLOCKS_UNROLLED if prev_block else ATTN_BLOCKS_ROLLED

    def lanes(pair):
        return slice(pair * LANES, (pair + 1) * LANES)

    if dil > 1:
        stage, tmp, qs, ks, vs, os_, ls = scratch
        assert dil in (ATTN_STRIDE, ATTN_STRIDE * ATTN_STRIDE)
        q_len = s // ATTN_STRIDE

        def classes():
            if dil == ATTN_STRIDE:
                return [(r, stage, r) for r in range(dil)]
            return [(r_lo + ATTN_STRIDE * r_hi, tmp, r_lo * q_len + r_hi)
                    for r_lo in range(ATTN_STRIDE) for r_hi in range(ATTN_STRIDE)]

        def deinterleave(dst, pair):
            if dil > ATTN_STRIDE:
                for r_lo in range(ATTN_STRIDE):
                    tmp[r_lo * q_len:(r_lo + 1) * q_len, :] = stage[pl.ds(r_lo, q_len, stride=ATTN_STRIDE), :]
            for r, buf, start in classes():
                dst[pair, r * sub_len:(r + 1) * sub_len, :] = (
                    buf[pl.ds(start, sub_len, stride=ATTN_STRIDE), :].astype(dst.dtype))

        def interleave(src, pair):
            for r, buf, start in classes():
                buf[pl.ds(start, sub_len, stride=ATTN_STRIDE), :] = src[pair, r * sub_len:(r + 1) * sub_len, :]
            if dil > ATTN_STRIDE:
                for r_lo in range(ATTN_STRIDE):
                    stage[pl.ds(r_lo, q_len, stride=ATTN_STRIDE), :] = tmp[r_lo * q_len:(r_lo + 1) * q_len, :]

        for src, dst in ((q_ref, qs), (k_ref, ks), (v_ref, vs)):
            for pair in range(n_pairs):
                stage[...] = src[0, :, lanes(pair)].astype(F32)
                deinterleave(dst, pair)

        def load(ref, src, pair, r0):
            return src[pair, pl.ds(r0, ATTN_BLOCK), :]
    else:
        qs, ks, vs = q_ref, k_ref, v_ref

        def load(ref, src, pair, r0):
            return ref[0, pl.ds(r0, ATTN_BLOCK), lanes(pair)]

    n_keys = 2 * ATTN_BLOCK if prev_block else ATTN_BLOCK
    qi = lax.broadcasted_iota(jnp.int32, (ATTN_BLOCK, n_keys), 0)
    ki = lax.broadcasted_iota(jnp.int32, (ATTN_BLOCK, n_keys), 1)
    dist = qi + (n_keys - ATTN_BLOCK) - ki
    band = (dist >= 0) & (dist <= n_back)
    bias_full = jnp.where(band, 0.0, -jnp.inf).astype(F32)
    bias_first = jnp.where(band & (ki >= ATTN_BLOCK), 0.0, -jnp.inf).astype(F32)
    lane = lax.broadcasted_iota(jnp.int32, (ATTN_BLOCK, LANES), 1)
    low_half = lane < ATTN_HEAD_DIM
    scale = ATTN_HEAD_DIM ** -0.5

    def block_body(fi, carry):
        chains = []
        for bi in range(blocks_per_iter):
            f = fi * blocks_per_iter + bi
            q0 = f * ATTN_BLOCK if isinstance(f, int) else pl.multiple_of(f * ATTN_BLOCK, ATTN_BLOCK)
            if prev_block:
                k0 = max(f - 1, 0) * ATTN_BLOCK
                bias = bias_full if f % nb > 0 else bias_first
            else:
                bias = bias_full
            for pair in range(n_pairs):
                qp = load(q_ref, qs, pair, q0) * scale
                kp = load(k_ref, ks, pair, q0)
                vp = load(v_ref, vs, pair, q0)
                if prev_block:
                    kp = jnp.concatenate([load(k_ref, ks, pair, k0), kp], axis=0)
                    vp = jnp.concatenate([load(v_ref, vs, pair, k0), vp], axis=0)
                for half in range(2):
                    sel = low_half if half == 0 else ~low_half
                    chains.append(dict(q0=q0, pair=pair, half=half, bias=bias, kp=kp, vp=vp,
                                       qm=jnp.where(sel, qp, jnp.zeros_like(qp))))
        for ch in chains:
            ch["sc"] = _mm_nt(ch.pop("qm"), ch.pop("kp"))
        for ch in chains:
            sc = ch.pop("sc") + ch.pop("bias")
            m = jnp.max(sc, axis=-1, keepdims=True)
            p = jnp.exp(sc - m)
            ch["denom"] = jnp.sum(p, axis=-1, keepdims=True)
            ch["m"] = m
            ch["p"] = p.astype(BF16)
        for ch in chains:
            ch["pv"] = _mm(ch.pop("p"), ch.pop("vp"))
        for c0 in range(0, len(chains), 2):
            lo, hi = chains[c0], chains[c0 + 1]
            o = jnp.where(low_half, lo["pv"] / lo["denom"], hi["pv"] / hi["denom"])
            lse = jnp.where(low_half, lo["m"] + jnp.log(lo["denom"]), hi["m"] + jnp.log(hi["denom"]))
            q0, pair = lo["q0"], lo["pair"]
            if dil > 1:
                os_[pair, pl.ds(q0, ATTN_BLOCK), :] = o
                ls[pair, pl.ds(q0, ATTN_BLOCK), :] = lse
            else:
                o_ref[0, pl.ds(q0, ATTN_BLOCK), lanes(pair)] = o.astype(o_ref.dtype)
                lse_ref[0, pl.ds(q0, ATTN_BLOCK), lanes(pair)] = lse
        return carry

    n_iters = dil * nb // blocks_per_iter
    if prev_block:
        for fi in range(n_iters):
            block_body(fi, 0)
    else:
        lax.fori_loop(0, n_iters, block_body, 0)

    if dil > 1:
        for pair in range(n_pairs):
            for src, dst in ((os_, o_ref), (ls, lse_ref)):
                interleave(src, pair)
                dst[0, :, lanes(pair)] = stage[...].astype(dst.dtype)


def _dilated_attention(p, group, col_base):
    b, s, _ = p.shape
    window, dil = DIL_CONFIGS[group]
    gw = ATTN_GROUP_WIDTH
    blk = col_base // gw + group
    step = ATTN_WIDTH // gw
    kern = functools.partial(_attn_kernel, dil=dil, n_back=window // dil)
    return pl.pallas_call(
        kern,
        grid=(b,),
        in_specs=[pl.BlockSpec((1, s, gw), lambda bi: (bi, 0, blk)),
                  pl.BlockSpec((1, s, gw), lambda bi: (bi, 0, blk + step)),
                  pl.BlockSpec((1, s, gw), lambda bi: (bi, 0, blk + 2 * step))],
        out_specs=[pl.BlockSpec((1, s, gw), lambda bi: (bi, 0, 0)),
                   pl.BlockSpec((1, s, gw), lambda bi: (bi, 0, 0))],
        out_shape=[jax.ShapeDtypeStruct((b, s, gw), BF16),
                   jax.ShapeDtypeStruct((b, s, gw), F32)],
        scratch_shapes=([] if dil == 1 else
                        [pltpu.VMEM((s, LANES), F32) for _ in range(2)]
                        + [pltpu.VMEM((gw // LANES, s, LANES), BF16) for _ in range(3)]
                        + [pltpu.VMEM((gw // LANES, s, LANES), F32) for _ in range(2)]),
        compiler_params=pltpu.CompilerParams(
            dimension_semantics=("parallel",), vmem_limit_bytes=VMEM_LIMIT),
        name=f"dilated_attn_g{group}",
    )(p, p, p)


MERGE_SUBTILES = 8


def _merge_kernel(x_ref, mod_ref, oa_ref, za_ref, ga_ref, gb_ref, zb_ref,
                  ob0_ref, ob1_ref, ob2_ref, l0_ref, l1_ref, l2_ref,
                  dnw_ref, pa_ref, pb_ref, wo_ref, fw_ref, out_ref):
    def silu_of_half(h):
        return h + h * jnp.tanh(h)

    gate = mod_ref[0][2:3]

    def stages(rows):
        oa = oa_ref[0, rows, :].astype(F32)
        za_half = za_ref[0, rows, :].astype(F32)
        parts = []
        for h in range(DN_HEADS):
            cq = slice(h * DN_DV, (h + 1) * DN_DV)
            blk = oa[:, cq]
            y = blk * lax.rsqrt(jnp.mean(blk * blk, axis=-1, keepdims=True) + EPS) * dnw_ref[...]
            parts.append(y * silu_of_half(za_half[:, cq]))
        ya_half = _mm(jnp.concatenate(parts, axis=1), pa_ref[...])
        yield

        l0, l1, l2 = l0_ref[0, rows, :], l1_ref[0, rows, :], l2_ref[0, rows, :]
        m = jnp.maximum(jnp.maximum(l0, l1), l2)
        e0, e1, e2 = jnp.exp(l0 - m), jnp.exp(l1 - m), jnp.exp(l2 - m)
        inv_den = 1.0 / (e0 + e1 + e2)
        ob = jnp.concatenate([ob0_ref[0, rows, :].astype(F32) * (e0 * inv_den),
                              ob1_ref[0, rows, :].astype(F32) * (e1 * inv_den),
                              ob2_ref[0, rows, :].astype(F32) * (e2 * inv_den)], axis=1)
        yb_half = _mm(ob * silu_of_half(zb_ref[0, rows, :].astype(F32)), pb_ref[...])
        yield

        merged = (ya_half + ya_half * jnp.tanh(ga_ref[0, rows, :].astype(F32))
                  + (yb_half + yb_half * jnp.tanh(gb_ref[0, rows, :].astype(F32))))
        delta = _mm(merged, wo_ref[...])
        yield

        xo = x_ref[0, rows, :] + gate * delta
        out_ref[0, rows, :] = xo * lax.rsqrt(jnp.mean(xo * xo, axis=-1, keepdims=True) + EPS) * fw_ref[...]
        yield

    tm = x_ref.shape[1]
    sub = tm // MERGE_SUBTILES
    _run_round_robin(*[stages(slice(i * sub, (i + 1) * sub)) for i in range(MERGE_SUBTILES)])


def _merge(x, mod3, o_a, p, obs, lses, dn_norm_w, pa, pb, wo, final_w, tm, cols):
    b, s, d = x.shape
    gw = ATTN_GROUP_WIDTH

    def row_spec(width, blk):
        return pl.BlockSpec((1, tm, width), lambda bi, i: (bi, i, blk))

    def full_spec(shape):
        return pl.BlockSpec(shape, lambda bi, i: (0,) * len(shape), pipeline_mode=pl.Buffered(1))

    return pl.pallas_call(
        _merge_kernel,
        grid=(b, s // tm),
        in_specs=[row_spec(d, 0),
                  pl.BlockSpec((1, 3, d), lambda bi, i: (bi, 0, 0)),
                  row_spec(DN_WIDTH, 0),
                  row_spec(DN_WIDTH, cols["za"] // DN_WIDTH),
                  row_spec(d, cols["ga"] // d),
                  row_spec(d, cols["gb"] // d),
                  row_spec(ATTN_WIDTH, cols["zb"] // ATTN_WIDTH),
                  row_spec(gw, 0), row_spec(gw, 0), row_spec(gw, 0),
                  row_spec(gw, 0), row_spec(gw, 0), row_spec(gw, 0),
                  full_spec((1, DN_DV)), full_spec(pa.shape), full_spec(pb.shape), full_spec(wo.shape),
                  full_spec((1, d))],
        out_specs=row_spec(d, 0),
        out_shape=jax.ShapeDtypeStruct((b, s, d), F32),
        compiler_params=pltpu.CompilerParams(
            dimension_semantics=("parallel", "parallel"), vmem_limit_bytes=VMEM_LIMIT),
        name="merge_outproj",
    )(x, mod3, o_a, p, p, p, p, *obs, *lses, dn_norm_w.reshape(1, DN_DV), pa, pb, wo, final_w.reshape(1, d))


W_PREP_COLS = 128


def _wprep_kernel(wt_ref, wc_ref, wp_ref, ws_ref, *, conv_width, segments, small_off):
    wc_ref[...] = wt_ref[0:conv_width, :].astype(BF16)
    for src, width, dst, scale in segments:
        blk = wt_ref[src:src + width, :]
        wp_ref[dst:dst + width, :] = (blk if scale == 1.0 else blk * scale).astype(BF16)
    n_small = 2 * DN_HEADS
    ws_ref[0:n_small, :] = wt_ref[small_off:small_off + n_small, :].astype(BF16)
    ws_ref[n_small:LANES, :] = jnp.zeros((LANES - n_small, wt_ref.shape[1]), BF16)


def _weight_prep(wt, conv_width, plain_width, segments, small_off):
    n_in, d = wt.shape
    assert d % W_PREP_COLS == 0 and small_off % 16 == 0 and all(sg[0] % 16 == 0 for sg in segments)
    kern = functools.partial(_wprep_kernel, conv_width=conv_width, segments=segments, small_off=small_off)
    return pl.pallas_call(
        kern,
        grid=(d // W_PREP_COLS,),
        in_specs=[pl.BlockSpec((n_in, W_PREP_COLS), lambda i: (0, i))],
        out_specs=[pl.BlockSpec((conv_width, W_PREP_COLS), lambda i: (0, i)),
                   pl.BlockSpec((plain_width, W_PREP_COLS), lambda i: (0, i)),
                   pl.BlockSpec((LANES, W_PREP_COLS), lambda i: (0, i))],
        out_shape=[jax.ShapeDtypeStruct((conv_width, d), BF16),
                   jax.ShapeDtypeStruct((plain_width, d), BF16),
                   jax.ShapeDtypeStruct((LANES, d), BF16)],
        compiler_params=pltpu.CompilerParams(dimension_semantics=("parallel",), vmem_limit_bytes=VMEM_LIMIT),
        name="weight_prep",
    )(wt)


def _layer(x, c, norm_w, ada_w, ada_b, w_in, conv_w, a_log, dt_bias, dn_norm_w, w_proj_a, w_proj_b, w_out,
           final_norm_w):
    b, s, d = x.shape
    qk = DN_HEADS * DN_DK
    splits = (qk, qk, DN_WIDTH, DN_WIDTH, DN_HEADS, DN_HEADS, ATTN_WIDTH, ATTN_WIDTH, ATTN_WIDTH, ATTN_WIDTH, d, d)
    names = ("qa", "ka", "va", "za", "beta", "a", "qb", "kb", "vb", "zb", "ga", "gb")
    offs = dict(zip(names, np.cumsum((0,) + splits[:-1]).tolist()))
    widths = dict(zip(names, splits))
    order = ("za", "ga", "gb", "qb", "kb", "vb", "zb")
    cols, pos = {}, 0
    for n in order:
        assert pos % widths[n] == 0, (n, pos)
        cols[n] = pos
        pos += widths[n]
    assert offs["qa"] == 0 and offs["ka"] == qk and offs["va"] == 2 * qk and offs["a"] == offs["beta"] + DN_HEADS
    halved = ("za", "ga", "gb", "zb")
    segments = tuple((offs[n], widths[n], cols[n], 0.5 if n in halved else 1.0) for n in order)
    w_conv, w_plain, w_small = _weight_prep(w_in.T, 2 * qk + DN_WIDTH, pos, segments, offs["beta"])
    pad_lo = jnp.zeros((DN_HEADS,), F32)
    pad_hi = jnp.zeros((LANES - 2 * DN_HEADS,), F32)
    alog_row = jnp.concatenate([pad_lo, a_log.astype(F32), pad_hi]).reshape(1, LANES)
    dtb_row = jnp.concatenate([pad_lo, dt_bias.astype(F32), pad_hi]).reshape(1, LANES)

    mod3 = _adaln_mod(c, ada_w, ada_b).reshape(b, 3, d)
    p_conv, p, small = _inproj(x, mod3, norm_w, w_conv, w_plain, w_small, conv_w)
    o_a = _deltanet(p_conv, small, alog_row, dtb_row, ts=min(s, 1024), group=4)
    obs, lses = [], []
    for g in range(len(DIL_CONFIGS)):
        o_g, lse_g = _dilated_attention(p, g, cols["qb"])
        obs.append(o_g)
        lses.append(lse_g)
    return _merge(x, mod3, o_a, p, obs, lses, dn_norm_w, (w_proj_a * 0.5).astype(BF16),
                  (w_proj_b * 0.5).astype(BF16), w_out.astype(BF16), final_norm_w, tm=min(s, 1024), cols=cols)


def kernel(x, c, norm_w, ada_w, ada_b, w_in, conv_w, a_log, dt_bias, dn_norm_w, w_proj_a, w_proj_b, w_out,
           final_norm_w):
    depth = norm_w.shape[0]
    assert depth == 1, "the final RMSNorm is fused into the single layer's output kernel"
    return _layer(x, c, norm_w[0], ada_w[0], ada_b[0], w_in[0], conv_w[0], a_log[0], dt_bias[0], dn_norm_w[0],
                  w_proj_a[0], w_proj_b[0], w_out[0], final_norm_w)
```

```python
import functools

import jax
import jax.numpy as jnp
import numpy as np
from jax import lax
from jax.experimental import pallas as pl
from jax.experimental.pallas import tpu as pltpu

F32 = jnp.float32
BF16 = jnp.bfloat16
HIGHEST = lax.Precision.HIGHEST

EPS = 1e-6
DN_HEADS = 8
DN_DK = 128
DN_DV = 128
DN_WIDTH = DN_HEADS * DN_DV
CONV_K = 4
CHUNK = 64
DIL_CONFIGS = ((128, 1), (512, 4), (2048, 16))
HEADS_PER_GROUP = 4
ATTN_HEAD_DIM = 64
ATTN_GROUP_WIDTH = HEADS_PER_GROUP * ATTN_HEAD_DIM
ATTN_WIDTH = ATTN_GROUP_WIDTH * len(DIL_CONFIGS)
ATTN_BLOCK = 128
LANES = 128
HALO = 8
VMEM_LIMIT = 56 * 1024 * 1024


def _mm(a, b):
    return jnp.dot(a.astype(BF16), b.astype(BF16), preferred_element_type=F32)


def _mm_nt(a, b):
    return lax.dot_general(a.astype(BF16), b.astype(BF16), (((1,), (1,)), ((), ())),
                           preferred_element_type=F32)


def _mm_tn(a, b):
    return lax.dot_general(a.astype(BF16), b.astype(BF16), (((0,), (0,)), ((), ())),
                           preferred_element_type=F32)


def _run_round_robin(*gens):
    gens = list(gens)
    while gens:
        for g in list(gens):
            if next(g, StopIteration) is StopIteration:
                gens.remove(g)


def _sigmoid(x):
    return 0.5 * jnp.tanh(0.5 * x) + 0.5


def _silu(x):
    return x * _sigmoid(x)


def _mod_kernel(c_ref, w_ref, b_ref, o_ref):
    sc = _silu(c_ref[...])
    o_ref[...] = jnp.dot(sc, w_ref[...], preferred_element_type=F32, precision=HIGHEST) + b_ref[...]


def _adaln_mod(c, ada_w, ada_b):
    b, d = c.shape
    return pl.pallas_call(
        _mod_kernel,
        grid=(3,),
        in_specs=[pl.BlockSpec((b, d), lambda j: (0, 0)),
                  pl.BlockSpec((d, d), lambda j: (0, j)),
                  pl.BlockSpec((1, d), lambda j: (0, j))],
        out_specs=pl.BlockSpec((b, d), lambda j: (0, j)),
        out_shape=jax.ShapeDtypeStruct((b, 3 * d), F32),
        name="adaln_mod",
    )(c, ada_w, ada_b.reshape(1, 3 * d))


CONV_ROWS = 64
CONV_SUB = 256
SUBLANES = 8
TOKEN_CHAINS = 3


def _conv_unit(buf, r0, cl, half_taps, shift_masks, norm_scale, skip_norm):
    a = buf[r0:r0 + HALO + CONV_ROWS, cl]
    n = CONV_ROWS // SUBLANES
    vs = [a[SUBLANES * i:SUBLANES * (i + 1)] for i in range(n + 1)]
    acc = [vs[i + 1] * half_taps[CONV_K - 1] for i in range(n)]
    for shift in range(1, CONV_K):
        rs = [pltpu.roll(v, shift, 0) for v in vs]
        tap = half_taps[CONV_K - 1 - shift]
        for i in range(n):
            acc[i] = acc[i] + jnp.where(shift_masks[shift], rs[i], rs[i + 1]) * tap
    h = jnp.concatenate(acc, axis=0)
    y = h + h * jnp.tanh(h)
    ss = jnp.sum(y * y, axis=-1, keepdims=True)
    out = y * jnp.where(skip_norm, 1.0, lax.rsqrt(ss + EPS) * norm_scale)
    token = jnp.broadcast_to(ss[CONV_ROWS - SUBLANES:], (SUBLANES, LANES))
    return out.astype(BF16), token


def _inproj_kernel(x_ref, mod_ref, nw_ref, wc_ref, wp_ref, ws_ref, cw_ref, pc_ref, pp_ref, small_ref,
                   h_scr, cbuf, tok, *, steps_per_seg):
    j = pl.program_id(1)
    s = x_ref.shape[1]

    def prologue():
        x = x_ref[0]
        m = mod_ref[0]
        y = x * lax.rsqrt(jnp.mean(x * x, axis=-1, keepdims=True) + EPS) * nw_ref[...]
        h = (y * (1.0 + m[1:2]) + m[0:1]).astype(BF16)
        h_scr[...] = h
        small_ref[0] = _mm_nt(h, ws_ref[...])
        cbuf[0:HALO, :] = jnp.zeros((HALO, CONV_SUB), F32)

    def step_body():
        cbuf[HALO:HALO + s, :] = _mm_nt(h_scr[...], wc_ref[...])
        pp_ref[0] = _mm_nt(h_scr[...], wp_ref[...]).astype(BF16)

        sub = lax.broadcasted_iota(jnp.int32, (SUBLANES, LANES), 0)
        shift_masks = [sub < shift for shift in range(CONV_K)]
        q_scale = jnp.where(j < steps_per_seg, DN_DK ** -0.5, 1.0)
        is_v = j >= 2 * steps_per_seg
        never = j < 0
        for c in range(TOKEN_CHAINS):
            tok[c] = jnp.zeros((SUBLANES, LANES), F32)
        n_unit = 0
        for c0 in range(0, CONV_SUB, DN_DK):
            cl = slice(c0, c0 + DN_DK)
            half_taps = [cw_ref[t:t + 1, cl] * 0.5 for t in range(CONV_K)]
            for r0 in range(0, s, CONV_ROWS):
                slot = n_unit % TOKEN_CHAINS
                held = tok[slot][0:1]
                taps = [jnp.where(never, held, tap) for tap in half_taps]
                out, token = _conv_unit(cbuf, r0, cl, taps, shift_masks, q_scale, is_v)
                pc_ref[0, r0:r0 + CONV_ROWS, cl] = out
                tok[slot] = token
                n_unit += 1

    @pl.when(j == 0)
    def _():
        prologue()
        step_body()

    @pl.when(j > 0)
    def _():
        step_body()


def _inproj(x, mod3, norm_w, w_conv, w_plain, w_small, conv_w):
    b, s, d = x.shape
    conv_width, plain_width = w_conv.shape[0], w_plain.shape[0]
    n_steps = conv_width // CONV_SUB
    tp = plain_width // n_steps
    assert n_steps % 3 == 0 and n_steps * CONV_SUB == conv_width and conv_w.shape[1] == conv_width
    assert tp * n_steps == plain_width and tp % LANES == 0 and s % CONV_ROWS == 0
    kern = functools.partial(_inproj_kernel, steps_per_seg=n_steps // 3)
    return pl.pallas_call(
        kern,
        grid=(b, n_steps),
        in_specs=[pl.BlockSpec((1, s, d), lambda bi, j: (bi, 0, 0)),
                  pl.BlockSpec((1, 3, d), lambda bi, j: (bi, 0, 0)),
                  pl.BlockSpec((1, d), lambda bi, j: (0, 0)),
                  pl.BlockSpec((CONV_SUB, d), lambda bi, j: (j, 0)),
                  pl.BlockSpec((tp, d), lambda bi, j: (j, 0)),
                  pl.BlockSpec((LANES, d), lambda bi, j: (0, 0)),
                  pl.BlockSpec((CONV_K, CONV_SUB), lambda bi, j: (0, j))],
        out_specs=[pl.BlockSpec((1, s, CONV_SUB), lambda bi, j: (bi, 0, j)),
                   pl.BlockSpec((1, s, tp), lambda bi, j: (bi, 0, j)),
                   pl.BlockSpec((1, s, LANES), lambda bi, j: (bi, 0, 0))],
        out_shape=[jax.ShapeDtypeStruct((b, s, conv_width), BF16),
                   jax.ShapeDtypeStruct((b, s, plain_width), BF16),
                   jax.ShapeDtypeStruct((b, s, LANES), F32)],
        scratch_shapes=[pltpu.VMEM((s, d), BF16),
                        pltpu.VMEM((HALO + s, CONV_SUB), F32),
                        pltpu.VMEM((TOKEN_CHAINS, SUBLANES, LANES), F32)],
        compiler_params=pltpu.CompilerParams(
            dimension_semantics=("parallel", "arbitrary"), vmem_limit_bytes=VMEM_LIMIT),
        name="norm_inproj",
    )(x, mod3, norm_w.reshape(1, d), w_conv, w_plain, w_small, conv_w)


def _level_mask(level, i, j):
    same_block = (i >> (level + 1)) == (j >> (level + 1))
    return same_block & (((i >> level) & 1) == 1) & (((j >> level) & 1) == 0)


N_LEVELS = CHUNK.bit_length() - 1


def _deltanet_kernel(q_ref, k_ref, v_ref, small_ref, alog_ref, dtb_ref, o_ref,
                     state, u_s, wq_s, akd_s, el_s, *, ts, group):
    @pl.when(pl.program_id(1) == 0)
    def _():
        state[...] = jnp.zeros_like(state)

    row = lax.broadcasted_iota(jnp.int32, (CHUNK, 2 * CHUNK), 0)
    lane = lax.broadcasted_iota(jnp.int32, (CHUNK, 2 * CHUNK), 1)
    col = lane & (CHUNK - 1)
    in_a = lane < CHUNK
    causal = row >= col
    tri_strict = (row > col).astype(F32)
    eye = (row == col).astype(F32)
    row_c = lax.broadcasted_iota(jnp.int32, (CHUNK, CHUNK), 0)
    col_c = lax.broadcasted_iota(jnp.int32, (CHUNK, CHUNK), 1)
    tri = (row_c >= col_c).astype(F32)
    level_masks = [_level_mask(level, row, col).astype(F32) for level in range(N_LEVELS)]
    neg_a = -jnp.exp(alog_ref[...])
    dtb = dtb_ref[...]
    n_chunks = ts // CHUNK
    n_pairs = DN_HEADS // 2
    zeros_wide = jnp.zeros((CHUNK, DN_DV + DN_DK), BF16)
    zeros_head = jnp.zeros((CHUNK, DN_DK), BF16)

    def blockdiag(m):
        return jnp.concatenate([jnp.where(in_a, m, 0.0), jnp.where(in_a, 0.0, m)], axis=0).astype(BF16)

    def prep_stages(gi):
        chains = []
        for ci in range(group):
            c = gi * group + ci
            r0 = c * CHUNK if isinstance(c, int) else pl.multiple_of(c * CHUNK, CHUNK)
            sm = small_ref[0, pl.ds(r0, CHUNK), :]
            beta_all = _sigmoid(sm)
            z = sm + dtb
            softplus = jnp.maximum(z, 0.0) + jnp.log(1.0 + jnp.exp(-jnp.abs(z)))
            g_all = neg_a * softplus
            gc_all = jnp.dot(tri, g_all, preferred_element_type=F32, precision=HIGHEST)
            gc_t = jnp.concatenate([gc_all, gc_all], axis=0).T
            g_last = gc_all[CHUNK - 1:CHUNK, :]
            e_in_all = jnp.exp(gc_all)
            e_out_all = jnp.exp(g_last - gc_all)
            slot = (gi % 2) * group + ci
            el_s[slot] = jnp.broadcast_to(jnp.exp(g_last), (8, LANES))
            for p in range(n_pairs):
                heads = (2 * p, 2 * p + 1)
                q, k, k_beta, rhs, k_dec = [], [], [], [], []
                for h in heads:
                    gl = DN_HEADS + h
                    cq = slice(h * DN_DK, (h + 1) * DN_DK)
                    q_h = q_ref[0, pl.ds(r0, CHUNK), cq].astype(F32)
                    k_h = k_ref[0, pl.ds(r0, CHUNK), cq].astype(F32)
                    v_h = v_ref[0, pl.ds(r0, CHUNK), slice(h * DN_DV, (h + 1) * DN_DV)].astype(F32)
                    beta = beta_all[:, h:h + 1]
                    e_in = e_in_all[:, gl:gl + 1]
                    kb_h = k_h * beta
                    q.append(q_h)
                    k.append(k_h.astype(BF16))
                    k_beta.append(kb_h)
                    rhs.append(jnp.concatenate([v_h * beta, kb_h * e_in], axis=1).astype(BF16))
                    k_dec.append(k_h * e_out_all[:, gl:gl + 1])
                    wq_s[slot, h, CHUNK:2 * CHUNK, :] = (q_h * e_in).astype(BF16)
                ga, gb = DN_HEADS + heads[0], DN_HEADS + heads[1]
                gc_col = jnp.where(in_a, gc_all[:, ga:ga + 1], gc_all[:, gb:gb + 1])
                gc_row = jnp.where(in_a[0:1], gc_t[ga:ga + 1, :], gc_t[gb:gb + 1, :])
                chains.append(dict(
                    slot=slot, p=p,
                    decay=jnp.exp(jnp.where(causal, gc_col - gc_row, -jnp.inf)),
                    kq=jnp.concatenate([jnp.concatenate(k_beta, axis=1), jnp.concatenate(q, axis=1)],
                                       axis=0).astype(BF16),
                    kk=jnp.concatenate([jnp.concatenate([k[0], zeros_head], axis=1),
                                        jnp.concatenate([zeros_head, k[1]], axis=1)], axis=0),
                    rhs=jnp.concatenate([jnp.concatenate([rhs[0], zeros_wide], axis=1),
                                         jnp.concatenate([zeros_wide, rhs[1]], axis=1)], axis=0),
                    k_dec_t=jnp.concatenate(k_dec, axis=0).T.astype(BF16)))

        for ch in chains:
            ch["kk_qk"] = _mm_nt(ch.pop("kq"), ch.pop("kk"))
        yield
        for ch in chains:
            kk_qk = ch.pop("kk_qk")
            decay = ch.pop("decay")
            ch["lmat"] = kk_qk[0:CHUNK] * decay * tri_strict
            ch["a_qk"] = (kk_qk[CHUNK:2 * CHUNK] * decay).astype(BF16)
            ch["x"] = eye - ch["lmat"] * level_masks[0]
        for level in range(1, N_LEVELS):
            for ch in chains:
                ch["y"] = _mm(ch["lmat"] * level_masks[level], blockdiag(ch["x"]))
            yield
            for ch in chains:
                ch["x"] = ch["x"] - _mm(ch["x"], blockdiag(ch.pop("y")))
            yield
        for ch in chains:
            ch["uw"] = _mm(ch.pop("x"), ch.pop("rhs"))
        for ch in chains:
            slot, p = ch["slot"], ch["p"]
            uw = ch.pop("uw")
            for i, h in enumerate((2 * p, 2 * p + 1)):
                base = i * (DN_DV + DN_DK)
                u_s[slot, h] = uw[:, base:base + DN_DV]
                wq_s[slot, h, 0:CHUNK, :] = uw[:, base + DN_DV:base + DN_DV + DN_DK].astype(BF16)
            akd_s[slot, p] = jnp.concatenate([ch["a_qk"], ch["k_dec_t"]], axis=0)
        yield

    def rec_stages(gi):
        for ci in range(group):
            c = gi * group + ci
            r0 = c * CHUNK if isinstance(c, int) else pl.multiple_of(c * CHUNK, CHUNK)
            slot = (gi % 2) * group + ci
            e_last_all = el_s[slot]
            rs = [_mm(wq_s[slot, h], state[h]) for h in range(DN_HEADS)]
            yield
            v_new = [(u_s[slot, h] - rs[h][0:CHUNK]).astype(BF16) for h in range(DN_HEADS)]
            zeros_v = jnp.zeros((CHUNK, DN_DV), BF16)
            av = [_mm(akd_s[slot, p],
                      jnp.concatenate([jnp.concatenate([v_new[2 * p], zeros_v], axis=1),
                                       jnp.concatenate([zeros_v, v_new[2 * p + 1]], axis=1)], axis=0))
                  for p in range(n_pairs)]
            for h in range(DN_HEADS):
                gl = DN_HEADS + h
                lanes_h = slice((h % 2) * DN_DV, (h % 2 + 1) * DN_DV)
                o = rs[h][CHUNK:2 * CHUNK] + av[h // 2][0:CHUNK, lanes_h]
                state[h] = state[h] * e_last_all[0:1, gl:gl + 1] + av[h // 2][CHUNK:CHUNK + DN_DK, lanes_h]
                o_ref[0, pl.ds(r0, CHUNK), slice(h * DN_DV, (h + 1) * DN_DV)] = o.astype(o_ref.dtype)
            yield

    run = _run_round_robin

    n_groups = n_chunks // group
    run(prep_stages(0))
    for gi in range(1, n_groups):
        run(prep_stages(gi), rec_stages(gi - 1))
    run(rec_stages(n_groups - 1))


def _deltanet(p, small, alog_row, dtb_row, ts, group):
    b, s, _ = p.shape
    width = DN_HEADS * DN_DK
    n_slots = 2 * group
    assert (ts // CHUNK) % group == 0
    kern = functools.partial(_deltanet_kernel, ts=ts, group=group)
    return pl.pallas_call(
        kern,
        grid=(b, s // ts),
        in_specs=[pl.BlockSpec((1, ts, width), lambda bi, t: (bi, t, 0)),
                  pl.BlockSpec((1, ts, width), lambda bi, t: (bi, t, 1)),
                  pl.BlockSpec((1, ts, width), lambda bi, t: (bi, t, 2)),
                  pl.BlockSpec((1, ts, LANES), lambda bi, t: (bi, t, 0)),
                  pl.BlockSpec((1, LANES), lambda bi, t: (0, 0)),
                  pl.BlockSpec((1, LANES), lambda bi, t: (0, 0))],
        out_specs=pl.BlockSpec((1, ts, DN_WIDTH), lambda bi, t: (bi, t, 0)),
        out_shape=jax.ShapeDtypeStruct((b, s, DN_WIDTH), BF16),
        scratch_shapes=[pltpu.VMEM((DN_HEADS, DN_DK, DN_DV), F32),
                        pltpu.VMEM((n_slots, DN_HEADS, CHUNK, DN_DV), F32),
                        pltpu.VMEM((n_slots, DN_HEADS, 2 * CHUNK, DN_DK), BF16),
                        pltpu.VMEM((n_slots, DN_HEADS // 2, CHUNK + DN_DK, 2 * CHUNK), BF16),
                        pltpu.VMEM((n_slots, 8, LANES), F32)],
        compiler_params=pltpu.CompilerParams(
            dimension_semantics=("parallel", "arbitrary"), vmem_limit_bytes=VMEM_LIMIT),
        name="gated_deltanet",
    )(p, p, p, small, alog_row, dtb_row)


ATTN_BLOCKS_UNROLLED = 2
ATTN_BLOCKS_ROLLED = 4
ATTN_STRIDE = 4


def _attn_kernel(q_ref, k_ref, v_ref, o_ref, lse_ref, *scratch, dil, n_back):
    s = q_ref.shape[1]
    sub_len = s // dil
    nb = sub_len // ATTN_BLOCK
    n_pairs = ATTN_GROUP_WIDTH // LANES
    prev_block = nb > 1
    blocks_per_iter = ATTN_BLOCKS_UNROLLED if prev_block else ATTN_BLOCKS_ROLLED

    def lanes(pair):
        return slice(pair * LANES, (pair + 1) * LANES)

    if dil > 1:
        stage, tmp, qs, ks, vs, os_, ls = scratch
        assert dil in (ATTN_STRIDE, ATTN_STRIDE * ATTN_STRIDE)
        q_len = s // ATTN_STRIDE

        def classes():
            if dil == ATTN_STRIDE:
                return [(r, stage, r) for r in range(dil)]
            return [(r_lo + ATTN_STRIDE * r_hi, tmp, r_lo * q_len + r_hi)
                    for r_lo in range(ATTN_STRIDE) for r_hi in range(ATTN_STRIDE)]

        def deinterleave(dst, pair):
            if dil > ATTN_STRIDE:
                for r_lo in range(ATTN_STRIDE):
                    tmp[r_lo * q_len:(r_lo + 1) * q_len, :] = stage[pl.ds(r_lo, q_len, stride=ATTN_STRIDE), :]
            for r, buf, start in classes():
                dst[pair, r * sub_len:(r + 1) * sub_len, :] = (
                    buf[pl.ds(start, sub_len, stride=ATTN_STRIDE), :].astype(dst.dtype))

        def interleave(src, pair):
            for r, buf, start in classes():
                buf[pl.ds(start, sub_len, stride=ATTN_STRIDE), :] = src[pair, r * sub_len:(r + 1) * sub_len, :]
            if dil > ATTN_STRIDE:
                for r_lo in range(ATTN_STRIDE):
                    stage[pl.ds(r_lo, q_len, stride=ATTN_STRIDE), :] = tmp[r_lo * q_len:(r_lo + 1) * q_len, :]

        for src, dst in ((q_ref, qs), (k_ref, ks), (v_ref, vs)):
            for pair in range(n_pairs):
                stage[...] = src[0, :, lanes(pair)].astype(F32)
                deinterleave(dst, pair)

        def load(ref, src, pair, r0):
            return src[pair, pl.ds(r0, ATTN_BLOCK), :]
    else:
        qs, ks, vs = q_ref, k_ref, v_ref

        def load(ref, src, pair, r0):
            return ref[0, pl.ds(r0, ATTN_BLOCK), lanes(pair)]

    n_keys = 2 * ATTN_BLOCK if prev_block else ATTN_BLOCK
    qi = lax.broadcasted_iota(jnp.int32, (ATTN_BLOCK, n_keys), 0)
    ki = lax.broadcasted_iota(jnp.int32, (ATTN_BLOCK, n_keys), 1)
    dist = qi + (n_keys - ATTN_BLOCK) - ki
    band = (dist >= 0) & (dist <= n_back)
    bias_full = jnp.where(band, 0.0, -jnp.inf).astype(F32)
    bias_first = jnp.where(band & (ki >= ATTN_BLOCK), 0.0, -jnp.inf).astype(F32)
    lane = lax.broadcasted_iota(jnp.int32, (ATTN_BLOCK, LANES), 1)
    low_half = lane < ATTN_HEAD_DIM
    scale = ATTN_HEAD_DIM ** -0.5

    def block_body(fi, carry):
        chains = []
        for bi in range(blocks_per_iter):
            f = fi * blocks_per_iter + bi
            q0 = f * ATTN_BLOCK if isinstance(f, int) else pl.multiple_of(f * ATTN_BLOCK, ATTN_BLOCK)
            if prev_block:
                k0 = max(f - 1, 0) * ATTN_BLOCK
                bias = bias_full if f % nb > 0 else bias_first
            else:
                bias = bias_full
            for pair in range(n_pairs):
                qp = load(q_ref, qs, pair, q0) * scale
                kp = load(k_ref, ks, pair, q0)
                vp = load(v_ref, vs, pair, q0)
                if prev_block:
                    kp = jnp.concatenate([load(k_ref, ks, pair, k0), kp], axis=0)
                    vp = jnp.concatenate([load(v_ref, vs, pair, k0), vp], axis=0)
                for half in range(2):
                    sel = low_half if half == 0 else ~low_half
                    chains.append(dict(q0=q0, pair=pair, half=half, bias=bias, kp=kp, vp=vp,
                                       qm=jnp.where(sel, qp, jnp.zeros_like(qp))))
        for ch in chains:
            ch["sc"] = _mm_nt(ch.pop("qm"), ch.pop("kp"))
        for ch in chains:
            sc = ch.pop("sc") + ch.pop("bias")
            m = jnp.max(sc, axis=-1, keepdims=True)
            p = jnp.exp(sc - m)
            ch["denom"] = jnp.sum(p, axis=-1, keepdims=True)
            ch["m"] = m
            ch["p"] = p.astype(BF16)
        for ch in chains:
            ch["pv"] = _mm(ch.pop("p"), ch.pop("vp"))
        for c0 in range(0, len(chains), 2):
            lo, hi = chains[c0], chains[c0 + 1]
            o = jnp.where(low_half, lo["pv"] / lo["denom"], hi["pv"] / hi["denom"])
            lse = jnp.where(low_half, lo["m"] + jnp.log(lo["denom"]), hi["m"] + jnp.log(hi["denom"]))
            q0, pair = lo["q0"], lo["pair"]
            if dil > 1:
                os_[pair, pl.ds(q0, ATTN_BLOCK), :] = o
                ls[pair, pl.ds(q0, ATTN_BLOCK), :] = lse
            else:
                o_ref[0, pl.ds(q0, ATTN_BLOCK), lanes(pair)] = o.astype(o_ref.dtype)
                lse_ref[0, pl.ds(q0, ATTN_BLOCK), lanes(pair)] = lse
        return carry

    n_iters = dil * nb // blocks_per_iter
    if prev_block:
        for fi in range(n_iters):
            block_body(fi, 0)
    else:
        lax.fori_loop(0, n_iters, block_body, 0)

    if dil > 1:
        for pair in range(n_pairs):
            for src, dst in ((os_, o_ref), (ls, lse_ref)):
                interleave(src, pair)
                dst[0, :, lanes(pair)] = stage[...].astype(dst.dtype)


def _dilated_attention(p, group, col_base):
    b, s, _ = p.shape
    window, dil = DIL_CONFIGS[group]
    gw = ATTN_GROUP_WIDTH
    blk = col_base // gw + group
    step = ATTN_WIDTH // gw
    kern = functools.partial(_attn_kernel, dil=dil, n_back=window // dil)
    return pl.pallas_call(
        kern,
        grid=(b,),
        in_specs=[pl.BlockSpec((1, s, gw), lambda bi: (bi, 0, blk)),
                  pl.BlockSpec((1, s, gw), lambda bi: (bi, 0, blk + step)),
                  pl.BlockSpec((1, s, gw), lambda bi: (bi, 0, blk + 2 * step))],
        out_specs=[pl.BlockSpec((1, s, gw), lambda bi: (bi, 0, 0)),
                   pl.BlockSpec((1, s, gw), lambda bi: (bi, 0, 0))],
        out_shape=[jax.ShapeDtypeStruct((b, s, gw), BF16),
                   jax.ShapeDtypeStruct((b, s, gw), F32)],
        scratch_shapes=([] if dil == 1 else
                        [pltpu.VMEM((s, LANES), F32) for _ in range(2)]
                        + [pltpu.VMEM((gw // LANES, s, LANES), BF16) for _ in range(3)]
                        + [pltpu.VMEM((gw // LANES, s, LANES), F32) for _ in range(2)]),
        compiler_params=pltpu.CompilerParams(
            dimension_semantics=("parallel",), vmem_limit_bytes=VMEM_LIMIT),
        name=f"dilated_attn_g{group}",
    )(p, p, p)


MERGE_SUBTILES = 8


def _merge_kernel(x_ref, mod_ref, oa_ref, za_ref, ga_ref, gb_ref, zb_ref,
                  ob0_ref, ob1_ref, ob2_ref, l0_ref, l1_ref, l2_ref,
                  dnw_ref, pa_ref, pb_ref, wo_ref, fw_ref, out_ref):
    def silu_of_half(h):
        return h + h * jnp.tanh(h)

    gate = mod_ref[0][2:3]

    def stages(rows):
        oa = oa_ref[0, rows, :].astype(F32)
        za_half = za_ref[0, rows, :].astype(F32)
        parts = []
        for h in range(DN_HEADS):
            cq = slice(h * DN_DV, (h + 1) * DN_DV)
            blk = oa[:, cq]
            y = blk * lax.rsqrt(jnp.mean(blk * blk, axis=-1, keepdims=True) + EPS) * dnw_ref[...]
            parts.append(y * silu_of_half(za_half[:, cq]))
        ya_half = _mm(jnp.concatenate(parts, axis=1), pa_ref[...])
        yield

        l0, l1, l2 = l0_ref[0, rows, :], l1_ref[0, rows, :], l2_ref[0, rows, :]
        m = jnp.maximum(jnp.maximum(l0, l1), l2)
        e0, e1, e2 = jnp.exp(l0 - m), jnp.exp(l1 - m), jnp.exp(l2 - m)
        inv_den = 1.0 / (e0 + e1 + e2)
        ob = jnp.concatenate([ob0_ref[0, rows, :].astype(F32) * (e0 * inv_den),
                              ob1_ref[0, rows, :].astype(F32) * (e1 * inv_den),
                              ob2_ref[0, rows, :].astype(F32) * (e2 * inv_den)], axis=1)
        yb_half = _mm(ob * silu_of_half(zb_ref[0, rows, :].astype(F32)), pb_ref[...])
        yield

        merged = (ya_half + ya_half * jnp.tanh(ga_ref[0, rows, :].astype(F32))
                  + (yb_half + yb_half * jnp.tanh(gb_ref[0, rows, :].astype(F32))))
        delta = _mm(merged, wo_ref[...])
        yield

        xo = x_ref[0, rows, :] + gate * delta
        out_ref[0, rows, :] = xo * lax.rsqrt(jnp.mean(xo * xo, axis=-1, keepdims=True) + EPS) * fw_ref[...]
        yield

    tm = x_ref.shape[1]
    sub = tm // MERGE_SUBTILES
    _run_round_robin(*[stages(slice(i * sub, (i + 1) * sub)) for i in range(MERGE_SUBTILES)])


def _merge(x, mod3, o_a, p, obs, lses, dn_norm_w, pa, pb, wo, final_w, tm, cols):
    b, s, d = x.shape
    gw = ATTN_GROUP_WIDTH

    def row_spec(width, blk):
        return pl.BlockSpec((1, tm, width), lambda bi, i: (bi, i, blk))

    def full_spec(shape):
        return pl.BlockSpec(shape, lambda bi, i: (0,) * len(shape), pipeline_mode=pl.Buffered(1))

    return pl.pallas_call(
        _merge_kernel,
        grid=(b, s // tm),
        in_specs=[row_spec(d, 0),
                  pl.BlockSpec((1, 3, d), lambda bi, i: (bi, 0, 0)),
                  row_spec(DN_WIDTH, 0),
                  row_spec(DN_WIDTH, cols["za"] // DN_WIDTH),
                  row_spec(d, cols["ga"] // d),
                  row_spec(d, cols["gb"] // d),
                  row_spec(ATTN_WIDTH, cols["zb"] // ATTN_WIDTH),
                  row_spec(gw, 0), row_spec(gw, 0), row_spec(gw, 0),
                  row_spec(gw, 0), row_spec(gw, 0), row_spec(gw, 0),
                  full_spec((1, DN_DV)), full_spec(pa.shape), full_spec(pb.shape), full_spec(wo.shape),
                  full_spec((1, d))],
        out_specs=row_spec(d, 0),
        out_shape=jax.ShapeDtypeStruct((b, s, d), F32),
        compiler_params=pltpu.CompilerParams(
            dimension_semantics=("parallel", "parallel"), vmem_limit_bytes=VMEM_LIMIT),
        name="merge_outproj",
    )(x, mod3, o_a, p, p, p, p, *obs, *lses, dn_norm_w.reshape(1, DN_DV), pa, pb, wo, final_w.reshape(1, d))


W_PREP_COLS = 128


def _wprep_kernel(wt_ref, wc_ref, wp_ref, ws_ref, *, conv_width, segments, small_off):
    wc_ref[...] = wt_ref[0:conv_width, :].astype(BF16)
    for src, width, dst, scale in segments:
        blk = wt_ref[src:src + width, :]
        wp_ref[dst:dst + width, :] = (blk if scale == 1.0 else blk * scale).astype(BF16)
    n_small = 2 * DN_HEADS
    ws_ref[0:n_small, :] = wt_ref[small_off:small_off + n_small, :].astype(BF16)
    ws_ref[n_small:LANES, :] = jnp.zeros((LANES - n_small, wt_ref.shape[1]), BF16)


def _weight_prep(wt, conv_width, plain_width, segments, small_off):
    n_in, d = wt.shape
    assert d % W_PREP_COLS == 0 and small_off % 16 == 0 and all(sg[0] % 16 == 0 for sg in segments)
    kern = functools.partial(_wprep_kernel, conv_width=conv_width, segments=segments, small_off=small_off)
    return pl.pallas_call(
        kern,
        grid=(d // W_PREP_COLS,),
        in_specs=[pl.BlockSpec((n_in, W_PREP_COLS), lambda i: (0, i))],
        out_specs=[pl.BlockSpec((conv_width, W_PREP_COLS), lambda i: (0, i)),
                   pl.BlockSpec((plain_width, W_PREP_COLS), lambda i: (0, i)),
                   pl.BlockSpec((LANES, W_PREP_COLS), lambda i: (0, i))],
        out_shape=[jax.ShapeDtypeStruct((conv_width, d), BF16),
                   jax.ShapeDtypeStruct((plain_width, d), BF16),
                   jax.ShapeDtypeStruct((LANES, d), BF16)],
        compiler_params=pltpu.CompilerParams(dimension_semantics=("parallel",), vmem_limit_bytes=VMEM_LIMIT),
        name="weight_prep",
    )(wt)


def _layer(x, c, norm_w, ada_w, ada_b, w_in, conv_w, a_log, dt_bias, dn_norm_w, w_proj_a, w_proj_b, w_out,
           final_norm_w):
    b, s, d = x.shape
    qk = DN_HEADS * DN_DK
    splits = (qk, qk, DN_WIDTH, DN_WIDTH, DN_HEADS, DN_HEADS, ATTN_WIDTH, ATTN_WIDTH, ATTN_WIDTH, ATTN_WIDTH, d, d)
    names = ("qa", "ka", "va", "za", "beta", "a", "qb", "kb", "vb", "zb", "ga", "gb")
    offs = dict(zip(names, np.cumsum((0,) + splits[:-1]).tolist()))
    widths = dict(zip(names, splits))
    order = ("za", "ga", "gb", "qb", "kb", "vb", "zb")
    cols, pos = {}, 0
    for n in order:
        assert pos % widths[n] == 0, (n, pos)
        cols[n] = pos
        pos += widths[n]
    assert offs["qa"] == 0 and offs["ka"] == qk and offs["va"] == 2 * qk and offs["a"] == offs["beta"] + DN_HEADS
    halved = ("za", "ga", "gb", "zb")
    segments = tuple((offs[n], widths[n], cols[n], 0.5 if n in halved else 1.0) for n in order)
    w_conv, w_plain, w_small = _weight_prep(w_in.T, 2 * qk + DN_WIDTH, pos, segments, offs["beta"])
    pad_lo = jnp.zeros((DN_HEADS,), F32)
    pad_hi = jnp.zeros((LANES - 2 * DN_HEADS,), F32)
    alog_row = jnp.concatenate([pad_lo, a_log.astype(F32), pad_hi]).reshape(1, LANES)
    dtb_row = jnp.concatenate([pad_lo, dt_bias.astype(F32), pad_hi]).reshape(1, LANES)

    mod3 = _adaln_mod(c, ada_w, ada_b).reshape(b, 3, d)
    p_conv, p, small = _inproj(x, mod3, norm_w, w_conv, w_plain, w_small, conv_w)
    o_a = _deltanet(p_conv, small, alog_row, dtb_row, ts=min(s, 1024), group=4)
    obs, lses = [], []
    for g in range(len(DIL_CONFIGS)):
        o_g, lse_g = _dilated_attention(p, g, cols["qb"])
        obs.append(o_g)
        lses.append(lse_g)
    return _merge(x, mod3, o_a, p, obs, lses, dn_norm_w, (w_proj_a * 0.5).astype(BF16),
                  (w_proj_b * 0.5).astype(BF16), w_out.astype(BF16), final_norm_w, tm=min(s, 1024), cols=cols)


def kernel(x, c, norm_w, ada_w, ada_b, w_in, conv_w, a_log, dt_bias, dn_norm_w, w_proj_a, w_proj_b, w_out,
           final_norm_w):
    depth = norm_w.shape[0]
    assert depth == 1, "the final RMSNorm is fused into the single layer's output kernel"
    return _layer(x, c, norm_w[0], ada_w[0], ada_b[0], w_in[0], conv_w[0], a_log[0], dt_bias[0], dn_norm_w[0],
                  w_proj_a[0], w_proj_b[0], w_out[0], final_norm_w)
```

```python
import functools

import jax
import jax.numpy as jnp
import numpy as np
from jax import lax
from jax.experimental import pallas as pl
from jax.experimental.pallas import tpu as pltpu

F32 = jnp.float32
BF16 = jnp.bfloat16
HIGHEST = lax.Precision.HIGHEST

EPS = 1e-6
DN_HEADS = 8
DN_DK = 128
DN_DV = 128
DN_WIDTH = DN_HEADS * DN_DV
CONV_K = 4
CHUNK = 64
DIL_CONFIGS = ((128, 1), (512, 4), (2048, 16))
HEADS_PER_GROUP = 4
ATTN_HEAD_DIM = 64
ATTN_GROUP_WIDTH = HEADS_PER_GROUP * ATTN_HEAD_DIM
ATTN_WIDTH = ATTN_GROUP_WIDTH * len(DIL_CONFIGS)
ATTN_BLOCK = 128
LANES = 128
HALO = 8
VMEM_LIMIT = 56 * 1024 * 1024


def _mm(a, b):
    return jnp.dot(a.astype(BF16), b.astype(BF16), preferred_element_type=F32)


def _mm_nt(a, b):
    return lax.dot_general(a.astype(BF16), b.astype(BF16), (((1,), (1,)), ((), ())),
                           preferred_element_type=F32)


def _mm_tn(a, b):
    return lax.dot_general(a.astype(BF16), b.astype(BF16), (((0,), (0,)), ((), ())),
                           preferred_element_type=F32)


def _run_round_robin(*gens):
    gens = list(gens)
    while gens:
        for g in list(gens):
            if next(g, StopIteration) is StopIteration:
                gens.remove(g)


def _sigmoid(x):
    return 0.5 * jnp.tanh(0.5 * x) + 0.5


def _silu(x):
    return x * _sigmoid(x)


def _mod_kernel(c_ref, w_ref, b_ref, o_ref):
    sc = _silu(c_ref[...])
    o_ref[...] = jnp.dot(sc, w_ref[...], preferred_element_type=F32, precision=HIGHEST) + b_ref[...]


def _adaln_mod(c, ada_w, ada_b):
    b, d = c.shape
    return pl.pallas_call(
        _mod_kernel,
        grid=(3,),
        in_specs=[pl.BlockSpec((b, d), lambda j: (0, 0)),
                  pl.BlockSpec((d, d), lambda j: (0, j)),
                  pl.BlockSpec((1, d), lambda j: (0, j))],
        out_specs=pl.BlockSpec((b, d), lambda j: (0, j)),
        out_shape=jax.ShapeDtypeStruct((b, 3 * d), F32),
        name="adaln_mod",
    )(c, ada_w, ada_b.reshape(1, 3 * d))


CONV_ROWS = 64
CONV_SUB = 256
SUBLANES = 8
TOKEN_CHAINS = 3


def _conv_unit(buf, r0, cl, half_taps, shift_masks, norm_scale, skip_norm):
    a = buf[r0:r0 + HALO + CONV_ROWS, cl]
    n = CONV_ROWS // SUBLANES
    vs = [a[SUBLANES * i:SUBLANES * (i + 1)] for i in range(n + 1)]
    acc = [vs[i + 1] * half_taps[CONV_K - 1] for i in range(n)]
    for shift in range(1, CONV_K):
        rs = [pltpu.roll(v, shift, 0) for v in vs]
        tap = half_taps[CONV_K - 1 - shift]
        for i in range(n):
            acc[i] = acc[i] + jnp.where(shift_masks[shift], rs[i], rs[i + 1]) * tap
    h = jnp.concatenate(acc, axis=0)
    y = h + h * jnp.tanh(h)
    ss = jnp.sum(y * y, axis=-1, keepdims=True)
    out = y * jnp.where(skip_norm, 1.0, lax.rsqrt(ss + EPS) * norm_scale)
    token = jnp.broadcast_to(ss[CONV_ROWS - SUBLANES:], (SUBLANES, LANES))
    return out.astype(BF16), token


def _inproj_kernel(x_ref, mod_ref, nw_ref, wc_ref, wp_ref, ws_ref, cw_ref, pc_ref, pp_ref, small_ref,
                   h_scr, cbuf, tok, *, steps_per_seg):
    j = pl.program_id(1)
    s = x_ref.shape[1]

    def prologue():
        x = x_ref[0]
        m = mod_ref[0]
        y = x * lax.rsqrt(jnp.mean(x * x, axis=-1, keepdims=True) + EPS) * nw_ref[...]
        h = (y * (1.0 + m[1:2]) + m[0:1]).astype(BF16)
        h_scr[...] = h
        small_ref[0] = _mm_nt(h, ws_ref[...])
        cbuf[0:HALO, :] = jnp.zeros((HALO, CONV_SUB), F32)

    def step_body():
        cbuf[HALO:HALO + s, :] = _mm_nt(h_scr[...], wc_ref[...])
        pp_ref[0] = _mm_nt(h_scr[...], wp_ref[...]).astype(BF16)

        sub = lax.broadcasted_iota(jnp.int32, (SUBLANES, LANES), 0)
        shift_masks = [sub < shift for shift in range(CONV_K)]
        q_scale = jnp.where(j < steps_per_seg, DN_DK ** -0.5, 1.0)
        is_v = j >= 2 * steps_per_seg
        never = j < 0
        for c in range(TOKEN_CHAINS):
            tok[c] = jnp.zeros((SUBLANES, LANES), F32)
        n_unit = 0
        for c0 in range(0, CONV_SUB, DN_DK):
            cl = slice(c0, c0 + DN_DK)
            half_taps = [cw_ref[t:t + 1, cl] * 0.5 for t in range(CONV_K)]
            for r0 in range(0, s, CONV_ROWS):
                slot = n_unit % TOKEN_CHAINS
                held = tok[slot][0:1]
                taps = [jnp.where(never, held, tap) for tap in half_taps]
                out, token = _conv_unit(cbuf, r0, cl, taps, shift_masks, q_scale, is_v)
                pc_ref[0, r0:r0 + CONV_ROWS, cl] = out
                tok[slot] = token
                n_unit += 1

    @pl.when(j == 0)
    def _():
        prologue()
        step_body()

    @pl.when(j > 0)
    def _():
        step_body()


def _inproj(x, mod3, norm_w, w_conv, w_plain, w_small, conv_w):
    b, s, d = x.shape
    conv_width, plain_width = w_conv.shape[0], w_plain.shape[0]
    n_steps = conv_width // CONV_SUB
    tp = plain_width // n_steps
    assert n_steps % 3 == 0 and n_steps * CONV_SUB == conv_width and conv_w.shape[1] == conv_width
    assert tp * n_steps == plain_width and tp % LANES == 0 and s % CONV_ROWS == 0
    kern = functools.partial(_inproj_kernel, steps_per_seg=n_steps // 3)
    return pl.pallas_call(
        kern,
        grid=(b, n_steps),
        in_specs=[pl.BlockSpec((1, s, d), lambda bi, j: (bi, 0, 0)),
                  pl.BlockSpec((1, 3, d), lambda bi, j: (bi, 0, 0)),
                  pl.BlockSpec((1, d), lambda bi, j: (0, 0)),
                  pl.BlockSpec((CONV_SUB, d), lambda bi, j: (j, 0)),
                  pl.BlockSpec((tp, d), lambda bi, j: (j, 0)),
                  pl.BlockSpec((LANES, d), lambda bi, j: (0, 0)),
                  pl.BlockSpec((CONV_K, CONV_SUB), lambda bi, j: (0, j))],
        out_specs=[pl.BlockSpec((1, s, CONV_SUB), lambda bi, j: (bi, 0, j)),
                   pl.BlockSpec((1, s, tp), lambda bi, j: (bi, 0, j)),
                   pl.BlockSpec((1, s, LANES), lambda bi, j: (bi, 0, 0))],
        out_shape=[jax.ShapeDtypeStruct((b, s, conv_width), BF16),
                   jax.ShapeDtypeStruct((b, s, plain_width), BF16),
                   jax.ShapeDtypeStruct((b, s, LANES), F32)],
        scratch_shapes=[pltpu.VMEM((s, d), BF16),
                        pltpu.VMEM((HALO + s, CONV_SUB), F32),
                        pltpu.VMEM((TOKEN_CHAINS, SUBLANES, LANES), F32)],
        compiler_params=pltpu.CompilerParams(
            dimension_semantics=("parallel", "arbitrary"), vmem_limit_bytes=VMEM_LIMIT),
        name="norm_inproj",
    )(x, mod3, norm_w.reshape(1, d), w_conv, w_plain, w_small, conv_w)


def _level_mask(level, i, j):
    same_block = (i >> (level + 1)) == (j >> (level + 1))
    return same_block & (((i >> level) & 1) == 1) & (((j >> level) & 1) == 0)


N_LEVELS = CHUNK.bit_length() - 1


def _deltanet_kernel(q_ref, k_ref, v_ref, small_ref, alog_ref, dtb_ref, o_ref,
                     state, u_s, wq_s, akd_s, el_s, *, ts, group):
    @pl.when(pl.program_id(1) == 0)
    def _():
        state[...] = jnp.zeros_like(state)

    row = lax.broadcasted_iota(jnp.int32, (CHUNK, 2 * CHUNK), 0)
    lane = lax.broadcasted_iota(jnp.int32, (CHUNK, 2 * CHUNK), 1)
    col = lane & (CHUNK - 1)
    in_a = lane < CHUNK
    causal = row >= col
    tri_strict = (row > col).astype(F32)
    eye = (row == col).astype(F32)
    row_c = lax.broadcasted_iota(jnp.int32, (CHUNK, CHUNK), 0)
    col_c = lax.broadcasted_iota(jnp.int32, (CHUNK, CHUNK), 1)
    tri = (row_c >= col_c).astype(F32)
    level_masks = [_level_mask(level, row, col).astype(F32) for level in range(N_LEVELS)]
    neg_a = -jnp.exp(alog_ref[...])
    dtb = dtb_ref[...]
    n_chunks = ts // CHUNK
    n_pairs = DN_HEADS // 2
    zeros_wide = jnp.zeros((CHUNK, DN_DV + DN_DK), BF16)
    zeros_head = jnp.zeros((CHUNK, DN_DK), BF16)

    def blockdiag(m):
        return jnp.concatenate([jnp.where(in_a, m, 0.0), jnp.where(in_a, 0.0, m)], axis=0).astype(BF16)

    def prep_stages(gi):
        chains = []
        for ci in range(group):
            c = gi * group + ci
            r0 = c * CHUNK if isinstance(c, int) else pl.multiple_of(c * CHUNK, CHUNK)
            sm = small_ref[0, pl.ds(r0, CHUNK), :]
            beta_all = _sigmoid(sm)
            z = sm + dtb
            softplus = jnp.maximum(z, 0.0) + jnp.log(1.0 + jnp.exp(-jnp.abs(z)))
            g_all = neg_a * softplus
            gc_all = jnp.dot(tri, g_all, preferred_element_type=F32, precision=HIGHEST)
            gc_t = jnp.concatenate([gc_all, gc_all], axis=0).T
            g_last = gc_all[CHUNK - 1:CHUNK, :]
            e_in_all = jnp.exp(gc_all)
            e_out_all = jnp.exp(g_last - gc_all)
            slot = (gi % 2) * group + ci
            el_s[slot] = jnp.broadcast_to(jnp.exp(g_last), (8, LANES))
            for p in range(n_pairs):
                heads = (2 * p, 2 * p + 1)
                q, k, k_beta, rhs, k_dec = [], [], [], [], []
                for h in heads:
                    gl = DN_HEADS + h
                    cq = slice(h * DN_DK, (h + 1) * DN_DK)
                    q_h = q_ref[0, pl.ds(r0, CHUNK), cq].astype(F32)
                    k_h = k_ref[0, pl.ds(r0, CHUNK), cq].astype(F32)
                    v_h = v_ref[0, pl.ds(r0, CHUNK), slice(h * DN_DV, (h + 1) * DN_DV)].astype(F32)
                    beta = beta_all[:, h:h + 1]
                    e_in = e_in_all[:, gl:gl + 1]
                    kb_h = k_h * beta
                    q.append(q_h)
                    k.append(k_h.astype(BF16))
                    k_beta.append(kb_h)
                    rhs.append(jnp.concatenate([v_h * beta, kb_h * e_in], axis=1).astype(BF16))
                    k_dec.append(k_h * e_out_all[:, gl:gl + 1])
                    wq_s[slot, h, CHUNK:2 * CHUNK, :] = (q_h * e_in).astype(BF16)
                ga, gb = DN_HEADS + heads[0], DN_HEADS + heads[1]
                gc_col = jnp.where(in_a, gc_all[:, ga:ga + 1], gc_all[:, gb:gb + 1])
                gc_row = jnp.where(in_a[0:1], gc_t[ga:ga + 1, :], gc_t[gb:gb + 1, :])
                chains.append(dict(
                    slot=slot, p=p,
                    decay=jnp.exp(jnp.where(causal, gc_col - gc_row, -jnp.inf)),
                    kq=jnp.concatenate([jnp.concatenate(k_beta, axis=1), jnp.concatenate(q, axis=1)],
                                       axis=0).astype(BF16),
                    kk=jnp.concatenate([jnp.concatenate([k[0], zeros_head], axis=1),
                                        jnp.concatenate([zeros_head, k[1]], axis=1)], axis=0),
                    rhs=jnp.concatenate([jnp.concatenate([rhs[0], zeros_wide], axis=1),
                                         jnp.concatenate([zeros_wide, rhs[1]], axis=1)], axis=0),
                    k_dec_t=jnp.concatenate(k_dec, axis=0).T.astype(BF16)))

        for ch in chains:
            ch["kk_qk"] = _mm_nt(ch.pop("kq"), ch.pop("kk"))
        yield
        for ch in chains:
            kk_qk = ch.pop("kk_qk")
            decay = ch.pop("decay")
            ch["lmat"] = kk_qk[0:CHUNK] * decay * tri_strict
            ch["a_qk"] = (kk_qk[CHUNK:2 * CHUNK] * decay).astype(BF16)
            ch["x"] = eye - ch["lmat"] * level_masks[0]
        for level in range(1, N_LEVELS):
            for ch in chains:
                ch["y"] = _mm(ch["lmat"] * level_masks[level], blockdiag(ch["x"]))
            yield
            for ch in chains:
                ch["x"] = ch["x"] - _mm(ch["x"], blockdiag(ch.pop("y")))
            yield
        for ch in chains:
            ch["uw"] = _mm(ch.pop("x"), ch.pop("rhs"))
        for ch in chains:
            slot, p = ch["slot"], ch["p"]
            uw = ch.pop("uw")
            for i, h in enumerate((2 * p, 2 * p + 1)):
                base = i * (DN_DV + DN_DK)
                u_s[slot, h] = uw[:, base:base + DN_DV]
                wq_s[slot, h, 0:CHUNK, :] = uw[:, base + DN_DV:base + DN_DV + DN_DK].astype(BF16)
            akd_s[slot, p] = jnp.concatenate([ch["a_qk"], ch["k_dec_t"]], axis=0)
        yield

    def rec_stages(gi):
        for ci in range(group):
            c = gi * group + ci
            r0 = c * CHUNK if isinstance(c, int) else pl.multiple_of(c * CHUNK, CHUNK)
            slot = (gi % 2) * group + ci
            e_last_all = el_s[slot]
            rs = [_mm(wq_s[slot, h], state[h]) for h in range(DN_HEADS)]
            yield
            v_new = [(u_s[slot, h] - rs[h][0:CHUNK]).astype(BF16) for h in range(DN_HEADS)]
            zeros_v = jnp.zeros((CHUNK, DN_DV), BF16)
            av = [_mm(akd_s[slot, p],
                      jnp.concatenate([jnp.concatenate([v_new[2 * p], zeros_v], axis=1),
                                       jnp.concatenate([zeros_v, v_new[2 * p + 1]], axis=1)], axis=0))
                  for p in range(n_pairs)]
            for h in range(DN_HEADS):
                gl = DN_HEADS + h
                lanes_h = slice((h % 2) * DN_DV, (h % 2 + 1) * DN_DV)
                o = rs[h][CHUNK:2 * CHUNK] + av[h // 2][0:CHUNK, lanes_h]
                state[h] = state[h] * e_last_all[0:1, gl:gl + 1] + av[h // 2][CHUNK:CHUNK + DN_DK, lanes_h]
                o_ref[0, pl.ds(r0, CHUNK), slice(h * DN_DV, (h + 1) * DN_DV)] = o.astype(o_ref.dtype)
            yield

    run = _run_round_robin

    n_groups = n_chunks // group
    run(prep_stages(0))
    for gi in range(1, n_groups):
        run(prep_stages(gi), rec_stages(gi - 1))
    run(rec_stages(n_groups - 1))


def _deltanet(p, small, alog_row, dtb_row, ts, group):
    b, s, _ = p.shape
    width = DN_HEADS * DN_DK
    n_slots = 2 * group
    assert (ts // CHUNK) % group == 0
    kern = functools.partial(_deltanet_kernel, ts=ts, group=group)
    return pl.pallas_call(
        kern,
        grid=(b, s // ts),
        in_specs=[pl.BlockSpec((1, ts, width), lambda bi, t: (bi, t, 0)),
                  pl.BlockSpec((1, ts, width), lambda bi, t: (bi, t, 1)),
                  pl.BlockSpec((1, ts, width), lambda bi, t: (bi, t, 2)),
                  pl.BlockSpec((1, ts, LANES), lambda bi, t: (bi, t, 0)),
                  pl.BlockSpec((1, LANES), lambda bi, t: (0, 0)),
                  pl.BlockSpec((1, LANES), lambda bi, t: (0, 0))],
        out_specs=pl.BlockSpec((1, ts, DN_WIDTH), lambda bi, t: (bi, t, 0)),
        out_shape=jax.ShapeDtypeStruct((b, s, DN_WIDTH), BF16),
        scratch_shapes=[pltpu.VMEM((DN_HEADS, DN_DK, DN_DV), F32),
                        pltpu.VMEM((n_slots, DN_HEADS, CHUNK, DN_DV), F32),
                        pltpu.VMEM((n_slots, DN_HEADS, 2 * CHUNK, DN_DK), BF16),
                        pltpu.VMEM((n_slots, DN_HEADS // 2, CHUNK + DN_DK, 2 * CHUNK), BF16),
                        pltpu.VMEM((n_slots, 8, LANES), F32)],
        compiler_params=pltpu.CompilerParams(
            dimension_semantics=("parallel", "arbitrary"), vmem_limit_bytes=VMEM_LIMIT),
        name="gated_deltanet",
    )(p, p, p, small, alog_row, dtb_row)


ATTN_BLOCKS_PER_GROUP = 2
ATTN_STRIDE = 4


def _attn_kernel(q_ref, k_ref, v_ref, o_ref, lse_ref, *scratch, dil, n_back):
    s = q_ref.shape[1]
    sub_len = s // dil
    nb = sub_len // ATTN_BLOCK
    n_pairs = ATTN_GROUP_WIDTH // LANES
    prev_block = nb > 1
    blocks_per_iter = ATTN_BLOCKS_PER_GROUP

    def lanes(pair):
        return slice(pair * LANES, (pair + 1) * LANES)

    if dil > 1:
        stage, tmp, qs, ks, vs, os_, ls = scratch
        assert dil in (ATTN_STRIDE, ATTN_STRIDE * ATTN_STRIDE)
        q_len = s // ATTN_STRIDE

        def classes():
            if dil == ATTN_STRIDE:
                return [(r, stage, r) for r in range(dil)]
            return [(r_lo + ATTN_STRIDE * r_hi, tmp, r_lo * q_len + r_hi)
                    for r_lo in range(ATTN_STRIDE) for r_hi in range(ATTN_STRIDE)]

        def deinterleave(dst, pair):
            if dil > ATTN_STRIDE:
                for r_lo in range(ATTN_STRIDE):
                    tmp[r_lo * q_len:(r_lo + 1) * q_len, :] = stage[pl.ds(r_lo, q_len, stride=ATTN_STRIDE), :]
            for r, buf, start in classes():
                dst[pair, r * sub_len:(r + 1) * sub_len, :] = (
                    buf[pl.ds(start, sub_len, stride=ATTN_STRIDE), :].astype(dst.dtype))

        def interleave(src, pair):
            for r, buf, start in classes():
                buf[pl.ds(start, sub_len, stride=ATTN_STRIDE), :] = src[pair, r * sub_len:(r + 1) * sub_len, :]
            if dil > ATTN_STRIDE:
                for r_lo in range(ATTN_STRIDE):
                    stage[pl.ds(r_lo, q_len, stride=ATTN_STRIDE), :] = tmp[r_lo * q_len:(r_lo + 1) * q_len, :]

        for src, dst in ((q_ref, qs), (k_ref, ks), (v_ref, vs)):
            for pair in range(n_pairs):
                stage[...] = src[0, :, lanes(pair)].astype(F32)
                deinterleave(dst, pair)

        def load(ref, src, pair, r0):
            return src[pair, pl.ds(r0, ATTN_BLOCK), :]
    else:
        qs, ks, vs = q_ref, k_ref, v_ref

        def load(ref, src, pair, r0):
            return ref[0, pl.ds(r0, ATTN_BLOCK), lanes(pair)]

    n_keys = 2 * ATTN_BLOCK if prev_block else ATTN_BLOCK
    qi = lax.broadcasted_iota(jnp.int32, (ATTN_BLOCK, n_keys), 0)
    ki = lax.broadcasted_iota(jnp.int32, (ATTN_BLOCK, n_keys), 1)
    dist = qi + (n_keys - ATTN_BLOCK) - ki
    band = (dist >= 0) & (dist <= n_back)
    bias_full = jnp.where(band, 0.0, -jnp.inf).astype(F32)
    bias_first = jnp.where(band & (ki >= ATTN_BLOCK), 0.0, -jnp.inf).astype(F32)
    lane = lax.broadcasted_iota(jnp.int32, (ATTN_BLOCK, LANES), 1)
    low_half = lane < ATTN_HEAD_DIM
    scale = ATTN_HEAD_DIM ** -0.5

    def block_body(fi):
        chains = []
        for bi in range(blocks_per_iter):
            f = fi * blocks_per_iter + bi
            q0 = f * ATTN_BLOCK
            if prev_block:
                k0 = max(f - 1, 0) * ATTN_BLOCK
                bias = bias_full if f % nb > 0 else bias_first
            else:
                bias = bias_full
            for pair in range(n_pairs):
                qp = load(q_ref, qs, pair, q0) * scale
                kp = load(k_ref, ks, pair, q0)
                vp = load(v_ref, vs, pair, q0)
                if prev_block:
                    kp = jnp.concatenate([load(k_ref, ks, pair, k0), kp], axis=0)
                    vp = jnp.concatenate([load(v_ref, vs, pair, k0), vp], axis=0)
                for half in range(2):
                    sel = low_half if half == 0 else ~low_half
                    chains.append(dict(q0=q0, pair=pair, half=half, bias=bias, kp=kp, vp=vp,
                                       qm=jnp.where(sel, qp, jnp.zeros_like(qp))))
        for ch in chains:
            ch["sc"] = _mm_nt(ch.pop("qm"), ch.pop("kp"))
        for ch in chains:
            sc = ch.pop("sc") + ch.pop("bias")
            m = jnp.max(sc, axis=-1, keepdims=True)
            p = jnp.exp(sc - m)
            ch["denom"] = jnp.sum(p, axis=-1, keepdims=True)
            ch["m"] = m
            ch["p"] = p.astype(BF16)
        for ch in chains:
            ch["pv"] = _mm(ch.pop("p"), ch.pop("vp"))
        for c0 in range(0, len(chains), 2):
            lo, hi = chains[c0], chains[c0 + 1]
            o = jnp.where(low_half, lo["pv"] / lo["denom"], hi["pv"] / hi["denom"])
            lse = jnp.where(low_half, lo["m"] + jnp.log(lo["denom"]), hi["m"] + jnp.log(hi["denom"]))
            q0, pair = lo["q0"], lo["pair"]
            if dil > 1:
                os_[pair, pl.ds(q0, ATTN_BLOCK), :] = o
                ls[pair, pl.ds(q0, ATTN_BLOCK), :] = lse
            else:
                o_ref[0, pl.ds(q0, ATTN_BLOCK), lanes(pair)] = o.astype(o_ref.dtype)
                lse_ref[0, pl.ds(q0, ATTN_BLOCK), lanes(pair)] = lse

    n_iters = dil * nb // blocks_per_iter
    for fi in range(n_iters):
        block_body(fi)

    if dil > 1:
        for pair in range(n_pairs):
            for src, dst in ((os_, o_ref), (ls, lse_ref)):
                interleave(src, pair)
                dst[0, :, lanes(pair)] = stage[...].astype(dst.dtype)


def _dilated_attention(p, group, col_base):
    b, s, _ = p.shape
    window, dil = DIL_CONFIGS[group]
    gw = ATTN_GROUP_WIDTH
    blk = col_base // gw + group
    step = ATTN_WIDTH // gw
    kern = functools.partial(_attn_kernel, dil=dil, n_back=window // dil)
    return pl.pallas_call(
        kern,
        grid=(b,),
        in_specs=[pl.BlockSpec((1, s, gw), lambda bi: (bi, 0, blk)),
                  pl.BlockSpec((1, s, gw), lambda bi: (bi, 0, blk + step)),
                  pl.BlockSpec((1, s, gw), lambda bi: (bi, 0, blk + 2 * step))],
        out_specs=[pl.BlockSpec((1, s, gw), lambda bi: (bi, 0, 0)),
                   pl.BlockSpec((1, s, gw), lambda bi: (bi, 0, 0))],
        out_shape=[jax.ShapeDtypeStruct((b, s, gw), BF16),
                   jax.ShapeDtypeStruct((b, s, gw), F32)],
        scratch_shapes=([] if dil == 1 else
                        [pltpu.VMEM((s, LANES), F32) for _ in range(2)]
                        + [pltpu.VMEM((gw // LANES, s, LANES), BF16) for _ in range(3)]
                        + [pltpu.VMEM((gw // LANES, s, LANES), F32) for _ in range(2)]),
        compiler_params=pltpu.CompilerParams(
            dimension_semantics=("parallel",), vmem_limit_bytes=VMEM_LIMIT),
        name=f"dilated_attn_g{group}",
    )(p, p, p)


MERGE_SUBTILES = 8


def _merge_kernel(x_ref, mod_ref, oa_ref, za_ref, ga_ref, gb_ref, zb_ref,
                  ob0_ref, ob1_ref, ob2_ref, l0_ref, l1_ref, l2_ref,
                  dnw_ref, pa_ref, pb_ref, wo_ref, fw_ref, out_ref):
    def silu_of_half(h):
        return h + h * jnp.tanh(h)

    gate = mod_ref[0][2:3]

    def stages(rows):
        oa = oa_ref[0, rows, :].astype(F32)
        za_half = za_ref[0, rows, :].astype(F32)
        parts = []
        for h in range(DN_HEADS):
            cq = slice(h * DN_DV, (h + 1) * DN_DV)
            blk = oa[:, cq]
            y = blk * lax.rsqrt(jnp.mean(blk * blk, axis=-1, keepdims=True) + EPS) * dnw_ref[...]
            parts.append(y * silu_of_half(za_half[:, cq]))
        ya_half = _mm(jnp.concatenate(parts, axis=1), pa_ref[...])
        yield

        l0, l1, l2 = l0_ref[0, rows, :], l1_ref[0, rows, :], l2_ref[0, rows, :]
        m = jnp.maximum(jnp.maximum(l0, l1), l2)
        e0, e1, e2 = jnp.exp(l0 - m), jnp.exp(l1 - m), jnp.exp(l2 - m)
        inv_den = 1.0 / (e0 + e1 + e2)
        ob = jnp.concatenate([ob0_ref[0, rows, :].astype(F32) * (e0 * inv_den),
                              ob1_ref[0, rows, :].astype(F32) * (e1 * inv_den),
                              ob2_ref[0, rows, :].astype(F32) * (e2 * inv_den)], axis=1)
        yb_half = _mm(ob * silu_of_half(zb_ref[0, rows, :].astype(F32)), pb_ref[...])
        yield

        merged = (ya_half + ya_half * jnp.tanh(ga_ref[0, rows, :].astype(F32))
                  + (yb_half + yb_half * jnp.tanh(gb_ref[0, rows, :].astype(F32))))
        delta = _mm(merged, wo_ref[...])
        yield

        xo = x_ref[0, rows, :] + gate * delta
        out_ref[0, rows, :] = xo * lax.rsqrt(jnp.mean(xo * xo, axis=-1, keepdims=True) + EPS) * fw_ref[...]
        yield

    tm = x_ref.shape[1]
    sub = tm // MERGE_SUBTILES
    _run_round_robin(*[stages(slice(i * sub, (i + 1) * sub)) for i in range(MERGE_SUBTILES)])


def _merge(x, mod3, o_a, p, obs, lses, dn_norm_w, pa, pb, wo, final_w, tm, cols):
    b, s, d = x.shape
    gw = ATTN_GROUP_WIDTH

    def row_spec(width, blk):
        return pl.BlockSpec((1, tm, width), lambda bi, i: (bi, i, blk))

    def full_spec(shape):
        return pl.BlockSpec(shape, lambda bi, i: (0,) * len(shape), pipeline_mode=pl.Buffered(1))

    return pl.pallas_call(
        _merge_kernel,
        grid=(b, s // tm),
        in_specs=[row_spec(d, 0),
                  pl.BlockSpec((1, 3, d), lambda bi, i: (bi, 0, 0)),
                  row_spec(DN_WIDTH, 0),
                  row_spec(DN_WIDTH, cols["za"] // DN_WIDTH),
                  row_spec(d, cols["ga"] // d),
                  row_spec(d, cols["gb"] // d),
                  row_spec(ATTN_WIDTH, cols["zb"] // ATTN_WIDTH),
                  row_spec(gw, 0), row_spec(gw, 0), row_spec(gw, 0),
                  row_spec(gw, 0), row_spec(gw, 0), row_spec(gw, 0),
                  full_spec((1, DN_DV)), full_spec(pa.shape), full_spec(pb.shape), full_spec(wo.shape),
                  full_spec((1, d))],
        out_specs=row_spec(d, 0),
        out_shape=jax.ShapeDtypeStruct((b, s, d), F32),
        compiler_params=pltpu.CompilerParams(
            dimension_semantics=("parallel", "parallel"), vmem_limit_bytes=VMEM_LIMIT),
        name="merge_outproj",
    )(x, mod3, o_a, p, p, p, p, *obs, *lses, dn_norm_w.reshape(1, DN_DV), pa, pb, wo, final_w.reshape(1, d))


W_PREP_COLS = 128


def _wprep_kernel(wt_ref, wc_ref, wp_ref, ws_ref, *, conv_width, segments, small_off):
    wc_ref[...] = wt_ref[0:conv_width, :].astype(BF16)
    for src, width, dst, scale in segments:
        blk = wt_ref[src:src + width, :]
        wp_ref[dst:dst + width, :] = (blk if scale == 1.0 else blk * scale).astype(BF16)
    n_small = 2 * DN_HEADS
    ws_ref[0:n_small, :] = wt_ref[small_off:small_off + n_small, :].astype(BF16)
    ws_ref[n_small:LANES, :] = jnp.zeros((LANES - n_small, wt_ref.shape[1]), BF16)


def _weight_prep(wt, conv_width, plain_width, segments, small_off):
    n_in, d = wt.shape
    assert d % W_PREP_COLS == 0 and small_off % 16 == 0 and all(sg[0] % 16 == 0 for sg in segments)
    kern = functools.partial(_wprep_kernel, conv_width=conv_width, segments=segments, small_off=small_off)
    return pl.pallas_call(
        kern,
        grid=(d // W_PREP_COLS,),
        in_specs=[pl.BlockSpec((n_in, W_PREP_COLS), lambda i: (0, i))],
        out_specs=[pl.BlockSpec((conv_width, W_PREP_COLS), lambda i: (0, i)),
                   pl.BlockSpec((plain_width, W_PREP_COLS), lambda i: (0, i)),
                   pl.BlockSpec((LANES, W_PREP_COLS), lambda i: (0, i))],
        out_shape=[jax.ShapeDtypeStruct((conv_width, d), BF16),
                   jax.ShapeDtypeStruct((plain_width, d), BF16),
                   jax.ShapeDtypeStruct((LANES, d), BF16)],
        compiler_params=pltpu.CompilerParams(dimension_semantics=("parallel",), vmem_limit_bytes=VMEM_LIMIT),
        name="weight_prep",
    )(wt)


def _layer(x, c, norm_w, ada_w, ada_b, w_in, conv_w, a_log, dt_bias, dn_norm_w, w_proj_a, w_proj_b, w_out,
           final_norm_w):
    b, s, d = x.shape
    qk = DN_HEADS * DN_DK
    splits = (qk, qk, DN_WIDTH, DN_WIDTH, DN_HEADS, DN_HEADS, ATTN_WIDTH, ATTN_WIDTH, ATTN_WIDTH, ATTN_WIDTH, d, d)
    names = ("qa", "ka", "va", "za", "beta", "a", "qb", "kb", "vb", "zb", "ga", "gb")
    offs = dict(zip(names, np.cumsum((0,) + splits[:-1]).tolist()))
    widths = dict(zip(names, splits))
    order = ("za", "ga", "gb", "qb", "kb", "vb", "zb")
    cols, pos = {}, 0
    for n in order:
        assert pos % widths[n] == 0, (n, pos)
        cols[n] = pos
        pos += widths[n]
    assert offs["qa"] == 0 and offs["ka"] == qk and offs["va"] == 2 * qk and offs["a"] == offs["beta"] + DN_HEADS
    halved = ("za", "ga", "gb", "zb")
    segments = tuple((offs[n], widths[n], cols[n], 0.5 if n in halved else 1.0) for n in order)
    w_conv, w_plain, w_small = _weight_prep(w_in.T, 2 * qk + DN_WIDTH, pos, segments, offs["beta"])
    pad_lo = jnp.zeros((DN_HEADS,), F32)
    pad_hi = jnp.zeros((LANES - 2 * DN_HEADS,), F32)
    alog_row = jnp.concatenate([pad_lo, a_log.astype(F32), pad_hi]).reshape(1, LANES)
    dtb_row = jnp.concatenate([pad_lo, dt_bias.astype(F32), pad_hi]).reshape(1, LANES)

    mod3 = _adaln_mod(c, ada_w, ada_b).reshape(b, 3, d)
    p_conv, p, small = _inproj(x, mod3, norm_w, w_conv, w_plain, w_small, conv_w)
    o_a = _deltanet(p_conv, small, alog_row, dtb_row, ts=min(s, 1024), group=4)
    obs, lses = [], []
    for g in range(len(DIL_CONFIGS)):
        o_g, lse_g = _dilated_attention(p, g, cols["qb"])
        obs.append(o_g)
        lses.append(lse_g)
    return _merge(x, mod3, o_a, p, obs, lses, dn_norm_w, (w_proj_a * 0.5).astype(BF16),
                  (w_proj_b * 0.5).astype(BF16), w_out.astype(BF16), final_norm_w, tm=min(s, 1024), cols=cols)


def kernel(x, c, norm_w, ada_w, ada_b, w_in, conv_w, a_log, dt_bias, dn_norm_w, w_proj_a, w_proj_b, w_out,
           final_norm_w):
    depth = norm_w.shape[0]
    assert depth == 1, "the final RMSNorm is fused into the single layer's output kernel"
    return _layer(x, c, norm_w[0], ada_w[0], ada_b[0], w_in[0], conv_w[0], a_log[0], dt_bias[0], dn_norm_w[0],
                  w_proj_a[0], w_proj_b[0], w_out[0], final_norm_w)
```

```python
import functools

import jax
import jax.numpy as jnp
import numpy as np
from jax import lax
from jax.experimental import pallas as pl
from jax.experimental.pallas import tpu as pltpu

F32 = jnp.float32
BF16 = jnp.bfloat16
HIGHEST = lax.Precision.HIGHEST

EPS = 1e-6
DN_HEADS = 8
DN_DK = 128
DN_DV = 128
DN_WIDTH = DN_HEADS * DN_DV
CONV_K = 4
CHUNK = 64
DIL_CONFIGS = ((128, 1), (512, 4), (2048, 16))
HEADS_PER_GROUP = 4
ATTN_HEAD_DIM = 64
ATTN_GROUP_WIDTH = HEADS_PER_GROUP * ATTN_HEAD_DIM
ATTN_WIDTH = ATTN_GROUP_WIDTH * len(DIL_CONFIGS)
ATTN_BLOCK = 128
LANES = 128
HALO = 8
VMEM_LIMIT = 56 * 1024 * 1024


def _mm(a, b):
    return jnp.dot(a.astype(BF16), b.astype(BF16), preferred_element_type=F32)


def _mm_nt(a, b):
    return lax.dot_general(a.astype(BF16), b.astype(BF16), (((1,), (1,)), ((), ())),
                           preferred_element_type=F32)


def _mm_tn(a, b):
    return lax.dot_general(a.astype(BF16), b.astype(BF16), (((0,), (0,)), ((), ())),
                           preferred_element_type=F32)


def _run_round_robin(*gens):
    gens = list(gens)
    while gens:
        for g in list(gens):
            if next(g, StopIteration) is StopIteration:
                gens.remove(g)


def _sigmoid(x):
    return 0.5 * jnp.tanh(0.5 * x) + 0.5


def _silu(x):
    return x * _sigmoid(x)


def _mod_kernel(c_ref, w_ref, b_ref, o_ref):
    sc = _silu(c_ref[...])
    o_ref[...] = jnp.dot(sc, w_ref[...], preferred_element_type=F32, precision=HIGHEST) + b_ref[...]


def _adaln_mod(c, ada_w, ada_b):
    b, d = c.shape
    return pl.pallas_call(
        _mod_kernel,
        grid=(3,),
        in_specs=[pl.BlockSpec((b, d), lambda j: (0, 0)),
                  pl.BlockSpec((d, d), lambda j: (0, j)),
                  pl.BlockSpec((1, d), lambda j: (0, j))],
        out_specs=pl.BlockSpec((b, d), lambda j: (0, j)),
        out_shape=jax.ShapeDtypeStruct((b, 3 * d), F32),
        name="adaln_mod",
    )(c, ada_w, ada_b.reshape(1, 3 * d))


CONV_ROWS = 64
CONV_SUB = 256
SUBLANES = 8
TOKEN_CHAINS = 3


def _conv_unit(buf, r0, cl, half_taps, shift_masks, norm_scale, skip_norm):
    a = buf[r0:r0 + HALO + CONV_ROWS, cl]
    n = CONV_ROWS // SUBLANES
    vs = [a[SUBLANES * i:SUBLANES * (i + 1)] for i in range(n + 1)]
    acc = [vs[i + 1] * half_taps[CONV_K - 1] for i in range(n)]
    for shift in range(1, CONV_K):
        rs = [pltpu.roll(v, shift, 0) for v in vs]
        tap = half_taps[CONV_K - 1 - shift]
        for i in range(n):
            acc[i] = acc[i] + jnp.where(shift_masks[shift], rs[i], rs[i + 1]) * tap
    h = jnp.concatenate(acc, axis=0)
    y = h + h * jnp.tanh(h)
    ss = jnp.sum(y * y, axis=-1, keepdims=True)
    out = y * jnp.where(skip_norm, 1.0, lax.rsqrt(ss + EPS) * norm_scale)
    token = jnp.broadcast_to(ss[CONV_ROWS - SUBLANES:], (SUBLANES, LANES))
    return out.astype(BF16), token


def _inproj_kernel(x_ref, mod_ref, nw_ref, wc_ref, wp_ref, ws_ref, cw_ref, pc_ref, pp_ref, small_ref,
                   h_scr, cbuf, tok, *, steps_per_seg):
    j = pl.program_id(1)
    s = x_ref.shape[1]

    def prologue():
        x = x_ref[0]
        m = mod_ref[0]
        y = x * lax.rsqrt(jnp.mean(x * x, axis=-1, keepdims=True) + EPS) * nw_ref[...]
        h = (y * (1.0 + m[1:2]) + m[0:1]).astype(BF16)
        h_scr[...] = h
        small_ref[0] = _mm_nt(h, ws_ref[...])
        cbuf[0:HALO, :] = jnp.zeros((HALO, CONV_SUB), F32)

    def step_body():
        cbuf[HALO:HALO + s, :] = _mm_nt(h_scr[...], wc_ref[...])
        pp_ref[0] = _mm_nt(h_scr[...], wp_ref[...]).astype(BF16)

        sub = lax.broadcasted_iota(jnp.int32, (SUBLANES, LANES), 0)
        shift_masks = [sub < shift for shift in range(CONV_K)]
        q_scale = jnp.where(j < steps_per_seg, DN_DK ** -0.5, 1.0)
        is_v = j >= 2 * steps_per_seg
        never = j < 0
        for c in range(TOKEN_CHAINS):
            tok[c] = jnp.zeros((SUBLANES, LANES), F32)
        n_unit = 0
        for c0 in range(0, CONV_SUB, DN_DK):
            cl = slice(c0, c0 + DN_DK)
            half_taps = [cw_ref[t:t + 1, cl] * 0.5 for t in range(CONV_K)]
            for r0 in range(0, s, CONV_ROWS):
                slot = n_unit % TOKEN_CHAINS
                held = tok[slot][0:1]
                taps = [jnp.where(never, held, tap) for tap in half_taps]
                out, token = _conv_unit(cbuf, r0, cl, taps, shift_masks, q_scale, is_v)
                pc_ref[0, r0:r0 + CONV_ROWS, cl] = out
                tok[slot] = token
                n_unit += 1

    @pl.when(j == 0)
    def _():
        prologue()
        step_body()

    @pl.when(j > 0)
    def _():
        step_body()


def _inproj(x, mod3, norm_w, w_conv, w_plain, w_small, conv_w):
    b, s, d = x.shape
    conv_width, plain_width = w_conv.shape[0], w_plain.shape[0]
    n_steps = conv_width // CONV_SUB
    tp = plain_width // n_steps
    assert n_steps % 3 == 0 and n_steps * CONV_SUB == conv_width and conv_w.shape[1] == conv_width
    assert tp * n_steps == plain_width and tp % LANES == 0 and s % CONV_ROWS == 0
    kern = functools.partial(_inproj_kernel, steps_per_seg=n_steps // 3)
    return pl.pallas_call(
        kern,
        grid=(b, n_steps),
        in_specs=[pl.BlockSpec((1, s, d), lambda bi, j: (bi, 0, 0)),
                  pl.BlockSpec((1, 3, d), lambda bi, j: (bi, 0, 0)),
                  pl.BlockSpec((1, d), lambda bi, j: (0, 0)),
                  pl.BlockSpec((CONV_SUB, d), lambda bi, j: (j, 0)),
                  pl.BlockSpec((tp, d), lambda bi, j: (j, 0)),
                  pl.BlockSpec((LANES, d), lambda bi, j: (0, 0)),
                  pl.BlockSpec((CONV_K, CONV_SUB), lambda bi, j: (0, j))],
        out_specs=[pl.BlockSpec((1, s, CONV_SUB), lambda bi, j: (bi, 0, j)),
                   pl.BlockSpec((1, s, tp), lambda bi, j: (bi, 0, j)),
                   pl.BlockSpec((1, s, LANES), lambda bi, j: (bi, 0, 0))],
        out_shape=[jax.ShapeDtypeStruct((b, s, conv_width), BF16),
                   jax.ShapeDtypeStruct((b, s, plain_width), BF16),
                   jax.ShapeDtypeStruct((b, s, LANES), F32)],
        scratch_shapes=[pltpu.VMEM((s, d), BF16),
                        pltpu.VMEM((HALO + s, CONV_SUB), F32),
                        pltpu.VMEM((TOKEN_CHAINS, SUBLANES, LANES), F32)],
        compiler_params=pltpu.CompilerParams(
            dimension_semantics=("parallel", "arbitrary"), vmem_limit_bytes=VMEM_LIMIT),
        name="norm_inproj",
    )(x, mod3, norm_w.reshape(1, d), w_conv, w_plain, w_small, conv_w)


def _level_mask(level, i, j):
    same_block = (i >> (level + 1)) == (j >> (level + 1))
    return same_block & (((i >> level) & 1) == 1) & (((j >> level) & 1) == 0)


N_LEVELS = CHUNK.bit_length() - 1


def _deltanet_kernel(q_ref, k_ref, v_ref, small_ref, alog_ref, dtb_ref, o_ref,
                     state, u_s, wq_s, akd_s, el_s, *, ts, group):
    @pl.when(pl.program_id(1) == 0)
    def _():
        state[...] = jnp.zeros_like(state)

    row = lax.broadcasted_iota(jnp.int32, (CHUNK, 2 * CHUNK), 0)
    lane = lax.broadcasted_iota(jnp.int32, (CHUNK, 2 * CHUNK), 1)
    col = lane & (CHUNK - 1)
    in_a = lane < CHUNK
    causal = row >= col
    tri_strict = (row > col).astype(F32)
    eye = (row == col).astype(F32)
    row_c = lax.broadcasted_iota(jnp.int32, (CHUNK, CHUNK), 0)
    col_c = lax.broadcasted_iota(jnp.int32, (CHUNK, CHUNK), 1)
    tri = (row_c >= col_c).astype(F32)
    level_masks = [_level_mask(level, row, col).astype(F32) for level in range(N_LEVELS)]
    neg_a = -jnp.exp(alog_ref[...])
    dtb = dtb_ref[...]
    n_chunks = ts // CHUNK
    n_pairs = DN_HEADS // 2
    zeros_wide = jnp.zeros((CHUNK, DN_DV + DN_DK), BF16)
    zeros_head = jnp.zeros((CHUNK, DN_DK), BF16)

    def blockdiag(m):
        return jnp.concatenate([jnp.where(in_a, m, 0.0), jnp.where(in_a, 0.0, m)], axis=0).astype(BF16)

    def prep_stages(gi):
        chains = []
        for ci in range(group):
            c = gi * group + ci
            r0 = c * CHUNK if isinstance(c, int) else pl.multiple_of(c * CHUNK, CHUNK)
            sm = small_ref[0, pl.ds(r0, CHUNK), :]
            beta_all = _sigmoid(sm)
            z = sm + dtb
            softplus = jnp.maximum(z, 0.0) + jnp.log(1.0 + jnp.exp(-jnp.abs(z)))
            g_all = neg_a * softplus
            gc_all = jnp.dot(tri, g_all, preferred_element_type=F32, precision=HIGHEST)
            gc_t = jnp.concatenate([gc_all, gc_all], axis=0).T
            g_last = gc_all[CHUNK - 1:CHUNK, :]
            e_in_all = jnp.exp(gc_all)
            e_out_all = jnp.exp(g_last - gc_all)
            slot = (gi % 2) * group + ci
            el_s[slot] = jnp.broadcast_to(jnp.exp(g_last), (8, LANES))
            for p in range(n_pairs):
                heads = (2 * p, 2 * p + 1)
                q, k, k_beta, rhs, k_dec = [], [], [], [], []
                for h in heads:
                    gl = DN_HEADS + h
                    cq = slice(h * DN_DK, (h + 1) * DN_DK)
                    q_h = q_ref[0, pl.ds(r0, CHUNK), cq].astype(F32)
                    k_h = k_ref[0, pl.ds(r0, CHUNK), cq].astype(F32)
                    v_h = v_ref[0, pl.ds(r0, CHUNK), slice(h * DN_DV, (h + 1) * DN_DV)].astype(F32)
                    beta = beta_all[:, h:h + 1]
                    e_in = e_in_all[:, gl:gl + 1]
                    kb_h = k_h * beta
                    q.append(q_h)
                    k.append(k_h.astype(BF16))
                    k_beta.append(kb_h)
                    rhs.append(jnp.concatenate([v_h * beta, kb_h * e_in], axis=1).astype(BF16))
                    k_dec.append(k_h * e_out_all[:, gl:gl + 1])
                    wq_s[slot, h, CHUNK:2 * CHUNK, :] = (q_h * e_in).astype(BF16)
                ga, gb = DN_HEADS + heads[0], DN_HEADS + heads[1]
                gc_col = jnp.where(in_a, gc_all[:, ga:ga + 1], gc_all[:, gb:gb + 1])
                gc_row = jnp.where(in_a[0:1], gc_t[ga:ga + 1, :], gc_t[gb:gb + 1, :])
                chains.append(dict(
                    slot=slot, p=p,
                    decay=jnp.exp(jnp.where(causal, gc_col - gc_row, -jnp.inf)),
                    kq=jnp.concatenate([jnp.concatenate(k_beta, axis=1), jnp.concatenate(q, axis=1)],
                                       axis=0).astype(BF16),
                    kk=jnp.concatenate([jnp.concatenate([k[0], zeros_head], axis=1),
                                        jnp.concatenate([zeros_head, k[1]], axis=1)], axis=0),
                    rhs=jnp.concatenate([jnp.concatenate([rhs[0], zeros_wide], axis=1),
                                         jnp.concatenate([zeros_wide, rhs[1]], axis=1)], axis=0),
                    k_dec_t=jnp.concatenate(k_dec, axis=0).T.astype(BF16)))

        for ch in chains:
            ch["kk_qk"] = _mm_nt(ch.pop("kq"), ch.pop("kk"))
        yield
        for ch in chains:
            kk_qk = ch.pop("kk_qk")
            decay = ch.pop("decay")
            ch["lmat"] = kk_qk[0:CHUNK] * decay * tri_strict
            ch["a_qk"] = (kk_qk[CHUNK:2 * CHUNK] * decay).astype(BF16)
            ch["x"] = eye - ch["lmat"] * level_masks[0]
        for level in range(1, N_LEVELS):
            for ch in chains:
                ch["y"] = _mm(ch["lmat"] * level_masks[level], blockdiag(ch["x"]))
            yield
            for ch in chains:
                ch["x"] = ch["x"] - _mm(ch["x"], blockdiag(ch.pop("y")))
            yield
        for ch in chains:
            ch["uw"] = _mm(ch.pop("x"), ch.pop("rhs"))
        for ch in chains:
            slot, p = ch["slot"], ch["p"]
            uw = ch.pop("uw")
            for i, h in enumerate((2 * p, 2 * p + 1)):
                base = i * (DN_DV + DN_DK)
                u_s[slot, h] = uw[:, base:base + DN_DV]
                wq_s[slot, h, 0:CHUNK, :] = uw[:, base + DN_DV:base + DN_DV + DN_DK].astype(BF16)
            akd_s[slot, p] = jnp.concatenate([ch["a_qk"], ch["k_dec_t"]], axis=0)
        yield

    def rec_stages(gi):
        for ci in range(group):
            c = gi * group + ci
            r0 = c * CHUNK if isinstance(c, int) else pl.multiple_of(c * CHUNK, CHUNK)
            slot = (gi % 2) * group + ci
            e_last_all = el_s[slot]
            rs = [_mm(wq_s[slot, h], state[h]) for h in range(DN_HEADS)]
            yield
            v_new = [(u_s[slot, h] - rs[h][0:CHUNK]).astype(BF16) for h in range(DN_HEADS)]
            zeros_v = jnp.zeros((CHUNK, DN_DV), BF16)
            av = [_mm(akd_s[slot, p],
                      jnp.concatenate([jnp.concatenate([v_new[2 * p], zeros_v], axis=1),
                                       jnp.concatenate([zeros_v, v_new[2 * p + 1]], axis=1)], axis=0))
                  for p in range(n_pairs)]
            for h in range(DN_HEADS):
                gl = DN_HEADS + h
                lanes_h = slice((h % 2) * DN_DV, (h % 2 + 1) * DN_DV)
                o = rs[h][CHUNK:2 * CHUNK] + av[h // 2][0:CHUNK, lanes_h]
                state[h] = state[h] * e_last_all[0:1, gl:gl + 1] + av[h // 2][CHUNK:CHUNK + DN_DK, lanes_h]
                o_ref[0, pl.ds(r0, CHUNK), slice(h * DN_DV, (h + 1) * DN_DV)] = o.astype(o_ref.dtype)
            yield

    run = _run_round_robin

    n_groups = n_chunks // group
    run(prep_stages(0))
    for gi in range(1, n_groups):
        run(prep_stages(gi), rec_stages(gi - 1))
    run(rec_stages(n_groups - 1))


def _deltanet(p, small, alog_row, dtb_row, ts, group):
    b, s, _ = p.shape
    width = DN_HEADS * DN_DK
    n_slots = 2 * group
    assert (ts // CHUNK) % group == 0
    kern = functools.partial(_deltanet_kernel, ts=ts, group=group)
    return pl.pallas_call(
        kern,
        grid=(b, s // ts),
        in_specs=[pl.BlockSpec((1, ts, width), lambda bi, t: (bi, t, 0)),
                  pl.BlockSpec((1, ts, width), lambda bi, t: (bi, t, 1)),
                  pl.BlockSpec((1, ts, width), lambda bi, t: (bi, t, 2)),
                  pl.BlockSpec((1, ts, LANES), lambda bi, t: (bi, t, 0)),
                  pl.BlockSpec((1, LANES), lambda bi, t: (0, 0)),
                  pl.BlockSpec((1, LANES), lambda bi, t: (0, 0))],
        out_specs=pl.BlockSpec((1, ts, DN_WIDTH), lambda bi, t: (bi, t, 0)),
        out_shape=jax.ShapeDtypeStruct((b, s, DN_WIDTH), BF16),
        scratch_shapes=[pltpu.VMEM((DN_HEADS, DN_DK, DN_DV), F32),
                        pltpu.VMEM((n_slots, DN_HEADS, CHUNK, DN_DV), F32),
                        pltpu.VMEM((n_slots, DN_HEADS, 2 * CHUNK, DN_DK), BF16),
                        pltpu.VMEM((n_slots, DN_HEADS // 2, CHUNK + DN_DK, 2 * CHUNK), BF16),
                        pltpu.VMEM((n_slots, 8, LANES), F32)],
        compiler_params=pltpu.CompilerParams(
            dimension_semantics=("parallel", "arbitrary"), vmem_limit_bytes=VMEM_LIMIT),
        name="gated_deltanet",
    )(p, p, p, small, alog_row, dtb_row)


ATTN_BLOCKS_PER_GROUP = 2
ATTN_STRIDE = 4


def _attn_kernel(q_ref, k_ref, v_ref, o_ref, lse_ref, *scratch, dil, n_back):
    s = q_ref.shape[1]
    sub_len = s // dil
    nb = sub_len // ATTN_BLOCK
    n_pairs = ATTN_GROUP_WIDTH // LANES
    prev_block = nb > 1
    blocks_per_iter = ATTN_BLOCKS_PER_GROUP

    def lanes(pair):
        return slice(pair * LANES, (pair + 1) * LANES)

    if dil > 1:
        stage, tmp, qs, ks, vs, os_, ls = scratch
        assert dil in (ATTN_STRIDE, ATTN_STRIDE * ATTN_STRIDE)
        q_len = s // ATTN_STRIDE

        def classes():
            if dil == ATTN_STRIDE:
                return [(r, stage, r) for r in range(dil)]
            return [(r_lo + ATTN_STRIDE * r_hi, tmp, r_lo * q_len + r_hi)
                    for r_lo in range(ATTN_STRIDE) for r_hi in range(ATTN_STRIDE)]

        def deinterleave(dst, pair):
            if dil > ATTN_STRIDE:
                for r_lo in range(ATTN_STRIDE):
                    tmp[r_lo * q_len:(r_lo + 1) * q_len, :] = stage[pl.ds(r_lo, q_len, stride=ATTN_STRIDE), :]
            for r, buf, start in classes():
                dst[pair, r * sub_len:(r + 1) * sub_len, :] = (
                    buf[pl.ds(start, sub_len, stride=ATTN_STRIDE), :].astype(dst.dtype))

        def interleave(src, pair):
            for r, buf, start in classes():
                buf[pl.ds(start, sub_len, stride=ATTN_STRIDE), :] = src[pair, r * sub_len:(r + 1) * sub_len, :]
            if dil > ATTN_STRIDE:
                for r_lo in range(ATTN_STRIDE):
                    stage[pl.ds(r_lo, q_len, stride=ATTN_STRIDE), :] = tmp[r_lo * q_len:(r_lo + 1) * q_len, :]

        for src, dst in ((q_ref, qs), (k_ref, ks), (v_ref, vs)):
            for pair in range(n_pairs):
                stage[...] = src[0, :, lanes(pair)].astype(F32)
                deinterleave(dst, pair)

        def load(ref, src, pair, r0):
            return src[pair, pl.ds(r0, ATTN_BLOCK), :]
    else:
        qs, ks, vs = q_ref, k_ref, v_ref

        def load(ref, src, pair, r0):
            return ref[0, pl.ds(r0, ATTN_BLOCK), lanes(pair)]

    n_keys = 2 * ATTN_BLOCK if prev_block else ATTN_BLOCK
    qi = lax.broadcasted_iota(jnp.int32, (ATTN_BLOCK, n_keys), 0)
    ki = lax.broadcasted_iota(jnp.int32, (ATTN_BLOCK, n_keys), 1)
    dist = qi + (n_keys - ATTN_BLOCK) - ki
    band = (dist >= 0) & (dist <= n_back)
    bias_full = jnp.where(band, 0.0, -jnp.inf).astype(F32)
    bias_first = jnp.where(band & (ki >= ATTN_BLOCK), 0.0, -jnp.inf).astype(F32)
    lane = lax.broadcasted_iota(jnp.int32, (ATTN_BLOCK, LANES), 1)
    low_half = lane < ATTN_HEAD_DIM
    scale = ATTN_HEAD_DIM ** -0.5

    def block_body(fi):
        chains = []
        for bi in range(blocks_per_iter):
            f = fi * blocks_per_iter + bi
            q0 = f * ATTN_BLOCK
            if prev_block:
                k0 = max(f - 1, 0) * ATTN_BLOCK
                bias = bias_full if f % nb > 0 else bias_first
            else:
                bias = bias_full
            for pair in range(n_pairs):
                qp = load(q_ref, qs, pair, q0) * scale
                kp = load(k_ref, ks, pair, q0)
                vp = load(v_ref, vs, pair, q0)
                if prev_block:
                    kp = jnp.concatenate([load(k_ref, ks, pair, k0), kp], axis=0)
                    vp = jnp.concatenate([load(v_ref, vs, pair, k0), vp], axis=0)
                for half in range(2):
                    sel = low_half if half == 0 else ~low_half
                    chains.append(dict(q0=q0, pair=pair, half=half, bias=bias, kp=kp, vp=vp,
                                       qm=jnp.where(sel, qp, jnp.zeros_like(qp))))
        for ch in chains:
            ch["sc"] = _mm_nt(ch.pop("qm"), ch.pop("kp"))
        for ch in chains:
            sc = ch.pop("sc") + ch.pop("bias")
            m = jnp.max(sc, axis=-1, keepdims=True)
            p = jnp.exp(sc - m)
            ch["denom"] = jnp.sum(p, axis=-1, keepdims=True)
            ch["m"] = m
            ch["p"] = p.astype(BF16)
        for ch in chains:
            ch["pv"] = _mm(ch.pop("p"), ch.pop("vp"))
        for c0 in range(0, len(chains), 2):
            lo, hi = chains[c0], chains[c0 + 1]
            o = jnp.where(low_half, lo["pv"] / lo["denom"], hi["pv"] / hi["denom"])
            lse = jnp.where(low_half, lo["m"] + jnp.log(lo["denom"]), hi["m"] + jnp.log(hi["denom"]))
            q0, pair = lo["q0"], lo["pair"]
            if dil > 1:
                os_[pair, pl.ds(q0, ATTN_BLOCK), :] = o
                ls[pair, pl.ds(q0, ATTN_BLOCK), :] = lse
            else:
                o_ref[0, pl.ds(q0, ATTN_BLOCK), lanes(pair)] = o.astype(o_ref.dtype)
                lse_ref[0, pl.ds(q0, ATTN_BLOCK), lanes(pair)] = lse

    n_iters = dil * nb // blocks_per_iter
    for fi in range(n_iters):
        block_body(fi)

    if dil > 1:
        for pair in range(n_pairs):
            for src, dst in ((os_, o_ref), (ls, lse_ref)):
                interleave(src, pair)
                dst[0, :, lanes(pair)] = stage[...].astype(dst.dtype)


def _dilated_attention(p, group, col_base):
    b, s, _ = p.shape
    window, dil = DIL_CONFIGS[group]
    gw = ATTN_GROUP_WIDTH
    blk = col_base // gw + group
    step = ATTN_WIDTH // gw
    kern = functools.partial(_attn_kernel, dil=dil, n_back=window // dil)
    return pl.pallas_call(
        kern,
        grid=(b,),
        in_specs=[pl.BlockSpec((1, s, gw), lambda bi: (bi, 0, blk)),
                  pl.BlockSpec((1, s, gw), lambda bi: (bi, 0, blk + step)),
                  pl.BlockSpec((1, s, gw), lambda bi: (bi, 0, blk + 2 * step))],
        out_specs=[pl.BlockSpec((1, s, gw), lambda bi: (bi, 0, 0)),
                   pl.BlockSpec((1, s, gw), lambda bi: (bi, 0, 0))],
        out_shape=[jax.ShapeDtypeStruct((b, s, gw), BF16),
                   jax.ShapeDtypeStruct((b, s, gw), F32)],
        scratch_shapes=([] if dil == 1 else
                        [pltpu.VMEM((s, LANES), F32) for _ in range(2)]
                        + [pltpu.VMEM((gw // LANES, s, LANES), BF16) for _ in range(3)]
                        + [pltpu.VMEM((gw // LANES, s, LANES), F32) for _ in range(2)]),
        compiler_params=pltpu.CompilerParams(
            dimension_semantics=("parallel",), vmem_limit_bytes=VMEM_LIMIT),
        name=f"dilated_attn_g{group}",
    )(p, p, p)


def _attn_all_kernel(*refs):
    n = len(DIL_CONFIGS)
    ins, outs, scratch = refs[:3 * n], refs[3 * n:5 * n], refs[5 * n:]
    for g, (window, dil) in enumerate(DIL_CONFIGS):
        _attn_kernel(*ins[3 * g:3 * g + 3], *outs[2 * g:2 * g + 2], *(scratch if dil > 1 else ()),
                     dil=dil, n_back=window // dil)


def _dilated_attention_all(p, col_base):
    b, s, _ = p.shape
    gw = ATTN_GROUP_WIDTH
    n = len(DIL_CONFIGS)
    step = ATTN_WIDTH // gw

    def in_spec(blk):
        return pl.BlockSpec((1, s, gw), lambda bi: (bi, 0, blk))

    out_spec = pl.BlockSpec((1, s, gw), lambda bi: (bi, 0, 0))
    res = pl.pallas_call(
        _attn_all_kernel,
        grid=(b,),
        in_specs=[in_spec(col_base // gw + g + t * step) for g in range(n) for t in range(3)],
        out_specs=[out_spec] * (2 * n),
        out_shape=[jax.ShapeDtypeStruct((b, s, gw), dt) for _ in range(n) for dt in (BF16, F32)],
        scratch_shapes=([pltpu.VMEM((s, LANES), F32) for _ in range(2)]
                        + [pltpu.VMEM((gw // LANES, s, LANES), BF16) for _ in range(3)]
                        + [pltpu.VMEM((gw // LANES, s, LANES), F32) for _ in range(2)]),
        compiler_params=pltpu.CompilerParams(
            dimension_semantics=("parallel",), vmem_limit_bytes=VMEM_LIMIT),
        name="dilated_attn_all",
    )(*([p] * (3 * n)))
    return list(res[0::2]), list(res[1::2])


MERGE_SUBTILES = 8


def _merge_kernel(x_ref, mod_ref, oa_ref, za_ref, ga_ref, gb_ref, zb_ref,
                  ob0_ref, ob1_ref, ob2_ref, l0_ref, l1_ref, l2_ref,
                  dnw_ref, pa_ref, pb_ref, wo_ref, fw_ref, out_ref):
    def silu_of_half(h):
        return h + h * jnp.tanh(h)

    gate = mod_ref[0][2:3]

    def stages(rows):
        oa = oa_ref[0, rows, :].astype(F32)
        za_half = za_ref[0, rows, :].astype(F32)
        parts = []
        for h in range(DN_HEADS):
            cq = slice(h * DN_DV, (h + 1) * DN_DV)
            blk = oa[:, cq]
            y = blk * lax.rsqrt(jnp.mean(blk * blk, axis=-1, keepdims=True) + EPS) * dnw_ref[...]
            parts.append(y * silu_of_half(za_half[:, cq]))
        ya_half = _mm(jnp.concatenate(parts, axis=1), pa_ref[...])
        yield

        l0, l1, l2 = l0_ref[0, rows, :], l1_ref[0, rows, :], l2_ref[0, rows, :]
        m = jnp.maximum(jnp.maximum(l0, l1), l2)
        e0, e1, e2 = jnp.exp(l0 - m), jnp.exp(l1 - m), jnp.exp(l2 - m)
        inv_den = 1.0 / (e0 + e1 + e2)
        ob = jnp.concatenate([ob0_ref[0, rows, :].astype(F32) * (e0 * inv_den),
                              ob1_ref[0, rows, :].astype(F32) * (e1 * inv_den),
                              ob2_ref[0, rows, :].astype(F32) * (e2 * inv_den)], axis=1)
        yb_half = _mm(ob * silu_of_half(zb_ref[0, rows, :].astype(F32)), pb_ref[...])
        yield

        merged = (ya_half + ya_half * jnp.tanh(ga_ref[0, rows, :].astype(F32))
                  + (yb_half + yb_half * jnp.tanh(gb_ref[0, rows, :].astype(F32))))
        delta = _mm(merged, wo_ref[...])
        yield

        xo = x_ref[0, rows, :] + gate * delta
        out_ref[0, rows, :] = xo * lax.rsqrt(jnp.mean(xo * xo, axis=-1, keepdims=True) + EPS) * fw_ref[...]
        yield

    tm = x_ref.shape[1]
    sub = tm // MERGE_SUBTILES
    _run_round_robin(*[stages(slice(i * sub, (i + 1) * sub)) for i in range(MERGE_SUBTILES)])


def _merge(x, mod3, o_a, p, obs, lses, dn_norm_w, pa, pb, wo, final_w, tm, cols):
    b, s, d = x.shape
    gw = ATTN_GROUP_WIDTH

    def row_spec(width, blk):
        return pl.BlockSpec((1, tm, width), lambda bi, i: (bi, i, blk))

    def full_spec(shape):
        return pl.BlockSpec(shape, lambda bi, i: (0,) * len(shape), pipeline_mode=pl.Buffered(1))

    return pl.pallas_call(
        _merge_kernel,
        grid=(b, s // tm),
        in_specs=[row_spec(d, 0),
                  pl.BlockSpec((1, 3, d), lambda bi, i: (bi, 0, 0)),
                  row_spec(DN_WIDTH, 0),
                  row_spec(DN_WIDTH, cols["za"] // DN_WIDTH),
                  row_spec(d, cols["ga"] // d),
                  row_spec(d, cols["gb"] // d),
                  row_spec(ATTN_WIDTH, cols["zb"] // ATTN_WIDTH),
                  row_spec(gw, 0), row_spec(gw, 0), row_spec(gw, 0),
                  row_spec(gw, 0), row_spec(gw, 0), row_spec(gw, 0),
                  full_spec((1, DN_DV)), full_spec(pa.shape), full_spec(pb.shape), full_spec(wo.shape),
                  full_spec((1, d))],
        out_specs=row_spec(d, 0),
        out_shape=jax.ShapeDtypeStruct((b, s, d), F32),
        compiler_params=pltpu.CompilerParams(
            dimension_semantics=("parallel", "parallel"), vmem_limit_bytes=VMEM_LIMIT),
        name="merge_outproj",
    )(x, mod3, o_a, p, p, p, p, *obs, *lses, dn_norm_w.reshape(1, DN_DV), pa, pb, wo, final_w.reshape(1, d))


W_PREP_COLS = 128


def _wprep_kernel(wt_ref, wc_ref, wp_ref, ws_ref, *, conv_width, segments, small_off):
    wc_ref[...] = wt_ref[0:conv_width, :].astype(BF16)
    for src, width, dst, scale in segments:
        blk = wt_ref[src:src + width, :]
        wp_ref[dst:dst + width, :] = (blk if scale == 1.0 else blk * scale).astype(BF16)
    n_small = 2 * DN_HEADS
    ws_ref[0:n_small, :] = wt_ref[small_off:small_off + n_small, :].astype(BF16)
    ws_ref[n_small:LANES, :] = jnp.zeros((LANES - n_small, wt_ref.shape[1]), BF16)


def _weight_prep(wt, conv_width, plain_width, segments, small_off):
    n_in, d = wt.shape
    assert d % W_PREP_COLS == 0 and small_off % 16 == 0 and all(sg[0] % 16 == 0 for sg in segments)
    kern = functools.partial(_wprep_kernel, conv_width=conv_width, segments=segments, small_off=small_off)
    return pl.pallas_call(
        kern,
        grid=(d // W_PREP_COLS,),
        in_specs=[pl.BlockSpec((n_in, W_PREP_COLS), lambda i: (0, i))],
        out_specs=[pl.BlockSpec((conv_width, W_PREP_COLS), lambda i: (0, i)),
                   pl.BlockSpec((plain_width, W_PREP_COLS), lambda i: (0, i)),
                   pl.BlockSpec((LANES, W_PREP_COLS), lambda i: (0, i))],
        out_shape=[jax.ShapeDtypeStruct((conv_width, d), BF16),
                   jax.ShapeDtypeStruct((plain_width, d), BF16),
                   jax.ShapeDtypeStruct((LANES, d), BF16)],
        compiler_params=pltpu.CompilerParams(dimension_semantics=("parallel",), vmem_limit_bytes=VMEM_LIMIT),
        name="weight_prep",
    )(wt)


def _layer(x, c, norm_w, ada_w, ada_b, w_in, conv_w, a_log, dt_bias, dn_norm_w, w_proj_a, w_proj_b, w_out,
           final_norm_w):
    b, s, d = x.shape
    qk = DN_HEADS * DN_DK
    splits = (qk, qk, DN_WIDTH, DN_WIDTH, DN_HEADS, DN_HEADS, ATTN_WIDTH, ATTN_WIDTH, ATTN_WIDTH, ATTN_WIDTH, d, d)
    names = ("qa", "ka", "va", "za", "beta", "a", "qb", "kb", "vb", "zb", "ga", "gb")
    offs = dict(zip(names, np.cumsum((0,) + splits[:-1]).tolist()))
    widths = dict(zip(names, splits))
    order = ("za", "ga", "gb", "qb", "kb", "vb", "zb")
    cols, pos = {}, 0
    for n in order:
        assert pos % widths[n] == 0, (n, pos)
        cols[n] = pos
        pos += widths[n]
    assert offs["qa"] == 0 and offs["ka"] == qk and offs["va"] == 2 * qk and offs["a"] == offs["beta"] + DN_HEADS
    halved = ("za", "ga", "gb", "zb")
    segments = tuple((offs[n], widths[n], cols[n], 0.5 if n in halved else 1.0) for n in order)
    w_conv, w_plain, w_small = _weight_prep(w_in.T, 2 * qk + DN_WIDTH, pos, segments, offs["beta"])
    pad_lo = jnp.zeros((DN_HEADS,), F32)
    pad_hi = jnp.zeros((LANES - 2 * DN_HEADS,), F32)
    alog_row = jnp.concatenate([pad_lo, a_log.astype(F32), pad_hi]).reshape(1, LANES)
    dtb_row = jnp.concatenate([pad_lo, dt_bias.astype(F32), pad_hi]).reshape(1, LANES)

    mod3 = _adaln_mod(c, ada_w, ada_b).reshape(b, 3, d)
    p_conv, p, small = _inproj(x, mod3, norm_w, w_conv, w_plain, w_small, conv_w)
    o_a = _deltanet(p_conv, small, alog_row, dtb_row, ts=min(s, 1024), group=4)
    obs, lses = _dilated_attention_all(p, cols["qb"])
    return _merge(x, mod3, o_a, p, obs, lses, dn_norm_w, (w_proj_a * 0.5).astype(BF16),
                  (w_proj_b * 0.5).astype(BF16), w_out.astype(BF16), final_norm_w, tm=min(s, 1024), cols=cols)


def kernel(x, c, norm_w, ada_w, ada_b, w_in, conv_w, a_log, dt_bias, dn_norm_w, w_proj_a, w_proj_b, w_out,
           final_norm_w):
    depth = norm_w.shape[0]
    assert depth == 1, "the final RMSNorm is fused into the single layer's output kernel"
    return _layer(x, c, norm_w[0], ada_w[0], ada_b[0], w_in[0], conv_w[0], a_log[0], dt_bias[0], dn_norm_w[0],
                  w_proj_a[0], w_proj_b[0], w_out[0], final_norm_w)
```
